```python
import jax
import jax.numpy as jnp
from jax import lax
import numpy as np

D_MODEL = 1024
BATCH = 8
SEQ = 2048
DEPTH = 2
DEC_BATCH = 128
DEC_SEQ = 1
PAST_LEN = 16384
PAGE_SIZE = 128

D_MIX = D_MODEL
SSD_W = D_MIX // 2
SSD_HEAD_DIM = 64
SSD_H = SSD_W // SSD_HEAD_DIM
SSD_N = 64
SSD_G = 2
SSD_CONV_W = 4
SSD_CONV_DIM = SSD_W + 2 * SSD_G * SSD_N
CHUNK = 128
RWKV_W = D_MIX // 4
RWKV_N = 64
RWKV_H = RWKV_W // RWKV_N
W_LORA = 32
A_LORA = 32
G_LORA = 64
RWKV_FEAT = 3 * RWKV_W + W_LORA + A_LORA + G_LORA
RWKV_GN_EPS = RWKV_N * 1e-5
GLA_W = D_MIX - SSD_W - RWKV_W
GLA_H = 4
GLA_DV = GLA_W // GLA_H
GLA_DK = GLA_DV // 2
GK_LORA = 16
GATE_NORMALIZER = 16.0
IN_SIZES = (SSD_W, SSD_CONV_DIM, SSD_H, RWKV_FEAT, GLA_H * GLA_DK, GLA_H * GLA_DK, GLA_W, GK_LORA, GLA_W)
IN_DIM = SSD_W + SSD_CONV_DIM + SSD_H + RWKV_FEAT + 2 * GLA_H * GLA_DK + GLA_W + GK_LORA + GLA_W
F_DENSE = ((8 * D_MODEL // 3 + 127) // 128) * 128
N_EXPERTS = 8
TOP_K = 2
F_EXPERT = D_MODEL
N_DENSE = (DEPTH + 1) // 2
N_MOE = DEPTH // 2
ALPHA = (2.0 * DEPTH) ** 0.25
BETA = (8.0 * DEPTH) ** -0.25
LN_EPS = 1e-5
RMS_EPS = 1e-6

kernel_name = 'hybrid_ssd_rwkv7_gla_decoder_step'


def _layernorm(x, g, b):
    xf = x.astype(jnp.float32)
    mu = jnp.mean(xf, axis=-1, keepdims=True)
    var = jnp.mean(jnp.square(xf - mu), axis=-1, keepdims=True)
    return ((xf - mu) * lax.rsqrt(var + LN_EPS)).astype(x.dtype) * g + b


def _rmsnorm(x, g):
    xf = x.astype(jnp.float32)
    return (xf * lax.rsqrt(jnp.mean(jnp.square(xf), axis=-1, keepdims=True) + RMS_EPS)).astype(x.dtype) * g


def _split(t, sizes):
    parts, off = [], 0
    for s in sizes:
        parts.append(t[..., off:off + s])
        off += s
    return parts


def _to_chunks(t, chunk):
    pad = (-t.shape[1]) % chunk
    t = jnp.pad(t, [(0, 0), (0, pad)] + [(0, 0)] * (t.ndim - 2))
    t = t.reshape((t.shape[0], t.shape[1] // chunk, chunk) + t.shape[2:])
    return jnp.moveaxis(t, 1, 0)


def _from_chunks(t, length):
    t = jnp.moveaxis(t, 0, 1)
    return t.reshape((t.shape[0], t.shape[1] * t.shape[2]) + t.shape[3:])[:, :length]


def _ssd_scan(x, dt, a, bm, cm, h0):
    length = x.shape[1]
    chunk = min(CHUNK, length)
    rep = SSD_H // SSD_G
    bm = jnp.repeat(bm, rep, axis=2)
    cm = jnp.repeat(cm, rep, axis=2)
    causal = jnp.tril(jnp.ones((chunk, chunk), dtype=bool))[None, :, :, None]

    def step(h, inp):
        xc, dtc, bc, cc = inp
        cum = jnp.cumsum(dtc.astype(jnp.float32) * a, axis=1)
        seg = cum[:, :, None, :] - cum[:, None, :, :]
        decay = jnp.exp(jnp.where(causal, seg, -jnp.inf))
        scores = jnp.einsum('bihn,bjhn->bijh', cc, bc) * decay * dtc[:, None, :, :]
        y = jnp.einsum('bijh,bjhp->bihp', scores, xc)
        y = y + jnp.einsum('bihn,bhpn->bihp', cc, h) * jnp.exp(cum)[..., None]
        tail = jnp.exp(cum[:, -1:, :] - cum) * dtc
        h_new = h * jnp.exp(cum[:, -1])[:, :, None, None] + jnp.einsum('bjhn,bjhp->bhpn', bc * tail[..., None], xc)
        return h_new.astype(h.dtype), y.astype(xc.dtype)

    xs = tuple(_to_chunks(t, chunk) for t in (x, dt, bm, cm))
    h, y = lax.scan(step, h0, xs)
    return _from_chunks(y, length), h


def _gla_scan(q, k, v, g, s0):
    length = q.shape[1]
    chunk = min(CHUNK, length)
    causal = jnp.tril(jnp.ones((chunk, chunk), dtype=bool))[None, :, :, None, None]

    def step(s, inp):
        qc, kc, vc, gc = inp
        cum = jnp.cumsum(gc.astype(jnp.float32), axis=1)
        seg = cum[:, :, None] - cum[:, None]
        decay = jnp.exp(jnp.where(causal, seg, -jnp.inf))
        att = jnp.einsum('bihd,bjhd,bijhd->bijh', qc, kc, decay)
        o = jnp.einsum('bijh,bjhe->bihe', att, vc)
        o = o + jnp.einsum('bihd,bhde->bihe', qc * jnp.exp(cum), s)
        tail = jnp.exp(cum[:, -1:] - cum)
        s_new = s * jnp.exp(cum[:, -1])[..., None] + jnp.einsum('bjhd,bjhe->bhde', kc * tail, vc)
        return s_new.astype(s.dtype), o.astype(vc.dtype)

    xs = tuple(_to_chunks(t, chunk) for t in (q, k, v, g))
    s, o = lax.scan(step, s0, xs)
    return _from_chunks(o, length), s


def _rwkv_scan(r, w, k, v, kk, a, s0):
    def step(s, inp):
        rt, wt, kt, vt, kkt, at = inp
        sa = jnp.einsum('bhvk,bhk->bhv', s, -kkt)
        s_new = s * wt[:, :, None, :] + sa[..., None] * (kkt * at)[:, :, None, :] + vt[..., None] * kt[:, :, None, :]
        o = jnp.einsum('bhvk,bhk->bhv', s_new, rt)
        return s_new.astype(s.dtype), o.astype(rt.dtype)

    xs = tuple(jnp.moveaxis(t, 1, 0) for t in (r, w, k, v, kk, a))
    s, o = lax.scan(step, s0, xs)
    return jnp.moveaxis(o, 0, 1), s


def _mixer(h, l, p, st):
    ssd_h0, conv0, rwkv_s0, shift0, gla_s0 = st
    bsz, length, _ = h.shape
    proj = h @ p['w_in'][l]
    z, xbc, dt_raw, rw, gq, gk, gv, glo, gg = _split(proj, IN_SIZES)

    conv_w = p['ssd_conv_w'][l]
    xpad = jnp.concatenate([conv0.astype(xbc.dtype), xbc], axis=1)
    conv = p['ssd_conv_b'][l] + xpad[:, 0:length] * conv_w[0]
    for i in range(1, SSD_CONV_W):
        conv = conv + xpad[:, i:i + length] * conv_w[i]
    new_conv = xpad[:, -(SSD_CONV_W - 1):]
    xs, bs, cs = _split(jax.nn.silu(conv), (SSD_W, SSD_G * SSD_N, SSD_G * SSD_N))
    xs = xs.reshape(bsz, length, SSD_H, SSD_HEAD_DIM)
    bs = bs.reshape(bsz, length, SSD_G, SSD_N)
    cs = cs.reshape(bsz, length, SSD_G, SSD_N)
    dt = jax.nn.softplus(dt_raw + p['ssd_dt_bias'][l])
    a = -jnp.exp(p['ssd_a_log'][l])
    y, ssd_h = _ssd_scan(xs, dt, a, bs, cs, ssd_h0)
    y = y + xs * p['ssd_d'][l][:, None]
    y_ssd = _rmsnorm(y.reshape(bsz, length, SSD_W) * jax.nn.silu(z), p['ssd_norm_g'][l])

    prev = jnp.concatenate([shift0[:, None].astype(rw.dtype), rw[:, :-1]], axis=1)
    rw_mix = rw + p['rwkv_mu'][l] * (prev - rw)
    new_shift = rw[:, -1]
    r, k, v, wl, al, gl = _split(rw_mix, (RWKV_W, RWKV_W, RWKV_W, W_LORA, A_LORA, G_LORA))
    w_log = -jax.nn.softplus(-(p['rwkv_w0'][l] + jnp.tanh(wl) @ p['rwkv_w2'][l])) - 0.5
    decay = jnp.exp(-jnp.exp(w_log))
    aic = jax.nn.sigmoid(p['rwkv_a0'][l] + al @ p['rwkv_a2'][l])
    gate = jax.nn.sigmoid(gl) @ p['rwkv_g2'][l]

    def hd(t):
        return t.reshape(bsz, length, RWKV_H, RWKV_N)

    kkf = hd(k * p['rwkv_k_k'][l]).astype(jnp.float32)
    kk = (kkf * lax.rsqrt(jnp.sum(jnp.square(kkf), axis=-1, keepdims=True) + 1e-12)).astype(k.dtype)
    k = k * (1.0 + (aic - 1.0) * p['rwkv_k_a'][l])
    rh, kh, vh = hd(r), hd(k), hd(v)
    o, rwkv_s = _rwkv_scan(rh, hd(decay), kh, vh, kk, hd(aic), rwkv_s0)
    of = o.astype(jnp.float32)
    mu = jnp.mean(of, axis=-1, keepdims=True)
    var = jnp.mean(jnp.square(of - mu), axis=-1, keepdims=True)
    o = ((of - mu) * lax.rsqrt(var + RWKV_GN_EPS)).astype(o.dtype).reshape(bsz, length, RWKV_W)
    o = o * p['rwkv_ln_g'][l] + p['rwkv_ln_b'][l]
    bonus = jnp.sum(rh * kh * p['rwkv_r_k'][l], axis=-1, keepdims=True) * vh
    y_rwkv = (o + bonus.reshape(bsz, length, RWKV_W)) * gate

    q = gq.reshape(bsz, length, GLA_H, GLA_DK) * (GLA_DK ** -0.5)
    kg = gk.reshape(bsz, length, GLA_H, GLA_DK)
    vg = gv.reshape(bsz, length, GLA_H, GLA_DV)
    lg = jax.nn.log_sigmoid(glo @ p['gla_w_gk2'][l] + p['gla_b_gk'][l]) / GATE_NORMALIZER
    og, gla_s = _gla_scan(q, kg, vg, lg.reshape(bsz, length, GLA_H, GLA_DK), gla_s0)
    y_gla = _rmsnorm(og, p['gla_norm_g'][l]).reshape(bsz, length, GLA_W) * jax.nn.silu(gg)

    y = jnp.concatenate([y_ssd, y_rwkv, y_gla], axis=-1) @ p['w_out'][l]
    return y, (ssd_h, new_conv, rwkv_s, new_shift, gla_s)


def _swiglu(h, wg, wu, wd):
    return (jax.nn.silu(h @ wg) * (h @ wu)) @ wd


def _moe(h, router, wg, wu, wd):
    logits = (h @ router).astype(jnp.float32)
    vals, idx = lax.top_k(logits, TOP_K)
    wts = jax.nn.softmax(vals, axis=-1)
    comb = jnp.sum(jax.nn.one_hot(idx, N_EXPERTS, dtype=jnp.float32) * wts[..., None], axis=-2).astype(h.dtype)
    out = jnp.zeros_like(h)
    for e in range(N_EXPERTS):
        out = out + comb[..., e:e + 1] * _swiglu(h, wg[e], wu[e], wd[e])
    return out


def _layer(x, c, l, p, st):
    mod = jax.nn.silu(c) @ p['w_ada'][l] + p['b_ada'][l]
    sh1, sc1, gt1, sh2, sc2, gt2 = [m[:, None, :] for m in jnp.split(mod, 6, axis=-1)]
    h = x * (1.0 + sc1) + sh1
    m, new_st = _mixer(h, l, p, st)
    x = _layernorm(ALPHA * x + (1.0 + gt1) * m, p['ln_mix_g'][l], p['ln_mix_b'][l])
    h = x * (1.0 + sc2) + sh2
    i = l // 2
    if l % 2 == 0:
        f = _swiglu(h, p['ffn_w_gate'][i], p['ffn_w_up'][i], p['ffn_w_down'][i])
    else:
        f = _moe(h, p['moe_router'][i], p['moe_w_gate'][i], p['moe_w_up'][i], p['moe_w_down'][i])
    x = _layernorm(ALPHA * x + (1.0 + gt2) * f, p['ln_ffn_g'][l], p['ln_ffn_b'][l])
    return x, new_st


def _run_group(x, c, states, p):
    new = ([], [], [], [], [])
    for l in range(DEPTH):
        x, st = _layer(x, c, l, p, tuple(s[l] for s in states))
        for acc, s in zip(new, st):
            acc.append(s)
    return x, tuple(jnp.stack(acc, axis=0) for acc in new)


def setup_inputs(seed: int = 0) -> dict:
    key = jax.random.key(seed)
    keys = iter(jax.random.split(key, 64))

    def nrm(shape, scale):
        return jax.random.normal(next(keys), shape, jnp.float32) * scale

    def gain(shape):
        return 1.0 + nrm(shape, 0.02)

    dt0 = jnp.exp(jax.random.uniform(next(keys), (DEPTH, SSD_H), jnp.float32, float(np.log(1e-3)), float(np.log(1e-1))))
    return {
        'x_prompt': nrm((BATCH, SEQ, D_MODEL), 1.0),
        'x_sample': nrm((DEC_BATCH, DEC_SEQ, D_MODEL), 1.0),
        'c_prompt': nrm((BATCH, D_MODEL), 1.0),
        'c_sample': nrm((DEC_BATCH, D_MODEL), 1.0),
        'state_ssd': nrm((DEPTH, DEC_BATCH, SSD_H, SSD_HEAD_DIM, SSD_N), 0.1),
        'state_ssd_conv': nrm((DEPTH, DEC_BATCH, SSD_CONV_W - 1, SSD_CONV_DIM), 0.5),
        'state_rwkv': nrm((DEPTH, DEC_BATCH, RWKV_H, RWKV_N, RWKV_N), 0.1),
        'state_rwkv_shift': nrm((DEPTH, DEC_BATCH, RWKV_FEAT), 0.5),
        'state_gla': nrm((DEPTH, DEC_BATCH, GLA_H, GLA_DK, GLA_DV), 0.1),
        'w_ada': nrm((DEPTH, D_MODEL, 6 * D_MODEL), 0.1 * D_MODEL ** -0.5),
        'b_ada': nrm((DEPTH, 6 * D_MODEL), 0.01),
        'w_in': nrm((DEPTH, D_MODEL, IN_DIM), D_MODEL ** -0.5),
        'w_out': nrm((DEPTH, D_MIX, D_MODEL), BETA * D_MIX ** -0.5),
        'ssd_conv_w': nrm((DEPTH, SSD_CONV_W, SSD_CONV_DIM), SSD_CONV_W ** -0.5),
        'ssd_conv_b': nrm((DEPTH, SSD_CONV_DIM), 0.02),
        'ssd_dt_bias': dt0 + jnp.log(-jnp.expm1(-dt0)),
        'ssd_a_log': jnp.log(jax.random.uniform(next(keys), (DEPTH, SSD_H), jnp.float32, 1.0, 16.0)),
        'ssd_d': 1.0 + nrm((DEPTH, SSD_H), 0.1),
        'ssd_norm_g': gain((DEPTH, SSD_W)),
        'rwkv_mu': jax.random.uniform(next(keys), (DEPTH, RWKV_FEAT), jnp.float32, 0.0, 1.0),
        'rwkv_w0': jax.random.uniform(next(keys), (DEPTH, RWKV_W), jnp.float32, -4.0, 0.0),
        'rwkv_w2': nrm((DEPTH, W_LORA, RWKV_W), 0.1 * W_LORA ** -0.5),
        'rwkv_a0': nrm((DEPTH, RWKV_W), 0.1),
        'rwkv_a2': nrm((DEPTH, A_LORA, RWKV_W), 0.1 * A_LORA ** -0.5),
        'rwkv_g2': nrm((DEPTH, G_LORA, RWKV_W), G_LORA ** -0.5),
        'rwkv_k_k': 0.85 + nrm((DEPTH, RWKV_W), 0.05),
        'rwkv_k_a': 1.0 + nrm((DEPTH, RWKV_W), 0.05),
        'rwkv_r_k': nrm((DEPTH, RWKV_H, RWKV_N), 0.1),
        'rwkv_ln_g': gain((DEPTH, RWKV_W)),
        'rwkv_ln_b': nrm((DEPTH, RWKV_W), 0.02),
        'gla_w_gk2': nrm((DEPTH, GK_LORA, GLA_H * GLA_DK), GK_LORA ** -0.5),
        'gla_b_gk': nrm((DEPTH, GLA_H * GLA_DK), 0.1),
        'gla_norm_g': gain((DEPTH, GLA_DV)),
        'ln_mix_g': gain((DEPTH, D_MODEL)),
        'ln_mix_b': nrm((DEPTH, D_MODEL), 0.02),
        'ln_ffn_g': gain((DEPTH, D_MODEL)),
        'ln_ffn_b': nrm((DEPTH, D_MODEL), 0.02),
        'ffn_w_gate': nrm((N_DENSE, D_MODEL, F_DENSE), D_MODEL ** -0.5),
        'ffn_w_up': nrm((N_DENSE, D_MODEL, F_DENSE), D_MODEL ** -0.5),
        'ffn_w_down': nrm((N_DENSE, F_DENSE, D_MODEL), BETA * F_DENSE ** -0.5),
        'moe_router': nrm((N_MOE, D_MODEL, N_EXPERTS), D_MODEL ** -0.5),
        'moe_w_gate': nrm((N_MOE, N_EXPERTS, D_MODEL, F_EXPERT), D_MODEL ** -0.5),
        'moe_w_up': nrm((N_MOE, N_EXPERTS, D_MODEL, F_EXPERT), D_MODEL ** -0.5),
        'moe_w_down': nrm((N_MOE, N_EXPERTS, F_EXPERT, D_MODEL), BETA * F_EXPERT ** -0.5),
    }


def reference(x_prompt, x_sample, c_prompt, c_sample, state_ssd, state_ssd_conv, state_rwkv, state_rwkv_shift, state_gla, w_ada, b_ada, w_in, w_out, ssd_conv_w, ssd_conv_b, ssd_dt_bias, ssd_a_log, ssd_d, ssd_norm_g, rwkv_mu, rwkv_w0, rwkv_w2, rwkv_a0, rwkv_a2, rwkv_g2, rwkv_k_k, rwkv_k_a, rwkv_r_k, rwkv_ln_g, rwkv_ln_b, gla_w_gk2, gla_b_gk, gla_norm_g, ln_mix_g, ln_mix_b, ln_ffn_g, ln_ffn_b, ffn_w_gate, ffn_w_up, ffn_w_down, moe_router, moe_w_gate, moe_w_up, moe_w_down):
    p = dict(w_ada=w_ada, b_ada=b_ada, w_in=w_in, w_out=w_out, ssd_conv_w=ssd_conv_w, ssd_conv_b=ssd_conv_b,
             ssd_dt_bias=ssd_dt_bias, ssd_a_log=ssd_a_log, ssd_d=ssd_d, ssd_norm_g=ssd_norm_g,
             rwkv_mu=rwkv_mu, rwkv_w0=rwkv_w0, rwkv_w2=rwkv_w2, rwkv_a0=rwkv_a0, rwkv_a2=rwkv_a2,
             rwkv_g2=rwkv_g2, rwkv_k_k=rwkv_k_k, rwkv_k_a=rwkv_k_a, rwkv_r_k=rwkv_r_k,
             rwkv_ln_g=rwkv_ln_g, rwkv_ln_b=rwkv_ln_b, gla_w_gk2=gla_w_gk2, gla_b_gk=gla_b_gk,
             gla_norm_g=gla_norm_g, ln_mix_g=ln_mix_g, ln_mix_b=ln_mix_b, ln_ffn_g=ln_ffn_g,
             ln_ffn_b=ln_ffn_b, ffn_w_gate=ffn_w_gate, ffn_w_up=ffn_w_up, ffn_w_down=ffn_w_down,
             moe_router=moe_router, moe_w_gate=moe_w_gate, moe_w_up=moe_w_up, moe_w_down=moe_w_down)
    bp = x_prompt.shape[0]
    zero_states = (jnp.zeros((DEPTH, bp) + state_ssd.shape[2:], state_ssd.dtype),
                   jnp.zeros((DEPTH, bp) + state_ssd_conv.shape[2:], state_ssd_conv.dtype),
                   jnp.zeros((DEPTH, bp) + state_rwkv.shape[2:], state_rwkv.dtype),
                   jnp.zeros((DEPTH, bp) + state_rwkv_shift.shape[2:], state_rwkv_shift.dtype),
                   jnp.zeros((DEPTH, bp) + state_gla.shape[2:], state_gla.dtype))
    y_prompt, (ps_ssd, ps_conv, ps_rwkv, ps_shift, ps_gla) = _run_group(x_prompt, c_prompt, zero_states, p)
    sample_states = (state_ssd, state_ssd_conv, state_rwkv, state_rwkv_shift, state_gla)
    y_sample, (ss_ssd, ss_conv, ss_rwkv, ss_shift, ss_gla) = _run_group(x_sample, c_sample, sample_states, p)
    return (y_prompt, y_sample, ps_ssd, ps_conv, ps_rwkv, ps_shift, ps_gla, ss_ssd, ss_conv, ss_rwkv, ss_shift, ss_gla)
```

```python
import functools

import jax
import jax.numpy as jnp
from jax import lax
from jax.experimental import pallas as pl
from jax.experimental.pallas import tpu as pltpu

F32 = jnp.float32
BF16 = jnp.bfloat16

D_MODEL = 1024
DEPTH = 2
SSD_W = 512
SSD_H = 8
SSD_P = 64
SSD_N = 64
SSD_G = 2
SSD_CONV_W = 4
SSD_CONV_DIM = 768
RWKV_W = 256
RWKV_H = 4
RWKV_N = 64
RWKV_FEAT = 896
RWKV_GN_EPS = RWKV_N * 1e-5
GLA_W = 256
GLA_H = 4
GLA_DK = 32
GLA_DV = 64
GK_LORA = 16
GATE_NORMALIZER = 16.0
F_DENSE = 2816
N_EXPERTS = 8
ALPHA = (2.0 * DEPTH) ** 0.25
LN_EPS = 1e-5
RMS_EPS = 1e-6

LANES = 128
SMALL_W = LANES
DT_OFF = 0
GLO_OFF = 8
SSD_SLAB = SSD_W + SSD_CONV_DIM + SMALL_W
GLA_SLAB = 2 * GLA_H * GLA_DK + 2 * GLA_W + SMALL_W
IN_PAD = SSD_SLAB + RWKV_FEAT + GLA_SLAB
SSD_CHUNK = 128
GLA_CHUNK = 128
GLA_SUB = 16
RWKV_CHUNK = 64
VMEM_LIMIT = 56 * 1024 * 1024


def _cparams(*sem):
    return pltpu.CompilerParams(dimension_semantics=sem, vmem_limit_bytes=VMEM_LIMIT)


def _mm(a, b):
    return jnp.dot(a.astype(BF16), b.astype(BF16), preferred_element_type=F32)


def _mm_nt(a, b):
    return lax.dot_general(a.astype(BF16), b.astype(BF16), (((1,), (1,)), ((), ())),
                           preferred_element_type=F32)


def _mm_hi(a, b):
    return jnp.dot(a, b, preferred_element_type=F32, precision=lax.Precision.HIGHEST)


def _sigmoid(x):
    return 1.0 / (1.0 + jnp.exp(-x))


def _silu(x):
    return x * _sigmoid(x)


def _softplus(x):
    return jnp.maximum(x, 0.0) + jnp.log1p(jnp.exp(-jnp.abs(x)))


def _iota(shape, dim):
    return lax.broadcasted_iota(jnp.int32, shape, dim)


def _tri_incl(n):
    return (_iota((n, n), 1) <= _iota((n, n), 0)).astype(F32)


def _seg_ones(n, seg, scale=1.0):
    return jnp.where(_iota((n, n), 0) // seg == _iota((n, n), 1) // seg, scale, 0.0).astype(F32)


def _layernorm(u, g, b):
    mu = jnp.mean(u, axis=-1, keepdims=True)
    d = u - mu
    var = jnp.mean(d * d, axis=-1, keepdims=True)
    return d * lax.rsqrt(var + LN_EPS) * g + b


def _ada_kernel(c_ref, w_ref, b_ref, o_ref):
    o_ref[0] = _mm(_silu(c_ref[...]), w_ref[0]) + b_ref[0]


def _ada(c_all, w_ada, b_ada):
    rows = c_all.shape[0]
    tn = 1536
    return pl.pallas_call(
        _ada_kernel,
        grid=(DEPTH, 6 * D_MODEL // tn),
        in_specs=[pl.BlockSpec((rows, D_MODEL), lambda l, j: (0, 0)),
                  pl.BlockSpec((1, D_MODEL, tn), lambda l, j: (l, 0, j)),
                  pl.BlockSpec((1, 1, tn), lambda l, j: (l, 0, j))],
        out_specs=pl.BlockSpec((1, rows, tn), lambda l, j: (l, 0, j)),
        out_shape=jax.ShapeDtypeStruct((DEPTH, rows, 6 * D_MODEL), F32),
        compiler_params=_cparams("arbitrary", "arbitrary"),
        name="ada",
    )(c_all, w_ada, b_ada.reshape(DEPTH, 1, 6 * D_MODEL))


def _inproj_kernel(x_ref, sc_ref, sh_ref, w_ref, o_ssd, o_rwkv, o_gla):
    h = (x_ref[0] * (1.0 + sc_ref[0]) + sh_ref[0]).astype(BF16)
    o_ssd[0] = jnp.dot(h, w_ref[:, :SSD_SLAB], preferred_element_type=F32)
    o_rwkv[0] = jnp.dot(h, w_ref[:, SSD_SLAB:SSD_SLAB + RWKV_FEAT], preferred_element_type=F32)
    o_gla[0] = jnp.dot(h, w_ref[:, SSD_SLAB + RWKV_FEAT:], preferred_element_type=F32)


def _mod_spec(mod, tm):
    if mod.shape[1] == 1:
        return pl.BlockSpec((1, 1, D_MODEL), lambda b, i, *_: (b, 0, 0))
    return pl.BlockSpec((1, tm, D_MODEL), lambda b, i, *_: (b, i, 0))


def _inproj(x, sc, sh, w_pad, tm):
    bsz, length, _ = x.shape
    tok = lambda width: pl.BlockSpec((1, tm, width), lambda b, i: (b, i, 0))
    return pl.pallas_call(
        _inproj_kernel,
        grid=(bsz, length // tm),
        in_specs=[tok(D_MODEL), _mod_spec(sc, tm), _mod_spec(sh, tm),
                  pl.BlockSpec((D_MODEL, IN_PAD), lambda b, i: (0, 0))],
        out_specs=[tok(SSD_SLAB), tok(RWKV_FEAT), tok(GLA_SLAB)],
        out_shape=[jax.ShapeDtypeStruct((bsz, length, SSD_SLAB), F32),
                   jax.ShapeDtypeStruct((bsz, length, RWKV_FEAT), F32),
                   jax.ShapeDtypeStruct((bsz, length, GLA_SLAB), F32)],
        compiler_params=_cparams("arbitrary", "arbitrary"),
        name="inproj",
    )(x, sc, sh, w_pad)


def _ssd_kernel(slab_ref, conv0_ref, h0_ref, cw_ref, cb_ref, dtb_ref, alog_ref, dexp_ref, ng_ref,
                y_ref, conv_out_ref, h_out_ref, ext_ref, h_ref):
    c = pl.program_id(1)
    n_chunks = pl.num_programs(1)
    ch = SSD_CHUNK

    @pl.when(c == 0)
    def _():
        ext_ref[0:8, :] = jnp.zeros((8, SSD_CONV_DIM), F32)
        ext_ref[5:8, :] = conv0_ref[0]
        h_ref[...] = h0_ref[0]

    z = slab_ref[0, :, 0:SSD_W]
    xbc = slab_ref[0, :, SSD_W:SSD_W + SSD_CONV_DIM]
    small = slab_ref[0, :, SSD_W + SSD_CONV_DIM:]

    ext_ref[8:8 + ch, :] = xbc
    conv = cb_ref[...] + xbc * cw_ref[3:4, :]
    for i in range(SSD_CONV_W - 1):
        conv = conv + ext_ref[5 + i:5 + i + ch, :] * cw_ref[i:i + 1, :]
    tail_rows = ext_ref[ch:ch + 8, :]
    ext_ref[0:8, :] = tail_rows

    @pl.when(c == n_chunks - 1)
    def _():
        conv_out_ref[0] = tail_rows[5:8, :]

    act = _silu(conv)
    xs = act[:, 0:SSD_W]
    bs = act[:, SSD_W:SSD_W + SSD_G * SSD_N]
    cs = act[:, SSD_W + SSD_G * SSD_N:]

    dt = _softplus(small + dtb_ref[...])
    a_row = -jnp.exp(alog_ref[...])
    cum = _mm_hi(_tri_incl(ch), dt * a_row)
    cum_t = cum.T
    dt_t = dt.T
    causal = _iota((ch, ch), 1) <= _iota((ch, ch), 0)

    ys = []
    for g in range(SSD_G):
        b_g = bs[:, g * SSD_N:(g + 1) * SSD_N]
        c_g = cs[:, g * SSD_N:(g + 1) * SSD_N]
        cb = _mm_nt(c_g, b_g)
        for hh in range(SSD_H // SSD_G):
            h = g * (SSD_H // SSD_G) + hh
            lane = DT_OFF + h
            cum_col = cum[:, lane:lane + 1]
            cum_row = cum_t[lane:lane + 1, :]
            dt_col = dt[:, lane:lane + 1]
            dt_row = dt_t[lane:lane + 1, :]
            x_h = xs[:, h * SSD_P:(h + 1) * SSD_P]
            decay = jnp.exp(jnp.where(causal, cum_col - cum_row, -jnp.inf))
            scores = cb * decay * dt_row
            h_prev = h_ref[h]
            y_h = _mm(scores, x_h) + _mm_nt(c_g, h_prev) * jnp.exp(cum_col)
            ys.append(y_h)
            cum_last = cum[ch - 1:ch, lane:lane + 1]
            tail = jnp.exp(cum_last - cum_col) * dt_col
            upd = lax.dot_general(x_h.astype(BF16), (b_g * tail).astype(BF16),
                                  (((0,), (0,)), ((), ())), preferred_element_type=F32)
            h_ref[h] = h_prev * jnp.exp(cum_last) + upd

    y = jnp.concatenate(ys, axis=-1) + xs * dexp_ref[...]
    gated = y * _silu(z)
    ms = jnp.mean(gated * gated, axis=-1, keepdims=True)
    y_ref[0] = (gated * lax.rsqrt(ms + RMS_EPS) * ng_ref[...]).astype(y_ref.dtype)

    @pl.when(c == n_chunks - 1)
    def _():
        h_out_ref[0] = h_ref[...]


def _ssd_prompt(slab, conv0, h0, cw, cb, dtb, alog, dexp, ng):
    bsz, length, _ = slab.shape
    vec = lambda width: pl.BlockSpec((1, width), lambda b, c: (0, 0))
    return pl.pallas_call(
        _ssd_kernel,
        grid=(bsz, length // SSD_CHUNK),
        in_specs=[pl.BlockSpec((1, SSD_CHUNK, SSD_SLAB), lambda b, c: (b, c, 0)),
                  pl.BlockSpec((1, SSD_CONV_W - 1, SSD_CONV_DIM), lambda b, c: (b, 0, 0)),
                  pl.BlockSpec((1, SSD_H, SSD_P, SSD_N), lambda b, c: (b, 0, 0, 0)),
                  pl.BlockSpec((SSD_CONV_W, SSD_CONV_DIM), lambda b, c: (0, 0)),
                  vec(SSD_CONV_DIM), vec(SMALL_W), vec(SMALL_W), vec(SSD_W), vec(SSD_W)],
        out_specs=[pl.BlockSpec((1, SSD_CHUNK, SSD_W), lambda b, c: (b, c, 0)),
                   pl.BlockSpec((1, SSD_CONV_W - 1, SSD_CONV_DIM), lambda b, c: (b, 0, 0)),
                   pl.BlockSpec((1, SSD_H, SSD_P, SSD_N), lambda b, c: (b, 0, 0, 0))],
        out_shape=[jax.ShapeDtypeStruct((bsz, length, SSD_W), BF16),
                   jax.ShapeDtypeStruct((bsz, SSD_CONV_W - 1, SSD_CONV_DIM), F32),
                   jax.ShapeDtypeStruct((bsz, SSD_H, SSD_P, SSD_N), F32)],
        scratch_shapes=[pltpu.VMEM((SSD_CHUNK + 8, SSD_CONV_DIM), F32),
                        pltpu.VMEM((SSD_H, SSD_P, SSD_N), F32)],
        compiler_params=_cparams("arbitrary", "arbitrary"),
        name="ssd_scan",
    )(slab, conv0, h0, cw, cb, dtb, alog, dexp, ng)


def _rwkv_token_math(rw, prev, mu, lora_w, w0, a0, k_k, k_a):
    mix = rw + mu * (prev - rw)
    r = mix[:, 0:RWKV_W]
    k = mix[:, RWKV_W:2 * RWKV_W]
    v = mix[:, 2 * RWKV_W:3 * RWKV_W]
    lora = mix[:, 3 * RWKV_W:]
    lane = _iota(lora.shape, 1)
    act = jnp.where(lane < 32, jnp.tanh(lora), jnp.where(lane < 64, lora, _sigmoid(lora)))
    lo = _mm(act, lora_w)
    w_log = -_softplus(-(w0 + lo[:, 0:RWKV_W])) - 0.5
    logw = -jnp.exp(w_log)
    aic = _sigmoid(a0 + lo[:, RWKV_W:2 * RWKV_W])
    gate = lo[:, 2 * RWKV_W:]
    kkf = k * k_k
    ss = _mm_hi(kkf * kkf, _seg_ones(RWKV_W, RWKV_N))
    kk = kkf * lax.rsqrt(ss + 1e-12)
    k2 = k * (1.0 + (aic - 1.0) * k_a)
    return r, k2, v, logw, aic, gate, kk


def _rwkv_finish(o, r, k2, v, gate, r_k, ln_g, ln_b):
    avg = _seg_ones(RWKV_W, RWKV_N, 1.0 / RWKV_N)
    mu = _mm_hi(o, avg)
    d = o - mu
    var = _mm_hi(d * d, avg)
    on = d * lax.rsqrt(var + RWKV_GN_EPS) * ln_g + ln_b
    bonus = _mm_hi(r * k2 * r_k, _seg_ones(RWKV_W, RWKV_N)) * v
    return (on + bonus) * gate


def _rwkv_kernel(rw_ref, shift0_ref, s0_ref, mu_ref, lw_ref, w0_ref, a0_ref, kk_ref, ka_ref,
                 rk_ref, lng_ref, lnb_ref, y_ref, shift_out_ref, s_out_ref, ext_ref, s_ref):
    c = pl.program_id(1)
    n_chunks = pl.num_programs(1)
    ch = RWKV_CHUNK
    nh = RWKV_H
    w = RWKV_W

    @pl.when(c == 0)
    def _():
        ext_ref[0:8, :] = jnp.zeros((8, RWKV_FEAT), F32)
        ext_ref[7:8, :] = shift0_ref[0]
        s_ref[...] = s0_ref[0]

    rw = rw_ref[0]
    ext_ref[8:8 + ch, :] = rw
    prev = ext_ref[7:7 + ch, :]
    last_row = rw[ch - 1:ch, :]
    ext_ref[7:8, :] = last_row

    @pl.when(c == n_chunks - 1)
    def _():
        shift_out_ref[0] = last_row

    r, k2, v, logw, aic, gate, kk = _rwkv_token_math(
        rw, prev, mu_ref[...], lw_ref[...], w0_ref[...], a0_ref[...], kk_ref[...], ka_ref[...])

    cumw = _mm_hi(_tri_incl(ch), logw)
    last = cumw[ch - 1:ch, :]
    inv_g = jnp.exp(-cumw)
    a_t = -kk * jnp.exp(cumw - logw)
    b_t = kk * aic * inv_g
    k_t = k2 * inv_g
    r_t = r * jnp.exp(cumw)
    to_end = jnp.exp(last - cumw)
    b_end = kk * aic * to_end
    k_end = k2 * to_end

    n = nh * ch
    head_mask = (_iota((n, w), 0) // ch == _iota((n, w), 1) // RWKV_N).astype(F32)
    stack = lambda x: jnp.concatenate([x] * nh, axis=0) * head_mask
    a4, r4, b4, k4, v4 = stack(a_t), stack(r_t), stack(b_t), stack(k_t), stack(v)

    row = _iota((n, n), 0)
    col = _iota((n, n), 1)
    same = row // ch == col // ch
    strict = same & (col < row)
    incl = same & (col <= row)
    lhs = jnp.concatenate([a4, r4], axis=0)
    rhs = jnp.concatenate([b4, k4], axis=0)
    prod = _mm_nt(lhs, rhs)
    a_ab = jnp.where(strict, prod[0:n, 0:n], 0.0)
    a_ak = jnp.where(strict, prod[0:n, n:2 * n], 0.0)
    a_rb = jnp.where(incl, prod[n:2 * n, 0:n], 0.0)
    a_rk = jnp.where(incl, prod[n:2 * n, n:2 * n], 0.0)

    t_inv = jnp.where(row == col, 1.0, 0.0).astype(F32) + a_ab
    x = a_ab
    power = 1
    while 2 * power < ch:
        x = _mm(x, x)
        t_inv = t_inv + _mm(t_inv, x)
        power *= 2

    s_bd = s_ref[...]
    p_bd = _mm(t_inv, _mm_nt(a4, s_bd) + _mm(a_ak, v4))
    o_bd = _mm_nt(r4, s_bd) + _mm(a_rb, p_bd) + _mm(a_rk, v4)
    unstack = lambda x: sum(x[h * ch:(h + 1) * ch, :] for h in range(nh))
    o = unstack(o_bd)
    p = unstack(p_bd)

    pv_t = jnp.concatenate([p, v], axis=0).T
    upd = _mm(pv_t, jnp.concatenate([b_end, k_end], axis=0))
    bd = _iota((w, w), 0) // RWKV_N == _iota((w, w), 1) // RWKV_N
    s_ref[...] = s_bd * jnp.exp(last) + jnp.where(bd, upd, 0.0)

    y_ref[0] = _rwkv_finish(o, r, k2, v, gate, rk_ref[...], lng_ref[...], lnb_ref[...]).astype(y_ref.dtype)

    @pl.when(c == n_chunks - 1)
    def _():
        s_out_ref[0] = s_ref[...]


def _rwkv_prompt(rw, shift0, s0_bd, mu, lora_w, w0, a0, k_k, k_a, r_k, ln_g, ln_b):
    bsz, length, _ = rw.shape
    vec = lambda width: pl.BlockSpec((1, width), lambda b, c: (0, 0))
    return pl.pallas_call(
        _rwkv_kernel,
        grid=(bsz, length // RWKV_CHUNK),
        in_specs=[pl.BlockSpec((1, RWKV_CHUNK, RWKV_FEAT), lambda b, c: (b, c, 0)),
                  pl.BlockSpec((1, 1, RWKV_FEAT), lambda b, c: (b, 0, 0)),
                  pl.BlockSpec((1, RWKV_W, RWKV_W), lambda b, c: (b, 0, 0)),
                  vec(RWKV_FEAT),
                  pl.BlockSpec((LANES, 3 * RWKV_W), lambda b, c: (0, 0)),
                  vec(RWKV_W), vec(RWKV_W), vec(RWKV_W), vec(RWKV_W), vec(RWKV_W), vec(RWKV_W),
                  vec(RWKV_W)],
        out_specs=[pl.BlockSpec((1, RWKV_CHUNK, RWKV_W), lambda b, c: (b, c, 0)),
                   pl.BlockSpec((1, 1, RWKV_FEAT), lambda b, c: (b, 0, 0)),
                   pl.BlockSpec((1, RWKV_W, RWKV_W), lambda b, c: (b, 0, 0))],
        out_shape=[jax.ShapeDtypeStruct((bsz, length, RWKV_W), BF16),
                   jax.ShapeDtypeStruct((bsz, 1, RWKV_FEAT), F32),
                   jax.ShapeDtypeStruct((bsz, RWKV_W, RWKV_W), F32)],
        scratch_shapes=[pltpu.VMEM((RWKV_CHUNK + 8, RWKV_FEAT), F32),
                        pltpu.VMEM((RWKV_W, RWKV_W), F32)],
        compiler_params=_cparams("arbitrary", "arbitrary"),
        name="rwkv_scan",
    )(rw, shift0, s0_bd, mu, lora_w, w0, a0, k_k, k_a, r_k, ln_g, ln_b)


def _gla_gate_log(small, gk2_pad, b_gk):
    x = _mm_hi(small, gk2_pad) + b_gk
    return -_softplus(-x) / GATE_NORMALIZER


def _gla_finish(o, gg, norm_g):
    ms = _mm_hi(o * o, _seg_ones(GLA_W, GLA_DV, 1.0 / GLA_DV))
    return o * lax.rsqrt(ms + RMS_EPS) * norm_g * _silu(gg)


def _gla_kernel(slab_ref, s0_ref, gk2_ref, bgk_ref, ng_ref, y_ref, s_out_ref, s_ref):
    c = pl.program_id(1)
    n_chunks = pl.num_programs(1)
    ch = GLA_CHUNK
    sub = GLA_SUB
    hk = GLA_H * GLA_DK

    @pl.when(c == 0)
    def _():
        s_ref[...] = s0_ref[0]

    q = slab_ref[0, :, 0:hk] * (GLA_DK ** -0.5)
    k = slab_ref[0, :, hk:2 * hk]
    v = slab_ref[0, :, 2 * hk:2 * hk + GLA_W]
    gg = slab_ref[0, :, 2 * hk + GLA_W:2 * hk + 2 * GLA_W]
    small = slab_ref[0, :, 2 * hk + 2 * GLA_W:]

    lg = _gla_gate_log(small, gk2_ref[...], bgk_ref[...])
    cum = _mm_hi(_tri_incl(ch), lg)
    last = cum[ch - 1:ch, :]
    s_bd = s_ref[...]

    o = _mm(q * jnp.exp(cum), s_bd)

    expand = (_iota((hk, GLA_W), 0) // GLA_DK == _iota((hk, GLA_W), 1) // GLA_DV).astype(F32)
    pos = _iota((ch, 1), 0) % sub
    for delta in range(sub):
        if delta == 0:
            k_s, c_s, v_s = k, cum, v
        else:
            k_s = pltpu.roll(k, delta, 0)
            c_s = pltpu.roll(cum, delta, 0)
            v_s = pltpu.roll(v, delta, 0)
        valid = pos >= delta
        w_pair = jnp.exp(jnp.where(valid, cum - c_s, 0.0))
        prod = jnp.where(valid, q * k_s * w_pair, 0.0)
        o = o + _mm(prod, expand) * v_s

    hmask_k = (_iota((GLA_H * sub, hk), 0) // sub == _iota((GLA_H * sub, hk), 1) // GLA_DK).astype(F32)
    hmask_v = (_iota((GLA_H * sub, GLA_W), 0) // sub == _iota((GLA_H * sub, GLA_W), 1) // GLA_DV).astype(F32)
    cross = [jnp.zeros((sub, GLA_W), F32)]
    t_idx = _iota((ch, 1), 0)
    for blk in range(1, ch // sub):
        lo = blk * sub
        ref_pt = cum[lo - 1:lo, :]
        qd = q[lo:lo + sub, :] * jnp.exp(cum[lo:lo + sub, :] - ref_pt)
        kp = k * jnp.exp(jnp.where(t_idx < lo, ref_pt - cum, -jnp.inf))
        q4 = jnp.concatenate([qd] * GLA_H, axis=0) * hmask_k
        att = _mm_nt(q4, kp)
        o4 = _mm(att, v) * hmask_v
        cross.append(sum(o4[h * sub:(h + 1) * sub, :] for h in range(GLA_H)))
    o = o + jnp.concatenate(cross, axis=0)

    kt_t = (k * jnp.exp(last - cum)).T
    bd = _iota((hk, GLA_W), 0) // GLA_DK == _iota((hk, GLA_W), 1) // GLA_DV
    s_ref[...] = s_bd * _col(jnp.exp(last)) + jnp.where(bd, _mm(kt_t, v), 0.0)

    y_ref[0] = _gla_finish(o, gg, ng_ref[...]).astype(y_ref.dtype)

    @pl.when(c == n_chunks - 1)
    def _():
        s_out_ref[0] = s_ref[...]


def _gla_prompt(slab, s0_bd, gk2_pad, b_gk, ng):
    bsz, length, _ = slab.shape
    hk = GLA_H * GLA_DK
    return pl.pallas_call(
        _gla_kernel,
        grid=(bsz, length // GLA_CHUNK),
        in_specs=[pl.BlockSpec((1, GLA_CHUNK, GLA_SLAB), lambda b, c: (b, c, 0)),
                  pl.BlockSpec((1, hk, GLA_W), lambda b, c: (b, 0, 0)),
                  pl.BlockSpec((SMALL_W, hk), lambda b, c: (0, 0)),
                  pl.BlockSpec((1, hk), lambda b, c: (0, 0)),
                  pl.BlockSpec((1, GLA_W), lambda b, c: (0, 0))],
        out_specs=[pl.BlockSpec((1, GLA_CHUNK, GLA_W), lambda b, c: (b, c, 0)),
                   pl.BlockSpec((1, hk, GLA_W), lambda b, c: (b, 0, 0))],
        out_shape=[jax.ShapeDtypeStruct((bsz, length, GLA_W), BF16),
                   jax.ShapeDtypeStruct((bsz, hk, GLA_W), F32)],
        scratch_shapes=[pltpu.VMEM((hk, GLA_W), F32)],
        compiler_params=_cparams("arbitrary", "arbitrary"),
        name="gla_scan",
    )(slab, s0_bd, gk2_pad, b_gk, ng)


def _col(row):
    n = row.shape[1]
    eye = _iota((n, n), 0) == _iota((n, n), 1)
    return jnp.sum(jnp.where(eye, row, 0.0), axis=1, keepdims=True)


def _row(col):
    n = col.shape[0]
    eye = _iota((n, n), 0) == _iota((n, n), 1)
    return jnp.sum(jnp.where(eye, col, 0.0), axis=0, keepdims=True)


def _step_kernel(ssd_ref, rw_ref, gla_ref, conv0_ref, h0_ref, shift0_ref, rs0_ref, gs0_ref,
                 cw_ref, cb_ref, dtb_ref, alog_ref, dexp_ref, sng_ref,
                 mu_ref, lw_ref, w0_ref, a0_ref, kk_ref, ka_ref, rk_ref, lng_ref, lnb_ref,
                 gk2_ref, bgk_ref, gng_ref,
                 y_ssd_ref, y_rwkv_ref, y_gla_ref, conv_out_ref, h_out_ref, rs_out_ref, gs_out_ref,
                 v_ssd, v_rwkv, v_gla, o_ssd, o_rwkv, o_gla):
    bt = ssd_ref.shape[0]
    hk = GLA_H * GLA_DK

    z = ssd_ref[:, 0:SSD_W]
    xbc = ssd_ref[:, SSD_W:SSD_W + SSD_CONV_DIM]
    small = ssd_ref[:, SSD_W + SSD_CONV_DIM:]
    conv = cb_ref[...] + xbc * cw_ref[3:4, :]
    for i in range(SSD_CONV_W - 1):
        conv = conv + conv0_ref[:, i * SSD_CONV_DIM:(i + 1) * SSD_CONV_DIM] * cw_ref[i:i + 1, :]
    conv_out_ref[:, 0:2 * SSD_CONV_DIM] = conv0_ref[:, SSD_CONV_DIM:]
    conv_out_ref[:, 2 * SSD_CONV_DIM:] = xbc
    act = _silu(conv)
    xs = act[:, 0:SSD_W]
    dt = _softplus(small + dtb_ref[...])
    da = jnp.exp(dt * (-jnp.exp(alog_ref[...])))
    v_ssd[:, 0:SSD_CONV_DIM] = act
    v_ssd[:, SSD_CONV_DIM:SSD_CONV_DIM + SMALL_W] = dt
    v_ssd[:, SSD_CONV_DIM + SMALL_W:] = da

    rw = rw_ref[...]
    r, k2, v, logw, aic, gate, kk = _rwkv_token_math(
        rw, shift0_ref[...], mu_ref[...], lw_ref[...], w0_ref[...], a0_ref[...], kk_ref[...], ka_ref[...])
    for j, t in enumerate((r, jnp.exp(logw), k2, v, kk, kk * aic)):
        v_rwkv[:, j * RWKV_W:(j + 1) * RWKV_W] = t

    gq = gla_ref[:, 0:hk] * (GLA_DK ** -0.5)
    gk = gla_ref[:, hk:2 * hk]
    gv = gla_ref[:, 2 * hk:2 * hk + GLA_W]
    gg = gla_ref[:, 2 * hk + GLA_W:2 * hk + 2 * GLA_W]
    lg = _gla_gate_log(gla_ref[:, 2 * hk + 2 * GLA_W:], gk2_ref[...], bgk_ref[...])
    v_gla[:, 0:hk] = gq
    v_gla[:, hk:2 * hk] = gk
    v_gla[:, 2 * hk:3 * hk] = jnp.exp(lg)
    v_gla[:, 3 * hk:] = gv

    def body(b, carry):
        row = pl.ds(b, 1)
        t_ssd = v_ssd[row, :]
        outs = []
        for h in range(SSD_H):
            g = h // (SSD_H // SSD_G)
            x_col = _col(t_ssd[:, h * SSD_P:(h + 1) * SSD_P])
            b_row = t_ssd[:, SSD_W + g * SSD_N:SSD_W + (g + 1) * SSD_N]
            c_row = t_ssd[:, SSD_W + SSD_G * SSD_N + g * SSD_N:SSD_W + SSD_G * SSD_N + (g + 1) * SSD_N]
            dt_s = t_ssd[:, SSD_CONV_DIM + DT_OFF + h:SSD_CONV_DIM + DT_OFF + h + 1]
            da_s = t_ssd[:, SSD_CONV_DIM + SMALL_W + DT_OFF + h:SSD_CONV_DIM + SMALL_W + DT_OFF + h + 1]
            h_new = h0_ref[b, h] * da_s + (x_col * dt_s) * b_row
            h_out_ref[b, h] = h_new
            outs.append(_row(jnp.sum(h_new * c_row, axis=1, keepdims=True)))
        o_ssd[row, :] = jnp.concatenate(outs, axis=1)
        t_rwkv = v_rwkv[row, :]
        outs = []
        for h in range(RWKV_H):
            seg = lambda j: t_rwkv[:, j * RWKV_W + h * RWKV_N:j * RWKV_W + (h + 1) * RWKV_N]
            r_row, w_row, k_row, v_row, kk_row, ka_row = (seg(j) for j in range(6))
            s = rs0_ref[b, h]
            sa = jnp.sum(s * (-kk_row), axis=1, keepdims=True)
            s_new = s * w_row + sa * ka_row + _col(v_row) * k_row
            rs_out_ref[b, h] = s_new
            outs.append(_row(jnp.sum(s_new * r_row, axis=1, keepdims=True)))
        o_rwkv[row, :] = jnp.concatenate(outs, axis=1)
        t_gla = v_gla[row, :]
        outs = []
        for h in range(GLA_H):
            seg = lambda j: t_gla[:, j * hk + h * GLA_DK:j * hk + (h + 1) * GLA_DK]
            q_col, k_col, g_col = _col(seg(0)), _col(seg(1)), _col(seg(2))
            v_row = t_gla[:, 3 * hk + h * GLA_DV:3 * hk + (h + 1) * GLA_DV]
            s_new = gs0_ref[b, h] * g_col + k_col * v_row
            gs_out_ref[b, h] = s_new
            outs.append(jnp.sum(s_new * q_col, axis=0, keepdims=True))
        o_gla[row, :] = jnp.concatenate(outs, axis=1)
        return carry

    lax.fori_loop(0, bt, body, 0)

    y = o_ssd[...] + xs * dexp_ref[...]
    gated = y * _silu(z)
    ms = jnp.mean(gated * gated, axis=-1, keepdims=True)
    y_ssd_ref[...] = (gated * lax.rsqrt(ms + RMS_EPS) * sng_ref[...]).astype(y_ssd_ref.dtype)
    y_rwkv_ref[...] = _rwkv_finish(o_rwkv[...], r, k2, v, gate, rk_ref[...], lng_ref[...],
                                   lnb_ref[...]).astype(y_rwkv_ref.dtype)
    y_gla_ref[...] = _gla_finish(o_gla[...], gg, gng_ref[...]).astype(y_gla_ref.dtype)


def _mixer_step(ssd_slab, rw, gla_slab, conv0, h0, shift0, rs0, gs0, ssd_p, rwkv_p, gla_p, bt=8):
    bsz = ssd_slab.shape[0]
    hk = GLA_H * GLA_DK
    tok = lambda width: pl.BlockSpec((bt, width), lambda i: (i, 0))
    st4 = lambda a, b, c: pl.BlockSpec((bt, a, b, c), lambda i: (i, 0, 0, 0))
    full = lambda arr: pl.BlockSpec(arr.shape, lambda i: (0,) * arr.ndim)
    params = tuple(ssd_p) + tuple(rwkv_p) + tuple(gla_p)
    return pl.pallas_call(
        _step_kernel,
        grid=(bsz // bt,),
        in_specs=[tok(SSD_SLAB), tok(RWKV_FEAT), tok(GLA_SLAB),
                  tok((SSD_CONV_W - 1) * SSD_CONV_DIM),
                  st4(SSD_H, SSD_P, SSD_N), tok(RWKV_FEAT),
                  st4(RWKV_H, RWKV_N, RWKV_N), st4(GLA_H, GLA_DK, GLA_DV)]
                 + [full(p) for p in params],
        out_specs=[tok(SSD_W), tok(RWKV_W), tok(GLA_W),
                   tok((SSD_CONV_W - 1) * SSD_CONV_DIM),
                   st4(SSD_H, SSD_P, SSD_N), st4(RWKV_H, RWKV_N, RWKV_N), st4(GLA_H, GLA_DK, GLA_DV)],
        out_shape=[jax.ShapeDtypeStruct((bsz, SSD_W), BF16),
                   jax.ShapeDtypeStruct((bsz, RWKV_W), BF16),
                   jax.ShapeDtypeStruct((bsz, GLA_W), BF16),
                   jax.ShapeDtypeStruct((bsz, (SSD_CONV_W - 1) * SSD_CONV_DIM), F32),
                   jax.ShapeDtypeStruct((bsz, SSD_H, SSD_P, SSD_N), F32),
                   jax.ShapeDtypeStruct((bsz, RWKV_H, RWKV_N, RWKV_N), F32),
                   jax.ShapeDtypeStruct((bsz, GLA_H, GLA_DK, GLA_DV), F32)],
        scratch_shapes=[pltpu.VMEM((bt, SSD_CONV_DIM + 2 * SMALL_W), F32),
                        pltpu.VMEM((bt, 6 * RWKV_W), F32),
                        pltpu.VMEM((bt, 3 * hk + GLA_W), F32),
                        pltpu.VMEM((bt, SSD_W), F32),
                        pltpu.VMEM((bt, RWKV_W), F32),
                        pltpu.VMEM((bt, GLA_W), F32)],
        compiler_params=_cparams("arbitrary"),
        name="mixer_step",
    )(ssd_slab, rw, gla_slab, conv0, h0, shift0, rs0, gs0, *params)


def _outproj_kernel(ys_ref, yr_ref, yg_ref, x_ref, gt_ref, w_ref, g_ref, b_ref, o_ref):
    m = (jnp.dot(ys_ref[0], w_ref[0:SSD_W, :], preferred_element_type=F32)
         + jnp.dot(yr_ref[0], w_ref[SSD_W:SSD_W + RWKV_W, :], preferred_element_type=F32)
         + jnp.dot(yg_ref[0], w_ref[SSD_W + RWKV_W:, :], preferred_element_type=F32))
    u = ALPHA * x_ref[0] + (1.0 + gt_ref[0]) * m
    o_ref[0] = _layernorm(u, g_ref[...], b_ref[...])


def _outproj(y_ssd, y_rwkv, y_gla, x, gt, w_out, ln_g, ln_b, tm):
    bsz, length, _ = x.shape
    tok = lambda width: pl.BlockSpec((1, tm, width), lambda b, i: (b, i, 0))
    vec = pl.BlockSpec((1, D_MODEL), lambda b, i: (0, 0))
    return pl.pallas_call(
        _outproj_kernel,
        grid=(bsz, length // tm),
        in_specs=[tok(SSD_W), tok(RWKV_W), tok(GLA_W), tok(D_MODEL), _mod_spec(gt, tm),
                  pl.BlockSpec((D_MODEL, D_MODEL), lambda b, i: (0, 0)), vec, vec],
        out_specs=tok(D_MODEL),
        out_shape=jax.ShapeDtypeStruct((bsz, length, D_MODEL), F32),
        compiler_params=_cparams("arbitrary", "arbitrary"),
        name="outproj_ln",
    )(y_ssd, y_rwkv, y_gla, x, gt, w_out, ln_g, ln_b)


def _ffn_kernel(x_ref, sc_ref, sh_ref, gt_ref, wg_ref, wu_ref, wd_ref, g_ref, b_ref, o_ref, h_ref, acc_ref):
    f = pl.program_id(2)

    @pl.when(f == 0)
    def _():
        h_ref[...] = (x_ref[0] * (1.0 + sc_ref[0]) + sh_ref[0]).astype(BF16)
        acc_ref[...] = jnp.zeros_like(acc_ref)

    h = h_ref[...]
    a = _silu(jnp.dot(h, wg_ref[...], preferred_element_type=F32)) * jnp.dot(
        h, wu_ref[...], preferred_element_type=F32)
    acc_ref[...] += jnp.dot(a.astype(BF16), wd_ref[...], preferred_element_type=F32)

    @pl.when(f == pl.num_programs(2) - 1)
    def _():
        u = ALPHA * x_ref[0] + (1.0 + gt_ref[0]) * acc_ref[...]
        o_ref[0] = _layernorm(u, g_ref[...], b_ref[...])


def _ffn(x, sc, sh, gt, wg, wu, wd, ln_g, ln_b, tm, tf):
    bsz, length, _ = x.shape
    tok = pl.BlockSpec((1, tm, D_MODEL), lambda b, i, f: (b, i, 0))
    vec = pl.BlockSpec((1, D_MODEL), lambda b, i, f: (0, 0))
    return pl.pallas_call(
        _ffn_kernel,
        grid=(bsz, length // tm, F_DENSE // tf),
        in_specs=[tok, _mod_spec(sc, tm), _mod_spec(sh, tm), _mod_spec(gt, tm),
                  pl.BlockSpec((D_MODEL, tf), lambda b, i, f: (0, f)),
                  pl.BlockSpec((D_MODEL, tf), lambda b, i, f: (0, f)),
                  pl.BlockSpec((tf, D_MODEL), lambda b, i, f: (f, 0)), vec, vec],
        out_specs=tok,
        out_shape=jax.ShapeDtypeStruct((bsz, length, D_MODEL), F32),
        scratch_shapes=[pltpu.VMEM((tm, D_MODEL), BF16), pltpu.VMEM((tm, D_MODEL), F32)],
        compiler_params=_cparams("arbitrary", "arbitrary", "arbitrary"),
        name="ffn_ln",
    )(x, sc, sh, gt, wg, wu, wd, ln_g, ln_b)


def _moe_kernel(x_ref, sc_ref, sh_ref, gt_ref, rt_ref, wg_ref, wu_ref, wd_ref, g_ref, b_ref, o_ref,
                h_ref, comb_ref, acc_ref):
    e = pl.program_id(2)

    @pl.when(e == 0)
    def _():
        h = x_ref[0] * (1.0 + sc_ref[0]) + sh_ref[0]
        h_ref[...] = h.astype(BF16)
        acc_ref[...] = jnp.zeros_like(acc_ref)
        lane = _iota((h.shape[0], LANES), 1).astype(F32)
        logits = jnp.where(lane < N_EXPERTS, _mm_hi(h, rt_ref[...]), -jnp.inf)
        m1 = jnp.max(logits, axis=-1, keepdims=True)
        i1 = jnp.min(jnp.where(logits == m1, lane, float(LANES)), axis=-1, keepdims=True)
        rest = jnp.where(lane == i1, -jnp.inf, logits)
        m2 = jnp.max(rest, axis=-1, keepdims=True)
        i2 = jnp.min(jnp.where(rest == m2, lane, float(LANES)), axis=-1, keepdims=True)
        e2 = jnp.exp(m2 - m1)
        den = 1.0 + e2
        comb_ref[...] = jnp.where(lane == i1, 1.0 / den, 0.0) + jnp.where(lane == i2, e2 / den, 0.0)

    h = h_ref[...]
    a = _silu(jnp.dot(h, wg_ref[0], preferred_element_type=F32)) * jnp.dot(
        h, wu_ref[0], preferred_element_type=F32)
    out_e = jnp.dot(a.astype(BF16), wd_ref[0], preferred_element_type=F32)
    lane = _iota(comb_ref.shape, 1)
    c_e = jnp.sum(jnp.where(lane == e, comb_ref[...], 0.0), axis=-1, keepdims=True)
    acc_ref[...] += c_e * out_e

    @pl.when(e == pl.num_programs(2) - 1)
    def _():
        u = ALPHA * x_ref[0] + (1.0 + gt_ref[0]) * acc_ref[...]
        o_ref[0] = _layernorm(u, g_ref[...], b_ref[...])


def _moe(x, sc, sh, gt, router_pad, wg, wu, wd, ln_g, ln_b, tm):
    bsz, length, _ = x.shape
    tok = pl.BlockSpec((1, tm, D_MODEL), lambda b, i, e: (b, i, 0))
    vec = pl.BlockSpec((1, D_MODEL), lambda b, i, e: (0, 0))
    wspec = pl.BlockSpec((1, D_MODEL, D_MODEL), lambda b, i, e: (e, 0, 0))
    return pl.pallas_call(
        _moe_kernel,
        grid=(bsz, length // tm, N_EXPERTS),
        in_specs=[tok, _mod_spec(sc, tm), _mod_spec(sh, tm), _mod_spec(gt, tm),
                  pl.BlockSpec((D_MODEL, LANES), lambda b, i, e: (0, 0)),
                  wspec, wspec, wspec, vec, vec],
        out_specs=tok,
        out_shape=jax.ShapeDtypeStruct((bsz, length, D_MODEL), F32),
        scratch_shapes=[pltpu.VMEM((tm, D_MODEL), BF16), pltpu.VMEM((tm, LANES), F32),
                        pltpu.VMEM((tm, D_MODEL), F32)],
        compiler_params=_cparams("arbitrary", "arbitrary", "arbitrary"),
        name="moe_ln",
    )(x, sc, sh, gt, router_pad, wg, wu, wd, ln_g, ln_b)


def _pad_lanes(vec, offset, width=SMALL_W):
    out = jnp.zeros((1, width), F32)
    return out.at[0, offset:offset + vec.shape[0]].set(vec)


def _layer_params(p, l):
    w_in = p["w_in"][l]
    off = [0]
    for s in (SSD_W, SSD_CONV_DIM, SSD_H, RWKV_FEAT, GLA_H * GLA_DK, GLA_H * GLA_DK, GLA_W, GK_LORA, GLA_W):
        off.append(off[-1] + s)
    piece = lambda i: w_in[:, off[i]:off[i + 1]]
    small = jnp.zeros((D_MODEL, SMALL_W), F32)
    small = small.at[:, DT_OFF:DT_OFF + SSD_H].set(piece(2)).at[:, GLO_OFF:GLO_OFF + GK_LORA].set(piece(7))
    w_pad = jnp.concatenate([piece(0), piece(1), small, piece(3), piece(4), piece(5), piece(6), piece(8), small],
                            axis=1).astype(BF16)
    ssd_p = (p["ssd_conv_w"][l], p["ssd_conv_b"][l][None, :],
             _pad_lanes(p["ssd_dt_bias"][l], DT_OFF), _pad_lanes(p["ssd_a_log"][l], DT_OFF),
             jnp.repeat(p["ssd_d"][l], SSD_P)[None, :], p["ssd_norm_g"][l][None, :])
    lora_w = jnp.zeros((LANES, 3 * RWKV_W), F32)
    lora_w = (lora_w.at[0:32, 0:RWKV_W].set(p["rwkv_w2"][l])
              .at[32:64, RWKV_W:2 * RWKV_W].set(p["rwkv_a2"][l])
              .at[64:128, 2 * RWKV_W:].set(p["rwkv_g2"][l])).astype(BF16)
    row = lambda name: p[name][l].reshape(1, -1)
    rwkv_p = (row("rwkv_mu"), lora_w, row("rwkv_w0"), row("rwkv_a0"), row("rwkv_k_k"), row("rwkv_k_a"),
              row("rwkv_r_k"), row("rwkv_ln_g"), row("rwkv_ln_b"))
    gk2_pad = jnp.zeros((SMALL_W, GLA_H * GLA_DK), F32).at[GLO_OFF:GLO_OFF + GK_LORA].set(p["gla_w_gk2"][l])
    gla_p = (gk2_pad, row("gla_b_gk"), jnp.tile(p["gla_norm_g"][l], GLA_H)[None, :])
    return w_pad, ssd_p, rwkv_p, gla_p


def _block_diag(s):
    b, h, r, c = s.shape
    eye = jnp.eye(h, dtype=s.dtype)
    return (s[:, :, :, None, :] * eye[None, :, None, :, None]).reshape(b, h * r, h * c)


def _block_diag_inv(s_bd, h):
    b, hr, hc = s_bd.shape
    r, c = hr // h, hc // h
    s = s_bd.reshape(b, h, r, h, c)
    return jnp.stack([s[:, i, :, i, :] for i in range(h)], axis=1)


def _tail(x, mod_l, l, p, y_ssd, y_rwkv, y_gla, tm):
    sh1, sc1, gt1, sh2, sc2, gt2 = mod_l
    row = lambda name: p[name][l].reshape(1, -1)
    x = _outproj(y_ssd, y_rwkv, y_gla, x, gt1, p["w_out"][l].astype(BF16), row("ln_mix_g"), row("ln_mix_b"), tm)
    i = l // 2
    if l % 2 == 0:
        tf = F_DENSE // 2
        x = _ffn(x, sc2, sh2, gt2, p["ffn_w_gate"][i].astype(BF16), p["ffn_w_up"][i].astype(BF16),
                 p["ffn_w_down"][i].astype(BF16), row("ln_ffn_g"), row("ln_ffn_b"), tm, tf)
    else:
        router_pad = jnp.zeros((D_MODEL, LANES), F32).at[:, :N_EXPERTS].set(p["moe_router"][i])
        x = _moe(x, sc2, sh2, gt2, router_pad, p["moe_w_gate"][i].astype(BF16), p["moe_w_up"][i].astype(BF16),
                 p["moe_w_down"][i].astype(BF16), row("ln_ffn_g"), row("ln_ffn_b"), tm)
    return x


def _forward(x_prompt, x_sample, c_prompt, c_sample, states, p):
    bp, seq, _ = x_prompt.shape
    bs = x_sample.shape[0]
    state_ssd, state_conv, state_rwkv, state_shift, state_gla = states
    mod = _ada(jnp.concatenate([c_prompt, c_sample], axis=0), p["w_ada"], p["b_ada"])

    xp = x_prompt
    xs = x_sample.reshape(1, bs, D_MODEL)
    tm_p = min(512, seq)
    outs_p = [[] for _ in range(5)]
    outs_s = [[] for _ in range(5)]
    for l in range(DEPTH):
        w_pad, ssd_p, rwkv_p, gla_p = _layer_params(p, l)
        mods = jnp.split(mod[l], 6, axis=-1)
        mod_p = [m[:bp, None, :] for m in mods]
        mod_s = [m[None, bp:, :] for m in mods]

        ssd_slab, rw, gla_slab = _inproj(xp, mod_p[1], mod_p[0], w_pad, tm_p)
        y_ssd, conv_new, h_new = _ssd_prompt(
            ssd_slab, jnp.zeros((bp, SSD_CONV_W - 1, SSD_CONV_DIM), F32),
            jnp.zeros((bp, SSD_H, SSD_P, SSD_N), F32), *ssd_p)
        y_rwkv, shift_new, rs_bd = _rwkv_prompt(
            rw, jnp.zeros((bp, 1, RWKV_FEAT), F32), jnp.zeros((bp, RWKV_W, RWKV_W), F32), *rwkv_p)
        y_gla, gs_bd = _gla_prompt(gla_slab, jnp.zeros((bp, GLA_H * GLA_DK, GLA_W), F32), *gla_p)
        xp = _tail(xp, mod_p, l, p, y_ssd, y_rwkv, y_gla, tm_p)
        for acc, s in zip(outs_p, (h_new, conv_new, _block_diag_inv(rs_bd, RWKV_H),
                                   shift_new.reshape(bp, RWKV_FEAT), _block_diag_inv(gs_bd, GLA_H))):
            acc.append(s)

        ssd_slab, rw, gla_slab = _inproj(xs, mod_s[1], mod_s[0], w_pad, bs)
        y_ssd, y_rwkv, y_gla, conv_new, h_new, rs_new, gs_new = _mixer_step(
            ssd_slab[0], rw[0], gla_slab[0], state_conv[l].reshape(bs, -1), state_ssd[l], state_shift[l],
            state_rwkv[l], state_gla[l], ssd_p, rwkv_p, gla_p)
        xs = _tail(xs, mod_s, l, p, y_ssd[None], y_rwkv[None], y_gla[None], bs)
        conv_new = conv_new.reshape(bs, SSD_CONV_W - 1, SSD_CONV_DIM)
        for acc, s in zip(outs_s, (h_new, conv_new, rs_new, rw[0], gs_new)):
            acc.append(s)

    stack = lambda accs: tuple(jnp.stack(a, axis=0) for a in accs)
    return (xp, xs.reshape(bs, 1, D_MODEL)) + stack(outs_p) + stack(outs_s)


def kernel(x_prompt, x_sample, c_prompt, c_sample, state_ssd, state_ssd_conv, state_rwkv, state_rwkv_shift, state_gla, w_ada, b_ada, w_in, w_out, ssd_conv_w, ssd_conv_b, ssd_dt_bias, ssd_a_log, ssd_d, ssd_norm_g, rwkv_mu, rwkv_w0, rwkv_w2, rwkv_a0, rwkv_a2, rwkv_g2, rwkv_k_k, rwkv_k_a, rwkv_r_k, rwkv_ln_g, rwkv_ln_b, gla_w_gk2, gla_b_gk, gla_norm_g, ln_mix_g, ln_mix_b, ln_ffn_g, ln_ffn_b, ffn_w_gate, ffn_w_up, ffn_w_down, moe_router, moe_w_gate, moe_w_up, moe_w_down):
    p = dict(w_ada=w_ada, b_ada=b_ada, w_in=w_in, w_out=w_out, ssd_conv_w=ssd_conv_w, ssd_conv_b=ssd_conv_b,
             ssd_dt_bias=ssd_dt_bias, ssd_a_log=ssd_a_log, ssd_d=ssd_d, ssd_norm_g=ssd_norm_g,
             rwkv_mu=rwkv_mu, rwkv_w0=rwkv_w0, rwkv_w2=rwkv_w2, rwkv_a0=rwkv_a0, rwkv_a2=rwkv_a2,
             rwkv_g2=rwkv_g2, rwkv_k_k=rwkv_k_k, rwkv_k_a=rwkv_k_a, rwkv_r_k=rwkv_r_k,
             rwkv_ln_g=rwkv_ln_g, rwkv_ln_b=rwkv_ln_b, gla_w_gk2=gla_w_gk2, gla_b_gk=gla_b_gk,
             gla_norm_g=gla_norm_g, ln_mix_g=ln_mix_g, ln_mix_b=ln_mix_b, ln_ffn_g=ln_ffn_g,
             ln_ffn_b=ln_ffn_b, ffn_w_gate=ffn_w_gate, ffn_w_up=ffn_w_up, ffn_w_down=ffn_w_down,
             moe_router=moe_router, moe_w_gate=moe_w_gate, moe_w_up=moe_w_up, moe_w_down=moe_w_down)
    states = (state_ssd, state_ssd_conv, state_rwkv, state_rwkv_shift, state_gla)
    return _forward(x_prompt, x_sample, c_prompt, c_sample, states, p)
```

```python
import jax
import jax.numpy as jnp
from jax import lax
from jax.experimental import pallas as pl
from jax.experimental.pallas import tpu as pltpu

F32 = jnp.float32
BF16 = jnp.bfloat16

D_MODEL = 1024
DEPTH = 2
SSD_W = 512
SSD_H = 8
SSD_P = 64
SSD_N = 64
SSD_G = 2
SSD_CONV_W = 4
SSD_CONV_DIM = 768
RWKV_W = 256
RWKV_H = 4
RWKV_N = 64
RWKV_FEAT = 896
RWKV_GN_EPS = RWKV_N * 1e-5
GLA_W = 256
GLA_H = 4
GLA_DK = 32
GLA_DV = 64
GLA_HK = GLA_H * GLA_DK
GK_LORA = 16
GATE_NORMALIZER = 16.0
F_DENSE = 2816
N_EXPERTS = 8
ALPHA = (2.0 * DEPTH) ** 0.25
LN_EPS = 1e-5
RMS_EPS = 1e-6

LANES = 128
SMALL_W = LANES
DT_OFF = 0
GLO_OFF = 8
SSD_SLAB = SSD_W + SSD_CONV_DIM + SMALL_W
GLA_SLAB = 2 * GLA_HK + 2 * GLA_W + SMALL_W
IN_PAD = SSD_SLAB + RWKV_FEAT + GLA_SLAB
SSD_CHUNK = 128
GLA_CHUNK = 128
GLA_SUB = 16
RWKV_CHUNK = 64
SSD_NB, RWKV_NB, GLA_NB = 2, 4, 4
STEP_GRID = 32
VMEM_LIMIT = 56 * 1024 * 1024


def _cparams(*sem):
    return pltpu.CompilerParams(dimension_semantics=sem, vmem_limit_bytes=VMEM_LIMIT)


def _mm(a, b):
    return jnp.dot(a.astype(BF16), b.astype(BF16), preferred_element_type=F32)


def _mm_nt(a, b):
    return lax.dot_general(a.astype(BF16), b.astype(BF16), (((1,), (1,)), ((), ())),
                           preferred_element_type=F32)


def _split_bf16(x, terms):
    parts = []
    for _ in range(terms):
        p = x.astype(BF16)
        parts.append(p)
        x = x - p.astype(F32)
    return parts


def _mm_sel(sel, x, terms=3):
    return sum(jnp.dot(sel, p, preferred_element_type=F32) for p in _split_bf16(x, terms))


def _mm_xsel(x, sel, terms=2):
    return sum(jnp.dot(p, sel, preferred_element_type=F32) for p in _split_bf16(x, terms))


def _mm_hi(a, b):
    return jnp.dot(a, b, preferred_element_type=F32, precision=lax.Precision.HIGHEST)


def _sigmoid(x):
    return 1.0 / (1.0 + jnp.exp(-x))


def _silu(x):
    return x * _sigmoid(x)


def _softplus(x):
    return jnp.maximum(x, 0.0) + jnp.log1p(jnp.exp(-jnp.abs(x)))


def _iota(shape, dim):
    return lax.broadcasted_iota(jnp.int32, shape, dim)


def _layernorm(u, g, b):
    mu = jnp.mean(u, axis=-1, keepdims=True)
    d = u - mu
    var = jnp.mean(d * d, axis=-1, keepdims=True)
    return d * lax.rsqrt(var + LN_EPS) * g + b


def _tri_incl(n):
    return (jnp.arange(n)[None, :] <= jnp.arange(n)[:, None]).astype(BF16)


def _seg_ones(n, seg, scale=1.0):
    idx = jnp.arange(n) // seg
    return jnp.where(idx[:, None] == idx[None, :], scale, 0.0).astype(BF16)


def _interleave(chains):
    chains = list(chains)
    while chains:
        alive = []
        for ch in chains:
            try:
                next(ch)
                alive.append(ch)
            except StopIteration:
                pass
        chains = alive


def _const_spec(arr):
    nd = arr.ndim
    return pl.BlockSpec(arr.shape, lambda *_: (0,) * nd)


def _ada_kernel(c_ref, w_ref, b_ref, o_ref):
    o_ref[0] = _mm(_silu(c_ref[...]), w_ref[0]) + b_ref[0]


def _ada(c_all, w_ada, b_ada):
    rows = c_all.shape[0]
    tn = 1536
    return pl.pallas_call(
        _ada_kernel,
        grid=(DEPTH, 6 * D_MODEL // tn),
        in_specs=[pl.BlockSpec((rows, D_MODEL), lambda l, j: (0, 0)),
                  pl.BlockSpec((1, D_MODEL, tn), lambda l, j: (l, 0, j)),
                  pl.BlockSpec((1, 1, tn), lambda l, j: (l, 0, j))],
        out_specs=pl.BlockSpec((1, rows, tn), lambda l, j: (l, 0, j)),
        out_shape=jax.ShapeDtypeStruct((DEPTH, rows, 6 * D_MODEL), F32),
        compiler_params=_cparams("arbitrary", "arbitrary"),
        name="ada",
    )(c_all, w_ada, b_ada.reshape(DEPTH, 1, 6 * D_MODEL))


def _inproj_kernel(x_ref, sc_ref, sh_ref, w_ref, o_ssd, o_rwkv, o_gla):
    h = (x_ref[0] * (1.0 + sc_ref[0]) + sh_ref[0]).astype(BF16)
    o_ssd[0] = jnp.dot(h, w_ref[:, :SSD_SLAB], preferred_element_type=F32)
    o_rwkv[0] = jnp.dot(h, w_ref[:, SSD_SLAB:SSD_SLAB + RWKV_FEAT], preferred_element_type=F32)
    o_gla[0] = jnp.dot(h, w_ref[:, SSD_SLAB + RWKV_FEAT:], preferred_element_type=F32)


def _mod_spec(mod, tm):
    if mod.shape[1] == 1:
        return pl.BlockSpec((1, 1, D_MODEL), lambda b, i, *_: (b, 0, 0))
    return pl.BlockSpec((1, tm, D_MODEL), lambda b, i, *_: (b, i, 0))


def _inproj(x, sc, sh, w_pad, tm):
    bsz, length, _ = x.shape
    tok = lambda width: pl.BlockSpec((1, tm, width), lambda b, i: (b, i, 0))
    return pl.pallas_call(
        _inproj_kernel,
        grid=(bsz, length // tm),
        in_specs=[tok(D_MODEL), _mod_spec(sc, tm), _mod_spec(sh, tm),
                  pl.BlockSpec((D_MODEL, IN_PAD), lambda b, i: (0, 0))],
        out_specs=[tok(SSD_SLAB), tok(RWKV_FEAT), tok(GLA_SLAB)],
        out_shape=[jax.ShapeDtypeStruct((bsz, length, SSD_SLAB), F32),
                   jax.ShapeDtypeStruct((bsz, length, RWKV_FEAT), F32),
                   jax.ShapeDtypeStruct((bsz, length, GLA_SLAB), F32)],
        compiler_params=_cparams("arbitrary", "arbitrary"),
        name="inproj",
    )(x, sc, sh, w_pad)


def _ssd_token_math(xbc, small, conv_taps, cw, cb, dtb, alog):
    conv = cb + xbc * cw[3:4, :]
    for i in range(SSD_CONV_W - 1):
        conv = conv + conv_taps[i] * cw[i:i + 1, :]
    dt = _softplus(small + dtb)
    return _silu(conv), dt, dt * (-jnp.exp(alog))


def _ssd_finish(y, xs, z, dexp, ng):
    gated = (y + xs * dexp) * _silu(z)
    ms = jnp.mean(gated * gated, axis=-1, keepdims=True)
    return gated * lax.rsqrt(ms + RMS_EPS) * ng


def _ssd_kernel(slab_ref, conv0_ref, h0_ref, cw_ref, cb_ref, dtb_ref, alog_ref, dexp_ref, ng_ref, tri_ref,
                y_ref, conv_out_ref, h_out_ref, ext_ref, h_ref):
    c = pl.program_id(1)
    n_chunks = pl.num_programs(1)
    ch = SSD_CHUNK
    causal = _iota((ch, ch), 1) <= _iota((ch, ch), 0)

    nb = slab_ref.shape[0]

    @pl.when(c == 0)
    def _():
        for j in range(nb):
            ext_ref[j, 0:8, :] = jnp.zeros((8, SSD_CONV_DIM), F32)
            ext_ref[j, 5:8, :] = conv0_ref[j]
            h_ref[j] = h0_ref[j]

    def chain(j):
        z = slab_ref[j, :, 0:SSD_W]
        xbc = slab_ref[j, :, SSD_W:SSD_W + SSD_CONV_DIM]
        small = slab_ref[j, :, SSD_W + SSD_CONV_DIM:]

        ext_ref[j, 8:8 + ch, :] = xbc
        taps = [ext_ref[j, 5 + i:5 + i + ch, :] for i in range(SSD_CONV_W - 1)]
        act, dt, dta = _ssd_token_math(xbc, small, taps, cw_ref[...], cb_ref[...], dtb_ref[...], alog_ref[...])
        ext_ref[j, 0:8, :] = ext_ref[j, ch:ch + 8, :]

        xs = act[:, 0:SSD_W]
        bs = act[:, SSD_W:SSD_W + SSD_G * SSD_N]
        cs = act[:, SSD_W + SSD_G * SSD_N:]
        yield
        cum = _mm_sel(tri_ref[...], dta)
        cum_t = cum.T
        dt_t = dt.T
        yield

        ys = []
        for g in range(SSD_G):
            b_g = bs[:, g * SSD_N:(g + 1) * SSD_N]
            c_g = cs[:, g * SSD_N:(g + 1) * SSD_N]
            cb = _mm_nt(c_g, b_g)
            yield
            for hh in range(SSD_H // SSD_G):
                h = g * (SSD_H // SSD_G) + hh
                lane = DT_OFF + h
                cum_col = cum[:, lane:lane + 1]
                cum_row = cum_t[lane:lane + 1, :]
                dt_col = dt[:, lane:lane + 1]
                dt_row = dt_t[lane:lane + 1, :]
                x_h = xs[:, h * SSD_P:(h + 1) * SSD_P]
                decay = jnp.exp(jnp.where(causal, cum_col - cum_row, -jnp.inf))
                scores = cb * decay * dt_row
                h_prev = h_ref[j, h]
                ys.append(_mm(scores, x_h) + _mm_nt(c_g, h_prev) * jnp.exp(cum_col))
                cum_last = cum[ch - 1:ch, lane:lane + 1]
                tail = jnp.exp(cum_last - cum_col) * dt_col
                upd = lax.dot_general(x_h.astype(BF16), (b_g * tail).astype(BF16),
                                      (((0,), (0,)), ((), ())), preferred_element_type=F32)
                h_ref[j, h] = h_prev * jnp.exp(cum_last) + upd
                yield

        y = jnp.concatenate(ys, axis=-1)
        y_ref[j] = _ssd_finish(y, xs, z, dexp_ref[...], ng_ref[...]).astype(y_ref.dtype)

    _interleave([chain(j) for j in range(nb)])

    @pl.when(c == n_chunks - 1)
    def _():
        for j in range(nb):
            conv_out_ref[j] = ext_ref[j, 5:8, :]
            h_out_ref[j] = h_ref[j]


def _ssd_prompt(slab, conv0, h0, cw, cb, dtb, alog, dexp, ng):
    bsz, length, _ = slab.shape
    nb = SSD_NB if bsz % SSD_NB == 0 else 1
    tri = _tri_incl(SSD_CHUNK)
    consts = (cw, cb, dtb, alog, dexp, ng, tri)
    return pl.pallas_call(
        _ssd_kernel,
        grid=(bsz // nb, length // SSD_CHUNK),
        in_specs=[pl.BlockSpec((nb, SSD_CHUNK, SSD_SLAB), lambda b, c: (b, c, 0)),
                  pl.BlockSpec((nb, SSD_CONV_W - 1, SSD_CONV_DIM), lambda b, c: (b, 0, 0)),
                  pl.BlockSpec((nb, SSD_H, SSD_P, SSD_N), lambda b, c: (b, 0, 0, 0))]
                 + [_const_spec(a) for a in consts],
        out_specs=[pl.BlockSpec((nb, SSD_CHUNK, SSD_W), lambda b, c: (b, c, 0)),
                   pl.BlockSpec((nb, SSD_CONV_W - 1, SSD_CONV_DIM), lambda b, c: (b, 0, 0)),
                   pl.BlockSpec((nb, SSD_H, SSD_P, SSD_N), lambda b, c: (b, 0, 0, 0))],
        out_shape=[jax.ShapeDtypeStruct((bsz, length, SSD_W), BF16),
                   jax.ShapeDtypeStruct((bsz, SSD_CONV_W - 1, SSD_CONV_DIM), F32),
                   jax.ShapeDtypeStruct((bsz, SSD_H, SSD_P, SSD_N), F32)],
        scratch_shapes=[pltpu.VMEM((nb, SSD_CHUNK + 8, SSD_CONV_DIM), F32),
                        pltpu.VMEM((nb, SSD_H, SSD_P, SSD_N), F32)],
        compiler_params=_cparams("arbitrary", "arbitrary"),
        name="ssd_scan",
    )(slab, conv0, h0, *consts)


def _rwkv_token_math(rw, prev, mu, lora_w, w0, a0, k_k, k_a, head_ones):
    mix = rw + mu * (prev - rw)
    r = mix[:, 0:RWKV_W]
    k = mix[:, RWKV_W:2 * RWKV_W]
    v = mix[:, 2 * RWKV_W:3 * RWKV_W]
    lora = mix[:, 3 * RWKV_W:]
    lane = _iota(lora.shape, 1)
    act = jnp.where(lane < 32, jnp.tanh(lora), jnp.where(lane < 64, lora, _sigmoid(lora)))
    lo = _mm(act, lora_w)
    w_log = -_softplus(-(w0 + lo[:, 0:RWKV_W])) - 0.5
    logw = -jnp.exp(w_log)
    aic = _sigmoid(a0 + lo[:, RWKV_W:2 * RWKV_W])
    gate = lo[:, 2 * RWKV_W:]
    kkf = k * k_k
    kk = kkf * lax.rsqrt(_mm_xsel(kkf * kkf, head_ones) + 1e-12)
    k2 = k * (1.0 + (aic - 1.0) * k_a)
    return r, k2, v, logw, aic, gate, kk


def _rwkv_finish(o, r, k2, v, gate, r_k, ln_g, ln_b, head_ones, head_avg):
    mu = _mm_xsel(o, head_avg)
    d = o - mu
    var = _mm_xsel(d * d, head_avg)
    on = d * lax.rsqrt(var + RWKV_GN_EPS) * ln_g + ln_b
    bonus = _mm_xsel(r * k2 * r_k, head_ones) * v
    return (on + bonus) * gate


def _rwkv_kernel(rw_ref, shift0_ref, s0_ref, mu_ref, lw_ref, w0_ref, a0_ref, kk_ref, ka_ref,
                 rk_ref, lng_ref, lnb_ref, ones_ref, avg_ref, tri_ref, strict_bd_ref, strict_st_ref,
                 incl_st_ref, y_ref, shift_out_ref, s_out_ref, ext_ref, s_ref):
    c = pl.program_id(1)
    n_chunks = pl.num_programs(1)
    ch = RWKV_CHUNK
    nh = RWKV_H
    n = nh * ch
    head_bd = ones_ref[...]
    head_bd_f = head_bd.astype(F32)
    tile4 = lambda x: jnp.concatenate([x] * nh, axis=0)
    unstack = lambda x: sum(x[h * ch:(h + 1) * ch, :] for h in range(nh))

    nb = rw_ref.shape[0]

    @pl.when(c == 0)
    def _():
        for j in range(nb):
            ext_ref[j, 0:8, :] = jnp.zeros((8, RWKV_FEAT), F32)
            ext_ref[j, 7:8, :] = shift0_ref[j]
            s_ref[j] = s0_ref[j]

    def chain(j):
        rw = rw_ref[j]
        ext_ref[j, 8:8 + ch, :] = rw
        prev = ext_ref[j, 7:7 + ch, :]
        ext_ref[j, 7:8, :] = rw[ch - 1:ch, :]

        r, k2, v, logw, aic, gate, kk = _rwkv_token_math(
            rw, prev, mu_ref[...], lw_ref[...], w0_ref[...], a0_ref[...], kk_ref[...], ka_ref[...], head_bd)
        yield

        cumw = _mm_sel(tri_ref[...], logw)
        yield
        last = cumw[ch - 1:ch, :]
        inv_g = jnp.exp(-cumw)
        to_end = jnp.exp(last - cumw)
        kka = kk * aic
        a_t = (-kk * jnp.exp(cumw - logw)).astype(BF16)
        r_t = (r * jnp.exp(cumw)).astype(BF16)
        b_t = (kka * inv_g).astype(BF16)
        k_t = (k2 * inv_g).astype(BF16)
        bk_end = jnp.concatenate([kka * to_end, k2 * to_end], axis=0).astype(BF16)

        a4 = tile4(a_t) * head_bd
        r4 = tile4(r_t) * head_bd
        nt = lambda x, y: lax.dot_general(x, y, (((1,), (1,)), ((), ())), preferred_element_type=F32)
        a_ab = nt(a4, tile4(b_t)) * strict_bd_ref[...]
        yield
        a_ak = nt(a4, k_t) * strict_st_ref[...]
        a_rbk = nt(r4, jnp.concatenate([b_t, k_t], axis=0)) * incl_st_ref[...]
        s_bd = s_ref[j]
        uo = nt(jnp.concatenate([a4, r4], axis=0), s_bd.astype(BF16))
        vb = v.astype(BF16)
        rhs = uo[0:n, :] + jnp.dot(a_ak.astype(BF16), vb, preferred_element_type=F32) * head_bd_f
        yield

        t_inv = jnp.where(_iota((n, n), 0) == _iota((n, n), 1), 1.0, 0.0) + a_ab
        x = a_ab
        power = 1
        while 2 * power < ch:
            xb = x.astype(BF16)
            x = jnp.dot(xb, xb, preferred_element_type=F32)
            yield
            t_inv = t_inv + _mm(t_inv, x)
            yield
            power *= 2

        p = unstack(_mm(t_inv, rhs))
        yield
        pv = jnp.concatenate([p.astype(BF16), vb], axis=0)
        o = unstack(uo[n:2 * n, :] + jnp.dot(a_rbk.astype(BF16), pv, preferred_element_type=F32) * head_bd_f)
        yield

        upd = jnp.dot(pv.astype(F32).T.astype(BF16), bk_end, preferred_element_type=F32)
        s_ref[j] = s_bd * jnp.exp(last) + upd * head_bd_f
        yield

        y_ref[j] = _rwkv_finish(o, r, k2, v, gate, rk_ref[...], lng_ref[...], lnb_ref[...],
                                head_bd, avg_ref[...]).astype(y_ref.dtype)

    _interleave([chain(j) for j in range(nb)])

    @pl.when(c == n_chunks - 1)
    def _():
        for j in range(nb):
            shift_out_ref[j] = ext_ref[j, 7:8, :]
            s_out_ref[j] = s_ref[j]


def _rwkv_consts():
    ch, nh = RWKV_CHUNK, RWKV_H
    n = nh * ch
    row = jnp.arange(n)
    t_of = row % ch
    same = (row[:, None] // ch) == (row[None, :] // ch)
    strict_bd = (same & (row[None, :] < row[:, None])).astype(F32)
    strict_st = (jnp.arange(ch)[None, :] < t_of[:, None]).astype(F32)
    incl_st = (jnp.arange(2 * ch)[None, :] % ch <= t_of[:, None]).astype(F32)
    return (_seg_ones(RWKV_W, RWKV_N), _seg_ones(RWKV_W, RWKV_N, 1.0 / RWKV_N), _tri_incl(ch),
            strict_bd, strict_st, incl_st)


def _rwkv_prompt(rw, shift0, s0_bd, *params):
    bsz, length, _ = rw.shape
    nb = RWKV_NB if bsz % RWKV_NB == 0 else 1
    consts = tuple(params) + _rwkv_consts()
    return pl.pallas_call(
        _rwkv_kernel,
        grid=(bsz // nb, length // RWKV_CHUNK),
        in_specs=[pl.BlockSpec((nb, RWKV_CHUNK, RWKV_FEAT), lambda b, c: (b, c, 0)),
                  pl.BlockSpec((nb, 1, RWKV_FEAT), lambda b, c: (b, 0, 0)),
                  pl.BlockSpec((nb, RWKV_W, RWKV_W), lambda b, c: (b, 0, 0))]
                 + [_const_spec(a) for a in consts],
        out_specs=[pl.BlockSpec((nb, RWKV_CHUNK, RWKV_W), lambda b, c: (b, c, 0)),
                   pl.BlockSpec((nb, 1, RWKV_FEAT), lambda b, c: (b, 0, 0)),
                   pl.BlockSpec((nb, RWKV_W, RWKV_W), lambda b, c: (b, 0, 0))],
        out_shape=[jax.ShapeDtypeStruct((bsz, length, RWKV_W), BF16),
                   jax.ShapeDtypeStruct((bsz, 1, RWKV_FEAT), F32),
                   jax.ShapeDtypeStruct((bsz, RWKV_W, RWKV_W), F32)],
        scratch_shapes=[pltpu.VMEM((nb, RWKV_CHUNK + 8, RWKV_FEAT), F32),
                        pltpu.VMEM((nb, RWKV_W, RWKV_W), F32)],
        compiler_params=_cparams("arbitrary", "arbitrary"),
        name="rwkv_scan",
    )(rw, shift0, s0_bd, *consts)


def _gla_gate_log(small, gk2_pad, b_gk):
    x = _mm_hi(small, gk2_pad) + b_gk
    return -_softplus(-x) / GATE_NORMALIZER


def _gla_finish(o, gg, norm_g, head_avg):
    ms = _mm_xsel(o * o, head_avg)
    return o * lax.rsqrt(ms + RMS_EPS) * norm_g * _silu(gg)


def _gla_kernel(slab_ref, s0_ref, gk2_ref, bgk_ref, ng_ref, avg_ref, tri_ref, expand_ref,
                y_ref, s_out_ref, s_ref):
    c = pl.program_id(1)
    n_chunks = pl.num_programs(1)
    ch = GLA_CHUNK
    sub = GLA_SUB
    hk = GLA_HK
    expand = expand_ref[...]
    expand_f = expand.astype(F32)
    pos = _iota((ch, 1), 0) % sub
    t_idx = _iota((ch, 1), 0)
    hmask_k = (_iota((GLA_H * sub, hk), 0) // sub == _iota((GLA_H * sub, hk), 1) // GLA_DK).astype(F32)
    hmask_v = (_iota((GLA_H * sub, GLA_W), 0) // sub == _iota((GLA_H * sub, GLA_W), 1) // GLA_DV).astype(F32)

    nb = slab_ref.shape[0]

    @pl.when(c == 0)
    def _():
        s_ref[...] = s0_ref[...]

    def chain(j):
        q = slab_ref[j, :, 0:hk] * (GLA_DK ** -0.5)
        k = slab_ref[j, :, hk:2 * hk]
        v = slab_ref[j, :, 2 * hk:2 * hk + GLA_W]
        gg = slab_ref[j, :, 2 * hk + GLA_W:2 * hk + 2 * GLA_W]
        small = slab_ref[j, :, 2 * hk + 2 * GLA_W:]

        lg = _gla_gate_log(small, gk2_ref[...], bgk_ref[...])
        yield
        cum = _mm_sel(tri_ref[...], lg)
        yield
        last = cum[ch - 1:ch, :]
        s_bd = s_ref[j]

        o = _mm(q * jnp.exp(cum), s_bd)

        group = 4
        for d0 in range(0, sub, group):
            prods, v_shift = [], []
            for delta in range(d0, d0 + group):
                if delta == 0:
                    k_s, c_s, v_s = k, cum, v
                else:
                    k_s = pltpu.roll(k, delta, 0)
                    c_s = pltpu.roll(cum, delta, 0)
                    v_s = pltpu.roll(v, delta, 0)
                valid = pos >= delta
                w_pair = jnp.exp(jnp.where(valid, cum - c_s, 0.0))
                prods.append(jnp.where(valid, q * k_s * w_pair, 0.0).astype(BF16))
                v_shift.append(v_s)
            yield
            att = jnp.dot(jnp.concatenate(prods, axis=0), expand, preferred_element_type=F32)
            o = o + sum(att[g * ch:(g + 1) * ch, :] * v_shift[g] for g in range(group))
            yield

        vb = v.astype(BF16)
        atts = []
        for blk in range(1, ch // sub):
            lo = blk * sub
            ref_pt = cum[lo - 1:lo, :]
            qd = q[lo:lo + sub, :] * jnp.exp(cum[lo:lo + sub, :] - ref_pt)
            kp = k * jnp.exp(jnp.where(t_idx < lo, ref_pt - cum, -jnp.inf))
            q4 = jnp.concatenate([qd] * GLA_H, axis=0) * hmask_k
            atts.append(_mm_nt(q4, kp).astype(BF16))
            yield
        o4 = jnp.dot(jnp.concatenate(atts, axis=0), vb, preferred_element_type=F32)
        yield
        cross = [jnp.zeros((sub, GLA_W), F32)]
        for blk in range(ch // sub - 1):
            o4_b = o4[blk * GLA_H * sub:(blk + 1) * GLA_H * sub, :] * hmask_v
            cross.append(sum(o4_b[h * sub:(h + 1) * sub, :] for h in range(GLA_H)))
        o = o + jnp.concatenate(cross, axis=0)

        kt_t = (k * jnp.exp(last - cum)).T
        s_ref[j] = s_bd * _col(jnp.exp(last)) + _mm(kt_t, vb) * expand_f
        yield

        y_ref[j] = _gla_finish(o, gg, ng_ref[...], avg_ref[...]).astype(y_ref.dtype)

    _interleave([chain(j) for j in range(nb)])

    @pl.when(c == n_chunks - 1)
    def _():
        s_out_ref[...] = s_ref[...]


def _gla_prompt(slab, s0_bd, gk2_pad, b_gk, ng):
    bsz, length, _ = slab.shape
    nb = GLA_NB if bsz % GLA_NB == 0 else 1
    expand = (jnp.arange(GLA_HK)[:, None] // GLA_DK == jnp.arange(GLA_W)[None, :] // GLA_DV).astype(BF16)
    consts = (gk2_pad, b_gk, ng, _seg_ones(GLA_W, GLA_DV, 1.0 / GLA_DV), _tri_incl(GLA_CHUNK), expand)
    return pl.pallas_call(
        _gla_kernel,
        grid=(bsz // nb, length // GLA_CHUNK),
        in_specs=[pl.BlockSpec((nb, GLA_CHUNK, GLA_SLAB), lambda b, c: (b, c, 0)),
                  pl.BlockSpec((nb, GLA_HK, GLA_W), lambda b, c: (b, 0, 0))]
                 + [_const_spec(a) for a in consts],
        out_specs=[pl.BlockSpec((nb, GLA_CHUNK, GLA_W), lambda b, c: (b, c, 0)),
                   pl.BlockSpec((nb, GLA_HK, GLA_W), lambda b, c: (b, 0, 0))],
        out_shape=[jax.ShapeDtypeStruct((bsz, length, GLA_W), BF16),
                   jax.ShapeDtypeStruct((bsz, GLA_HK, GLA_W), F32)],
        scratch_shapes=[pltpu.VMEM((nb, GLA_HK, GLA_W), F32)],
        compiler_params=_cparams("arbitrary", "arbitrary"),
        name="gla_scan",
    )(slab, s0_bd, *consts)


def _col(row):
    n = row.shape[1]
    eye = _iota((n, n), 0) == _iota((n, n), 1)
    return jnp.sum(jnp.where(eye, row, 0.0), axis=1, keepdims=True)


def _store_t(dst_ref, row0, x):
    for cblk in range(x.shape[1] // LANES):
        dst_ref[row0 + cblk * LANES:row0 + (cblk + 1) * LANES, :] = x[:, cblk * LANES:(cblk + 1) * LANES].T


def _load_t(src_ref, width):
    return jnp.concatenate([src_ref[cblk * LANES:(cblk + 1) * LANES, :].T for cblk in range(width // LANES)],
                           axis=1)


_TS_X, _TS_B, _TS_C, _TS_DT, _TS_DA = 0, SSD_W, SSD_W + SSD_G * SSD_N, SSD_CONV_DIM, SSD_CONV_DIM + SMALL_W
_TR_R, _TR_W, _TR_K, _TR_V, _TR_KK, _TR_KA = (i * RWKV_W for i in range(6))
_TG_Q, _TG_K, _TG_G, _TG_V = 0, GLA_HK, 2 * GLA_HK, 3 * GLA_HK


def _step_kernel(ssd_ref, rw_ref, gla_ref, conv0_ref, shift0_ref, h0_ref, rs0_ref, gs0_ref,
                 cw_ref, cb_ref, dtb_ref, alog_ref, dexp_ref, sng_ref,
                 mu_ref, lw_ref, w0_ref, a0_ref, kk_ref, ka_ref, rk_ref, lng_ref, lnb_ref,
                 gk2_ref, bgk_ref, gng_ref, ones_ref, ravg_ref, gavg_ref,
                 y_ssd_ref, y_rwkv_ref, y_gla_ref, conv_out_ref, h_out_ref, rs_out_ref, gs_out_ref,
                 t_ssd, t_rwkv, t_gla, o_ssd_t, o_rwkv_t, o_gla_t):
    i = pl.program_id(0)
    hk = GLA_HK

    def ssd_tokens():
        xbc = ssd_ref[:, SSD_W:SSD_W + SSD_CONV_DIM]
        taps = [conv0_ref[:, t * SSD_CONV_DIM:(t + 1) * SSD_CONV_DIM] for t in range(SSD_CONV_W - 1)]
        return xbc, _ssd_token_math(xbc, ssd_ref[:, SSD_W + SSD_CONV_DIM:], taps, cw_ref[...], cb_ref[...],
                                    dtb_ref[...], alog_ref[...])

    def rwkv_tokens():
        return _rwkv_token_math(rw_ref[...], shift0_ref[...], mu_ref[...], lw_ref[...], w0_ref[...],
                                a0_ref[...], kk_ref[...], ka_ref[...], ones_ref[...])

    @pl.when(i == 0)
    def _():
        xbc, (act, dt, dta) = ssd_tokens()
        conv_out_ref[:, 0:2 * SSD_CONV_DIM] = conv0_ref[:, SSD_CONV_DIM:]
        conv_out_ref[:, 2 * SSD_CONV_DIM:] = xbc
        _store_t(t_ssd, _TS_X, act)
        _store_t(t_ssd, _TS_DT, dt)
        _store_t(t_ssd, _TS_DA, jnp.exp(dta))
        r, k2, v, logw, aic, gate, kk = rwkv_tokens()
        for off, x in ((_TR_R, r), (_TR_W, jnp.exp(logw)), (_TR_K, k2), (_TR_V, v), (_TR_KK, kk),
                       (_TR_KA, kk * aic)):
            _store_t(t_rwkv, off, x)
        lg = _gla_gate_log(gla_ref[:, 2 * hk + 2 * GLA_W:], gk2_ref[...], bgk_ref[...])
        _store_t(t_gla, _TG_Q, gla_ref[:, 0:hk] * (GLA_DK ** -0.5))
        _store_t(t_gla, _TG_K, gla_ref[:, hk:2 * hk])
        _store_t(t_gla, _TG_G, jnp.exp(lg))
        _store_t(t_gla, _TG_V, gla_ref[:, 2 * hk:2 * hk + GLA_W])
        o_gla_t[...] = jnp.zeros_like(o_gla_t)

    def row(ref, idx):
        return ref[pl.ds(idx, 1), :]

    def rows(ref, idx, count):
        return ref[pl.ds(pl.multiple_of(idx, count), count), :]

    tiles = h0_ref.shape[1] // LANES
    per_head = SSD_P * SSD_N // h0_ref.shape[1]
    h = i // per_head
    p0 = (i % per_head) * (2 * tiles)
    b_t = rows(t_ssd, _TS_B + (h // (SSD_H // SSD_G)) * SSD_N, SSD_N)
    c_t = rows(t_ssd, _TS_C + (h // (SSD_H // SSD_G)) * SSD_N, SSD_N)
    da_row = row(t_ssd, _TS_DA + DT_OFF + h)
    dt_row = row(t_ssd, _TS_DT + DT_OFF + h)
    for t in range(tiles):
        tile = h0_ref[:, t * LANES:(t + 1) * LANES].T
        new = []
        for half in range(2):
            p_idx = h * SSD_P + p0 + 2 * t + half
            h_new = (tile[half * SSD_N:(half + 1) * SSD_N, :] * da_row
                     + (row(t_ssd, _TS_X + p_idx) * dt_row) * b_t)
            o_ssd_t[pl.ds(p_idx, 1), :] = jnp.sum(h_new * c_t, axis=0, keepdims=True)
            new.append(h_new)
        h_out_ref[:, t * LANES:(t + 1) * LANES] = jnp.concatenate(new, axis=0).T

    tiles = rs0_ref.shape[1] // LANES
    per_head = RWKV_N * RWKV_N // rs0_ref.shape[1]
    h = i // per_head
    v0 = (i % per_head) * (2 * tiles)
    seg = lambda off: rows(t_rwkv, off + h * RWKV_N, RWKV_N)
    r_t, w_t, k_t, kk_t, ka_t = seg(_TR_R), seg(_TR_W), seg(_TR_K), seg(_TR_KK), seg(_TR_KA)
    for t in range(tiles):
        tile = rs0_ref[:, t * LANES:(t + 1) * LANES].T
        new = []
        for half in range(2):
            v_idx = h * RWKV_N + v0 + 2 * t + half
            s = tile[half * RWKV_N:(half + 1) * RWKV_N, :]
            sa = jnp.sum(s * (-kk_t), axis=0, keepdims=True)
            s_new = s * w_t + sa * ka_t + row(t_rwkv, _TR_V + v_idx) * k_t
            o_rwkv_t[pl.ds(v_idx, 1), :] = jnp.sum(s_new * r_t, axis=0, keepdims=True)
            new.append(s_new)
        rs_out_ref[:, t * LANES:(t + 1) * LANES] = jnp.concatenate(new, axis=0).T

    tiles = gs0_ref.shape[1] // LANES
    per_head = GLA_DK * GLA_DV // gs0_ref.shape[1]
    h = i // per_head
    k0 = (i % per_head) * (2 * tiles)
    v_t = rows(t_gla, _TG_V + h * GLA_DV, GLA_DV)
    acc = jnp.zeros((GLA_DV, LANES), F32)
    for t in range(tiles):
        tile = gs0_ref[:, t * LANES:(t + 1) * LANES].T
        new = []
        for half in range(2):
            k_idx = h * GLA_DK + k0 + 2 * t + half
            s_new = (tile[half * GLA_DV:(half + 1) * GLA_DV, :] * row(t_gla, _TG_G + k_idx)
                     + row(t_gla, _TG_K + k_idx) * v_t)
            acc = acc + row(t_gla, _TG_Q + k_idx) * s_new
            new.append(s_new)
        gs_out_ref[:, t * LANES:(t + 1) * LANES] = jnp.concatenate(new, axis=0).T
    o_rows = pl.ds(pl.multiple_of(h * GLA_DV, GLA_DV), GLA_DV)
    o_gla_t[o_rows, :] = o_gla_t[o_rows, :] + acc

    @pl.when(i == pl.num_programs(0) - 1)
    def _():
        _, (act, _, _) = ssd_tokens()
        y_ssd_ref[...] = _ssd_finish(_load_t(o_ssd_t, SSD_W), act[:, 0:SSD_W], ssd_ref[:, 0:SSD_W],
                                     dexp_ref[...], sng_ref[...]).astype(y_ssd_ref.dtype)
        r, k2, v, _, _, gate, _ = rwkv_tokens()
        y_rwkv_ref[...] = _rwkv_finish(_load_t(o_rwkv_t, RWKV_W), r, k2, v, gate, rk_ref[...], lng_ref[...],
                                       lnb_ref[...], ones_ref[...], ravg_ref[...]).astype(y_rwkv_ref.dtype)
        y_gla_ref[...] = _gla_finish(_load_t(o_gla_t, GLA_W), gla_ref[:, 2 * hk + GLA_W:2 * hk + 2 * GLA_W],
                                     gng_ref[...], gavg_ref[...]).astype(y_gla_ref.dtype)


def _mixer_step(ssd_slab, rw, gla_slab, conv0, shift0, h0, rs0, gs0, ssd_p, rwkv_p, gla_p):
    bsz = ssd_slab.shape[0]
    assert bsz == LANES, "the single-token mixer keeps exactly one vreg row of batch entries on the lanes"
    flat = lambda s: s.reshape(bsz, -1)
    h0, rs0, gs0 = flat(h0), flat(rs0), flat(gs0)
    consts = (tuple(ssd_p) + tuple(rwkv_p) + tuple(gla_p)
              + (_seg_ones(RWKV_W, RWKV_N), _seg_ones(RWKV_W, RWKV_N, 1.0 / RWKV_N),
                 _seg_ones(GLA_W, GLA_DV, 1.0 / GLA_DV)))
    full = lambda a: pl.BlockSpec(a.shape, lambda i: (0,) * a.ndim)
    cols = lambda a: pl.BlockSpec((bsz, a.shape[1] // STEP_GRID), lambda i: (0, i))
    tokens = (ssd_slab, rw, gla_slab, conv0, shift0)
    outs = pl.pallas_call(
        _step_kernel,
        grid=(STEP_GRID,),
        in_specs=[full(a) for a in tokens] + [cols(h0), cols(rs0), cols(gs0)] + [full(a) for a in consts],
        out_specs=[full(jax.ShapeDtypeStruct((bsz, w), BF16)) for w in (SSD_W, RWKV_W, GLA_W)]
                  + [full(conv0), cols(h0), cols(rs0), cols(gs0)],
        out_shape=[jax.ShapeDtypeStruct((bsz, SSD_W), BF16),
                   jax.ShapeDtypeStruct((bsz, RWKV_W), BF16),
                   jax.ShapeDtypeStruct((bsz, GLA_W), BF16),
                   jax.ShapeDtypeStruct(conv0.shape, F32),
                   jax.ShapeDtypeStruct(h0.shape, F32),
                   jax.ShapeDtypeStruct(rs0.shape, F32),
                   jax.ShapeDtypeStruct(gs0.shape, F32)],
        scratch_shapes=[pltpu.VMEM((SSD_CONV_DIM + 2 * SMALL_W, LANES), F32),
                        pltpu.VMEM((6 * RWKV_W, LANES), F32),
                        pltpu.VMEM((3 * GLA_HK + GLA_W, LANES), F32),
                        pltpu.VMEM((SSD_W, LANES), F32),
                        pltpu.VMEM((RWKV_W, LANES), F32),
                        pltpu.VMEM((GLA_W, LANES), F32)],
        compiler_params=_cparams("arbitrary"),
        name="mixer_step",
    )(*tokens, h0, rs0, gs0, *consts)
    y_ssd, y_rwkv, y_gla, conv_new, h_new, rs_new, gs_new = outs
    return (y_ssd, y_rwkv, y_gla, conv_new.reshape(bsz, SSD_CONV_W - 1, SSD_CONV_DIM),
            h_new.reshape(bsz, SSD_H, SSD_P, SSD_N), rs_new.reshape(bsz, RWKV_H, RWKV_N, RWKV_N),
            gs_new.reshape(bsz, GLA_H, GLA_DK, GLA_DV))


def _outproj_kernel(ys_ref, yr_ref, yg_ref, x_ref, gt_ref, w_ref, g_ref, b_ref, o_ref):
    m = (jnp.dot(ys_ref[0], w_ref[0:SSD_W, :], preferred_element_type=F32)
         + jnp.dot(yr_ref[0], w_ref[SSD_W:SSD_W + RWKV_W, :], preferred_element_type=F32)
         + jnp.dot(yg_ref[0], w_ref[SSD_W + RWKV_W:, :], preferred_element_type=F32))
    u = ALPHA * x_ref[0] + (1.0 + gt_ref[0]) * m
    o_ref[0] = _layernorm(u, g_ref[...], b_ref[...])


def _outproj(y_ssd, y_rwkv, y_gla, x, gt, w_out, ln_g, ln_b, tm):
    bsz, length, _ = x.shape
    tok = lambda width: pl.BlockSpec((1, tm, width), lambda b, i: (b, i, 0))
    vec = pl.BlockSpec((1, D_MODEL), lambda b, i: (0, 0))
    return pl.pallas_call(
        _outproj_kernel,
        grid=(bsz, length // tm),
        in_specs=[tok(SSD_W), tok(RWKV_W), tok(GLA_W), tok(D_MODEL), _mod_spec(gt, tm),
                  pl.BlockSpec((D_MODEL, D_MODEL), lambda b, i: (0, 0)), vec, vec],
        out_specs=tok(D_MODEL),
        out_shape=jax.ShapeDtypeStruct((bsz, length, D_MODEL), F32),
        compiler_params=_cparams("arbitrary", "arbitrary"),
        name="outproj_ln",
    )(y_ssd, y_rwkv, y_gla, x, gt, w_out, ln_g, ln_b)


def _ffn_kernel(x_ref, sc_ref, sh_ref, gt_ref, wg_ref, wu_ref, wd_ref, g_ref, b_ref, o_ref, h_ref, acc_ref):
    f = pl.program_id(2)

    @pl.when(f == 0)
    def _():
        h_ref[...] = (x_ref[0] * (1.0 + sc_ref[0]) + sh_ref[0]).astype(BF16)
        acc_ref[...] = jnp.zeros_like(acc_ref)

    h = h_ref[...]
    a = _silu(jnp.dot(h, wg_ref[...], preferred_element_type=F32)) * jnp.dot(
        h, wu_ref[...], preferred_element_type=F32)
    acc_ref[...] += jnp.dot(a.astype(BF16), wd_ref[...], preferred_element_type=F32)

    @pl.when(f == pl.num_programs(2) - 1)
    def _():
        u = ALPHA * x_ref[0] + (1.0 + gt_ref[0]) * acc_ref[...]
        o_ref[0] = _layernorm(u, g_ref[...], b_ref[...])


def _ffn(x, sc, sh, gt, wg, wu, wd, ln_g, ln_b, tm, tf):
    bsz, length, _ = x.shape
    tok = pl.BlockSpec((1, tm, D_MODEL), lambda b, i, f: (b, i, 0))
    vec = pl.BlockSpec((1, D_MODEL), lambda b, i, f: (0, 0))
    return pl.pallas_call(
        _ffn_kernel,
        grid=(bsz, length // tm, F_DENSE // tf),
        in_specs=[tok, _mod_spec(sc, tm), _mod_spec(sh, tm), _mod_spec(gt, tm),
                  pl.BlockSpec((D_MODEL, tf), lambda b, i, f: (0, f)),
                  pl.BlockSpec((D_MODEL, tf), lambda b, i, f: (0, f)),
                  pl.BlockSpec((tf, D_MODEL), lambda b, i, f: (f, 0)), vec, vec],
        out_specs=tok,
        out_shape=jax.ShapeDtypeStruct((bsz, length, D_MODEL), F32),
        scratch_shapes=[pltpu.VMEM((tm, D_MODEL), BF16), pltpu.VMEM((tm, D_MODEL), F32)],
        compiler_params=_cparams("arbitrary", "arbitrary", "arbitrary"),
        name="ffn_ln",
    )(x, sc, sh, gt, wg, wu, wd, ln_g, ln_b)


def _moe_kernel(x_ref, sc_ref, sh_ref, gt_ref, rt_ref, wg_ref, wu_ref, wd_ref, g_ref, b_ref, o_ref,
                h_ref, comb_ref, acc_ref):
    e = pl.program_id(2)

    @pl.when(e == 0)
    def _():
        h = x_ref[0] * (1.0 + sc_ref[0]) + sh_ref[0]
        h_ref[...] = h.astype(BF16)
        acc_ref[...] = jnp.zeros_like(acc_ref)
        lane = _iota((h.shape[0], LANES), 1).astype(F32)
        logits = jnp.where(lane < N_EXPERTS, _mm_hi(h, rt_ref[...]), -jnp.inf)
        m1 = jnp.max(logits, axis=-1, keepdims=True)
        i1 = jnp.min(jnp.where(logits == m1, lane, float(LANES)), axis=-1, keepdims=True)
        rest = jnp.where(lane == i1, -jnp.inf, logits)
        m2 = jnp.max(rest, axis=-1, keepdims=True)
        i2 = jnp.min(jnp.where(rest == m2, lane, float(LANES)), axis=-1, keepdims=True)
        e2 = jnp.exp(m2 - m1)
        den = 1.0 + e2
        comb_ref[...] = jnp.where(lane == i1, 1.0 / den, 0.0) + jnp.where(lane == i2, e2 / den, 0.0)

    h = h_ref[...]
    a = _silu(jnp.dot(h, wg_ref[0], preferred_element_type=F32)) * jnp.dot(
        h, wu_ref[0], preferred_element_type=F32)
    out_e = jnp.dot(a.astype(BF16), wd_ref[0], preferred_element_type=F32)
    lane = _iota(comb_ref.shape, 1)
    c_e = jnp.sum(jnp.where(lane == e, comb_ref[...], 0.0), axis=-1, keepdims=True)
    acc_ref[...] += c_e * out_e

    @pl.when(e == pl.num_programs(2) - 1)
    def _():
        u = ALPHA * x_ref[0] + (1.0 + gt_ref[0]) * acc_ref[...]
        o_ref[0] = _layernorm(u, g_ref[...], b_ref[...])


def _moe(x, sc, sh, gt, router_pad, wg, wu, wd, ln_g, ln_b, tm):
    bsz, length, _ = x.shape
    tok = pl.BlockSpec((1, tm, D_MODEL), lambda b, i, e: (b, i, 0))
    vec = pl.BlockSpec((1, D_MODEL), lambda b, i, e: (0, 0))
    wspec = pl.BlockSpec((1, D_MODEL, D_MODEL), lambda b, i, e: (e, 0, 0))
    return pl.pallas_call(
        _moe_kernel,
        grid=(bsz, length // tm, N_EXPERTS),
        in_specs=[tok, _mod_spec(sc, tm), _mod_spec(sh, tm), _mod_spec(gt, tm),
                  pl.BlockSpec((D_MODEL, LANES), lambda b, i, e: (0, 0)),
                  wspec, wspec, wspec, vec, vec],
        out_specs=tok,
        out_shape=jax.ShapeDtypeStruct((bsz, length, D_MODEL), F32),
        scratch_shapes=[pltpu.VMEM((tm, D_MODEL), BF16), pltpu.VMEM((tm, LANES), F32),
                        pltpu.VMEM((tm, D_MODEL), F32)],
        compiler_params=_cparams("arbitrary", "arbitrary", "arbitrary"),
        name="moe_ln",
    )(x, sc, sh, gt, router_pad, wg, wu, wd, ln_g, ln_b)


def _pad_lanes(vec, offset, width=SMALL_W):
    out = jnp.zeros((1, width), F32)
    return out.at[0, offset:offset + vec.shape[0]].set(vec)


def _layer_params(p, l):
    w_in = p["w_in"][l]
    off = [0]
    for s in (SSD_W, SSD_CONV_DIM, SSD_H, RWKV_FEAT, GLA_HK, GLA_HK, GLA_W, GK_LORA, GLA_W):
        off.append(off[-1] + s)
    piece = lambda i: w_in[:, off[i]:off[i + 1]]
    small = jnp.zeros((D_MODEL, SMALL_W), F32)
    small = small.at[:, DT_OFF:DT_OFF + SSD_H].set(piece(2)).at[:, GLO_OFF:GLO_OFF + GK_LORA].set(piece(7))
    w_pad = jnp.concatenate([piece(0), piece(1), small, piece(3), piece(4), piece(5), piece(6), piece(8), small],
                            axis=1).astype(BF16)
    ssd_p = (p["ssd_conv_w"][l], p["ssd_conv_b"][l][None, :],
             _pad_lanes(p["ssd_dt_bias"][l], DT_OFF), _pad_lanes(p["ssd_a_log"][l], DT_OFF),
             jnp.repeat(p["ssd_d"][l], SSD_P)[None, :], p["ssd_norm_g"][l][None, :])
    lora_w = jnp.zeros((LANES, 3 * RWKV_W), F32)
    lora_w = (lora_w.at[0:32, 0:RWKV_W].set(p["rwkv_w2"][l])
              .at[32:64, RWKV_W:2 * RWKV_W].set(p["rwkv_a2"][l])
              .at[64:128, 2 * RWKV_W:].set(p["rwkv_g2"][l])).astype(BF16)
    row = lambda name: p[name][l].reshape(1, -1)
    rwkv_p = (row("rwkv_mu"), lora_w, row("rwkv_w0"), row("rwkv_a0"), row("rwkv_k_k"), row("rwkv_k_a"),
              row("rwkv_r_k"), row("rwkv_ln_g"), row("rwkv_ln_b"))
    gk2_pad = jnp.zeros((SMALL_W, GLA_HK), F32).at[GLO_OFF:GLO_OFF + GK_LORA].set(p["gla_w_gk2"][l])
    gla_p = (gk2_pad, row("gla_b_gk"), jnp.tile(p["gla_norm_g"][l], GLA_H)[None, :])
    return w_pad, ssd_p, rwkv_p, gla_p


def _block_diag_inv(s_bd, h):
    b, hr, hc = s_bd.shape
    r, c = hr // h, hc // h
    s = s_bd.reshape(b, h, r, h, c)
    return jnp.stack([s[:, i, :, i, :] for i in range(h)], axis=1)


def _tail(x, mod_l, l, p, y_ssd, y_rwkv, y_gla, tm):
    sh1, sc1, gt1, sh2, sc2, gt2 = mod_l
    row = lambda name: p[name][l].reshape(1, -1)
    x = _outproj(y_ssd, y_rwkv, y_gla, x, gt1, p["w_out"][l].astype(BF16), row("ln_mix_g"), row("ln_mix_b"), tm)
    i = l // 2
    if l % 2 == 0:
        tf = F_DENSE // 2
        x = _ffn(x, sc2, sh2, gt2, p["ffn_w_gate"][i].astype(BF16), p["ffn_w_up"][i].astype(BF16),
                 p["ffn_w_down"][i].astype(BF16), row("ln_ffn_g"), row("ln_ffn_b"), tm, tf)
    else:
        router_pad = jnp.zeros((D_MODEL, LANES), F32).at[:, :N_EXPERTS].set(p["moe_router"][i])
        x = _moe(x, sc2, sh2, gt2, router_pad, p["moe_w_gate"][i].astype(BF16), p["moe_w_up"][i].astype(BF16),
                 p["moe_w_down"][i].astype(BF16), row("ln_ffn_g"), row("ln_ffn_b"), tm)
    return x


def _forward(x_prompt, x_sample, c_prompt, c_sample, states, p):
    bp, seq, _ = x_prompt.shape
    bs = x_sample.shape[0]
    state_ssd, state_conv, state_rwkv, state_shift, state_gla = states
    mod = _ada(jnp.concatenate([c_prompt, c_sample], axis=0), p["w_ada"], p["b_ada"])

    xp = x_prompt
    xs = x_sample.reshape(1, bs, D_MODEL)
    tm_p = min(512, seq)
    outs_p = [[] for _ in range(5)]
    outs_s = [[] for _ in range(5)]
    for l in range(DEPTH):
        w_pad, ssd_p, rwkv_p, gla_p = _layer_params(p, l)
        mods = jnp.split(mod[l], 6, axis=-1)
        mod_p = [m[:bp, None, :] for m in mods]
        mod_s = [m[None, bp:, :] for m in mods]

        ssd_slab, rw, gla_slab = _inproj(xp, mod_p[1], mod_p[0], w_pad, tm_p)
        y_ssd, conv_new, h_new = _ssd_prompt(
            ssd_slab, jnp.zeros((bp, SSD_CONV_W - 1, SSD_CONV_DIM), F32),
            jnp.zeros((bp, SSD_H, SSD_P, SSD_N), F32), *ssd_p)
        y_rwkv, shift_new, rs_bd = _rwkv_prompt(
            rw, jnp.zeros((bp, 1, RWKV_FEAT), F32), jnp.zeros((bp, RWKV_W, RWKV_W), F32), *rwkv_p)
        y_gla, gs_bd = _gla_prompt(gla_slab, jnp.zeros((bp, GLA_HK, GLA_W), F32), *gla_p)
        xp = _tail(xp, mod_p, l, p, y_ssd, y_rwkv, y_gla, tm_p)
        for acc, s in zip(outs_p, (h_new, conv_new, _block_diag_inv(rs_bd, RWKV_H),
                                   shift_new.reshape(bp, RWKV_FEAT), _block_diag_inv(gs_bd, GLA_H))):
            acc.append(s)

        ssd_slab, rw, gla_slab = _inproj(xs, mod_s[1], mod_s[0], w_pad, bs)
        y_ssd, y_rwkv, y_gla, conv_new, h_new, rs_new, gs_new = _mixer_step(
            ssd_slab[0], rw[0], gla_slab[0], state_conv[l].reshape(bs, -1), state_shift[l], state_ssd[l],
            state_rwkv[l], state_gla[l], ssd_p, rwkv_p, gla_p)
        xs = _tail(xs, mod_s, l, p, y_ssd[None], y_rwkv[None], y_gla[None], bs)
        for acc, s in zip(outs_s, (h_new, conv_new, rs_new, rw[0], gs_new)):
            acc.append(s)

    stack = lambda accs: tuple(jnp.stack(a, axis=0) for a in accs)
    return (xp, xs.reshape(bs, 1, D_MODEL)) + stack(outs_p) + stack(outs_s)


def kernel(x_prompt, x_sample, c_prompt, c_sample, state_ssd, state_ssd_conv, state_rwkv, state_rwkv_shift, state_gla, w_ada, b_ada, w_in, w_out, ssd_conv_w, ssd_conv_b, ssd_dt_bias, ssd_a_log, ssd_d, ssd_norm_g, rwkv_mu, rwkv_w0, rwkv_w2, rwkv_a0, rwkv_a2, rwkv_g2, rwkv_k_k, rwkv_k_a, rwkv_r_k, rwkv_ln_g, rwkv_ln_b, gla_w_gk2, gla_b_gk, gla_norm_g, ln_mix_g, ln_mix_b, ln_ffn_g, ln_ffn_b, ffn_w_gate, ffn_w_up, ffn_w_down, moe_router, moe_w_gate, moe_w_up, moe_w_down):
    p = dict(w_ada=w_ada, b_ada=b_ada, w_in=w_in, w_out=w_out, ssd_conv_w=ssd_conv_w, ssd_conv_b=ssd_conv_b,
             ssd_dt_bias=ssd_dt_bias, ssd_a_log=ssd_a_log, ssd_d=ssd_d, ssd_norm_g=ssd_norm_g,
             rwkv_mu=rwkv_mu, rwkv_w0=rwkv_w0, rwkv_w2=rwkv_w2, rwkv_a0=rwkv_a0, rwkv_a2=rwkv_a2,
             rwkv_g2=rwkv_g2, rwkv_k_k=rwkv_k_k, rwkv_k_a=rwkv_k_a, rwkv_r_k=rwkv_r_k,
             rwkv_ln_g=rwkv_ln_g, rwkv_ln_b=rwkv_ln_b, gla_w_gk2=gla_w_gk2, gla_b_gk=gla_b_gk,
             gla_norm_g=gla_norm_g, ln_mix_g=ln_mix_g, ln_mix_b=ln_mix_b, ln_ffn_g=ln_ffn_g,
             ln_ffn_b=ln_ffn_b, ffn_w_gate=ffn_w_gate, ffn_w_up=ffn_w_up, ffn_w_down=ffn_w_down,
             moe_router=moe_router, moe_w_gate=moe_w_gate, moe_w_up=moe_w_up, moe_w_down=moe_w_down)
    states = (state_ssd, state_ssd_conv, state_rwkv, state_rwkv_shift, state_gla)
    return _forward(x_prompt, x_sample, c_prompt, c_sample, states, p)
```

```python
import functools

import jax
import jax.numpy as jnp
from jax import lax
from jax.experimental import pallas as pl
from jax.experimental.pallas import tpu as pltpu

F32 = jnp.float32
BF16 = jnp.bfloat16

D_MODEL = 1024
DEPTH = 2
SSD_W = 512
SSD_H = 8
SSD_P = 64
SSD_N = 64
SSD_G = 2
SSD_CONV_W = 4
SSD_CONV_DIM = 768
RWKV_W = 256
RWKV_H = 4
RWKV_N = 64
RWKV_FEAT = 896
RWKV_GN_EPS = RWKV_N * 1e-5
GLA_W = 256
GLA_H = 4
GLA_DK = 32
GLA_DV = 64
GLA_HK = GLA_H * GLA_DK
GK_LORA = 16
GATE_NORMALIZER = 16.0
F_DENSE = 2816
N_EXPERTS = 8
ALPHA = (2.0 * DEPTH) ** 0.25
LN_EPS = 1e-5
RMS_EPS = 1e-6

LANES = 128
SMALL_W = LANES
DT_OFF = 0
GLO_OFF = 8
SSD_SLAB = SSD_W + SSD_CONV_DIM + SMALL_W
GLA_SLAB = 2 * GLA_HK + 2 * GLA_W + SMALL_W
IN_PAD = SSD_SLAB + RWKV_FEAT + GLA_SLAB
SSD_CHUNK = 128
GLA_CHUNK = 128
GLA_SUB = 16
RWKV_CHUNK = 64
SSD_NB, RWKV_NB, GLA_NB = 2, 8, 4
RWKV_SKEW = 1
MOE_CAP_ROWS = 192
STEP_GRID = 32
VMEM_LIMIT = 56 * 1024 * 1024


def _cparams(*sem):
    return pltpu.CompilerParams(dimension_semantics=sem, vmem_limit_bytes=VMEM_LIMIT)


def _mm(a, b):
    return jnp.dot(a.astype(BF16), b.astype(BF16), preferred_element_type=F32)


def _mm_nt(a, b):
    return lax.dot_general(a.astype(BF16), b.astype(BF16), (((1,), (1,)), ((), ())),
                           preferred_element_type=F32)


def _split_bf16(x, terms):
    parts = []
    for _ in range(terms):
        p = x.astype(BF16)
        parts.append(p)
        x = x - p.astype(F32)
    return parts


def _mm_sel(sel, x, terms=3):
    return sum(jnp.dot(sel, p, preferred_element_type=F32) for p in _split_bf16(x, terms))


def _mm_xsel(x, sel, terms=2):
    return sum(jnp.dot(p, sel, preferred_element_type=F32) for p in _split_bf16(x, terms))


def _mm_hi(a, b):
    return jnp.dot(a, b, preferred_element_type=F32, precision=lax.Precision.HIGHEST)


def _sigmoid(x):
    return 1.0 / (1.0 + jnp.exp(-x))


def _silu(x):
    return x * _sigmoid(x)


def _softplus(x):
    return jnp.maximum(x, 0.0) + jnp.log1p(jnp.exp(-jnp.abs(x)))


def _iota(shape, dim):
    return lax.broadcasted_iota(jnp.int32, shape, dim)


def _layernorm(u, g, b):
    mu = jnp.mean(u, axis=-1, keepdims=True)
    d = u - mu
    var = jnp.mean(d * d, axis=-1, keepdims=True)
    return d * lax.rsqrt(var + LN_EPS) * g + b


def _tri_incl(n):
    return (jnp.arange(n)[None, :] <= jnp.arange(n)[:, None]).astype(BF16)


def _seg_ones(n, seg, scale=1.0):
    idx = jnp.arange(n) // seg
    return jnp.where(idx[:, None] == idx[None, :], scale, 0.0).astype(BF16)


def _interleave(chains, skew=0):
    chains = list(chains)
    done = [False] * len(chains)
    rnd = 0
    while not all(done):
        for i, ch in enumerate(chains):
            if done[i] or rnd < i * skew:
                continue
            try:
                next(ch)
            except StopIteration:
                done[i] = True
        rnd += 1


def _const_spec(arr):
    nd = arr.ndim
    return pl.BlockSpec(arr.shape, lambda *_: (0,) * nd)


def _ada_kernel(c_ref, w_ref, b_ref, o_ref):
    o_ref[0] = _mm(_silu(c_ref[...]), w_ref[0]) + b_ref[0]


def _ada(c_all, w_ada, b_ada):
    rows = c_all.shape[0]
    tn = 1536
    return pl.pallas_call(
        _ada_kernel,
        grid=(DEPTH, 6 * D_MODEL // tn),
        in_specs=[pl.BlockSpec((rows, D_MODEL), lambda l, j: (0, 0)),
                  pl.BlockSpec((1, D_MODEL, tn), lambda l, j: (l, 0, j)),
                  pl.BlockSpec((1, 1, tn), lambda l, j: (l, 0, j))],
        out_specs=pl.BlockSpec((1, rows, tn), lambda l, j: (l, 0, j)),
        out_shape=jax.ShapeDtypeStruct((DEPTH, rows, 6 * D_MODEL), F32),
        compiler_params=_cparams("arbitrary", "arbitrary"),
        name="ada",
    )(c_all, w_ada, b_ada.reshape(DEPTH, 1, 6 * D_MODEL))


def _inproj_kernel(x_ref, sc_ref, sh_ref, w_ref, o_ssd, o_rwkv, o_gla):
    h = (x_ref[0] * (1.0 + sc_ref[0]) + sh_ref[0]).astype(BF16)
    o_ssd[0] = jnp.dot(h, w_ref[:, :SSD_SLAB], preferred_element_type=F32)
    o_rwkv[0] = jnp.dot(h, w_ref[:, SSD_SLAB:SSD_SLAB + RWKV_FEAT], preferred_element_type=F32)
    o_gla[0] = jnp.dot(h, w_ref[:, SSD_SLAB + RWKV_FEAT:], preferred_element_type=F32)


def _mod_spec(mod, tm):
    if mod.shape[1] == 1:
        return pl.BlockSpec((1, 1, D_MODEL), lambda b, i, *_: (b, 0, 0))
    return pl.BlockSpec((1, tm, D_MODEL), lambda b, i, *_: (b, i, 0))


def _inproj(x, sc, sh, w_pad, tm):
    bsz, length, _ = x.shape
    tok = lambda width: pl.BlockSpec((1, tm, width), lambda b, i: (b, i, 0))
    return pl.pallas_call(
        _inproj_kernel,
        grid=(bsz, length // tm),
        in_specs=[tok(D_MODEL), _mod_spec(sc, tm), _mod_spec(sh, tm),
                  pl.BlockSpec((D_MODEL, IN_PAD), lambda b, i: (0, 0))],
        out_specs=[tok(SSD_SLAB), tok(RWKV_FEAT), tok(GLA_SLAB)],
        out_shape=[jax.ShapeDtypeStruct((bsz, length, SSD_SLAB), F32),
                   jax.ShapeDtypeStruct((bsz, length, RWKV_FEAT), F32),
                   jax.ShapeDtypeStruct((bsz, length, GLA_SLAB), F32)],
        compiler_params=_cparams("arbitrary", "arbitrary"),
        name="inproj",
    )(x, sc, sh, w_pad)


def _ssd_token_math(xbc, small, conv_taps, cw, cb, dtb, alog):
    conv = cb + xbc * cw[3:4, :]
    for i in range(SSD_CONV_W - 1):
        conv = conv + conv_taps[i] * cw[i:i + 1, :]
    dt = _softplus(small + dtb)
    return _silu(conv), dt, dt * (-jnp.exp(alog))


def _ssd_finish(y, xs, z, dexp, ng):
    gated = (y + xs * dexp) * _silu(z)
    ms = jnp.mean(gated * gated, axis=-1, keepdims=True)
    return gated * lax.rsqrt(ms + RMS_EPS) * ng


def _ssd_kernel(slab_ref, conv0_ref, h0_ref, cw_ref, cb_ref, dtb_ref, alog_ref, dexp_ref, ng_ref, tri_ref,
                y_ref, conv_out_ref, h_out_ref, ext_ref, h_ref):
    c = pl.program_id(1)
    n_chunks = pl.num_programs(1)
    ch = SSD_CHUNK
    causal = _iota((ch, ch), 1) <= _iota((ch, ch), 0)

    nb = slab_ref.shape[0]

    @pl.when(c == 0)
    def _():
        for j in range(nb):
            ext_ref[j, 0:8, :] = jnp.zeros((8, SSD_CONV_DIM), F32)
            ext_ref[j, 5:8, :] = conv0_ref[j]
            h_ref[j] = h0_ref[j]

    def chain(j):
        z = slab_ref[j, :, 0:SSD_W]
        xbc = slab_ref[j, :, SSD_W:SSD_W + SSD_CONV_DIM]
        small = slab_ref[j, :, SSD_W + SSD_CONV_DIM:]

        ext_ref[j, 8:8 + ch, :] = xbc
        taps = [ext_ref[j, 5 + i:5 + i + ch, :] for i in range(SSD_CONV_W - 1)]
        act, dt, dta = _ssd_token_math(xbc, small, taps, cw_ref[...], cb_ref[...], dtb_ref[...], alog_ref[...])
        ext_ref[j, 0:8, :] = ext_ref[j, ch:ch + 8, :]

        xs = act[:, 0:SSD_W]
        bs = act[:, SSD_W:SSD_W + SSD_G * SSD_N]
        cs = act[:, SSD_W + SSD_G * SSD_N:]
        yield
        cum = _mm_sel(tri_ref[...], dta)
        cum_t = cum.T
        dt_t = dt.T
        yield

        ys = []
        for g in range(SSD_G):
            b_g = bs[:, g * SSD_N:(g + 1) * SSD_N]
            c_g = cs[:, g * SSD_N:(g + 1) * SSD_N]
            cb = _mm_nt(c_g, b_g)
            yield
            for hh in range(SSD_H // SSD_G):
                h = g * (SSD_H // SSD_G) + hh
                lane = DT_OFF + h
                cum_col = cum[:, lane:lane + 1]
                cum_row = cum_t[lane:lane + 1, :]
                dt_col = dt[:, lane:lane + 1]
                dt_row = dt_t[lane:lane + 1, :]
                x_h = xs[:, h * SSD_P:(h + 1) * SSD_P]
                decay = jnp.exp(jnp.where(causal, cum_col - cum_row, -jnp.inf))
                scores = cb * decay * dt_row
                h_prev = h_ref[j, h]
                ys.append(_mm(scores, x_h) + _mm_nt(c_g, h_prev) * jnp.exp(cum_col))
                cum_last = cum[ch - 1:ch, lane:lane + 1]
                tail = jnp.exp(cum_last - cum_col) * dt_col
                upd = lax.dot_general(x_h.astype(BF16), (b_g * tail).astype(BF16),
                                      (((0,), (0,)), ((), ())), preferred_element_type=F32)
                h_ref[j, h] = h_prev * jnp.exp(cum_last) + upd
                yield

        y = jnp.concatenate(ys, axis=-1)
        y_ref[j] = _ssd_finish(y, xs, z, dexp_ref[...], ng_ref[...]).astype(y_ref.dtype)

    _interleave([chain(j) for j in range(nb)])

    @pl.when(c == n_chunks - 1)
    def _():
        for j in range(nb):
            conv_out_ref[j] = ext_ref[j, 5:8, :]
            h_out_ref[j] = h_ref[j]


def _ssd_prompt(slab, conv0, h0, cw, cb, dtb, alog, dexp, ng):
    bsz, length, _ = slab.shape
    nb = SSD_NB if bsz % SSD_NB == 0 else 1
    tri = _tri_incl(SSD_CHUNK)
    consts = (cw, cb, dtb, alog, dexp, ng, tri)
    return pl.pallas_call(
        _ssd_kernel,
        grid=(bsz // nb, length // SSD_CHUNK),
        in_specs=[pl.BlockSpec((nb, SSD_CHUNK, SSD_SLAB), lambda b, c: (b, c, 0)),
                  pl.BlockSpec((nb, SSD_CONV_W - 1, SSD_CONV_DIM), lambda b, c: (b, 0, 0)),
                  pl.BlockSpec((nb, SSD_H, SSD_P, SSD_N), lambda b, c: (b, 0, 0, 0))]
                 + [_const_spec(a) for a in consts],
        out_specs=[pl.BlockSpec((nb, SSD_CHUNK, SSD_W), lambda b, c: (b, c, 0)),
                   pl.BlockSpec((nb, SSD_CONV_W - 1, SSD_CONV_DIM), lambda b, c: (b, 0, 0)),
                   pl.BlockSpec((nb, SSD_H, SSD_P, SSD_N), lambda b, c: (b, 0, 0, 0))],
        out_shape=[jax.ShapeDtypeStruct((bsz, length, SSD_W), BF16),
                   jax.ShapeDtypeStruct((bsz, SSD_CONV_W - 1, SSD_CONV_DIM), F32),
                   jax.ShapeDtypeStruct((bsz, SSD_H, SSD_P, SSD_N), F32)],
        scratch_shapes=[pltpu.VMEM((nb, SSD_CHUNK + 8, SSD_CONV_DIM), F32),
                        pltpu.VMEM((nb, SSD_H, SSD_P, SSD_N), F32)],
        compiler_params=_cparams("arbitrary", "arbitrary"),
        name="ssd_scan",
    )(slab, conv0, h0, *consts)


def _rwkv_token_math(rw, prev, mu, lora_w, w0, a0, k_k, k_a, head_ones):
    mix = rw + mu * (prev - rw)
    r = mix[:, 0:RWKV_W]
    k = mix[:, RWKV_W:2 * RWKV_W]
    v = mix[:, 2 * RWKV_W:3 * RWKV_W]
    lora = mix[:, 3 * RWKV_W:]
    lane = _iota(lora.shape, 1)
    act = jnp.where(lane < 32, jnp.tanh(lora), jnp.where(lane < 64, lora, _sigmoid(lora)))
    lo = _mm(act, lora_w)
    w_log = -_softplus(-(w0 + lo[:, 0:RWKV_W])) - 0.5
    logw = -jnp.exp(w_log)
    aic = _sigmoid(a0 + lo[:, RWKV_W:2 * RWKV_W])
    gate = lo[:, 2 * RWKV_W:]
    kkf = k * k_k
    kk = kkf * lax.rsqrt(_mm_xsel(kkf * kkf, head_ones) + 1e-12)
    k2 = k * (1.0 + (aic - 1.0) * k_a)
    return r, k2, v, logw, aic, gate, kk


def _rwkv_finish(o, r, k2, v, gate, r_k, ln_g, ln_b, head_ones, head_avg):
    mu = _mm_xsel(o, head_avg)
    d = o - mu
    var = _mm_xsel(d * d, head_avg)
    on = d * lax.rsqrt(var + RWKV_GN_EPS) * ln_g + ln_b
    bonus = _mm_xsel(r * k2 * r_k, head_ones) * v
    return (on + bonus) * gate


def _rwkv_kernel(rw_ref, shift0_ref, s0_ref, mu_ref, lw_ref, w0_ref, a0_ref, kk_ref, ka_ref,
                 rk_ref, lng_ref, lnb_ref, ones_ref, avg_ref, tri_ref, strict_ref, incl_ref, eye_ref,
                 y_ref, shift_out_ref, s_out_ref, ext_ref, s_ref):
    c = pl.program_id(1)
    n_chunks = pl.num_programs(1)
    ch = RWKV_CHUNK
    nh = RWKV_H
    w = RWKV_W
    head_bd = ones_ref[...]
    head_bd_f = head_bd.astype(F32)
    tile4 = lambda x: jnp.concatenate([x] * nh, axis=0)

    nb = rw_ref.shape[0]

    @pl.when(c == 0)
    def _():
        for j in range(nb):
            ext_ref[j, 0:8, :] = jnp.zeros((8, RWKV_FEAT), F32)
            ext_ref[j, 7:8, :] = shift0_ref[j]
            s_ref[j] = s0_ref[j]

    def chain(j):
        rw = rw_ref[j]
        ext_ref[j, 8:8 + ch, :] = rw
        prev = ext_ref[j, 7:7 + ch, :]
        ext_ref[j, 7:8, :] = rw[ch - 1:ch, :]

        r, k2, v, logw, aic, gate, kk = _rwkv_token_math(
            rw, prev, mu_ref[...], lw_ref[...], w0_ref[...], a0_ref[...], kk_ref[...], ka_ref[...], head_bd)
        yield

        cumw = _mm_sel(tri_ref[...], logw)
        yield
        last = cumw[ch - 1:ch, :]
        inv_g = jnp.exp(-cumw)
        to_end = jnp.exp(last - cumw)
        kka = kk * aic
        a_t = (-kk * jnp.exp(cumw - logw)).astype(BF16)
        r_t = (r * jnp.exp(cumw)).astype(BF16)
        b_t = (kka * inv_g).astype(BF16)
        k_t = (k2 * inv_g).astype(BF16)
        bk_end = jnp.concatenate([kka * to_end, k2 * to_end], axis=0).astype(BF16)

        bd = lambda x: tile4(x.astype(BF16)) * head_bd
        nt = lambda x, y: lax.dot_general(x, y, (((1,), (1,)), ((), ())), preferred_element_type=F32)
        vb = v.astype(BF16)
        ar = jnp.concatenate([a_t, r_t], axis=0)
        prod = nt(ar, jnp.concatenate([bd(b_t), bd(k_t)], axis=0))
        a_ab = prod[0:ch, 0:w] * strict_ref[...]
        a_ak = prod[0:ch, w:2 * w] * strict_ref[...]
        a_rbk = (prod[ch:2 * ch, :] * incl_ref[...]).astype(BF16)
        yield
        s_bd = s_ref[j]
        uo = nt(ar, s_bd.astype(BF16))
        rhs = uo[0:ch, :] + jnp.dot(a_ak.astype(BF16), bd(vb), preferred_element_type=F32)
        yield

        t_inv = eye_ref[...] + a_ab
        x = a_ab
        x_bd = bd(x)
        power = 1
        while 2 * power < ch:
            x = jnp.dot(x.astype(BF16), x_bd, preferred_element_type=F32)
            x_bd = bd(x)
            yield
            t_inv = t_inv + jnp.dot(t_inv.astype(BF16), x_bd, preferred_element_type=F32)
            yield
            power *= 2

        p = jnp.dot(t_inv.astype(BF16), bd(rhs), preferred_element_type=F32)
        yield
        pb = p.astype(BF16)
        o = uo[ch:2 * ch, :] + jnp.dot(a_rbk, jnp.concatenate([bd(pb), bd(vb)], axis=0),
                                       preferred_element_type=F32)
        pv = jnp.concatenate([pb, vb], axis=0)
        yield

        upd = jnp.dot(pv.astype(F32).T.astype(BF16), bk_end, preferred_element_type=F32)
        s_ref[j] = s_bd * jnp.exp(last) + upd * head_bd_f
        yield

        y_ref[j] = _rwkv_finish(o, r, k2, v, gate, rk_ref[...], lng_ref[...], lnb_ref[...],
                                head_bd, avg_ref[...]).astype(y_ref.dtype)

    _interleave([chain(j) for j in range(nb)], skew=RWKV_SKEW)

    @pl.when(c == n_chunks - 1)
    def _():
        for j in range(nb):
            shift_out_ref[j] = ext_ref[j, 7:8, :]
            s_out_ref[j] = s_ref[j]


def _rwkv_consts():
    ch = RWKV_CHUNK
    assert ch == RWKV_N, "bd() reuses the head mask, which needs chunk rows == head width"
    t = jnp.arange(ch)[:, None]
    j = jnp.arange(2 * RWKV_W)[None, :] % ch
    strict = (j[:, :RWKV_W] < t).astype(F32)
    incl = (j <= t).astype(F32)
    eye = (j[:, :RWKV_W] == t).astype(F32)
    return (_seg_ones(RWKV_W, RWKV_N), _seg_ones(RWKV_W, RWKV_N, 1.0 / RWKV_N), _tri_incl(ch),
            strict, incl, eye)


def _rwkv_prompt(rw, shift0, s0_bd, *params):
    bsz, length, _ = rw.shape
    nb = RWKV_NB if bsz % RWKV_NB == 0 else 1
    consts = tuple(params) + _rwkv_consts()
    return pl.pallas_call(
        _rwkv_kernel,
        grid=(bsz // nb, length // RWKV_CHUNK),
        in_specs=[pl.BlockSpec((nb, RWKV_CHUNK, RWKV_FEAT), lambda b, c: (b, c, 0)),
                  pl.BlockSpec((nb, 1, RWKV_FEAT), lambda b, c: (b, 0, 0)),
                  pl.BlockSpec((nb, RWKV_W, RWKV_W), lambda b, c: (b, 0, 0))]
                 + [_const_spec(a) for a in consts],
        out_specs=[pl.BlockSpec((nb, RWKV_CHUNK, RWKV_W), lambda b, c: (b, c, 0)),
                   pl.BlockSpec((nb, 1, RWKV_FEAT), lambda b, c: (b, 0, 0)),
                   pl.BlockSpec((nb, RWKV_W, RWKV_W), lambda b, c: (b, 0, 0))],
        out_shape=[jax.ShapeDtypeStruct((bsz, length, RWKV_W), BF16),
                   jax.ShapeDtypeStruct((bsz, 1, RWKV_FEAT), F32),
                   jax.ShapeDtypeStruct((bsz, RWKV_W, RWKV_W), F32)],
        scratch_shapes=[pltpu.VMEM((nb, RWKV_CHUNK + 8, RWKV_FEAT), F32),
                        pltpu.VMEM((nb, RWKV_W, RWKV_W), F32)],
        compiler_params=_cparams("arbitrary", "arbitrary"),
        name="rwkv_scan",
    )(rw, shift0, s0_bd, *consts)


def _gla_gate_log(small, gk2_pad, b_gk):
    x = _mm_hi(small, gk2_pad) + b_gk
    return -_softplus(-x) / GATE_NORMALIZER


def _gla_finish(o, gg, norm_g, head_avg):
    ms = _mm_xsel(o * o, head_avg)
    return o * lax.rsqrt(ms + RMS_EPS) * norm_g * _silu(gg)


def _gla_kernel(slab_ref, s0_ref, gk2_ref, bgk_ref, ng_ref, avg_ref, tri_ref, expand_ref,
                y_ref, s_out_ref, s_ref):
    c = pl.program_id(1)
    n_chunks = pl.num_programs(1)
    ch = GLA_CHUNK
    sub = GLA_SUB
    hk = GLA_HK
    expand = expand_ref[...]
    expand_f = expand.astype(F32)
    pos = _iota((ch, 1), 0) % sub
    t_idx = _iota((ch, 1), 0)
    hmask_k = (_iota((GLA_H * sub, hk), 0) // sub == _iota((GLA_H * sub, hk), 1) // GLA_DK).astype(F32)
    hmask_v = (_iota((GLA_H * sub, GLA_W), 0) // sub == _iota((GLA_H * sub, GLA_W), 1) // GLA_DV).astype(F32)

    nb = slab_ref.shape[0]

    @pl.when(c == 0)
    def _():
        s_ref[...] = s0_ref[...]

    def chain(j):
        q = slab_ref[j, :, 0:hk] * (GLA_DK ** -0.5)
        k = slab_ref[j, :, hk:2 * hk]
        v = slab_ref[j, :, 2 * hk:2 * hk + GLA_W]
        gg = slab_ref[j, :, 2 * hk + GLA_W:2 * hk + 2 * GLA_W]
        small = slab_ref[j, :, 2 * hk + 2 * GLA_W:]

        lg = _gla_gate_log(small, gk2_ref[...], bgk_ref[...])
        yield
        cum = _mm_sel(tri_ref[...], lg)
        yield
        last = cum[ch - 1:ch, :]
        s_bd = s_ref[j]

        o = _mm(q * jnp.exp(cum), s_bd)

        group = 4
        for d0 in range(0, sub, group):
            prods, v_shift = [], []
            for delta in range(d0, d0 + group):
                if delta == 0:
                    k_s, c_s, v_s = k, cum, v
                else:
                    k_s = pltpu.roll(k, delta, 0)
                    c_s = pltpu.roll(cum, delta, 0)
                    v_s = pltpu.roll(v, delta, 0)
                valid = pos >= delta
                w_pair = jnp.exp(jnp.where(valid, cum - c_s, 0.0))
                prods.append(jnp.where(valid, q * k_s * w_pair, 0.0).astype(BF16))
                v_shift.append(v_s)
            yield
            att = jnp.dot(jnp.concatenate(prods, axis=0), expand, preferred_element_type=F32)
            o = o + sum(att[g * ch:(g + 1) * ch, :] * v_shift[g] for g in range(group))
            yield

        vb = v.astype(BF16)
        atts = []
        for blk in range(1, ch // sub):
            lo = blk * sub
            ref_pt = cum[lo - 1:lo, :]
            qd = q[lo:lo + sub, :] * jnp.exp(cum[lo:lo + sub, :] - ref_pt)
            kp = k * jnp.exp(jnp.where(t_idx < lo, ref_pt - cum, -jnp.inf))
            q4 = jnp.concatenate([qd] * GLA_H, axis=0) * hmask_k
            atts.append(_mm_nt(q4, kp).astype(BF16))
            yield
        o4 = jnp.dot(jnp.concatenate(atts, axis=0), vb, preferred_element_type=F32)
        yield
        cross = [jnp.zeros((sub, GLA_W), F32)]
        for blk in range(ch // sub - 1):
            o4_b = o4[blk * GLA_H * sub:(blk + 1) * GLA_H * sub, :] * hmask_v
            cross.append(sum(o4_b[h * sub:(h + 1) * sub, :] for h in range(GLA_H)))
        o = o + jnp.concatenate(cross, axis=0)

        kt_t = (k * jnp.exp(last - cum)).T
        s_ref[j] = s_bd * _col(jnp.exp(last)) + _mm(kt_t, vb) * expand_f
        yield

        y_ref[j] = _gla_finish(o, gg, ng_ref[...], avg_ref[...]).astype(y_ref.dtype)

    _interleave([chain(j) for j in range(nb)])

    @pl.when(c == n_chunks - 1)
    def _():
        s_out_ref[...] = s_ref[...]


def _gla_prompt(slab, s0_bd, gk2_pad, b_gk, ng):
    bsz, length, _ = slab.shape
    nb = GLA_NB if bsz % GLA_NB == 0 else 1
    expand = (jnp.arange(GLA_HK)[:, None] // GLA_DK == jnp.arange(GLA_W)[None, :] // GLA_DV).astype(BF16)
    consts = (gk2_pad, b_gk, ng, _seg_ones(GLA_W, GLA_DV, 1.0 / GLA_DV), _tri_incl(GLA_CHUNK), expand)
    return pl.pallas_call(
        _gla_kernel,
        grid=(bsz // nb, length // GLA_CHUNK),
        in_specs=[pl.BlockSpec((nb, GLA_CHUNK, GLA_SLAB), lambda b, c: (b, c, 0)),
                  pl.BlockSpec((nb, GLA_HK, GLA_W), lambda b, c: (b, 0, 0))]
                 + [_const_spec(a) for a in consts],
        out_specs=[pl.BlockSpec((nb, GLA_CHUNK, GLA_W), lambda b, c: (b, c, 0)),
                   pl.BlockSpec((nb, GLA_HK, GLA_W), lambda b, c: (b, 0, 0))],
        out_shape=[jax.ShapeDtypeStruct((bsz, length, GLA_W), BF16),
                   jax.ShapeDtypeStruct((bsz, GLA_HK, GLA_W), F32)],
        scratch_shapes=[pltpu.VMEM((nb, GLA_HK, GLA_W), F32)],
        compiler_params=_cparams("arbitrary", "arbitrary"),
        name="gla_scan",
    )(slab, s0_bd, *consts)


def _col(row):
    n = row.shape[1]
    eye = _iota((n, n), 0) == _iota((n, n), 1)
    return jnp.sum(jnp.where(eye, row, 0.0), axis=1, keepdims=True)


def _store_t(dst_ref, row0, x):
    for cblk in range(x.shape[1] // LANES):
        dst_ref[row0 + cblk * LANES:row0 + (cblk + 1) * LANES, :] = x[:, cblk * LANES:(cblk + 1) * LANES].T


def _load_t(src_ref, width):
    return jnp.concatenate([src_ref[cblk * LANES:(cblk + 1) * LANES, :].T for cblk in range(width // LANES)],
                           axis=1)


_TS_X, _TS_B, _TS_C, _TS_DT, _TS_DA = 0, SSD_W, SSD_W + SSD_G * SSD_N, SSD_CONV_DIM, SSD_CONV_DIM + SMALL_W
_TR_R, _TR_W, _TR_K, _TR_V, _TR_KK, _TR_KA = (i * RWKV_W for i in range(6))
_TG_Q, _TG_K, _TG_G, _TG_V = 0, GLA_HK, 2 * GLA_HK, 3 * GLA_HK


def _step_kernel(ssd_ref, rw_ref, gla_ref, conv0_ref, shift0_ref, h0_ref, rs0_ref, gs0_ref,
                 cw_ref, cb_ref, dtb_ref, alog_ref, dexp_ref, sng_ref,
                 mu_ref, lw_ref, w0_ref, a0_ref, kk_ref, ka_ref, rk_ref, lng_ref, lnb_ref,
                 gk2_ref, bgk_ref, gng_ref, ones_ref, ravg_ref, gavg_ref,
                 y_ssd_ref, y_rwkv_ref, y_gla_ref, conv_out_ref, h_out_ref, rs_out_ref, gs_out_ref,
                 t_ssd, t_rwkv, t_gla, o_ssd_t, o_rwkv_t, o_gla_t):
    i = pl.program_id(0)
    hk = GLA_HK

    def ssd_tokens():
        xbc = ssd_ref[:, SSD_W:SSD_W + SSD_CONV_DIM]
        taps = [conv0_ref[:, t * SSD_CONV_DIM:(t + 1) * SSD_CONV_DIM] for t in range(SSD_CONV_W - 1)]
        return xbc, _ssd_token_math(xbc, ssd_ref[:, SSD_W + SSD_CONV_DIM:], taps, cw_ref[...], cb_ref[...],
                                    dtb_ref[...], alog_ref[...])

    def rwkv_tokens():
        return _rwkv_token_math(rw_ref[...], shift0_ref[...], mu_ref[...], lw_ref[...], w0_ref[...],
                                a0_ref[...], kk_ref[...], ka_ref[...], ones_ref[...])

    @pl.when(i == 0)
    def _():
        xbc, (act, dt, dta) = ssd_tokens()
        conv_out_ref[:, 0:2 * SSD_CONV_DIM] = conv0_ref[:, SSD_CONV_DIM:]
        conv_out_ref[:, 2 * SSD_CONV_DIM:] = xbc
        _store_t(t_ssd, _TS_X, act)
        _store_t(t_ssd, _TS_DT, dt)
        _store_t(t_ssd, _TS_DA, jnp.exp(dta))
        r, k2, v, logw, aic, gate, kk = rwkv_tokens()
        for off, x in ((_TR_R, r), (_TR_W, jnp.exp(logw)), (_TR_K, k2), (_TR_V, v), (_TR_KK, kk),
                       (_TR_KA, kk * aic)):
            _store_t(t_rwkv, off, x)
        lg = _gla_gate_log(gla_ref[:, 2 * hk + 2 * GLA_W:], gk2_ref[...], bgk_ref[...])
        _store_t(t_gla, _TG_Q, gla_ref[:, 0:hk] * (GLA_DK ** -0.5))
        _store_t(t_gla, _TG_K, gla_ref[:, hk:2 * hk])
        _store_t(t_gla, _TG_G, jnp.exp(lg))
        _store_t(t_gla, _TG_V, gla_ref[:, 2 * hk:2 * hk + GLA_W])
        o_gla_t[...] = jnp.zeros_like(o_gla_t)

    def row(ref, idx):
        return ref[pl.ds(idx, 1), :]

    def rows(ref, idx, count):
        return ref[pl.ds(pl.multiple_of(idx, count), count), :]

    tiles = h0_ref.shape[1] // LANES
    per_head = SSD_P * SSD_N // h0_ref.shape[1]
    h = i // per_head
    p0 = (i % per_head) * (2 * tiles)
    b_t = rows(t_ssd, _TS_B + (h // (SSD_H // SSD_G)) * SSD_N, SSD_N)
    c_t = rows(t_ssd, _TS_C + (h // (SSD_H // SSD_G)) * SSD_N, SSD_N)
    da_row = row(t_ssd, _TS_DA + DT_OFF + h)
    dt_row = row(t_ssd, _TS_DT + DT_OFF + h)
    for t in range(tiles):
        tile = h0_ref[:, t * LANES:(t + 1) * LANES].T
        new = []
        for half in range(2):
            p_idx = h * SSD_P + p0 + 2 * t + half
            h_new = (tile[half * SSD_N:(half + 1) * SSD_N, :] * da_row
                     + (row(t_ssd, _TS_X + p_idx) * dt_row) * b_t)
            o_ssd_t[pl.ds(p_idx, 1), :] = jnp.sum(h_new * c_t, axis=0, keepdims=True)
            new.append(h_new)
        h_out_ref[:, t * LANES:(t + 1) * LANES] = jnp.concatenate(new, axis=0).T

    tiles = rs0_ref.shape[1] // LANES
    per_head = RWKV_N * RWKV_N // rs0_ref.shape[1]
    h = i // per_head
    v0 = (i % per_head) * (2 * tiles)
    seg = lambda off: rows(t_rwkv, off + h * RWKV_N, RWKV_N)
    r_t, w_t, k_t, kk_t, ka_t = seg(_TR_R), seg(_TR_W), seg(_TR_K), seg(_TR_KK), seg(_TR_KA)
    for t in range(tiles):
        tile = rs0_ref[:, t * LANES:(t + 1) * LANES].T
        new = []
        for half in range(2):
            v_idx = h * RWKV_N + v0 + 2 * t + half
            s = tile[half * RWKV_N:(half + 1) * RWKV_N, :]
            sa = jnp.sum(s * (-kk_t), axis=0, keepdims=True)
            s_new = s * w_t + sa * ka_t + row(t_rwkv, _TR_V + v_idx) * k_t
            o_rwkv_t[pl.ds(v_idx, 1), :] = jnp.sum(s_new * r_t, axis=0, keepdims=True)
            new.append(s_new)
        rs_out_ref[:, t * LANES:(t + 1) * LANES] = jnp.concatenate(new, axis=0).T

    tiles = gs0_ref.shape[1] // LANES
    per_head = GLA_DK * GLA_DV // gs0_ref.shape[1]
    h = i // per_head
    k0 = (i % per_head) * (2 * tiles)
    v_t = rows(t_gla, _TG_V + h * GLA_DV, GLA_DV)
    acc = jnp.zeros((GLA_DV, LANES), F32)
    for t in range(tiles):
        tile = gs0_ref[:, t * LANES:(t + 1) * LANES].T
        new = []
        for half in range(2):
            k_idx = h * GLA_DK + k0 + 2 * t + half
            s_new = (tile[half * GLA_DV:(half + 1) * GLA_DV, :] * row(t_gla, _TG_G + k_idx)
                     + row(t_gla, _TG_K + k_idx) * v_t)
            acc = acc + row(t_gla, _TG_Q + k_idx) * s_new
            new.append(s_new)
        gs_out_ref[:, t * LANES:(t + 1) * LANES] = jnp.concatenate(new, axis=0).T
    o_rows = pl.ds(pl.multiple_of(h * GLA_DV, GLA_DV), GLA_DV)
    o_gla_t[o_rows, :] = o_gla_t[o_rows, :] + acc

    @pl.when(i == pl.num_programs(0) - 1)
    def _():
        _, (act, _, _) = ssd_tokens()
        y_ssd_ref[...] = _ssd_finish(_load_t(o_ssd_t, SSD_W), act[:, 0:SSD_W], ssd_ref[:, 0:SSD_W],
                                     dexp_ref[...], sng_ref[...]).astype(y_ssd_ref.dtype)
        r, k2, v, _, _, gate, _ = rwkv_tokens()
        y_rwkv_ref[...] = _rwkv_finish(_load_t(o_rwkv_t, RWKV_W), r, k2, v, gate, rk_ref[...], lng_ref[...],
                                       lnb_ref[...], ones_ref[...], ravg_ref[...]).astype(y_rwkv_ref.dtype)
        y_gla_ref[...] = _gla_finish(_load_t(o_gla_t, GLA_W), gla_ref[:, 2 * hk + GLA_W:2 * hk + 2 * GLA_W],
                                     gng_ref[...], gavg_ref[...]).astype(y_gla_ref.dtype)


def _mixer_step(ssd_slab, rw, gla_slab, conv0, shift0, h0, rs0, gs0, ssd_p, rwkv_p, gla_p):
    bsz = ssd_slab.shape[0]
    assert bsz == LANES, "the single-token mixer keeps exactly one vreg row of batch entries on the lanes"
    flat = lambda s: s.reshape(bsz, -1)
    h0, rs0, gs0 = flat(h0), flat(rs0), flat(gs0)
    consts = (tuple(ssd_p) + tuple(rwkv_p) + tuple(gla_p)
              + (_seg_ones(RWKV_W, RWKV_N), _seg_ones(RWKV_W, RWKV_N, 1.0 / RWKV_N),
                 _seg_ones(GLA_W, GLA_DV, 1.0 / GLA_DV)))
    full = lambda a: pl.BlockSpec(a.shape, lambda i: (0,) * a.ndim)
    cols = lambda a: pl.BlockSpec((bsz, a.shape[1] // STEP_GRID), lambda i: (0, i))
    tokens = (ssd_slab, rw, gla_slab, conv0, shift0)
    outs = pl.pallas_call(
        _step_kernel,
        grid=(STEP_GRID,),
        in_specs=[full(a) for a in tokens] + [cols(h0), cols(rs0), cols(gs0)] + [full(a) for a in consts],
        out_specs=[full(jax.ShapeDtypeStruct((bsz, w), BF16)) for w in (SSD_W, RWKV_W, GLA_W)]
                  + [full(conv0), cols(h0), cols(rs0), cols(gs0)],
        out_shape=[jax.ShapeDtypeStruct((bsz, SSD_W), BF16),
                   jax.ShapeDtypeStruct((bsz, RWKV_W), BF16),
                   jax.ShapeDtypeStruct((bsz, GLA_W), BF16),
                   jax.ShapeDtypeStruct(conv0.shape, F32),
                   jax.ShapeDtypeStruct(h0.shape, F32),
                   jax.ShapeDtypeStruct(rs0.shape, F32),
                   jax.ShapeDtypeStruct(gs0.shape, F32)],
        scratch_shapes=[pltpu.VMEM((SSD_CONV_DIM + 2 * SMALL_W, LANES), F32),
                        pltpu.VMEM((6 * RWKV_W, LANES), F32),
                        pltpu.VMEM((3 * GLA_HK + GLA_W, LANES), F32),
                        pltpu.VMEM((SSD_W, LANES), F32),
                        pltpu.VMEM((RWKV_W, LANES), F32),
                        pltpu.VMEM((GLA_W, LANES), F32)],
        compiler_params=_cparams("arbitrary"),
        name="mixer_step",
    )(*tokens, h0, rs0, gs0, *consts)
    y_ssd, y_rwkv, y_gla, conv_new, h_new, rs_new, gs_new = outs
    return (y_ssd, y_rwkv, y_gla, conv_new.reshape(bsz, SSD_CONV_W - 1, SSD_CONV_DIM),
            h_new.reshape(bsz, SSD_H, SSD_P, SSD_N), rs_new.reshape(bsz, RWKV_H, RWKV_N, RWKV_N),
            gs_new.reshape(bsz, GLA_H, GLA_DK, GLA_DV))


def _outproj_kernel(ys_ref, yr_ref, yg_ref, x_ref, gt_ref, w_ref, g_ref, b_ref, o_ref):
    m = (jnp.dot(ys_ref[0], w_ref[0:SSD_W, :], preferred_element_type=F32)
         + jnp.dot(yr_ref[0], w_ref[SSD_W:SSD_W + RWKV_W, :], preferred_element_type=F32)
         + jnp.dot(yg_ref[0], w_ref[SSD_W + RWKV_W:, :], preferred_element_type=F32))
    u = ALPHA * x_ref[0] + (1.0 + gt_ref[0]) * m
    o_ref[0] = _layernorm(u, g_ref[...], b_ref[...])


def _outproj(y_ssd, y_rwkv, y_gla, x, gt, w_out, ln_g, ln_b, tm):
    bsz, length, _ = x.shape
    tok = lambda width: pl.BlockSpec((1, tm, width), lambda b, i: (b, i, 0))
    vec = pl.BlockSpec((1, D_MODEL), lambda b, i: (0, 0))
    return pl.pallas_call(
        _outproj_kernel,
        grid=(bsz, length // tm),
        in_specs=[tok(SSD_W), tok(RWKV_W), tok(GLA_W), tok(D_MODEL), _mod_spec(gt, tm),
                  pl.BlockSpec((D_MODEL, D_MODEL), lambda b, i: (0, 0)), vec, vec],
        out_specs=tok(D_MODEL),
        out_shape=jax.ShapeDtypeStruct((bsz, length, D_MODEL), F32),
        compiler_params=_cparams("arbitrary", "arbitrary"),
        name="outproj_ln",
    )(y_ssd, y_rwkv, y_gla, x, gt, w_out, ln_g, ln_b)


def _ffn_kernel(x_ref, sc_ref, sh_ref, gt_ref, wg_ref, wu_ref, wd_ref, g_ref, b_ref, o_ref, h_ref, acc_ref):
    f = pl.program_id(2)

    @pl.when(f == 0)
    def _():
        h_ref[...] = (x_ref[0] * (1.0 + sc_ref[0]) + sh_ref[0]).astype(BF16)
        acc_ref[...] = jnp.zeros_like(acc_ref)

    h = h_ref[...]
    a = _silu(jnp.dot(h, wg_ref[...], preferred_element_type=F32)) * jnp.dot(
        h, wu_ref[...], preferred_element_type=F32)
    acc_ref[...] += jnp.dot(a.astype(BF16), wd_ref[...], preferred_element_type=F32)

    @pl.when(f == pl.num_programs(2) - 1)
    def _():
        u = ALPHA * x_ref[0] + (1.0 + gt_ref[0]) * acc_ref[...]
        o_ref[0] = _layernorm(u, g_ref[...], b_ref[...])


def _ffn(x, sc, sh, gt, wg, wu, wd, ln_g, ln_b, tm, tf):
    bsz, length, _ = x.shape
    tok = pl.BlockSpec((1, tm, D_MODEL), lambda b, i, f: (b, i, 0))
    vec = pl.BlockSpec((1, D_MODEL), lambda b, i, f: (0, 0))
    return pl.pallas_call(
        _ffn_kernel,
        grid=(bsz, length // tm, F_DENSE // tf),
        in_specs=[tok, _mod_spec(sc, tm), _mod_spec(sh, tm), _mod_spec(gt, tm),
                  pl.BlockSpec((D_MODEL, tf), lambda b, i, f: (0, f)),
                  pl.BlockSpec((D_MODEL, tf), lambda b, i, f: (0, f)),
                  pl.BlockSpec((tf, D_MODEL), lambda b, i, f: (f, 0)), vec, vec],
        out_specs=tok,
        out_shape=jax.ShapeDtypeStruct((bsz, length, D_MODEL), F32),
        scratch_shapes=[pltpu.VMEM((tm, D_MODEL), BF16), pltpu.VMEM((tm, D_MODEL), F32)],
        compiler_params=_cparams("arbitrary", "arbitrary", "arbitrary"),
        name="ffn_ln",
    )(x, sc, sh, gt, wg, wu, wd, ln_g, ln_b)


def _moe_kernel(x_ref, sc_ref, sh_ref, gt_ref, rt_ref, wg_ref, wu_ref, wd_ref, g_ref, b_ref, tri_ref, o_ref,
                h_ref, comb_ref, rank_ref, rank_t_ref, cnt_ref, acc_ref, *, half, cap):
    e = pl.program_id(2)
    tm = 2 * half

    @pl.when(e == 0)
    def _():
        h = x_ref[0] * (1.0 + sc_ref[0]) + sh_ref[0]
        h_ref[...] = h.astype(BF16)
        acc_ref[...] = jnp.zeros_like(acc_ref)
        lane = _iota((tm, LANES), 1).astype(F32)
        logits = jnp.where(lane < N_EXPERTS, _mm_hi(h, rt_ref[...]), -jnp.inf)
        m1 = jnp.max(logits, axis=-1, keepdims=True)
        i1 = jnp.min(jnp.where(logits == m1, lane, float(LANES)), axis=-1, keepdims=True)
        rest = jnp.where(lane == i1, -jnp.inf, logits)
        m2 = jnp.max(rest, axis=-1, keepdims=True)
        i2 = jnp.min(jnp.where(rest == m2, lane, float(LANES)), axis=-1, keepdims=True)
        e2 = jnp.exp(m2 - m1)
        den = 1.0 + e2
        comb_ref[...] = jnp.where(lane == i1, 1.0 / den, 0.0) + jnp.where(lane == i2, e2 / den, 0.0)
        sel = (lane == i1) | (lane == i2)
        sel_f = sel.astype(F32)
        for hf in range(2):
            rows = slice(hf * half, (hf + 1) * half)
            s = sel_f[rows, :]
            before = jnp.dot(tri_ref[...], s.astype(BF16), preferred_element_type=F32)
            rank_ref[rows, :] = jnp.where(sel[rows, :], before, -1.0)
            cnt_ref[hf:hf + 1, :] = jnp.sum(s, axis=0, keepdims=True)
        for blk in range(tm // LANES):
            rows = slice(blk * LANES, (blk + 1) * LANES)
            rank_t_ref[:, rows] = rank_ref[rows, :].T

    on_lane = _iota((tm, LANES), 1) == e
    comb_e = jnp.sum(jnp.where(on_lane, comb_ref[...], 0.0), axis=-1, keepdims=True)
    rank_e = jnp.sum(jnp.where(on_lane, rank_ref[...], 0.0), axis=-1, keepdims=True)
    rank_t_e = rank_t_ref[pl.ds(e, 1), :]
    n_max = jnp.max(jnp.where(_iota((2, LANES), 1) == e, cnt_ref[...], 0.0))
    n_pass = (n_max.astype(jnp.int32) + (cap - 1)) // cap

    def one_pass(p, carry):
        off = (p * cap).astype(F32)
        slot_col = _iota((cap, 1), 0).astype(F32) + off
        slot_row = _iota((1, cap), 1).astype(F32) + off
        xs = []
        for hf in range(2):
            rows = slice(hf * half, (hf + 1) * half)
            pick = (rank_t_e[:, rows] == slot_col).astype(BF16)
            xs.append(jnp.dot(pick, h_ref[rows, :], preferred_element_type=F32).astype(BF16))
        xc = jnp.concatenate(xs, axis=0)
        a = _silu(jnp.dot(xc, wg_ref[0], preferred_element_type=F32)) * jnp.dot(
            xc, wu_ref[0], preferred_element_type=F32)
        out_e = jnp.dot(a.astype(BF16), wd_ref[0], preferred_element_type=F32)
        for hf in range(2):
            rows = slice(hf * half, (hf + 1) * half)
            put = (rank_e[rows, :] == slot_row).astype(BF16)
            back = sum(jnp.dot(put, part, preferred_element_type=F32)
                       for part in _split_bf16(out_e[hf * cap:(hf + 1) * cap, :], 2))
            acc_ref[rows, :] += comb_e[rows, :] * back
        return carry

    lax.fori_loop(0, n_pass, one_pass, 0)

    @pl.when(e == pl.num_programs(2) - 1)
    def _():
        u = ALPHA * x_ref[0] + (1.0 + gt_ref[0]) * acc_ref[...]
        o_ref[0] = _layernorm(u, g_ref[...], b_ref[...])


def _moe(x, sc, sh, gt, router_pad, wg, wu, wd, ln_g, ln_b, tm):
    bsz, length, _ = x.shape
    assert tm % LANES == 0
    half = tm // 2
    cap = MOE_CAP_ROWS if half >= 2 * MOE_CAP_ROWS else half
    tri = (jnp.arange(half)[None, :] < jnp.arange(half)[:, None]).astype(BF16)
    tok = pl.BlockSpec((1, tm, D_MODEL), lambda b, i, e: (b, i, 0))
    vec = pl.BlockSpec((1, D_MODEL), lambda b, i, e: (0, 0))
    wspec = pl.BlockSpec((1, D_MODEL, D_MODEL), lambda b, i, e: (e, 0, 0))
    return pl.pallas_call(
        functools.partial(_moe_kernel, half=half, cap=cap),
        grid=(bsz, length // tm, N_EXPERTS),
        in_specs=[tok, _mod_spec(sc, tm), _mod_spec(sh, tm), _mod_spec(gt, tm),
                  pl.BlockSpec((D_MODEL, LANES), lambda b, i, e: (0, 0)),
                  wspec, wspec, wspec, vec, vec, _const_spec(tri)],
        out_specs=tok,
        out_shape=jax.ShapeDtypeStruct((bsz, length, D_MODEL), F32),
        scratch_shapes=[pltpu.VMEM((tm, D_MODEL), BF16), pltpu.VMEM((tm, LANES), F32),
                        pltpu.VMEM((tm, LANES), F32), pltpu.VMEM((LANES, tm), F32),
                        pltpu.VMEM((2, LANES), F32), pltpu.VMEM((tm, D_MODEL), F32)],
        compiler_params=_cparams("arbitrary", "arbitrary", "arbitrary"),
        name="moe_ln",
    )(x, sc, sh, gt, router_pad, wg, wu, wd, ln_g, ln_b, tri)


def _pad_lanes(vec, offset, width=SMALL_W):
    out = jnp.zeros((1, width), F32)
    return out.at[0, offset:offset + vec.shape[0]].set(vec)


def _layer_params(p, l):
    w_in = p["w_in"][l]
    off = [0]
    for s in (SSD_W, SSD_CONV_DIM, SSD_H, RWKV_FEAT, GLA_HK, GLA_HK, GLA_W, GK_LORA, GLA_W):
        off.append(off[-1] + s)
    piece = lambda i: w_in[:, off[i]:off[i + 1]]
    small = jnp.zeros((D_MODEL, SMALL_W), F32)
    small = small.at[:, DT_OFF:DT_OFF + SSD_H].set(piece(2)).at[:, GLO_OFF:GLO_OFF + GK_LORA].set(piece(7))
    w_pad = jnp.concatenate([piece(0), piece(1), small, piece(3), piece(4), piece(5), piece(6), piece(8), small],
                            axis=1).astype(BF16)
    ssd_p = (p["ssd_conv_w"][l], p["ssd_conv_b"][l][None, :],
             _pad_lanes(p["ssd_dt_bias"][l], DT_OFF), _pad_lanes(p["ssd_a_log"][l], DT_OFF),
             jnp.repeat(p["ssd_d"][l], SSD_P)[None, :], p["ssd_norm_g"][l][None, :])
    lora_w = jnp.zeros((LANES, 3 * RWKV_W), F32)
    lora_w = (lora_w.at[0:32, 0:RWKV_W].set(p["rwkv_w2"][l])
              .at[32:64, RWKV_W:2 * RWKV_W].set(p["rwkv_a2"][l])
              .at[64:128, 2 * RWKV_W:].set(p["rwkv_g2"][l])).astype(BF16)
    row = lambda name: p[name][l].reshape(1, -1)
    rwkv_p = (row("rwkv_mu"), lora_w, row("rwkv_w0"), row("rwkv_a0"), row("rwkv_k_k"), row("rwkv_k_a"),
              row("rwkv_r_k"), row("rwkv_ln_g"), row("rwkv_ln_b"))
    gk2_pad = jnp.zeros((SMALL_W, GLA_HK), F32).at[GLO_OFF:GLO_OFF + GK_LORA].set(p["gla_w_gk2"][l])
    gla_p = (gk2_pad, row("gla_b_gk"), jnp.tile(p["gla_norm_g"][l], GLA_H)[None, :])
    return w_pad, ssd_p, rwkv_p, gla_p


def _block_diag_inv(s_bd, h):
    b, hr, hc = s_bd.shape
    r, c = hr // h, hc // h
    s = s_bd.reshape(b, h, r, h, c)
    return jnp.stack([s[:, i, :, i, :] for i in range(h)], axis=1)


def _tail(x, mod_l, l, p, y_ssd, y_rwkv, y_gla, tm):
    sh1, sc1, gt1, sh2, sc2, gt2 = mod_l
    row = lambda name: p[name][l].reshape(1, -1)
    x = _outproj(y_ssd, y_rwkv, y_gla, x, gt1, p["w_out"][l].astype(BF16), row("ln_mix_g"), row("ln_mix_b"), tm)
    i = l // 2
    if l % 2 == 0:
        tf = F_DENSE // 2
        x = _ffn(x, sc2, sh2, gt2, p["ffn_w_gate"][i].astype(BF16), p["ffn_w_up"][i].astype(BF16),
                 p["ffn_w_down"][i].astype(BF16), row("ln_ffn_g"), row("ln_ffn_b"), tm, tf)
    else:
        router_pad = jnp.zeros((D_MODEL, LANES), F32).at[:, :N_EXPERTS].set(p["moe_router"][i])
        x = _moe(x, sc2, sh2, gt2, router_pad, p["moe_w_gate"][i].astype(BF16), p["moe_w_up"][i].astype(BF16),
                 p["moe_w_down"][i].astype(BF16), row("ln_ffn_g"), row("ln_ffn_b"), min(2 * tm, x.shape[1]))
    return x


def _forward(x_prompt, x_sample, c_prompt, c_sample, states, p):
    bp, seq, _ = x_prompt.shape
    bs = x_sample.shape[0]
    state_ssd, state_conv, state_rwkv, state_shift, state_gla = states
    mod = _ada(jnp.concatenate([c_prompt, c_sample], axis=0), p["w_ada"], p["b_ada"])

    xp = x_prompt
    xs = x_sample.reshape(1, bs, D_MODEL)
    tm_p = min(512, seq)
    outs_p = [[] for _ in range(5)]
    outs_s = [[] for _ in range(5)]
    for l in range(DEPTH):
        w_pad, ssd_p, rwkv_p, gla_p = _layer_params(p, l)
        mods = jnp.split(mod[l], 6, axis=-1)
        mod_p = [m[:bp, None, :] for m in mods]
        mod_s = [m[None, bp:, :] for m in mods]

        ssd_slab, rw, gla_slab = _inproj(xp, mod_p[1], mod_p[0], w_pad, tm_p)
        y_ssd, conv_new, h_new = _ssd_prompt(
            ssd_slab, jnp.zeros((bp, SSD_CONV_W - 1, SSD_CONV_DIM), F32),
            jnp.zeros((bp, SSD_H, SSD_P, SSD_N), F32), *ssd_p)
        y_rwkv, shift_new, rs_bd = _rwkv_prompt(
            rw, jnp.zeros((bp, 1, RWKV_FEAT), F32), jnp.zeros((bp, RWKV_W, RWKV_W), F32), *rwkv_p)
        y_gla, gs_bd = _gla_prompt(gla_slab, jnp.zeros((bp, GLA_HK, GLA_W), F32), *gla_p)
        xp = _tail(xp, mod_p, l, p, y_ssd, y_rwkv, y_gla, tm_p)
        for acc, s in zip(outs_p, (h_new, conv_new, _block_diag_inv(rs_bd, RWKV_H),
                                   shift_new.reshape(bp, RWKV_FEAT), _block_diag_inv(gs_bd, GLA_H))):
            acc.append(s)

        ssd_slab, rw, gla_slab = _inproj(xs, mod_s[1], mod_s[0], w_pad, bs)
        y_ssd, y_rwkv, y_gla, conv_new, h_new, rs_new, gs_new = _mixer_step(
            ssd_slab[0], rw[0], gla_slab[0], state_conv[l].reshape(bs, -1), state_shift[l], state_ssd[l],
            state_rwkv[l], state_gla[l], ssd_p, rwkv_p, gla_p)
        xs = _tail(xs, mod_s, l, p, y_ssd[None], y_rwkv[None], y_gla[None], bs)
        for acc, s in zip(outs_s, (h_new, conv_new, rs_new, rw[0], gs_new)):
            acc.append(s)

    stack = lambda accs: tuple(jnp.stack(a, axis=0) for a in accs)
    return (xp, xs.reshape(bs, 1, D_MODEL)) + stack(outs_p) + stack(outs_s)


def kernel(x_prompt, x_sample, c_prompt, c_sample, state_ssd, state_ssd_conv, state_rwkv, state_rwkv_shift, state_gla, w_ada, b_ada, w_in, w_out, ssd_conv_w, ssd_conv_b, ssd_dt_bias, ssd_a_log, ssd_d, ssd_norm_g, rwkv_mu, rwkv_w0, rwkv_w2, rwkv_a0, rwkv_a2, rwkv_g2, rwkv_k_k, rwkv_k_a, rwkv_r_k, rwkv_ln_g, rwkv_ln_b, gla_w_gk2, gla_b_gk, gla_norm_g, ln_mix_g, ln_mix_b, ln_ffn_g, ln_ffn_b, ffn_w_gate, ffn_w_up, ffn_w_down, moe_router, moe_w_gate, moe_w_up, moe_w_down):
    p = dict(w_ada=w_ada, b_ada=b_ada, w_in=w_in, w_out=w_out, ssd_conv_w=ssd_conv_w, ssd_conv_b=ssd_conv_b,
             ssd_dt_bias=ssd_dt_bias, ssd_a_log=ssd_a_log, ssd_d=ssd_d, ssd_norm_g=ssd_norm_g,
             rwkv_mu=rwkv_mu, rwkv_w0=rwkv_w0, rwkv_w2=rwkv_w2, rwkv_a0=rwkv_a0, rwkv_a2=rwkv_a2,
             rwkv_g2=rwkv_g2, rwkv_k_k=rwkv_k_k, rwkv_k_a=rwkv_k_a, rwkv_r_k=rwkv_r_k,
             rwkv_ln_g=rwkv_ln_g, rwkv_ln_b=rwkv_ln_b, gla_w_gk2=gla_w_gk2, gla_b_gk=gla_b_gk,
             gla_norm_g=gla_norm_g, ln_mix_g=ln_mix_g, ln_mix_b=ln_mix_b, ln_ffn_g=ln_ffn_g,
             ln_ffn_b=ln_ffn_b, ffn_w_gate=ffn_w_gate, ffn_w_up=ffn_w_up, ffn_w_down=ffn_w_down,
             moe_router=moe_router, moe_w_gate=moe_w_gate, moe_w_up=moe_w_up, moe_w_down=moe_w_down)
    states = (state_ssd, state_ssd_conv, state_rwkv, state_rwkv_shift, state_gla)
    return _forward(x_prompt, x_sample, c_prompt, c_sample, states, p)
```

```python
import functools

import jax
import jax.numpy as jnp
from jax import lax
from jax.experimental import pallas as pl
from jax.experimental.pallas import tpu as pltpu

F32 = jnp.float32
BF16 = jnp.bfloat16

D_MODEL = 1024
DEPTH = 2
SSD_W = 512
SSD_H = 8
SSD_P = 64
SSD_N = 64
SSD_G = 2
SSD_CONV_W = 4
SSD_CONV_DIM = 768
RWKV_W = 256
RWKV_H = 4
RWKV_N = 64
RWKV_FEAT = 896
RWKV_GN_EPS = RWKV_N * 1e-5
GLA_W = 256
GLA_H = 4
GLA_DK = 32
GLA_DV = 64
GLA_HK = GLA_H * GLA_DK
GK_LORA = 16
GATE_NORMALIZER = 16.0
F_DENSE = 2816
N_EXPERTS = 8
ALPHA = (2.0 * DEPTH) ** 0.25
LN_EPS = 1e-5
RMS_EPS = 1e-6

LANES = 128
SMALL_W = LANES
DT_OFF = 0
GLO_OFF = 8
SSD_SLAB = SSD_W + SSD_CONV_DIM + SMALL_W
GLA_SLAB = 2 * GLA_HK + 2 * GLA_W + SMALL_W
IN_PAD = SSD_SLAB + RWKV_FEAT + GLA_SLAB
SSD_CHUNK = 128
GLA_CHUNK = 128
GLA_SUB = 16
RWKV_CHUNK = 64
SSD_NB, RWKV_NB, GLA_NB = 2, 8, 4
RWKV_SKEW = 1
MOE_CAP_ROWS = 192
STEP_GRID = 32
VMEM_LIMIT = 56 * 1024 * 1024


def _cparams(*sem):
    return pltpu.CompilerParams(dimension_semantics=sem, vmem_limit_bytes=VMEM_LIMIT)


def _mm(a, b):
    return jnp.dot(a.astype(BF16), b.astype(BF16), preferred_element_type=F32)


def _mm_nt(a, b):
    return lax.dot_general(a.astype(BF16), b.astype(BF16), (((1,), (1,)), ((), ())),
                           preferred_element_type=F32)


def _split_bf16(x, terms):
    parts = []
    for _ in range(terms):
        p = x.astype(BF16)
        parts.append(p)
        x = x - p.astype(F32)
    return parts


def _mm_sel(sel, x, terms=3):
    return sum(jnp.dot(sel, p, preferred_element_type=F32) for p in _split_bf16(x, terms))


def _mm_xsel(x, sel, terms=2):
    return sum(jnp.dot(p, sel, preferred_element_type=F32) for p in _split_bf16(x, terms))


def _mm_hi(a, b):
    return jnp.dot(a, b, preferred_element_type=F32, precision=lax.Precision.HIGHEST)


def _sigmoid(x):
    return 1.0 / (1.0 + jnp.exp(-x))


def _silu(x):
    return x * _sigmoid(x)


def _softplus(x):
    return jnp.maximum(x, 0.0) + jnp.log1p(jnp.exp(-jnp.abs(x)))


def _iota(shape, dim):
    return lax.broadcasted_iota(jnp.int32, shape, dim)


def _layernorm(u, g, b):
    mu = jnp.mean(u, axis=-1, keepdims=True)
    d = u - mu
    var = jnp.mean(d * d, axis=-1, keepdims=True)
    return d * lax.rsqrt(var + LN_EPS) * g + b


def _tri_incl(n):
    return (jnp.arange(n)[None, :] <= jnp.arange(n)[:, None]).astype(BF16)


def _seg_ones(n, seg, scale=1.0):
    idx = jnp.arange(n) // seg
    return jnp.where(idx[:, None] == idx[None, :], scale, 0.0).astype(BF16)


def _interleave(chains, skew=0):
    chains = list(chains)
    done = [False] * len(chains)
    rnd = 0
    while not all(done):
        for i, ch in enumerate(chains):
            if done[i] or rnd < i * skew:
                continue
            try:
                next(ch)
            except StopIteration:
                done[i] = True
        rnd += 1


def _const_spec(arr):
    nd = arr.ndim
    return pl.BlockSpec(arr.shape, lambda *_: (0,) * nd)


def _ada_kernel(c_ref, w_ref, b_ref, o_ref):
    o_ref[0] = _mm(_silu(c_ref[...]), w_ref[0]) + b_ref[0]


def _ada(c_all, w_ada, b_ada):
    rows = c_all.shape[0]
    tn = 1536
    return pl.pallas_call(
        _ada_kernel,
        grid=(DEPTH, 6 * D_MODEL // tn),
        in_specs=[pl.BlockSpec((rows, D_MODEL), lambda l, j: (0, 0)),
                  pl.BlockSpec((1, D_MODEL, tn), lambda l, j: (l, 0, j)),
                  pl.BlockSpec((1, 1, tn), lambda l, j: (l, 0, j))],
        out_specs=pl.BlockSpec((1, rows, tn), lambda l, j: (l, 0, j)),
        out_shape=jax.ShapeDtypeStruct((DEPTH, rows, 6 * D_MODEL), F32),
        compiler_params=_cparams("arbitrary", "arbitrary"),
        name="ada",
    )(c_all, w_ada, b_ada.reshape(DEPTH, 1, 6 * D_MODEL))


def _inproj_kernel(x_ref, sc_ref, sh_ref, w_ref, o_ssd, o_rwkv, o_gla):
    h = (x_ref[0] * (1.0 + sc_ref[0]) + sh_ref[0]).astype(BF16)
    o_ssd[0] = jnp.dot(h, w_ref[:, :SSD_SLAB], preferred_element_type=F32)
    o_rwkv[0] = jnp.dot(h, w_ref[:, SSD_SLAB:SSD_SLAB + RWKV_FEAT], preferred_element_type=F32)
    o_gla[0] = jnp.dot(h, w_ref[:, SSD_SLAB + RWKV_FEAT:], preferred_element_type=F32)


def _mod_spec(mod, tm):
    if mod.shape[1] == 1:
        return pl.BlockSpec((1, 1, D_MODEL), lambda b, i, *_: (b, 0, 0))
    return pl.BlockSpec((1, tm, D_MODEL), lambda b, i, *_: (b, i, 0))


def _inproj(x, sc, sh, w_pad, tm):
    bsz, length, _ = x.shape
    tok = lambda width: pl.BlockSpec((1, tm, width), lambda b, i: (b, i, 0))
    return pl.pallas_call(
        _inproj_kernel,
        grid=(bsz, length // tm),
        in_specs=[tok(D_MODEL), _mod_spec(sc, tm), _mod_spec(sh, tm),
                  pl.BlockSpec((D_MODEL, IN_PAD), lambda b, i: (0, 0))],
        out_specs=[tok(SSD_SLAB), tok(RWKV_FEAT), tok(GLA_SLAB)],
        out_shape=[jax.ShapeDtypeStruct((bsz, length, SSD_SLAB), F32),
                   jax.ShapeDtypeStruct((bsz, length, RWKV_FEAT), F32),
                   jax.ShapeDtypeStruct((bsz, length, GLA_SLAB), F32)],
        compiler_params=_cparams("arbitrary", "arbitrary"),
        name="inproj",
    )(x, sc, sh, w_pad)


def _ssd_token_math(xbc, small, conv_taps, cw, cb, dtb, alog):
    conv = cb + xbc * cw[3:4, :]
    for i in range(SSD_CONV_W - 1):
        conv = conv + conv_taps[i] * cw[i:i + 1, :]
    dt = _softplus(small + dtb)
    return _silu(conv), dt, dt * (-jnp.exp(alog))


def _ssd_finish(y, xs, z, dexp, ng):
    gated = (y + xs * dexp) * _silu(z)
    ms = jnp.mean(gated * gated, axis=-1, keepdims=True)
    return gated * lax.rsqrt(ms + RMS_EPS) * ng


def _ssd_kernel(slab_ref, conv0_ref, h0_ref, cw_ref, cb_ref, dtb_ref, alog_ref, dexp_ref, ng_ref, tri_ref,
                head_x_ref, group_mask_ref, pair_mask_ref, y_ref, conv_out_ref, h_out_ref, ext_ref, h_ref):
    c = pl.program_id(1)
    n_chunks = pl.num_programs(1)
    ch = SSD_CHUNK
    causal = _iota((ch, ch), 1) <= _iota((ch, ch), 0)

    nb = slab_ref.shape[0]

    @pl.when(c == 0)
    def _():
        for j in range(nb):
            ext_ref[j, 0:8, :] = jnp.zeros((8, SSD_CONV_DIM), F32)
            ext_ref[j, 5:8, :] = conv0_ref[j]
            h_ref[j] = h0_ref[j]

    def chain(j):
        z = slab_ref[j, :, 0:SSD_W]
        xbc = slab_ref[j, :, SSD_W:SSD_W + SSD_CONV_DIM]
        small = slab_ref[j, :, SSD_W + SSD_CONV_DIM:]

        ext_ref[j, 8:8 + ch, :] = xbc
        taps = [ext_ref[j, 5 + i:5 + i + ch, :] for i in range(SSD_CONV_W - 1)]
        act, dt, dta = _ssd_token_math(xbc, small, taps, cw_ref[...], cb_ref[...], dtb_ref[...], alog_ref[...])
        ext_ref[j, 0:8, :] = ext_ref[j, ch:ch + 8, :]

        xs = act[:, 0:SSD_W]
        bs = act[:, SSD_W:SSD_W + SSD_G * SSD_N]
        cs = act[:, SSD_W + SSD_G * SSD_N:]
        yield
        cum = _mm_sel(tri_ref[...], dta)
        cum_t = cum.T
        dt_t = dt.T
        cum_x = _mm_xsel(cum, head_x_ref[...], 3)
        dt_x = _mm_xsel(dt, head_x_ref[...], 3)
        yield
        last_x = cum_x[ch - 1:ch, :]
        xs_tail = (xs * (jnp.exp(last_x - cum_x) * dt_x)).astype(BF16)
        h_bd = h_ref[j]
        y_state = jnp.dot(cs.astype(BF16), h_bd.astype(BF16), preferred_element_type=F32) * jnp.exp(cum_x)
        upd = jnp.dot(bs.T.astype(BF16), xs_tail, preferred_element_type=F32)
        h_ref[j] = h_bd * jnp.exp(last_x) + upd * group_mask_ref[...]
        yield

        lane_group = _iota((1, SSD_G * SSD_N), 1) // SSD_N
        xs_b = xs.astype(BF16)
        ys = []
        for g in range(SSD_G):
            cb = _mm_nt(jnp.where(lane_group == g, cs, 0.0), bs)
            yield
            for pair in range(SSD_H // SSD_G // 2):
                scores = []
                for hh in range(2):
                    lane = DT_OFF + g * (SSD_H // SSD_G) + 2 * pair + hh
                    decay = jnp.exp(jnp.where(causal, cum[:, lane:lane + 1] - cum_t[lane:lane + 1, :], -jnp.inf))
                    scores.append((cb * decay * dt_t[lane:lane + 1, :]).astype(BF16))
                blk = g * (SSD_H // SSD_G) // 2 + pair
                x_pair = xs_b[:, blk * LANES:(blk + 1) * LANES]
                x_bd = jnp.concatenate([x_pair, x_pair], axis=0) * pair_mask_ref[...]
                ys.append(jnp.dot(jnp.concatenate(scores, axis=1), x_bd, preferred_element_type=F32))
                yield

        y = jnp.concatenate(ys, axis=-1) + y_state
        y_ref[j] = _ssd_finish(y, xs, z, dexp_ref[...], ng_ref[...]).astype(y_ref.dtype)

    _interleave([chain(j) for j in range(nb)])

    @pl.when(c == n_chunks - 1)
    def _():
        for j in range(nb):
            conv_out_ref[j] = ext_ref[j, 5:8, :]
            h_out_ref[j] = h_ref[j]


def _ssd_prompt(slab, conv0, h0, cw, cb, dtb, alog, dexp, ng):
    bsz, length, _ = slab.shape
    nb = SSD_NB if bsz % SSD_NB == 0 else 1
    gn, hp = SSD_G * SSD_N, SSD_H * SSD_P
    lane_head = jnp.arange(hp) // SSD_P
    head_x = (jnp.arange(SMALL_W)[:, None] == DT_OFF + lane_head[None, :]).astype(BF16)
    group_mask = (jnp.arange(gn)[:, None] // SSD_N == lane_head[None, :] // (SSD_H // SSD_G)).astype(F32)
    pair_mask = (jnp.arange(2 * SSD_CHUNK)[:, None] // SSD_CHUNK == jnp.arange(2 * SSD_P)[None, :] // SSD_P
                 ).astype(BF16)
    consts = (cw, cb, dtb, alog, dexp, ng, _tri_incl(SSD_CHUNK), head_x, group_mask, pair_mask)
    return pl.pallas_call(
        _ssd_kernel,
        grid=(bsz // nb, length // SSD_CHUNK),
        in_specs=[pl.BlockSpec((nb, SSD_CHUNK, SSD_SLAB), lambda b, c: (b, c, 0)),
                  pl.BlockSpec((nb, SSD_CONV_W - 1, SSD_CONV_DIM), lambda b, c: (b, 0, 0)),
                  pl.BlockSpec((nb, gn, hp), lambda b, c: (b, 0, 0))]
                 + [_const_spec(a) for a in consts],
        out_specs=[pl.BlockSpec((nb, SSD_CHUNK, SSD_W), lambda b, c: (b, c, 0)),
                   pl.BlockSpec((nb, SSD_CONV_W - 1, SSD_CONV_DIM), lambda b, c: (b, 0, 0)),
                   pl.BlockSpec((nb, gn, hp), lambda b, c: (b, 0, 0))],
        out_shape=[jax.ShapeDtypeStruct((bsz, length, SSD_W), BF16),
                   jax.ShapeDtypeStruct((bsz, SSD_CONV_W - 1, SSD_CONV_DIM), F32),
                   jax.ShapeDtypeStruct((bsz, gn, hp), F32)],
        scratch_shapes=[pltpu.VMEM((nb, SSD_CHUNK + 8, SSD_CONV_DIM), F32),
                        pltpu.VMEM((nb, gn, hp), F32)],
        compiler_params=_cparams("arbitrary", "arbitrary"),
        name="ssd_scan",
    )(slab, conv0, h0, *consts)


def _rwkv_token_math(rw, prev, mu, lora_w, w0, a0, k_k, k_a, head_ones):
    mix = rw + mu * (prev - rw)
    r = mix[:, 0:RWKV_W]
    k = mix[:, RWKV_W:2 * RWKV_W]
    v = mix[:, 2 * RWKV_W:3 * RWKV_W]
    lora = mix[:, 3 * RWKV_W:]
    lane = _iota(lora.shape, 1)
    act = jnp.where(lane < 32, jnp.tanh(lora), jnp.where(lane < 64, lora, _sigmoid(lora)))
    lo = _mm(act, lora_w)
    w_log = -_softplus(-(w0 + lo[:, 0:RWKV_W])) - 0.5
    logw = -jnp.exp(w_log)
    aic = _sigmoid(a0 + lo[:, RWKV_W:2 * RWKV_W])
    gate = lo[:, 2 * RWKV_W:]
    kkf = k * k_k
    kk = kkf * lax.rsqrt(_mm_xsel(kkf * kkf, head_ones) + 1e-12)
    k2 = k * (1.0 + (aic - 1.0) * k_a)
    return r, k2, v, logw, aic, gate, kk


def _rwkv_finish(o, r, k2, v, gate, r_k, ln_g, ln_b, head_ones, head_avg):
    mu = _mm_xsel(o, head_avg)
    d = o - mu
    var = _mm_xsel(d * d, head_avg)
    on = d * lax.rsqrt(var + RWKV_GN_EPS) * ln_g + ln_b
    bonus = _mm_xsel(r * k2 * r_k, head_ones) * v
    return (on + bonus) * gate


def _rwkv_kernel(rw_ref, shift0_ref, s0_ref, mu_ref, lw_ref, w0_ref, a0_ref, kk_ref, ka_ref,
                 rk_ref, lng_ref, lnb_ref, ones_ref, avg_ref, tri_ref, strict_ref, incl_ref, eye_ref,
                 y_ref, shift_out_ref, s_out_ref, ext_ref, s_ref):
    c = pl.program_id(1)
    n_chunks = pl.num_programs(1)
    ch = RWKV_CHUNK
    nh = RWKV_H
    w = RWKV_W
    head_bd = ones_ref[...]
    head_bd_f = head_bd.astype(F32)
    tile4 = lambda x: jnp.concatenate([x] * nh, axis=0)

    nb = rw_ref.shape[0]

    @pl.when(c == 0)
    def _():
        for j in range(nb):
            ext_ref[j, 0:8, :] = jnp.zeros((8, RWKV_FEAT), F32)
            ext_ref[j, 7:8, :] = shift0_ref[j]
            s_ref[j] = s0_ref[j]

    def chain(j):
        rw = rw_ref[j]
        ext_ref[j, 8:8 + ch, :] = rw
        prev = ext_ref[j, 7:7 + ch, :]
        ext_ref[j, 7:8, :] = rw[ch - 1:ch, :]

        r, k2, v, logw, aic, gate, kk = _rwkv_token_math(
            rw, prev, mu_ref[...], lw_ref[...], w0_ref[...], a0_ref[...], kk_ref[...], ka_ref[...], head_bd)
        yield

        cumw = _mm_sel(tri_ref[...], logw)
        yield
        last = cumw[ch - 1:ch, :]
        inv_g = jnp.exp(-cumw)
        to_end = jnp.exp(last - cumw)
        kka = kk * aic
        a_t = (-kk * jnp.exp(cumw - logw)).astype(BF16)
        r_t = (r * jnp.exp(cumw)).astype(BF16)
        b_t = (kka * inv_g).astype(BF16)
        k_t = (k2 * inv_g).astype(BF16)
        bk_end = jnp.concatenate([kka * to_end, k2 * to_end], axis=0).astype(BF16)

        bd = lambda x: tile4(x.astype(BF16)) * head_bd
        nt = lambda x, y: lax.dot_general(x, y, (((1,), (1,)), ((), ())), preferred_element_type=F32)
        vb = v.astype(BF16)
        ar = jnp.concatenate([a_t, r_t], axis=0)
        prod = nt(ar, jnp.concatenate([bd(b_t), bd(k_t)], axis=0))
        a_ab = prod[0:ch, 0:w] * strict_ref[...]
        a_ak = prod[0:ch, w:2 * w] * strict_ref[...]
        a_rbk = (prod[ch:2 * ch, :] * incl_ref[...]).astype(BF16)
        yield
        s_bd = s_ref[j]
        uo = nt(ar, s_bd.astype(BF16))
        rhs = uo[0:ch, :] + jnp.dot(a_ak.astype(BF16), bd(vb), preferred_element_type=F32)
        yield

        t_inv = eye_ref[...] + a_ab
        x = a_ab
        x_bd = bd(x)
        power = 1
        while 2 * power < ch:
            x = jnp.dot(x.astype(BF16), x_bd, preferred_element_type=F32)
            x_bd = bd(x)
            yield
            t_inv = t_inv + jnp.dot(t_inv.astype(BF16), x_bd, preferred_element_type=F32)
            yield
            power *= 2

        p = jnp.dot(t_inv.astype(BF16), bd(rhs), preferred_element_type=F32)
        yield
        pb = p.astype(BF16)
        o = uo[ch:2 * ch, :] + jnp.dot(a_rbk, jnp.concatenate([bd(pb), bd(vb)], axis=0),
                                       preferred_element_type=F32)
        pv = jnp.concatenate([pb, vb], axis=0)
        yield

        upd = jnp.dot(pv.astype(F32).T.astype(BF16), bk_end, preferred_element_type=F32)
        s_ref[j] = s_bd * jnp.exp(last) + upd * head_bd_f
        yield

        y_ref[j] = _rwkv_finish(o, r, k2, v, gate, rk_ref[...], lng_ref[...], lnb_ref[...],
                                head_bd, avg_ref[...]).astype(y_ref.dtype)

    _interleave([chain(j) for j in range(nb)], skew=RWKV_SKEW)

    @pl.when(c == n_chunks - 1)
    def _():
        for j in range(nb):
            shift_out_ref[j] = ext_ref[j, 7:8, :]
            s_out_ref[j] = s_ref[j]


def _rwkv_consts():
    ch = RWKV_CHUNK
    assert ch == RWKV_N, "bd() reuses the head mask, which needs chunk rows == head width"
    t = jnp.arange(ch)[:, None]
    j = jnp.arange(2 * RWKV_W)[None, :] % ch
    strict = (j[:, :RWKV_W] < t).astype(F32)
    incl = (j <= t).astype(F32)
    eye = (j[:, :RWKV_W] == t).astype(F32)
    return (_seg_ones(RWKV_W, RWKV_N), _seg_ones(RWKV_W, RWKV_N, 1.0 / RWKV_N), _tri_incl(ch),
            strict, incl, eye)


def _rwkv_prompt(rw, shift0, s0_bd, *params):
    bsz, length, _ = rw.shape
    nb = RWKV_NB if bsz % RWKV_NB == 0 else 1
    consts = tuple(params) + _rwkv_consts()
    return pl.pallas_call(
        _rwkv_kernel,
        grid=(bsz // nb, length // RWKV_CHUNK),
        in_specs=[pl.BlockSpec((nb, RWKV_CHUNK, RWKV_FEAT), lambda b, c: (b, c, 0)),
                  pl.BlockSpec((nb, 1, RWKV_FEAT), lambda b, c: (b, 0, 0)),
                  pl.BlockSpec((nb, RWKV_W, RWKV_W), lambda b, c: (b, 0, 0))]
                 + [_const_spec(a) for a in consts],
        out_specs=[pl.BlockSpec((nb, RWKV_CHUNK, RWKV_W), lambda b, c: (b, c, 0)),
                   pl.BlockSpec((nb, 1, RWKV_FEAT), lambda b, c: (b, 0, 0)),
                   pl.BlockSpec((nb, RWKV_W, RWKV_W), lambda b, c: (b, 0, 0))],
        out_shape=[jax.ShapeDtypeStruct((bsz, length, RWKV_W), BF16),
                   jax.ShapeDtypeStruct((bsz, 1, RWKV_FEAT), F32),
                   jax.ShapeDtypeStruct((bsz, RWKV_W, RWKV_W), F32)],
        scratch_shapes=[pltpu.VMEM((nb, RWKV_CHUNK + 8, RWKV_FEAT), F32),
                        pltpu.VMEM((nb, RWKV_W, RWKV_W), F32)],
        compiler_params=_cparams("arbitrary", "arbitrary"),
        name="rwkv_scan",
    )(rw, shift0, s0_bd, *consts)


def _gla_gate_log(small, gk2_pad, b_gk):
    x = _mm_hi(small, gk2_pad) + b_gk
    return -_softplus(-x) / GATE_NORMALIZER


def _gla_finish(o, gg, norm_g, head_avg):
    ms = _mm_xsel(o * o, head_avg)
    return o * lax.rsqrt(ms + RMS_EPS) * norm_g * _silu(gg)


def _gla_kernel(slab_ref, s0_ref, gk2_ref, bgk_ref, ng_ref, avg_ref, tri_ref, expand_ref,
                y_ref, s_out_ref, s_ref):
    c = pl.program_id(1)
    n_chunks = pl.num_programs(1)
    ch = GLA_CHUNK
    sub = GLA_SUB
    hk = GLA_HK
    expand = expand_ref[...]
    expand_f = expand.astype(F32)
    pos = _iota((ch, 1), 0) % sub
    t_idx = _iota((ch, 1), 0)
    hmask_k = (_iota((GLA_H * sub, hk), 0) // sub == _iota((GLA_H * sub, hk), 1) // GLA_DK).astype(F32)
    hmask_v = (_iota((GLA_H * sub, GLA_W), 0) // sub == _iota((GLA_H * sub, GLA_W), 1) // GLA_DV).astype(F32)

    nb = slab_ref.shape[0]

    @pl.when(c == 0)
    def _():
        s_ref[...] = s0_ref[...]

    def chain(j):
        q = slab_ref[j, :, 0:hk] * (GLA_DK ** -0.5)
        k = slab_ref[j, :, hk:2 * hk]
        v = slab_ref[j, :, 2 * hk:2 * hk + GLA_W]
        gg = slab_ref[j, :, 2 * hk + GLA_W:2 * hk + 2 * GLA_W]
        small = slab_ref[j, :, 2 * hk + 2 * GLA_W:]

        lg = _gla_gate_log(small, gk2_ref[...], bgk_ref[...])
        yield
        cum = _mm_sel(tri_ref[...], lg)
        yield
        last = cum[ch - 1:ch, :]
        s_bd = s_ref[j]

        o = _mm(q * jnp.exp(cum), s_bd)

        group = 4
        for d0 in range(0, sub, group):
            prods, v_shift = [], []
            for delta in range(d0, d0 + group):
                if delta == 0:
                    k_s, c_s, v_s = k, cum, v
                else:
                    k_s = pltpu.roll(k, delta, 0)
                    c_s = pltpu.roll(cum, delta, 0)
                    v_s = pltpu.roll(v, delta, 0)
                valid = pos >= delta
                w_pair = jnp.exp(jnp.where(valid, cum - c_s, 0.0))
                prods.append(jnp.where(valid, q * k_s * w_pair, 0.0).astype(BF16))
                v_shift.append(v_s)
            yield
            att = jnp.dot(jnp.concatenate(prods, axis=0), expand, preferred_element_type=F32)
            o = o + sum(att[g * ch:(g + 1) * ch, :] * v_shift[g] for g in range(group))
            yield

        vb = v.astype(BF16)
        atts = []
        for blk in range(1, ch // sub):
            lo = blk * sub
            ref_pt = cum[lo - 1:lo, :]
            qd = q[lo:lo + sub, :] * jnp.exp(cum[lo:lo + sub, :] - ref_pt)
            kp = k * jnp.exp(jnp.where(t_idx < lo, ref_pt - cum, -jnp.inf))
            q4 = jnp.concatenate([qd] * GLA_H, axis=0) * hmask_k
            atts.append(_mm_nt(q4, kp).astype(BF16))
            yield
        o4 = jnp.dot(jnp.concatenate(atts, axis=0), vb, preferred_element_type=F32)
        yield
        cross = [jnp.zeros((sub, GLA_W), F32)]
        for blk in range(ch // sub - 1):
            o4_b = o4[blk * GLA_H * sub:(blk + 1) * GLA_H * sub, :] * hmask_v
            cross.append(sum(o4_b[h * sub:(h + 1) * sub, :] for h in range(GLA_H)))
        o = o + jnp.concatenate(cross, axis=0)

        kt_t = (k * jnp.exp(last - cum)).T
        s_ref[j] = s_bd * _col(jnp.exp(last)) + _mm(kt_t, vb) * expand_f
        yield

        y_ref[j] = _gla_finish(o, gg, ng_ref[...], avg_ref[...]).astype(y_ref.dtype)

    _interleave([chain(j) for j in range(nb)])

    @pl.when(c == n_chunks - 1)
    def _():
        s_out_ref[...] = s_ref[...]


def _gla_prompt(slab, s0_bd, gk2_pad, b_gk, ng):
    bsz, length, _ = slab.shape
    nb = GLA_NB if bsz % GLA_NB == 0 else 1
    expand = (jnp.arange(GLA_HK)[:, None] // GLA_DK == jnp.arange(GLA_W)[None, :] // GLA_DV).astype(BF16)
    consts = (gk2_pad, b_gk, ng, _seg_ones(GLA_W, GLA_DV, 1.0 / GLA_DV), _tri_incl(GLA_CHUNK), expand)
    return pl.pallas_call(
        _gla_kernel,
        grid=(bsz // nb, length // GLA_CHUNK),
        in_specs=[pl.BlockSpec((nb, GLA_CHUNK, GLA_SLAB), lambda b, c: (b, c, 0)),
                  pl.BlockSpec((nb, GLA_HK, GLA_W), lambda b, c: (b, 0, 0))]
                 + [_const_spec(a) for a in consts],
        out_specs=[pl.BlockSpec((nb, GLA_CHUNK, GLA_W), lambda b, c: (b, c, 0)),
                   pl.BlockSpec((nb, GLA_HK, GLA_W), lambda b, c: (b, 0, 0))],
        out_shape=[jax.ShapeDtypeStruct((bsz, length, GLA_W), BF16),
                   jax.ShapeDtypeStruct((bsz, GLA_HK, GLA_W), F32)],
        scratch_shapes=[pltpu.VMEM((nb, GLA_HK, GLA_W), F32)],
        compiler_params=_cparams("arbitrary", "arbitrary"),
        name="gla_scan",
    )(slab, s0_bd, *consts)


def _col(row):
    n = row.shape[1]
    eye = _iota((n, n), 0) == _iota((n, n), 1)
    return jnp.sum(jnp.where(eye, row, 0.0), axis=1, keepdims=True)


def _store_t(dst_ref, row0, x):
    for cblk in range(x.shape[1] // LANES):
        dst_ref[row0 + cblk * LANES:row0 + (cblk + 1) * LANES, :] = x[:, cblk * LANES:(cblk + 1) * LANES].T


def _load_t(src_ref, width):
    return jnp.concatenate([src_ref[cblk * LANES:(cblk + 1) * LANES, :].T for cblk in range(width // LANES)],
                           axis=1)


_TS_X, _TS_B, _TS_C, _TS_DT, _TS_DA = 0, SSD_W, SSD_W + SSD_G * SSD_N, SSD_CONV_DIM, SSD_CONV_DIM + SMALL_W
_TR_R, _TR_W, _TR_K, _TR_V, _TR_KK, _TR_KA = (i * RWKV_W for i in range(6))
_TG_Q, _TG_K, _TG_G, _TG_V = 0, GLA_HK, 2 * GLA_HK, 3 * GLA_HK


def _step_kernel(ssd_ref, rw_ref, gla_ref, conv0_ref, shift0_ref, h0_ref, rs0_ref, gs0_ref,
                 cw_ref, cb_ref, dtb_ref, alog_ref, dexp_ref, sng_ref,
                 mu_ref, lw_ref, w0_ref, a0_ref, kk_ref, ka_ref, rk_ref, lng_ref, lnb_ref,
                 gk2_ref, bgk_ref, gng_ref, ones_ref, ravg_ref, gavg_ref,
                 y_ssd_ref, y_rwkv_ref, y_gla_ref, conv_out_ref, h_out_ref, rs_out_ref, gs_out_ref,
                 t_ssd, t_rwkv, t_gla, o_ssd_t, o_rwkv_t, o_gla_t):
    i = pl.program_id(0)
    hk = GLA_HK

    def ssd_tokens():
        xbc = ssd_ref[:, SSD_W:SSD_W + SSD_CONV_DIM]
        taps = [conv0_ref[:, t * SSD_CONV_DIM:(t + 1) * SSD_CONV_DIM] for t in range(SSD_CONV_W - 1)]
        return xbc, _ssd_token_math(xbc, ssd_ref[:, SSD_W + SSD_CONV_DIM:], taps, cw_ref[...], cb_ref[...],
                                    dtb_ref[...], alog_ref[...])

    def rwkv_tokens():
        return _rwkv_token_math(rw_ref[...], shift0_ref[...], mu_ref[...], lw_ref[...], w0_ref[...],
                                a0_ref[...], kk_ref[...], ka_ref[...], ones_ref[...])

    @pl.when(i == 0)
    def _():
        xbc, (act, dt, dta) = ssd_tokens()
        conv_out_ref[:, 0:2 * SSD_CONV_DIM] = conv0_ref[:, SSD_CONV_DIM:]
        conv_out_ref[:, 2 * SSD_CONV_DIM:] = xbc
        _store_t(t_ssd, _TS_X, act)
        _store_t(t_ssd, _TS_DT, dt)
        _store_t(t_ssd, _TS_DA, jnp.exp(dta))
        r, k2, v, logw, aic, gate, kk = rwkv_tokens()
        for off, x in ((_TR_R, r), (_TR_W, jnp.exp(logw)), (_TR_K, k2), (_TR_V, v), (_TR_KK, kk),
                       (_TR_KA, kk * aic)):
            _store_t(t_rwkv, off, x)
        lg = _gla_gate_log(gla_ref[:, 2 * hk + 2 * GLA_W:], gk2_ref[...], bgk_ref[...])
        _store_t(t_gla, _TG_Q, gla_ref[:, 0:hk] * (GLA_DK ** -0.5))
        _store_t(t_gla, _TG_K, gla_ref[:, hk:2 * hk])
        _store_t(t_gla, _TG_G, jnp.exp(lg))
        _store_t(t_gla, _TG_V, gla_ref[:, 2 * hk:2 * hk + GLA_W])
        o_gla_t[...] = jnp.zeros_like(o_gla_t)

    def row(ref, idx):
        return ref[pl.ds(idx, 1), :]

    def rows(ref, idx, count):
        return ref[pl.ds(pl.multiple_of(idx, count), count), :]

    units = h0_ref.shape[0] // SSD_N
    per_head = SSD_P // units
    h = i // per_head
    p0 = (i % per_head) * units
    b_t = rows(t_ssd, _TS_B + (h // (SSD_H // SSD_G)) * SSD_N, SSD_N)
    c_t = rows(t_ssd, _TS_C + (h // (SSD_H // SSD_G)) * SSD_N, SSD_N)
    da_row = row(t_ssd, _TS_DA + DT_OFF + h)
    dt_row = row(t_ssd, _TS_DT + DT_OFF + h)
    for u in range(units):
        p_idx = h * SSD_P + p0 + u
        blk = slice(u * SSD_N, (u + 1) * SSD_N)
        h_new = h0_ref[blk, :] * da_row + (row(t_ssd, _TS_X + p_idx) * dt_row) * b_t
        o_ssd_t[pl.ds(p_idx, 1), :] = jnp.sum(h_new * c_t, axis=0, keepdims=True)
        h_out_ref[blk, :] = h_new

    units = rs0_ref.shape[0] // RWKV_N
    per_head = RWKV_N // units
    h = i // per_head
    v0 = (i % per_head) * units
    seg = lambda off: rows(t_rwkv, off + h * RWKV_N, RWKV_N)
    r_t, w_t, k_t, kk_t, ka_t = seg(_TR_R), seg(_TR_W), seg(_TR_K), seg(_TR_KK), seg(_TR_KA)
    for u in range(units):
        v_idx = h * RWKV_N + v0 + u
        blk = slice(u * RWKV_N, (u + 1) * RWKV_N)
        s = rs0_ref[blk, :]
        sa = jnp.sum(s * (-kk_t), axis=0, keepdims=True)
        s_new = s * w_t + sa * ka_t + row(t_rwkv, _TR_V + v_idx) * k_t
        o_rwkv_t[pl.ds(v_idx, 1), :] = jnp.sum(s_new * r_t, axis=0, keepdims=True)
        rs_out_ref[blk, :] = s_new

    units = gs0_ref.shape[0] // GLA_DV
    per_head = GLA_DK // units
    h = i // per_head
    k0 = (i % per_head) * units
    v_t = rows(t_gla, _TG_V + h * GLA_DV, GLA_DV)
    acc = jnp.zeros((GLA_DV, LANES), F32)
    for u in range(units):
        k_idx = h * GLA_DK + k0 + u
        blk = slice(u * GLA_DV, (u + 1) * GLA_DV)
        s_new = gs0_ref[blk, :] * row(t_gla, _TG_G + k_idx) + row(t_gla, _TG_K + k_idx) * v_t
        acc = acc + row(t_gla, _TG_Q + k_idx) * s_new
        gs_out_ref[blk, :] = s_new
    o_rows = pl.ds(pl.multiple_of(h * GLA_DV, GLA_DV), GLA_DV)
    o_gla_t[o_rows, :] = o_gla_t[o_rows, :] + acc

    @pl.when(i == pl.num_programs(0) - 1)
    def _():
        _, (act, _, _) = ssd_tokens()
        y_ssd_ref[...] = _ssd_finish(_load_t(o_ssd_t, SSD_W), act[:, 0:SSD_W], ssd_ref[:, 0:SSD_W],
                                     dexp_ref[...], sng_ref[...]).astype(y_ssd_ref.dtype)
        r, k2, v, _, _, gate, _ = rwkv_tokens()
        y_rwkv_ref[...] = _rwkv_finish(_load_t(o_rwkv_t, RWKV_W), r, k2, v, gate, rk_ref[...], lng_ref[...],
                                       lnb_ref[...], ones_ref[...], ravg_ref[...]).astype(y_rwkv_ref.dtype)
        y_gla_ref[...] = _gla_finish(_load_t(o_gla_t, GLA_W), gla_ref[:, 2 * hk + GLA_W:2 * hk + 2 * GLA_W],
                                     gng_ref[...], gavg_ref[...]).astype(y_gla_ref.dtype)


def _mixer_step(ssd_slab, rw, gla_slab, conv0, shift0, h0, rs0, gs0, ssd_p, rwkv_p, gla_p):
    bsz = ssd_slab.shape[0]
    assert bsz == LANES, "the single-token mixer keeps exactly one vreg row of batch entries on the lanes"
    flat_t = lambda s: s.reshape(bsz, -1).T
    h0, rs0, gs0 = flat_t(h0), flat_t(rs0), flat_t(gs0)
    consts = (tuple(ssd_p) + tuple(rwkv_p) + tuple(gla_p)
              + (_seg_ones(RWKV_W, RWKV_N), _seg_ones(RWKV_W, RWKV_N, 1.0 / RWKV_N),
                 _seg_ones(GLA_W, GLA_DV, 1.0 / GLA_DV)))
    full = lambda a: pl.BlockSpec(a.shape, lambda i: (0,) * a.ndim)
    cols = lambda a: pl.BlockSpec((a.shape[0] // STEP_GRID, bsz), lambda i: (i, 0))
    tokens = (ssd_slab, rw, gla_slab, conv0, shift0)
    outs = pl.pallas_call(
        _step_kernel,
        grid=(STEP_GRID,),
        in_specs=[full(a) for a in tokens] + [cols(h0), cols(rs0), cols(gs0)] + [full(a) for a in consts],
        out_specs=[full(jax.ShapeDtypeStruct((bsz, w), BF16)) for w in (SSD_W, RWKV_W, GLA_W)]
                  + [full(conv0), cols(h0), cols(rs0), cols(gs0)],
        out_shape=[jax.ShapeDtypeStruct((bsz, SSD_W), BF16),
                   jax.ShapeDtypeStruct((bsz, RWKV_W), BF16),
                   jax.ShapeDtypeStruct((bsz, GLA_W), BF16),
                   jax.ShapeDtypeStruct(conv0.shape, F32),
                   jax.ShapeDtypeStruct(h0.shape, F32),
                   jax.ShapeDtypeStruct(rs0.shape, F32),
                   jax.ShapeDtypeStruct(gs0.shape, F32)],
        scratch_shapes=[pltpu.VMEM((SSD_CONV_DIM + 2 * SMALL_W, LANES), F32),
                        pltpu.VMEM((6 * RWKV_W, LANES), F32),
                        pltpu.VMEM((3 * GLA_HK + GLA_W, LANES), F32),
                        pltpu.VMEM((SSD_W, LANES), F32),
                        pltpu.VMEM((RWKV_W, LANES), F32),
                        pltpu.VMEM((GLA_W, LANES), F32)],
        compiler_params=_cparams("arbitrary"),
        name="mixer_step",
    )(*tokens, h0, rs0, gs0, *consts)
    y_ssd, y_rwkv, y_gla, conv_new, h_new, rs_new, gs_new = outs
    return (y_ssd, y_rwkv, y_gla, conv_new.reshape(bsz, SSD_CONV_W - 1, SSD_CONV_DIM),
            h_new.T.reshape(bsz, SSD_H, SSD_P, SSD_N), rs_new.T.reshape(bsz, RWKV_H, RWKV_N, RWKV_N),
            gs_new.T.reshape(bsz, GLA_H, GLA_DK, GLA_DV))


def _outproj_kernel(ys_ref, yr_ref, yg_ref, x_ref, gt_ref, w_ref, g_ref, b_ref, o_ref):
    m = (jnp.dot(ys_ref[0], w_ref[0:SSD_W, :], preferred_element_type=F32)
         + jnp.dot(yr_ref[0], w_ref[SSD_W:SSD_W + RWKV_W, :], preferred_element_type=F32)
         + jnp.dot(yg_ref[0], w_ref[SSD_W + RWKV_W:, :], preferred_element_type=F32))
    u = ALPHA * x_ref[0] + (1.0 + gt_ref[0]) * m
    o_ref[0] = _layernorm(u, g_ref[...], b_ref[...])


def _outproj(y_ssd, y_rwkv, y_gla, x, gt, w_out, ln_g, ln_b, tm):
    bsz, length, _ = x.shape
    tok = lambda width: pl.BlockSpec((1, tm, width), lambda b, i: (b, i, 0))
    vec = pl.BlockSpec((1, D_MODEL), lambda b, i: (0, 0))
    return pl.pallas_call(
        _outproj_kernel,
        grid=(bsz, length // tm),
        in_specs=[tok(SSD_W), tok(RWKV_W), tok(GLA_W), tok(D_MODEL), _mod_spec(gt, tm),
                  pl.BlockSpec((D_MODEL, D_MODEL), lambda b, i: (0, 0)), vec, vec],
        out_specs=tok(D_MODEL),
        out_shape=jax.ShapeDtypeStruct((bsz, length, D_MODEL), F32),
        compiler_params=_cparams("arbitrary", "arbitrary"),
        name="outproj_ln",
    )(y_ssd, y_rwkv, y_gla, x, gt, w_out, ln_g, ln_b)


def _ffn_kernel(x_ref, sc_ref, sh_ref, gt_ref, wg_ref, wu_ref, wd_ref, g_ref, b_ref, o_ref, h_ref, acc_ref):
    f = pl.program_id(2)

    @pl.when(f == 0)
    def _():
        h_ref[...] = (x_ref[0] * (1.0 + sc_ref[0]) + sh_ref[0]).astype(BF16)
        acc_ref[...] = jnp.zeros_like(acc_ref)

    h = h_ref[...]
    a = _silu(jnp.dot(h, wg_ref[...], preferred_element_type=F32)) * jnp.dot(
        h, wu_ref[...], preferred_element_type=F32)
    acc_ref[...] += jnp.dot(a.astype(BF16), wd_ref[...], preferred_element_type=F32)

    @pl.when(f == pl.num_programs(2) - 1)
    def _():
        u = ALPHA * x_ref[0] + (1.0 + gt_ref[0]) * acc_ref[...]
        o_ref[0] = _layernorm(u, g_ref[...], b_ref[...])


def _ffn(x, sc, sh, gt, wg, wu, wd, ln_g, ln_b, tm, tf):
    bsz, length, _ = x.shape
    tok = pl.BlockSpec((1, tm, D_MODEL), lambda b, i, f: (b, i, 0))
    vec = pl.BlockSpec((1, D_MODEL), lambda b, i, f: (0, 0))
    return pl.pallas_call(
        _ffn_kernel,
        grid=(bsz, length // tm, F_DENSE // tf),
        in_specs=[tok, _mod_spec(sc, tm), _mod_spec(sh, tm), _mod_spec(gt, tm),
                  pl.BlockSpec((D_MODEL, tf), lambda b, i, f: (0, f)),
                  pl.BlockSpec((D_MODEL, tf), lambda b, i, f: (0, f)),
                  pl.BlockSpec((tf, D_MODEL), lambda b, i, f: (f, 0)), vec, vec],
        out_specs=tok,
        out_shape=jax.ShapeDtypeStruct((bsz, length, D_MODEL), F32),
        scratch_shapes=[pltpu.VMEM((tm, D_MODEL), BF16), pltpu.VMEM((tm, D_MODEL), F32)],
        compiler_params=_cparams("arbitrary", "arbitrary", "arbitrary"),
        name="ffn_ln",
    )(x, sc, sh, gt, wg, wu, wd, ln_g, ln_b)


def _moe_kernel(x_ref, sc_ref, sh_ref, gt_ref, rt_ref, wg_ref, wu_ref, wd_ref, g_ref, b_ref, tri_ref, o_ref,
                h_ref, comb_ref, rank_ref, rank_t_ref, cnt_ref, acc_ref, *, half, cap):
    e = pl.program_id(2)
    tm = 2 * half

    @pl.when(e == 0)
    def _():
        h = x_ref[0] * (1.0 + sc_ref[0]) + sh_ref[0]
        h_ref[...] = h.astype(BF16)
        acc_ref[...] = jnp.zeros_like(acc_ref)
        lane = _iota((tm, LANES), 1).astype(F32)
        logits = jnp.where(lane < N_EXPERTS, _mm_hi(h, rt_ref[...]), -jnp.inf)
        m1 = jnp.max(logits, axis=-1, keepdims=True)
        i1 = jnp.min(jnp.where(logits == m1, lane, float(LANES)), axis=-1, keepdims=True)
        rest = jnp.where(lane == i1, -jnp.inf, logits)
        m2 = jnp.max(rest, axis=-1, keepdims=True)
        i2 = jnp.min(jnp.where(rest == m2, lane, float(LANES)), axis=-1, keepdims=True)
        e2 = jnp.exp(m2 - m1)
        den = 1.0 + e2
        comb_ref[...] = jnp.where(lane == i1, 1.0 / den, 0.0) + jnp.where(lane == i2, e2 / den, 0.0)
        sel = (lane == i1) | (lane == i2)
        sel_f = sel.astype(F32)
        for hf in range(2):
            rows = slice(hf * half, (hf + 1) * half)
            s = sel_f[rows, :]
            before = jnp.dot(tri_ref[...], s.astype(BF16), preferred_element_type=F32)
            rank_ref[rows, :] = jnp.where(sel[rows, :], before, -1.0)
            cnt_ref[hf:hf + 1, :] = jnp.sum(s, axis=0, keepdims=True)
        for blk in range(tm // LANES):
            rows = slice(blk * LANES, (blk + 1) * LANES)
            rank_t_ref[:, rows] = rank_ref[rows, :].T

    on_lane = _iota((tm, LANES), 1) == e
    comb_e = jnp.sum(jnp.where(on_lane, comb_ref[...], 0.0), axis=-1, keepdims=True)
    rank_e = jnp.sum(jnp.where(on_lane, rank_ref[...], 0.0), axis=-1, keepdims=True)
    rank_t_e = rank_t_ref[pl.ds(e, 1), :]
    n_max = jnp.max(jnp.where(_iota((2, LANES), 1) == e, cnt_ref[...], 0.0))
    n_pass = (n_max.astype(jnp.int32) + (cap - 1)) // cap

    def one_pass(p, carry):
        off = (p * cap).astype(F32)
        slot_col = _iota((cap, 1), 0).astype(F32) + off
        slot_row = _iota((1, cap), 1).astype(F32) + off
        xs = []
        for hf in range(2):
            rows = slice(hf * half, (hf + 1) * half)
            pick = (rank_t_e[:, rows] == slot_col).astype(BF16)
            xs.append(jnp.dot(pick, h_ref[rows, :], preferred_element_type=F32).astype(BF16))
        xc = jnp.concatenate(xs, axis=0)
        a = _silu(jnp.dot(xc, wg_ref[0], preferred_element_type=F32)) * jnp.dot(
            xc, wu_ref[0], preferred_element_type=F32)
        out_e = jnp.dot(a.astype(BF16), wd_ref[0], preferred_element_type=F32)
        for hf in range(2):
            rows = slice(hf * half, (hf + 1) * half)
            put = (rank_e[rows, :] == slot_row).astype(BF16)
            back = jnp.dot(put, out_e[hf * cap:(hf + 1) * cap, :].astype(BF16), preferred_element_type=F32)
            acc_ref[rows, :] += comb_e[rows, :] * back
        return carry

    lax.fori_loop(0, n_pass, one_pass, 0)

    @pl.when(e == pl.num_programs(2) - 1)
    def _():
        u = ALPHA * x_ref[0] + (1.0 + gt_ref[0]) * acc_ref[...]
        o_ref[0] = _layernorm(u, g_ref[...], b_ref[...])


def _moe(x, sc, sh, gt, router_pad, wg, wu, wd, ln_g, ln_b, tm):
    bsz, length, _ = x.shape
    assert tm % LANES == 0
    half = tm // 2
    cap = MOE_CAP_ROWS if half >= 2 * MOE_CAP_ROWS else half
    tri = (jnp.arange(half)[None, :] < jnp.arange(half)[:, None]).astype(BF16)
    tok = pl.BlockSpec((1, tm, D_MODEL), lambda b, i, e: (b, i, 0))
    vec = pl.BlockSpec((1, D_MODEL), lambda b, i, e: (0, 0))
    wspec = pl.BlockSpec((1, D_MODEL, D_MODEL), lambda b, i, e: (e, 0, 0))
    return pl.pallas_call(
        functools.partial(_moe_kernel, half=half, cap=cap),
        grid=(bsz, length // tm, N_EXPERTS),
        in_specs=[tok, _mod_spec(sc, tm), _mod_spec(sh, tm), _mod_spec(gt, tm),
                  pl.BlockSpec((D_MODEL, LANES), lambda b, i, e: (0, 0)),
                  wspec, wspec, wspec, vec, vec, _const_spec(tri)],
        out_specs=tok,
        out_shape=jax.ShapeDtypeStruct((bsz, length, D_MODEL), F32),
        scratch_shapes=[pltpu.VMEM((tm, D_MODEL), BF16), pltpu.VMEM((tm, LANES), F32),
                        pltpu.VMEM((tm, LANES), F32), pltpu.VMEM((LANES, tm), F32),
                        pltpu.VMEM((2, LANES), F32), pltpu.VMEM((tm, D_MODEL), F32)],
        compiler_params=_cparams("arbitrary", "arbitrary", "arbitrary"),
        name="moe_ln",
    )(x, sc, sh, gt, router_pad, wg, wu, wd, ln_g, ln_b, tri)


def _pad_lanes(vec, offset, width=SMALL_W):
    out = jnp.zeros((1, width), F32)
    return out.at[0, offset:offset + vec.shape[0]].set(vec)


def _layer_params(p, l):
    w_in = p["w_in"][l]
    off = [0]
    for s in (SSD_W, SSD_CONV_DIM, SSD_H, RWKV_FEAT, GLA_HK, GLA_HK, GLA_W, GK_LORA, GLA_W):
        off.append(off[-1] + s)
    piece = lambda i: w_in[:, off[i]:off[i + 1]]
    small = jnp.zeros((D_MODEL, SMALL_W), F32)
    small = small.at[:, DT_OFF:DT_OFF + SSD_H].set(piece(2)).at[:, GLO_OFF:GLO_OFF + GK_LORA].set(piece(7))
    w_pad = jnp.concatenate([piece(0), piece(1), small, piece(3), piece(4), piece(5), piece(6), piece(8), small],
                            axis=1).astype(BF16)
    ssd_p = (p["ssd_conv_w"][l], p["ssd_conv_b"][l][None, :],
             _pad_lanes(p["ssd_dt_bias"][l], DT_OFF), _pad_lanes(p["ssd_a_log"][l], DT_OFF),
             jnp.repeat(p["ssd_d"][l], SSD_P)[None, :], p["ssd_norm_g"][l][None, :])
    lora_w = jnp.zeros((LANES, 3 * RWKV_W), F32)
    lora_w = (lora_w.at[0:32, 0:RWKV_W].set(p["rwkv_w2"][l])
              .at[32:64, RWKV_W:2 * RWKV_W].set(p["rwkv_a2"][l])
              .at[64:128, 2 * RWKV_W:].set(p["rwkv_g2"][l])).astype(BF16)
    row = lambda name: p[name][l].reshape(1, -1)
    rwkv_p = (row("rwkv_mu"), lora_w, row("rwkv_w0"), row("rwkv_a0"), row("rwkv_k_k"), row("rwkv_k_a"),
              row("rwkv_r_k"), row("rwkv_ln_g"), row("rwkv_ln_b"))
    gk2_pad = jnp.zeros((SMALL_W, GLA_HK), F32).at[GLO_OFF:GLO_OFF + GK_LORA].set(p["gla_w_gk2"][l])
    gla_p = (gk2_pad, row("gla_b_gk"), jnp.tile(p["gla_norm_g"][l], GLA_H)[None, :])
    return w_pad, ssd_p, rwkv_p, gla_p


def _block_diag_inv(s_bd, h):
    b, hr, hc = s_bd.shape
    r, c = hr // h, hc // h
    s = s_bd.reshape(b, h, r, h, c)
    return jnp.stack([s[:, i, :, i, :] for i in range(h)], axis=1)


def _ssd_state_unpack(h_bd):
    b = h_bd.shape[0]
    s = h_bd.reshape(b, SSD_G, SSD_N, SSD_H, SSD_P)
    per_group = SSD_H // SSD_G
    heads = [s[:, h // per_group, :, h, :] for h in range(SSD_H)]
    return jnp.swapaxes(jnp.stack(heads, axis=1), 2, 3)


def _tail(x, mod_l, l, p, y_ssd, y_rwkv, y_gla, tm):
    sh1, sc1, gt1, sh2, sc2, gt2 = mod_l
    row = lambda name: p[name][l].reshape(1, -1)
    x = _outproj(y_ssd, y_rwkv, y_gla, x, gt1, p["w_out"][l].astype(BF16), row("ln_mix_g"), row("ln_mix_b"), tm)
    i = l // 2
    if l % 2 == 0:
        tf = F_DENSE // 2
        x = _ffn(x, sc2, sh2, gt2, p["ffn_w_gate"][i].astype(BF16), p["ffn_w_up"][i].astype(BF16),
                 p["ffn_w_down"][i].astype(BF16), row("ln_ffn_g"), row("ln_ffn_b"), tm, tf)
    else:
        router_pad = jnp.zeros((D_MODEL, LANES), F32).at[:, :N_EXPERTS].set(p["moe_router"][i])
        x = _moe(x, sc2, sh2, gt2, router_pad, p["moe_w_gate"][i].astype(BF16), p["moe_w_up"][i].astype(BF16),
                 p["moe_w_down"][i].astype(BF16), row("ln_ffn_g"), row("ln_ffn_b"), min(2 * tm, x.shape[1]))
    return x


def _forward(x_prompt, x_sample, c_prompt, c_sample, states, p):
    bp, seq, _ = x_prompt.shape
    bs = x_sample.shape[0]
    state_ssd, state_conv, state_rwkv, state_shift, state_gla = states
    mod = _ada(jnp.concatenate([c_prompt, c_sample], axis=0), p["w_ada"], p["b_ada"])

    xp = x_prompt
    xs = x_sample.reshape(1, bs, D_MODEL)
    tm_p = min(512, seq)
    outs_p = [[] for _ in range(5)]
    outs_s = [[] for _ in range(5)]
    for l in range(DEPTH):
        w_pad, ssd_p, rwkv_p, gla_p = _layer_params(p, l)
        mods = jnp.split(mod[l], 6, axis=-1)
        mod_p = [m[:bp, None, :] for m in mods]
        mod_s = [m[None, bp:, :] for m in mods]

        ssd_slab, rw, gla_slab = _inproj(xp, mod_p[1], mod_p[0], w_pad, tm_p)
        y_ssd, conv_new, h_new = _ssd_prompt(
            ssd_slab, jnp.zeros((bp, SSD_CONV_W - 1, SSD_CONV_DIM), F32),
            jnp.zeros((bp, SSD_G * SSD_N, SSD_H * SSD_P), F32), *ssd_p)
        h_new = _ssd_state_unpack(h_new)
        y_rwkv, shift_new, rs_bd = _rwkv_prompt(
            rw, jnp.zeros((bp, 1, RWKV_FEAT), F32), jnp.zeros((bp, RWKV_W, RWKV_W), F32), *rwkv_p)
        y_gla, gs_bd = _gla_prompt(gla_slab, jnp.zeros((bp, GLA_HK, GLA_W), F32), *gla_p)
        xp = _tail(xp, mod_p, l, p, y_ssd, y_rwkv, y_gla, tm_p)
        for acc, s in zip(outs_p, (h_new, conv_new, _block_diag_inv(rs_bd, RWKV_H),
                                   shift_new.reshape(bp, RWKV_FEAT), _block_diag_inv(gs_bd, GLA_H))):
            acc.append(s)

        ssd_slab, rw, gla_slab = _inproj(xs, mod_s[1], mod_s[0], w_pad, bs)
        y_ssd, y_rwkv, y_gla, conv_new, h_new, rs_new, gs_new = _mixer_step(
            ssd_slab[0], rw[0], gla_slab[0], state_conv[l].reshape(bs, -1), state_shift[l], state_ssd[l],
            state_rwkv[l], state_gla[l], ssd_p, rwkv_p, gla_p)
        xs = _tail(xs, mod_s, l, p, y_ssd[None], y_rwkv[None], y_gla[None], bs)
        for acc, s in zip(outs_s, (h_new, conv_new, rs_new, rw[0], gs_new)):
            acc.append(s)

    stack = lambda accs: tuple(jnp.stack(a, axis=0) for a in accs)
    return (xp, xs.reshape(bs, 1, D_MODEL)) + stack(outs_p) + stack(outs_s)


def kernel(x_prompt, x_sample, c_prompt, c_sample, state_ssd, state_ssd_conv, state_rwkv, state_rwkv_shift, state_gla, w_ada, b_ada, w_in, w_out, ssd_conv_w, ssd_conv_b, ssd_dt_bias, ssd_a_log, ssd_d, ssd_norm_g, rwkv_mu, rwkv_w0, rwkv_w2, rwkv_a0, rwkv_a2, rwkv_g2, rwkv_k_k, rwkv_k_a, rwkv_r_k, rwkv_ln_g, rwkv_ln_b, gla_w_gk2, gla_b_gk, gla_norm_g, ln_mix_g, ln_mix_b, ln_ffn_g, ln_ffn_b, ffn_w_gate, ffn_w_up, ffn_w_down, moe_router, moe_w_gate, moe_w_up, moe_w_down):
    p = dict(w_ada=w_ada, b_ada=b_ada, w_in=w_in, w_out=w_out, ssd_conv_w=ssd_conv_w, ssd_conv_b=ssd_conv_b,
             ssd_dt_bias=ssd_dt_bias, ssd_a_log=ssd_a_log, ssd_d=ssd_d, ssd_norm_g=ssd_norm_g,
             rwkv_mu=rwkv_mu, rwkv_w0=rwkv_w0, rwkv_w2=rwkv_w2, rwkv_a0=rwkv_a0, rwkv_a2=rwkv_a2,
             rwkv_g2=rwkv_g2, rwkv_k_k=rwkv_k_k, rwkv_k_a=rwkv_k_a, rwkv_r_k=rwkv_r_k,
             rwkv_ln_g=rwkv_ln_g, rwkv_ln_b=rwkv_ln_b, gla_w_gk2=gla_w_gk2, gla_b_gk=gla_b_gk,
             gla_norm_g=gla_norm_g, ln_mix_g=ln_mix_g, ln_mix_b=ln_mix_b, ln_ffn_g=ln_ffn_g,
             ln_ffn_b=ln_ffn_b, ffn_w_gate=ffn_w_gate, ffn_w_up=ffn_w_up, ffn_w_down=ffn_w_down,
             moe_router=moe_router, moe_w_gate=moe_w_gate, moe_w_up=moe_w_up, moe_w_down=moe_w_down)
    states = (state_ssd, state_ssd_conv, state_rwkv, state_rwkv_shift, state_gla)
    return _forward(x_prompt, x_sample, c_prompt, c_sample, states, p)
```

```python
import functools

import jax
import jax.numpy as jnp
from jax import lax
from jax.experimental import pallas as pl
from jax.experimental.pallas import tpu as pltpu

F32 = jnp.float32
BF16 = jnp.bfloat16

D_MODEL = 1024
DEPTH = 2
SSD_W = 512
SSD_H = 8
SSD_P = 64
SSD_N = 64
SSD_G = 2
SSD_CONV_W = 4
SSD_CONV_DIM = 768
RWKV_W = 256
RWKV_H = 4
RWKV_N = 64
RWKV_FEAT = 896
RWKV_GN_EPS = RWKV_N * 1e-5
GLA_W = 256
GLA_H = 4
GLA_DK = 32
GLA_DV = 64
GLA_HK = GLA_H * GLA_DK
GK_LORA = 16
GATE_NORMALIZER = 16.0
F_DENSE = 2816
N_EXPERTS = 8
ALPHA = (2.0 * DEPTH) ** 0.25
LN_EPS = 1e-5
RMS_EPS = 1e-6

LANES = 128
SMALL_W = LANES
DT_OFF = 0
GLO_OFF = 8
SSD_SLAB = SSD_W + SSD_CONV_DIM + SMALL_W
GLA_SLAB = 2 * GLA_HK + 2 * GLA_W + SMALL_W
IN_PAD = SSD_SLAB + RWKV_FEAT + GLA_SLAB
SSD_CHUNK = 128
GLA_CHUNK = 128
GLA_SUB = 8
RWKV_CHUNK = 64
SSD_NB, RWKV_NB, GLA_NB = 2, 8, 4
RWKV_SKEW = 1
MOE_CAP_ROWS = 160
STEP_GRID = 32
VMEM_LIMIT = 56 * 1024 * 1024


def _cparams(*sem):
    return pltpu.CompilerParams(dimension_semantics=sem, vmem_limit_bytes=VMEM_LIMIT)


def _mm(a, b):
    return jnp.dot(a.astype(BF16), b.astype(BF16), preferred_element_type=F32)


def _mm_nt(a, b):
    return lax.dot_general(a.astype(BF16), b.astype(BF16), (((1,), (1,)), ((), ())),
                           preferred_element_type=F32)


def _split_bf16(x, terms):
    parts = []
    for _ in range(terms):
        p = x.astype(BF16)
        parts.append(p)
        x = x - p.astype(F32)
    return parts


def _mm_sel(sel, x, terms=3):
    return sum(jnp.dot(sel, p, preferred_element_type=F32) for p in _split_bf16(x, terms))


def _mm_xsel(x, sel, terms=2):
    return sum(jnp.dot(p, sel, preferred_element_type=F32) for p in _split_bf16(x, terms))


def _mm_hi(a, b):
    a_hi, a_lo = _split_bf16(a, 2)
    b_hi, b_lo = _split_bf16(b, 2)
    dot = lambda x, y: jnp.dot(x, y, preferred_element_type=F32)
    return dot(a_hi, b_hi) + (dot(a_hi, b_lo) + dot(a_lo, b_hi))


def _sigmoid(x):
    return 1.0 / (1.0 + jnp.exp(-x))


def _silu(x):
    return x * _sigmoid(x)


def _softplus(x):
    return jnp.maximum(x, 0.0) + jnp.log1p(jnp.exp(-jnp.abs(x)))


def _iota(shape, dim):
    return lax.broadcasted_iota(jnp.int32, shape, dim)


def _layernorm(u, g, b):
    mu = jnp.mean(u, axis=-1, keepdims=True)
    d = u - mu
    var = jnp.mean(d * d, axis=-1, keepdims=True)
    return d * lax.rsqrt(var + LN_EPS) * g + b


def _tri_incl(n):
    return (jnp.arange(n)[None, :] <= jnp.arange(n)[:, None]).astype(BF16)


def _seg_ones(n, seg, scale=1.0):
    idx = jnp.arange(n) // seg
    return jnp.where(idx[:, None] == idx[None, :], scale, 0.0).astype(BF16)


def _interleave(chains, skew=0):
    chains = list(chains)
    done = [False] * len(chains)
    rnd = 0
    while not all(done):
        for i, ch in enumerate(chains):
            if done[i] or rnd < i * skew:
                continue
            try:
                next(ch)
            except StopIteration:
                done[i] = True
        rnd += 1


def _const_spec(arr):
    nd = arr.ndim
    return pl.BlockSpec(arr.shape, lambda *_: (0,) * nd)


def _ada_kernel(c_ref, w_ref, b_ref, o_ref):
    o_ref[0] = _mm(_silu(c_ref[...]), w_ref[0]) + b_ref[0]


def _ada(c_all, w_ada, b_ada):
    rows = c_all.shape[0]
    tn = 1536
    return pl.pallas_call(
        _ada_kernel,
        grid=(DEPTH, 6 * D_MODEL // tn),
        in_specs=[pl.BlockSpec((rows, D_MODEL), lambda l, j: (0, 0)),
                  pl.BlockSpec((1, D_MODEL, tn), lambda l, j: (l, 0, j)),
                  pl.BlockSpec((1, 1, tn), lambda l, j: (l, 0, j))],
        out_specs=pl.BlockSpec((1, rows, tn), lambda l, j: (l, 0, j)),
        out_shape=jax.ShapeDtypeStruct((DEPTH, rows, 6 * D_MODEL), F32),
        compiler_params=_cparams("arbitrary", "arbitrary"),
        name="ada",
    )(c_all, w_ada, b_ada.reshape(DEPTH, 1, 6 * D_MODEL))


def _inproj_kernel(x_ref, sc_ref, sh_ref, w_ref, o_ssd, o_rwkv, o_gla):
    h = (x_ref[0] * (1.0 + sc_ref[0]) + sh_ref[0]).astype(BF16)
    o_ssd[0] = jnp.dot(h, w_ref[:, :SSD_SLAB], preferred_element_type=F32)
    o_rwkv[0] = jnp.dot(h, w_ref[:, SSD_SLAB:SSD_SLAB + RWKV_FEAT], preferred_element_type=F32)
    o_gla[0] = jnp.dot(h, w_ref[:, SSD_SLAB + RWKV_FEAT:], preferred_element_type=F32)


def _mod_spec(mod, tm):
    if mod.shape[1] == 1:
        return pl.BlockSpec((1, 1, D_MODEL), lambda b, i, *_: (b, 0, 0))
    return pl.BlockSpec((1, tm, D_MODEL), lambda b, i, *_: (b, i, 0))


def _inproj(x, sc, sh, w_pad, tm):
    bsz, length, _ = x.shape
    tok = lambda width: pl.BlockSpec((1, tm, width), lambda b, i: (b, i, 0))
    return pl.pallas_call(
        _inproj_kernel,
        grid=(bsz, length // tm),
        in_specs=[tok(D_MODEL), _mod_spec(sc, tm), _mod_spec(sh, tm),
                  pl.BlockSpec((D_MODEL, IN_PAD), lambda b, i: (0, 0))],
        out_specs=[tok(SSD_SLAB), tok(RWKV_FEAT), tok(GLA_SLAB)],
        out_shape=[jax.ShapeDtypeStruct((bsz, length, SSD_SLAB), F32),
                   jax.ShapeDtypeStruct((bsz, length, RWKV_FEAT), F32),
                   jax.ShapeDtypeStruct((bsz, length, GLA_SLAB), F32)],
        compiler_params=_cparams("arbitrary", "arbitrary"),
        name="inproj",
    )(x, sc, sh, w_pad)


def _ssd_token_math(xbc, small, conv_taps, cw, cb, dtb, alog):
    conv = cb + xbc * cw[3:4, :]
    for i in range(SSD_CONV_W - 1):
        conv = conv + conv_taps[i] * cw[i:i + 1, :]
    dt = _softplus(small + dtb)
    return _silu(conv), dt, dt * (-jnp.exp(alog))


def _ssd_finish(y, xs, z, dexp, ng):
    gated = (y + xs * dexp) * _silu(z)
    ms = jnp.mean(gated * gated, axis=-1, keepdims=True)
    return gated * lax.rsqrt(ms + RMS_EPS) * ng


def _ssd_kernel(slab_ref, conv0_ref, h0_ref, cw_ref, cb_ref, dtb_ref, alog_ref, dexp_ref, ng_ref, tri_ref,
                head_x_ref, group_mask_ref, pair_mask_ref, y_ref, conv_out_ref, h_out_ref, ext_ref, h_ref):
    c = pl.program_id(1)
    n_chunks = pl.num_programs(1)
    ch = SSD_CHUNK
    causal = _iota((ch, ch), 1) <= _iota((ch, ch), 0)

    nb = slab_ref.shape[0]

    @pl.when(c == 0)
    def _():
        for j in range(nb):
            ext_ref[j, 0:8, :] = jnp.zeros((8, SSD_CONV_DIM), F32)
            ext_ref[j, 5:8, :] = conv0_ref[j]
            h_ref[j] = h0_ref[j]

    def chain(j):
        z = slab_ref[j, :, 0:SSD_W]
        xbc = slab_ref[j, :, SSD_W:SSD_W + SSD_CONV_DIM]
        small = slab_ref[j, :, SSD_W + SSD_CONV_DIM:]

        ext_ref[j, 8:8 + ch, :] = xbc
        taps = [ext_ref[j, 5 + i:5 + i + ch, :] for i in range(SSD_CONV_W - 1)]
        act, dt, dta = _ssd_token_math(xbc, small, taps, cw_ref[...], cb_ref[...], dtb_ref[...], alog_ref[...])
        ext_ref[j, 0:8, :] = ext_ref[j, ch:ch + 8, :]

        xs = act[:, 0:SSD_W]
        bs = act[:, SSD_W:SSD_W + SSD_G * SSD_N]
        cs = act[:, SSD_W + SSD_G * SSD_N:]
        yield
        cum = _mm_sel(tri_ref[...], dta)
        cum_t = cum.T
        dt_t = dt.T
        cum_x = _mm_xsel(cum, head_x_ref[...], 3)
        dt_x = _mm_xsel(dt, head_x_ref[...], 3)
        yield
        last_x = cum_x[ch - 1:ch, :]
        xs_tail = (xs * (jnp.exp(last_x - cum_x) * dt_x)).astype(BF16)
        h_bd = h_ref[j]
        y_state = jnp.dot(cs.astype(BF16), h_bd.astype(BF16), preferred_element_type=F32) * jnp.exp(cum_x)
        upd = jnp.dot(bs.T.astype(BF16), xs_tail, preferred_element_type=F32)
        h_ref[j] = h_bd * jnp.exp(last_x) + upd * group_mask_ref[...]
        yield

        lane_group = _iota((1, SSD_G * SSD_N), 1) // SSD_N
        xs_b = xs.astype(BF16)
        ys = []
        for g in range(SSD_G):
            cb = _mm_nt(jnp.where(lane_group == g, cs, 0.0), bs)
            yield
            for pair in range(SSD_H // SSD_G // 2):
                scores = []
                for hh in range(2):
                    lane = DT_OFF + g * (SSD_H // SSD_G) + 2 * pair + hh
                    decay = jnp.exp(jnp.where(causal, cum[:, lane:lane + 1] - cum_t[lane:lane + 1, :], -jnp.inf))
                    scores.append((cb * decay * dt_t[lane:lane + 1, :]).astype(BF16))
                blk = g * (SSD_H // SSD_G) // 2 + pair
                x_pair = xs_b[:, blk * LANES:(blk + 1) * LANES]
                x_bd = jnp.concatenate([x_pair, x_pair], axis=0) * pair_mask_ref[...]
                ys.append(jnp.dot(jnp.concatenate(scores, axis=1), x_bd, preferred_element_type=F32))
                yield

        y = jnp.concatenate(ys, axis=-1) + y_state
        y_ref[j] = _ssd_finish(y, xs, z, dexp_ref[...], ng_ref[...]).astype(y_ref.dtype)

    _interleave([chain(j) for j in range(nb)])

    @pl.when(c == n_chunks - 1)
    def _():
        for j in range(nb):
            conv_out_ref[j] = ext_ref[j, 5:8, :]
            h_out_ref[j] = h_ref[j]


def _ssd_prompt(slab, conv0, h0, cw, cb, dtb, alog, dexp, ng):
    bsz, length, _ = slab.shape
    nb = SSD_NB if bsz % SSD_NB == 0 else 1
    gn, hp = SSD_G * SSD_N, SSD_H * SSD_P
    lane_head = jnp.arange(hp) // SSD_P
    head_x = (jnp.arange(SMALL_W)[:, None] == DT_OFF + lane_head[None, :]).astype(BF16)
    group_mask = (jnp.arange(gn)[:, None] // SSD_N == lane_head[None, :] // (SSD_H // SSD_G)).astype(F32)
    pair_mask = (jnp.arange(2 * SSD_CHUNK)[:, None] // SSD_CHUNK == jnp.arange(2 * SSD_P)[None, :] // SSD_P
                 ).astype(BF16)
    consts = (cw, cb, dtb, alog, dexp, ng, _tri_incl(SSD_CHUNK), head_x, group_mask, pair_mask)
    return pl.pallas_call(
        _ssd_kernel,
        grid=(bsz // nb, length // SSD_CHUNK),
        in_specs=[pl.BlockSpec((nb, SSD_CHUNK, SSD_SLAB), lambda b, c: (b, c, 0)),
                  pl.BlockSpec((nb, SSD_CONV_W - 1, SSD_CONV_DIM), lambda b, c: (b, 0, 0)),
                  pl.BlockSpec((nb, gn, hp), lambda b, c: (b, 0, 0))]
                 + [_const_spec(a) for a in consts],
        out_specs=[pl.BlockSpec((nb, SSD_CHUNK, SSD_W), lambda b, c: (b, c, 0)),
                   pl.BlockSpec((nb, SSD_CONV_W - 1, SSD_CONV_DIM), lambda b, c: (b, 0, 0)),
                   pl.BlockSpec((nb, gn, hp), lambda b, c: (b, 0, 0))],
        out_shape=[jax.ShapeDtypeStruct((bsz, length, SSD_W), BF16),
                   jax.ShapeDtypeStruct((bsz, SSD_CONV_W - 1, SSD_CONV_DIM), F32),
                   jax.ShapeDtypeStruct((bsz, gn, hp), F32)],
        scratch_shapes=[pltpu.VMEM((nb, SSD_CHUNK + 8, SSD_CONV_DIM), F32),
                        pltpu.VMEM((nb, gn, hp), F32)],
        compiler_params=_cparams("arbitrary", "arbitrary"),
        name="ssd_scan",
    )(slab, conv0, h0, *consts)


def _rwkv_token_math(rw, prev, mu, lora_w, w0, a0, k_k, k_a, head_ones):
    mix = rw + mu * (prev - rw)
    r = mix[:, 0:RWKV_W]
    k = mix[:, RWKV_W:2 * RWKV_W]
    v = mix[:, 2 * RWKV_W:3 * RWKV_W]
    lora = mix[:, 3 * RWKV_W:]
    lane = _iota(lora.shape, 1)
    act = jnp.where(lane < 32, jnp.tanh(lora), jnp.where(lane < 64, lora, _sigmoid(lora)))
    lo = _mm(act, lora_w)
    w_log = -_softplus(-(w0 + lo[:, 0:RWKV_W])) - 0.5
    logw = -jnp.exp(w_log)
    aic = _sigmoid(a0 + lo[:, RWKV_W:2 * RWKV_W])
    gate = lo[:, 2 * RWKV_W:]
    kkf = k * k_k
    kk = kkf * lax.rsqrt(_mm_xsel(kkf * kkf, head_ones) + 1e-12)
    k2 = k * (1.0 + (aic - 1.0) * k_a)
    return r, k2, v, logw, aic, gate, kk


def _rwkv_finish(o, r, k2, v, gate, r_k, ln_g, ln_b, head_ones, head_avg):
    mu = _mm_xsel(o, head_avg)
    d = o - mu
    var = _mm_xsel(d * d, head_avg)
    on = d * lax.rsqrt(var + RWKV_GN_EPS) * ln_g + ln_b
    bonus = _mm_xsel(r * k2 * r_k, head_ones) * v
    return (on + bonus) * gate


def _rwkv_kernel(rw_ref, shift0_ref, s0_ref, mu_ref, lw_ref, w0_ref, a0_ref, kk_ref, ka_ref,
                 rk_ref, lng_ref, lnb_ref, ones_ref, avg_ref, tri_ref, strict_ref, incl_ref, eye_ref,
                 y_ref, shift_out_ref, s_out_ref, ext_ref, s_ref):
    c = pl.program_id(1)
    n_chunks = pl.num_programs(1)
    ch = RWKV_CHUNK
    nh = RWKV_H
    w = RWKV_W
    head_bd = ones_ref[...]
    head_bd_f = head_bd.astype(F32)
    tile4 = lambda x: jnp.concatenate([x] * nh, axis=0)

    nb = rw_ref.shape[0]

    @pl.when(c == 0)
    def _():
        for j in range(nb):
            ext_ref[j, 0:8, :] = jnp.zeros((8, RWKV_FEAT), F32)
            ext_ref[j, 7:8, :] = shift0_ref[j]
            s_ref[j] = s0_ref[j]

    def chain(j):
        rw = rw_ref[j]
        ext_ref[j, 8:8 + ch, :] = rw
        prev = ext_ref[j, 7:7 + ch, :]
        ext_ref[j, 7:8, :] = rw[ch - 1:ch, :]

        r, k2, v, logw, aic, gate, kk = _rwkv_token_math(
            rw, prev, mu_ref[...], lw_ref[...], w0_ref[...], a0_ref[...], kk_ref[...], ka_ref[...], head_bd)
        yield

        cumw = _mm_sel(tri_ref[...], logw)
        yield
        last = cumw[ch - 1:ch, :]
        inv_g = jnp.exp(-cumw)
        to_end = jnp.exp(last - cumw)
        kka = kk * aic
        a_t = (-kk * jnp.exp(cumw - logw)).astype(BF16)
        r_t = (r * jnp.exp(cumw)).astype(BF16)
        b_t = (kka * inv_g).astype(BF16)
        k_t = (k2 * inv_g).astype(BF16)
        bk_end = jnp.concatenate([kka * to_end, k2 * to_end], axis=0).astype(BF16)

        bd = lambda x: tile4(x.astype(BF16)) * head_bd
        nt = lambda x, y: lax.dot_general(x, y, (((1,), (1,)), ((), ())), preferred_element_type=F32)
        vb = v.astype(BF16)
        ar = jnp.concatenate([a_t, r_t], axis=0)
        prod = nt(ar, jnp.concatenate([bd(b_t), bd(k_t)], axis=0))
        a_ab = prod[0:ch, 0:w] * strict_ref[...]
        a_ak = prod[0:ch, w:2 * w] * strict_ref[...]
        a_rbk = (prod[ch:2 * ch, :] * incl_ref[...]).astype(BF16)
        yield
        s_bd = s_ref[j]
        uo = nt(ar, s_bd.astype(BF16))
        rhs = uo[0:ch, :] + jnp.dot(a_ak.astype(BF16), bd(vb), preferred_element_type=F32)
        yield

        t_inv = eye_ref[...] + a_ab
        x = a_ab
        x_bd = bd(x)
        power = 1
        while 2 * power < ch:
            x = jnp.dot(x.astype(BF16), x_bd, preferred_element_type=F32)
            x_bd = bd(x)
            yield
            t_inv = t_inv + jnp.dot(t_inv.astype(BF16), x_bd, preferred_element_type=F32)
            yield
            power *= 2

        p = jnp.dot(t_inv.astype(BF16), bd(rhs), preferred_element_type=F32)
        yield
        pb = p.astype(BF16)
        o = uo[ch:2 * ch, :] + jnp.dot(a_rbk, jnp.concatenate([bd(pb), bd(vb)], axis=0),
                                       preferred_element_type=F32)
        pv = jnp.concatenate([pb, vb], axis=0)
        yield

        upd = jnp.dot(pv.astype(F32).T.astype(BF16), bk_end, preferred_element_type=F32)
        s_ref[j] = s_bd * jnp.exp(last) + upd * head_bd_f
        yield

        y_ref[j] = _rwkv_finish(o, r, k2, v, gate, rk_ref[...], lng_ref[...], lnb_ref[...],
                                head_bd, avg_ref[...]).astype(y_ref.dtype)

    _interleave([chain(j) for j in range(nb)], skew=RWKV_SKEW)

    @pl.when(c == n_chunks - 1)
    def _():
        for j in range(nb):
            shift_out_ref[j] = ext_ref[j, 7:8, :]
            s_out_ref[j] = s_ref[j]


def _rwkv_consts():
    ch = RWKV_CHUNK
    assert ch == RWKV_N, "bd() reuses the head mask, which needs chunk rows == head width"
    t = jnp.arange(ch)[:, None]
    j = jnp.arange(2 * RWKV_W)[None, :] % ch
    strict = (j[:, :RWKV_W] < t).astype(F32)
    incl = (j <= t).astype(F32)
    eye = (j[:, :RWKV_W] == t).astype(F32)
    return (_seg_ones(RWKV_W, RWKV_N), _seg_ones(RWKV_W, RWKV_N, 1.0 / RWKV_N), _tri_incl(ch),
            strict, incl, eye)


def _rwkv_prompt(rw, shift0, s0_bd, *params):
    bsz, length, _ = rw.shape
    nb = RWKV_NB if bsz % RWKV_NB == 0 else 1
    consts = tuple(params) + _rwkv_consts()
    return pl.pallas_call(
        _rwkv_kernel,
        grid=(bsz // nb, length // RWKV_CHUNK),
        in_specs=[pl.BlockSpec((nb, RWKV_CHUNK, RWKV_FEAT), lambda b, c: (b, c, 0)),
                  pl.BlockSpec((nb, 1, RWKV_FEAT), lambda b, c: (b, 0, 0)),
                  pl.BlockSpec((nb, RWKV_W, RWKV_W), lambda b, c: (b, 0, 0))]
                 + [_const_spec(a) for a in consts],
        out_specs=[pl.BlockSpec((nb, RWKV_CHUNK, RWKV_W), lambda b, c: (b, c, 0)),
                   pl.BlockSpec((nb, 1, RWKV_FEAT), lambda b, c: (b, 0, 0)),
                   pl.BlockSpec((nb, RWKV_W, RWKV_W), lambda b, c: (b, 0, 0))],
        out_shape=[jax.ShapeDtypeStruct((bsz, length, RWKV_W), BF16),
                   jax.ShapeDtypeStruct((bsz, 1, RWKV_FEAT), F32),
                   jax.ShapeDtypeStruct((bsz, RWKV_W, RWKV_W), F32)],
        scratch_shapes=[pltpu.VMEM((nb, RWKV_CHUNK + 8, RWKV_FEAT), F32),
                        pltpu.VMEM((nb, RWKV_W, RWKV_W), F32)],
        compiler_params=_cparams("arbitrary", "arbitrary"),
        name="rwkv_scan",
    )(rw, shift0, s0_bd, *consts)


def _gla_gate_log(small, gk2_pad, b_gk):
    x = _mm_hi(small, gk2_pad) + b_gk
    return -_softplus(-x) / GATE_NORMALIZER


def _gla_finish(o, gg, norm_g, head_avg):
    ms = _mm_xsel(o * o, head_avg)
    return o * lax.rsqrt(ms + RMS_EPS) * norm_g * _silu(gg)


def _gla_kernel(slab_ref, s0_ref, gk2_ref, bgk_ref, ng_ref, avg_ref, tri_ref, expand_ref,
                y_ref, s_out_ref, s_ref):
    c = pl.program_id(1)
    n_chunks = pl.num_programs(1)
    ch = GLA_CHUNK
    sub = GLA_SUB
    hk = GLA_HK
    expand = expand_ref[...]
    expand_f = expand.astype(F32)
    pos = _iota((ch, 1), 0) % sub
    t_idx = _iota((ch, 1), 0)
    hmask_k = (_iota((GLA_H * sub, hk), 0) // sub == _iota((GLA_H * sub, hk), 1) // GLA_DK).astype(F32)
    hmask_v = (_iota((GLA_H * sub, GLA_W), 0) // sub == _iota((GLA_H * sub, GLA_W), 1) // GLA_DV).astype(F32)

    nb = slab_ref.shape[0]

    @pl.when(c == 0)
    def _():
        s_ref[...] = s0_ref[...]

    def chain(j):
        q = slab_ref[j, :, 0:hk] * (GLA_DK ** -0.5)
        k = slab_ref[j, :, hk:2 * hk]
        v = slab_ref[j, :, 2 * hk:2 * hk + GLA_W]
        gg = slab_ref[j, :, 2 * hk + GLA_W:2 * hk + 2 * GLA_W]
        small = slab_ref[j, :, 2 * hk + 2 * GLA_W:]

        lg = _gla_gate_log(small, gk2_ref[...], bgk_ref[...])
        yield
        cum = _mm_sel(tri_ref[...], lg)
        yield
        last = cum[ch - 1:ch, :]
        s_bd = s_ref[j]

        o = _mm(q * jnp.exp(cum), s_bd)

        group = 4
        for d0 in range(0, sub, group):
            prods, v_shift = [], []
            for delta in range(d0, d0 + group):
                if delta == 0:
                    k_s, c_s, v_s = k, cum, v
                else:
                    k_s = pltpu.roll(k, delta, 0)
                    c_s = pltpu.roll(cum, delta, 0)
                    v_s = pltpu.roll(v, delta, 0)
                valid = pos >= delta
                w_pair = jnp.exp(jnp.where(valid, cum - c_s, 0.0))
                prods.append(jnp.where(valid, q * k_s * w_pair, 0.0).astype(BF16))
                v_shift.append(v_s)
            yield
            att = jnp.dot(jnp.concatenate(prods, axis=0), expand, preferred_element_type=F32)
            o = o + sum(att[g * ch:(g + 1) * ch, :] * v_shift[g] for g in range(group))
            yield

        vb = v.astype(BF16)
        atts = []
        for blk in range(1, ch // sub):
            lo = blk * sub
            ref_pt = cum[lo - 1:lo, :]
            qd = q[lo:lo + sub, :] * jnp.exp(cum[lo:lo + sub, :] - ref_pt)
            kp = k * jnp.exp(jnp.where(t_idx < lo, ref_pt - cum, -jnp.inf))
            q4 = jnp.concatenate([qd] * GLA_H, axis=0) * hmask_k
            atts.append(_mm_nt(q4, kp).astype(BF16))
            yield
        o4 = jnp.dot(jnp.concatenate(atts, axis=0), vb, preferred_element_type=F32)
        yield
        cross = [jnp.zeros((sub, GLA_W), F32)]
        for blk in range(ch // sub - 1):
            o4_b = o4[blk * GLA_H * sub:(blk + 1) * GLA_H * sub, :] * hmask_v
            cross.append(sum(o4_b[h * sub:(h + 1) * sub, :] for h in range(GLA_H)))
        o = o + jnp.concatenate(cross, axis=0)

        kt_t = (k * jnp.exp(last - cum)).T
        s_ref[j] = s_bd * _col(jnp.exp(last)) + _mm(kt_t, vb) * expand_f
        yield

        y_ref[j] = _gla_finish(o, gg, ng_ref[...], avg_ref[...]).astype(y_ref.dtype)

    _interleave([chain(j) for j in range(nb)])

    @pl.when(c == n_chunks - 1)
    def _():
        s_out_ref[...] = s_ref[...]


def _gla_prompt(slab, s0_bd, gk2_pad, b_gk, ng):
    bsz, length, _ = slab.shape
    nb = GLA_NB if bsz % GLA_NB == 0 else 1
    expand = (jnp.arange(GLA_HK)[:, None] // GLA_DK == jnp.arange(GLA_W)[None, :] // GLA_DV).astype(BF16)
    consts = (gk2_pad, b_gk, ng, _seg_ones(GLA_W, GLA_DV, 1.0 / GLA_DV), _tri_incl(GLA_CHUNK), expand)
    return pl.pallas_call(
        _gla_kernel,
        grid=(bsz // nb, length // GLA_CHUNK),
        in_specs=[pl.BlockSpec((nb, GLA_CHUNK, GLA_SLAB), lambda b, c: (b, c, 0)),
                  pl.BlockSpec((nb, GLA_HK, GLA_W), lambda b, c: (b, 0, 0))]
                 + [_const_spec(a) for a in consts],
        out_specs=[pl.BlockSpec((nb, GLA_CHUNK, GLA_W), lambda b, c: (b, c, 0)),
                   pl.BlockSpec((nb, GLA_HK, GLA_W), lambda b, c: (b, 0, 0))],
        out_shape=[jax.ShapeDtypeStruct((bsz, length, GLA_W), BF16),
                   jax.ShapeDtypeStruct((bsz, GLA_HK, GLA_W), F32)],
        scratch_shapes=[pltpu.VMEM((nb, GLA_HK, GLA_W), F32)],
        compiler_params=_cparams("arbitrary", "arbitrary"),
        name="gla_scan",
    )(slab, s0_bd, *consts)


def _col(row):
    n = row.shape[1]
    eye = _iota((n, n), 0) == _iota((n, n), 1)
    return jnp.sum(jnp.where(eye, row, 0.0), axis=1, keepdims=True)


def _store_t(dst_ref, row0, x):
    for cblk in range(x.shape[1] // LANES):
        dst_ref[row0 + cblk * LANES:row0 + (cblk + 1) * LANES, :] = x[:, cblk * LANES:(cblk + 1) * LANES].T


def _load_t(src_ref, width):
    return jnp.concatenate([src_ref[cblk * LANES:(cblk + 1) * LANES, :].T for cblk in range(width // LANES)],
                           axis=1)


_TS_X, _TS_B, _TS_C, _TS_DT, _TS_DA = 0, SSD_W, SSD_W + SSD_G * SSD_N, SSD_CONV_DIM, SSD_CONV_DIM + SMALL_W
_TR_R, _TR_W, _TR_K, _TR_V, _TR_KK, _TR_KA = (i * RWKV_W for i in range(6))
_TG_Q, _TG_K, _TG_G, _TG_V = 0, GLA_HK, 2 * GLA_HK, 3 * GLA_HK


def _step_kernel(ssd_ref, rw_ref, gla_ref, conv0_ref, shift0_ref, h0_ref, rs0_ref, gs0_ref,
                 cw_ref, cb_ref, dtb_ref, alog_ref, dexp_ref, sng_ref,
                 mu_ref, lw_ref, w0_ref, a0_ref, kk_ref, ka_ref, rk_ref, lng_ref, lnb_ref,
                 gk2_ref, bgk_ref, gng_ref, ones_ref, ravg_ref, gavg_ref,
                 y_ssd_ref, y_rwkv_ref, y_gla_ref, conv_out_ref, h_out_ref, rs_out_ref, gs_out_ref,
                 t_ssd, t_rwkv, t_gla, o_ssd_t, o_rwkv_t, o_gla_t):
    i = pl.program_id(0)
    hk = GLA_HK

    def ssd_tokens():
        xbc = ssd_ref[:, SSD_W:SSD_W + SSD_CONV_DIM]
        taps = [conv0_ref[:, t * SSD_CONV_DIM:(t + 1) * SSD_CONV_DIM] for t in range(SSD_CONV_W - 1)]
        return xbc, _ssd_token_math(xbc, ssd_ref[:, SSD_W + SSD_CONV_DIM:], taps, cw_ref[...], cb_ref[...],
                                    dtb_ref[...], alog_ref[...])

    def rwkv_tokens():
        return _rwkv_token_math(rw_ref[...], shift0_ref[...], mu_ref[...], lw_ref[...], w0_ref[...],
                                a0_ref[...], kk_ref[...], ka_ref[...], ones_ref[...])

    @pl.when(i == 0)
    def _():
        xbc, (act, dt, dta) = ssd_tokens()
        conv_out_ref[:, 0:2 * SSD_CONV_DIM] = conv0_ref[:, SSD_CONV_DIM:]
        conv_out_ref[:, 2 * SSD_CONV_DIM:] = xbc
        _store_t(t_ssd, _TS_X, act)
        _store_t(t_ssd, _TS_DT, dt)
        _store_t(t_ssd, _TS_DA, jnp.exp(dta))
        r, k2, v, logw, aic, gate, kk = rwkv_tokens()
        for off, x in ((_TR_R, r), (_TR_W, jnp.exp(logw)), (_TR_K, k2), (_TR_V, v), (_TR_KK, kk),
                       (_TR_KA, kk * aic)):
            _store_t(t_rwkv, off, x)
        lg = _gla_gate_log(gla_ref[:, 2 * hk + 2 * GLA_W:], gk2_ref[...], bgk_ref[...])
        _store_t(t_gla, _TG_Q, gla_ref[:, 0:hk] * (GLA_DK ** -0.5))
        _store_t(t_gla, _TG_K, gla_ref[:, hk:2 * hk])
        _store_t(t_gla, _TG_G, jnp.exp(lg))
        _store_t(t_gla, _TG_V, gla_ref[:, 2 * hk:2 * hk + GLA_W])
        o_gla_t[...] = jnp.zeros_like(o_gla_t)

    def row(ref, idx):
        return ref[pl.ds(idx, 1), :]

    def rows(ref, idx, count):
        return ref[pl.ds(pl.multiple_of(idx, count), count), :]

    units = h0_ref.shape[0] // SSD_N
    per_head = SSD_P // units
    h = i // per_head
    p0 = (i % per_head) * units
    b_t = rows(t_ssd, _TS_B + (h // (SSD_H // SSD_G)) * SSD_N, SSD_N)
    c_t = rows(t_ssd, _TS_C + (h // (SSD_H // SSD_G)) * SSD_N, SSD_N)
    da_row = row(t_ssd, _TS_DA + DT_OFF + h)
    dt_row = row(t_ssd, _TS_DT + DT_OFF + h)
    for u in range(units):
        p_idx = h * SSD_P + p0 + u
        blk = slice(u * SSD_N, (u + 1) * SSD_N)
        h_new = h0_ref[blk, :] * da_row + (row(t_ssd, _TS_X + p_idx) * dt_row) * b_t
        o_ssd_t[pl.ds(p_idx, 1), :] = jnp.sum(h_new * c_t, axis=0, keepdims=True)
        h_out_ref[blk, :] = h_new

    units = rs0_ref.shape[0] // RWKV_N
    per_head = RWKV_N // units
    h = i // per_head
    v0 = (i % per_head) * units
    seg = lambda off: rows(t_rwkv, off + h * RWKV_N, RWKV_N)
    r_t, w_t, k_t, kk_t, ka_t = seg(_TR_R), seg(_TR_W), seg(_TR_K), seg(_TR_KK), seg(_TR_KA)
    for u in range(units):
        v_idx = h * RWKV_N + v0 + u
        blk = slice(u * RWKV_N, (u + 1) * RWKV_N)
        s = rs0_ref[blk, :]
        sa = jnp.sum(s * (-kk_t), axis=0, keepdims=True)
        s_new = s * w_t + sa * ka_t + row(t_rwkv, _TR_V + v_idx) * k_t
        o_rwkv_t[pl.ds(v_idx, 1), :] = jnp.sum(s_new * r_t, axis=0, keepdims=True)
        rs_out_ref[blk, :] = s_new

    units = gs0_ref.shape[0] // GLA_DV
    per_head = GLA_DK // units
    h = i // per_head
    k0 = (i % per_head) * units
    v_t = rows(t_gla, _TG_V + h * GLA_DV, GLA_DV)
    acc = jnp.zeros((GLA_DV, LANES), F32)
    for u in range(units):
        k_idx = h * GLA_DK + k0 + u
        blk = slice(u * GLA_DV, (u + 1) * GLA_DV)
        s_new = gs0_ref[blk, :] * row(t_gla, _TG_G + k_idx) + row(t_gla, _TG_K + k_idx) * v_t
        acc = acc + row(t_gla, _TG_Q + k_idx) * s_new
        gs_out_ref[blk, :] = s_new
    o_rows = pl.ds(pl.multiple_of(h * GLA_DV, GLA_DV), GLA_DV)
    o_gla_t[o_rows, :] = o_gla_t[o_rows, :] + acc

    @pl.when(i == pl.num_programs(0) - 1)
    def _():
        _, (act, _, _) = ssd_tokens()
        y_ssd_ref[...] = _ssd_finish(_load_t(o_ssd_t, SSD_W), act[:, 0:SSD_W], ssd_ref[:, 0:SSD_W],
                                     dexp_ref[...], sng_ref[...]).astype(y_ssd_ref.dtype)
        r, k2, v, _, _, gate, _ = rwkv_tokens()
        y_rwkv_ref[...] = _rwkv_finish(_load_t(o_rwkv_t, RWKV_W), r, k2, v, gate, rk_ref[...], lng_ref[...],
                                       lnb_ref[...], ones_ref[...], ravg_ref[...]).astype(y_rwkv_ref.dtype)
        y_gla_ref[...] = _gla_finish(_load_t(o_gla_t, GLA_W), gla_ref[:, 2 * hk + GLA_W:2 * hk + 2 * GLA_W],
                                     gng_ref[...], gavg_ref[...]).astype(y_gla_ref.dtype)


def _mixer_step(ssd_slab, rw, gla_slab, conv0, shift0, h0, rs0, gs0, ssd_p, rwkv_p, gla_p):
    bsz = ssd_slab.shape[0]
    assert bsz == LANES, "the single-token mixer keeps exactly one vreg row of batch entries on the lanes"
    flat_t = lambda s: s.reshape(bsz, -1).T
    h0, rs0, gs0 = flat_t(h0), flat_t(rs0), flat_t(gs0)
    consts = (tuple(ssd_p) + tuple(rwkv_p) + tuple(gla_p)
              + (_seg_ones(RWKV_W, RWKV_N), _seg_ones(RWKV_W, RWKV_N, 1.0 / RWKV_N),
                 _seg_ones(GLA_W, GLA_DV, 1.0 / GLA_DV)))
    full = lambda a: pl.BlockSpec(a.shape, lambda i: (0,) * a.ndim)
    cols = lambda a: pl.BlockSpec((a.shape[0] // STEP_GRID, bsz), lambda i: (i, 0))
    tokens = (ssd_slab, rw, gla_slab, conv0, shift0)
    outs = pl.pallas_call(
        _step_kernel,
        grid=(STEP_GRID,),
        in_specs=[full(a) for a in tokens] + [cols(h0), cols(rs0), cols(gs0)] + [full(a) for a in consts],
        out_specs=[full(jax.ShapeDtypeStruct((bsz, w), BF16)) for w in (SSD_W, RWKV_W, GLA_W)]
                  + [full(conv0), cols(h0), cols(rs0), cols(gs0)],
        out_shape=[jax.ShapeDtypeStruct((bsz, SSD_W), BF16),
                   jax.ShapeDtypeStruct((bsz, RWKV_W), BF16),
                   jax.ShapeDtypeStruct((bsz, GLA_W), BF16),
                   jax.ShapeDtypeStruct(conv0.shape, F32),
                   jax.ShapeDtypeStruct(h0.shape, F32),
                   jax.ShapeDtypeStruct(rs0.shape, F32),
                   jax.ShapeDtypeStruct(gs0.shape, F32)],
        scratch_shapes=[pltpu.VMEM((SSD_CONV_DIM + 2 * SMALL_W, LANES), F32),
                        pltpu.VMEM((6 * RWKV_W, LANES), F32),
                        pltpu.VMEM((3 * GLA_HK + GLA_W, LANES), F32),
                        pltpu.VMEM((SSD_W, LANES), F32),
                        pltpu.VMEM((RWKV_W, LANES), F32),
                        pltpu.VMEM((GLA_W, LANES), F32)],
        compiler_params=_cparams("arbitrary"),
        name="mixer_step",
    )(*tokens, h0, rs0, gs0, *consts)
    y_ssd, y_rwkv, y_gla, conv_new, h_new, rs_new, gs_new = outs
    return (y_ssd, y_rwkv, y_gla, conv_new.reshape(bsz, SSD_CONV_W - 1, SSD_CONV_DIM),
            h_new.T.reshape(bsz, SSD_H, SSD_P, SSD_N), rs_new.T.reshape(bsz, RWKV_H, RWKV_N, RWKV_N),
            gs_new.T.reshape(bsz, GLA_H, GLA_DK, GLA_DV))


def _mod_rows(mod_ref, rows):
    return mod_ref[0] if mod_ref.shape[1] == 1 else mod_ref[0, rows, :]


def _row_parts(tm):
    return 2 if tm % 256 == 0 else 1


def _outproj_kernel(ys_ref, yr_ref, yg_ref, x_ref, gt_ref, w_ref, g_ref, b_ref, o_ref):
    tm = x_ref.shape[1]
    parts = _row_parts(tm)

    def chain(r):
        rows = slice(r * tm // parts, (r + 1) * tm // parts)
        m = (jnp.dot(ys_ref[0, rows, :], w_ref[0:SSD_W, :], preferred_element_type=F32)
             + jnp.dot(yr_ref[0, rows, :], w_ref[SSD_W:SSD_W + RWKV_W, :], preferred_element_type=F32)
             + jnp.dot(yg_ref[0, rows, :], w_ref[SSD_W + RWKV_W:, :], preferred_element_type=F32))
        yield
        u = ALPHA * x_ref[0, rows, :] + (1.0 + _mod_rows(gt_ref, rows)) * m
        o_ref[0, rows, :] = _layernorm(u, g_ref[...], b_ref[...])

    _interleave([chain(r) for r in range(parts)], skew=1)


def _outproj(y_ssd, y_rwkv, y_gla, x, gt, w_out, ln_g, ln_b, tm):
    bsz, length, _ = x.shape
    tok = lambda width: pl.BlockSpec((1, tm, width), lambda b, i: (b, i, 0))
    vec = pl.BlockSpec((1, D_MODEL), lambda b, i: (0, 0))
    return pl.pallas_call(
        _outproj_kernel,
        grid=(bsz, length // tm),
        in_specs=[tok(SSD_W), tok(RWKV_W), tok(GLA_W), tok(D_MODEL), _mod_spec(gt, tm),
                  pl.BlockSpec((D_MODEL, D_MODEL), lambda b, i: (0, 0)), vec, vec],
        out_specs=tok(D_MODEL),
        out_shape=jax.ShapeDtypeStruct((bsz, length, D_MODEL), F32),
        compiler_params=_cparams("arbitrary", "arbitrary"),
        name="outproj_ln",
    )(y_ssd, y_rwkv, y_gla, x, gt, w_out, ln_g, ln_b)


def _ffn_kernel(x_ref, sc_ref, sh_ref, gt_ref, wg_ref, wu_ref, wd_ref, g_ref, b_ref, o_ref):
    tm = x_ref.shape[1]
    parts = _row_parts(tm)

    def chain(r):
        rows = slice(r * tm // parts, (r + 1) * tm // parts)
        x = x_ref[0, rows, :]
        h = (x * (1.0 + _mod_rows(sc_ref, rows)) + _mod_rows(sh_ref, rows)).astype(BF16)
        yield
        gate = jnp.dot(h, wg_ref[...], preferred_element_type=F32)
        up = jnp.dot(h, wu_ref[...], preferred_element_type=F32)
        yield
        f = jnp.dot((_silu(gate) * up).astype(BF16), wd_ref[...], preferred_element_type=F32)
        yield
        u = ALPHA * x + (1.0 + _mod_rows(gt_ref, rows)) * f
        o_ref[0, rows, :] = _layernorm(u, g_ref[...], b_ref[...])

    _interleave([chain(r) for r in range(parts)], skew=1)


def _ffn(x, sc, sh, gt, wg, wu, wd, ln_g, ln_b, tm):
    bsz, length, _ = x.shape
    tok = pl.BlockSpec((1, tm, D_MODEL), lambda b, i: (b, i, 0))
    vec = pl.BlockSpec((1, D_MODEL), lambda b, i: (0, 0))
    resident = lambda a: pl.BlockSpec(a.shape, lambda b, i: (0, 0), pipeline_mode=pl.Buffered(1))
    return pl.pallas_call(
        _ffn_kernel,
        grid=(bsz, length // tm),
        in_specs=[tok, _mod_spec(sc, tm), _mod_spec(sh, tm), _mod_spec(gt, tm),
                  resident(wg), resident(wu), resident(wd), vec, vec],
        out_specs=tok,
        out_shape=jax.ShapeDtypeStruct((bsz, length, D_MODEL), F32),
        compiler_params=_cparams("arbitrary", "arbitrary"),
        name="ffn_ln",
    )(x, sc, sh, gt, wg, wu, wd, ln_g, ln_b)


def _moe_kernel(x_ref, sc_ref, sh_ref, gt_ref, rt_ref, wg_ref, wu_ref, wd_ref, g_ref, b_ref, tri_ref, o_ref,
                h_ref, comb_ref, rank_ref, rank_t_ref, cnt_ref, acc_ref, *, half, cap):
    e = pl.program_id(2)
    tm = 2 * half

    @pl.when(e == 0)
    def _():
        h = x_ref[0] * (1.0 + sc_ref[0]) + sh_ref[0]
        h_ref[...] = h.astype(BF16)
        acc_ref[...] = jnp.zeros_like(acc_ref)
        lane = _iota((tm, LANES), 1).astype(F32)
        logits = jnp.where(lane < N_EXPERTS, _mm_hi(h, rt_ref[...]), -jnp.inf)
        m1 = jnp.max(logits, axis=-1, keepdims=True)
        i1 = jnp.min(jnp.where(logits == m1, lane, float(LANES)), axis=-1, keepdims=True)
        rest = jnp.where(lane == i1, -jnp.inf, logits)
        m2 = jnp.max(rest, axis=-1, keepdims=True)
        i2 = jnp.min(jnp.where(rest == m2, lane, float(LANES)), axis=-1, keepdims=True)
        e2 = jnp.exp(m2 - m1)
        den = 1.0 + e2
        comb_ref[...] = jnp.where(lane == i1, 1.0 / den, 0.0) + jnp.where(lane == i2, e2 / den, 0.0)
        sel = (lane == i1) | (lane == i2)
        sel_f = sel.astype(F32)
        for hf in range(2):
            rows = slice(hf * half, (hf + 1) * half)
            s = sel_f[rows, :]
            before = jnp.dot(tri_ref[...], s.astype(BF16), preferred_element_type=F32)
            rank_ref[rows, :] = jnp.where(sel[rows, :], before, -1.0)
            cnt_ref[hf:hf + 1, :] = jnp.sum(s, axis=0, keepdims=True)
        for blk in range(tm // LANES):
            rows = slice(blk * LANES, (blk + 1) * LANES)
            rank_t_ref[:, rows] = rank_ref[rows, :].T

    on_lane = _iota((tm, LANES), 1) == e
    comb_e = jnp.sum(jnp.where(on_lane, comb_ref[...], 0.0), axis=-1, keepdims=True)
    rank_e = jnp.sum(jnp.where(on_lane, rank_ref[...], 0.0), axis=-1, keepdims=True)
    rank_t_e = rank_t_ref[pl.ds(e, 1), :]
    n_max = jnp.max(jnp.where(_iota((2, LANES), 1) == e, cnt_ref[...], 0.0))
    n_pass = (n_max.astype(jnp.int32) + (cap - 1)) // cap

    def one_pass(p, carry):
        off = (p * cap).astype(F32)
        slot_col = _iota((cap, 1), 0).astype(F32) + off
        slot_row = _iota((1, cap), 1).astype(F32) + off
        xs = []
        for hf in range(2):
            rows = slice(hf * half, (hf + 1) * half)
            pick = (rank_t_e[:, rows] == slot_col).astype(BF16)
            xs.append(jnp.dot(pick, h_ref[rows, :], preferred_element_type=F32).astype(BF16))
        xc = jnp.concatenate(xs, axis=0)
        a = _silu(jnp.dot(xc, wg_ref[0], preferred_element_type=F32)) * jnp.dot(
            xc, wu_ref[0], preferred_element_type=F32)
        out_e = jnp.dot(a.astype(BF16), wd_ref[0], preferred_element_type=F32)
        for hf in range(2):
            rows = slice(hf * half, (hf + 1) * half)
            put = (rank_e[rows, :] == slot_row).astype(BF16)
            back = jnp.dot(put, out_e[hf * cap:(hf + 1) * cap, :].astype(BF16), preferred_element_type=F32)
            acc_ref[rows, :] += comb_e[rows, :] * back
        return carry

    lax.fori_loop(0, n_pass, one_pass, 0)

    @pl.when(e == pl.num_programs(2) - 1)
    def _():
        u = ALPHA * x_ref[0] + (1.0 + gt_ref[0]) * acc_ref[...]
        o_ref[0] = _layernorm(u, g_ref[...], b_ref[...])


def _moe(x, sc, sh, gt, router_pad, wg, wu, wd, ln_g, ln_b, tm):
    bsz, length, _ = x.shape
    assert tm % LANES == 0
    half = tm // 2
    cap = MOE_CAP_ROWS if half >= 2 * MOE_CAP_ROWS else half
    tri = (jnp.arange(half)[None, :] < jnp.arange(half)[:, None]).astype(BF16)
    tok = pl.BlockSpec((1, tm, D_MODEL), lambda b, i, e: (b, i, 0))
    vec = pl.BlockSpec((1, D_MODEL), lambda b, i, e: (0, 0))
    wspec = pl.BlockSpec((1, D_MODEL, D_MODEL), lambda b, i, e: (e, 0, 0))
    return pl.pallas_call(
        functools.partial(_moe_kernel, half=half, cap=cap),
        grid=(bsz, length // tm, N_EXPERTS),
        in_specs=[tok, _mod_spec(sc, tm), _mod_spec(sh, tm), _mod_spec(gt, tm),
                  pl.BlockSpec((D_MODEL, LANES), lambda b, i, e: (0, 0)),
                  wspec, wspec, wspec, vec, vec, _const_spec(tri)],
        out_specs=tok,
        out_shape=jax.ShapeDtypeStruct((bsz, length, D_MODEL), F32),
        scratch_shapes=[pltpu.VMEM((tm, D_MODEL), BF16), pltpu.VMEM((tm, LANES), F32),
                        pltpu.VMEM((tm, LANES), F32), pltpu.VMEM((LANES, tm), F32),
                        pltpu.VMEM((2, LANES), F32), pltpu.VMEM((tm, D_MODEL), F32)],
        compiler_params=_cparams("arbitrary", "arbitrary", "arbitrary"),
        name="moe_ln",
    )(x, sc, sh, gt, router_pad, wg, wu, wd, ln_g, ln_b, tri)


def _pad_lanes(vec, offset, width=SMALL_W):
    out = jnp.zeros((1, width), F32)
    return out.at[0, offset:offset + vec.shape[0]].set(vec)


def _layer_params(p, l):
    w_in = p["w_in"][l]
    off = [0]
    for s in (SSD_W, SSD_CONV_DIM, SSD_H, RWKV_FEAT, GLA_HK, GLA_HK, GLA_W, GK_LORA, GLA_W):
        off.append(off[-1] + s)
    piece = lambda i: w_in[:, off[i]:off[i + 1]]
    small = jnp.zeros((D_MODEL, SMALL_W), F32)
    small = small.at[:, DT_OFF:DT_OFF + SSD_H].set(piece(2)).at[:, GLO_OFF:GLO_OFF + GK_LORA].set(piece(7))
    w_pad = jnp.concatenate([piece(0), piece(1), small, piece(3), piece(4), piece(5), piece(6), piece(8), small],
                            axis=1).astype(BF16)
    ssd_p = (p["ssd_conv_w"][l], p["ssd_conv_b"][l][None, :],
             _pad_lanes(p["ssd_dt_bias"][l], DT_OFF), _pad_lanes(p["ssd_a_log"][l], DT_OFF),
             jnp.repeat(p["ssd_d"][l], SSD_P)[None, :], p["ssd_norm_g"][l][None, :])
    lora_w = jnp.zeros((LANES, 3 * RWKV_W), F32)
    lora_w = (lora_w.at[0:32, 0:RWKV_W].set(p["rwkv_w2"][l])
              .at[32:64, RWKV_W:2 * RWKV_W].set(p["rwkv_a2"][l])
              .at[64:128, 2 * RWKV_W:].set(p["rwkv_g2"][l])).astype(BF16)
    row = lambda name: p[name][l].reshape(1, -1)
    rwkv_p = (row("rwkv_mu"), lora_w, row("rwkv_w0"), row("rwkv_a0"), row("rwkv_k_k"), row("rwkv_k_a"),
              row("rwkv_r_k"), row("rwkv_ln_g"), row("rwkv_ln_b"))
    gk2_pad = jnp.zeros((SMALL_W, GLA_HK), F32).at[GLO_OFF:GLO_OFF + GK_LORA].set(p["gla_w_gk2"][l])
    gla_p = (gk2_pad, row("gla_b_gk"), jnp.tile(p["gla_norm_g"][l], GLA_H)[None, :])
    return w_pad, ssd_p, rwkv_p, gla_p


def _block_diag_inv(s_bd, h):
    b, hr, hc = s_bd.shape
    r, c = hr // h, hc // h
    s = s_bd.reshape(b, h, r, h, c)
    return jnp.stack([s[:, i, :, i, :] for i in range(h)], axis=1)


def _ssd_state_unpack(h_bd):
    b = h_bd.shape[0]
    s = h_bd.reshape(b, SSD_G, SSD_N, SSD_H, SSD_P)
    per_group = SSD_H // SSD_G
    heads = [s[:, h // per_group, :, h, :] for h in range(SSD_H)]
    return jnp.swapaxes(jnp.stack(heads, axis=1), 2, 3)


def _tail(x, mod_l, l, p, y_ssd, y_rwkv, y_gla, tm):
    sh1, sc1, gt1, sh2, sc2, gt2 = mod_l
    row = lambda name: p[name][l].reshape(1, -1)
    x = _outproj(y_ssd, y_rwkv, y_gla, x, gt1, p["w_out"][l].astype(BF16), row("ln_mix_g"), row("ln_mix_b"), tm)
    i = l // 2
    if l % 2 == 0:
        x = _ffn(x, sc2, sh2, gt2, p["ffn_w_gate"][i].astype(BF16), p["ffn_w_up"][i].astype(BF16),
                 p["ffn_w_down"][i].astype(BF16), row("ln_ffn_g"), row("ln_ffn_b"), tm)
    else:
        router_pad = jnp.zeros((D_MODEL, LANES), F32).at[:, :N_EXPERTS].set(p["moe_router"][i])
        x = _moe(x, sc2, sh2, gt2, router_pad, p["moe_w_gate"][i].astype(BF16), p["moe_w_up"][i].astype(BF16),
                 p["moe_w_down"][i].astype(BF16), row("ln_ffn_g"), row("ln_ffn_b"), min(2 * tm, x.shape[1]))
    return x


def _forward(x_prompt, x_sample, c_prompt, c_sample, states, p):
    bp, seq, _ = x_prompt.shape
    bs = x_sample.shape[0]
    state_ssd, state_conv, state_rwkv, state_shift, state_gla = states
    mod = _ada(jnp.concatenate([c_prompt, c_sample], axis=0), p["w_ada"], p["b_ada"])

    xp = x_prompt
    xs = x_sample.reshape(1, bs, D_MODEL)
    tm_p = min(512, seq)
    outs_p = [[] for _ in range(5)]
    outs_s = [[] for _ in range(5)]
    for l in range(DEPTH):
        w_pad, ssd_p, rwkv_p, gla_p = _layer_params(p, l)
        mods = jnp.split(mod[l], 6, axis=-1)
        mod_p = [m[:bp, None, :] for m in mods]
        mod_s = [m[None, bp:, :] for m in mods]

        ssd_slab, rw, gla_slab = _inproj(xp, mod_p[1], mod_p[0], w_pad, tm_p)
        y_ssd, conv_new, h_new = _ssd_prompt(
            ssd_slab, jnp.zeros((bp, SSD_CONV_W - 1, SSD_CONV_DIM), F32),
            jnp.zeros((bp, SSD_G * SSD_N, SSD_H * SSD_P), F32), *ssd_p)
        h_new = _ssd_state_unpack(h_new)
        y_rwkv, shift_new, rs_bd = _rwkv_prompt(
            rw, jnp.zeros((bp, 1, RWKV_FEAT), F32), jnp.zeros((bp, RWKV_W, RWKV_W), F32), *rwkv_p)
        y_gla, gs_bd = _gla_prompt(gla_slab, jnp.zeros((bp, GLA_HK, GLA_W), F32), *gla_p)
        xp = _tail(xp, mod_p, l, p, y_ssd, y_rwkv, y_gla, tm_p)
        for acc, s in zip(outs_p, (h_new, conv_new, _block_diag_inv(rs_bd, RWKV_H),
                                   shift_new.reshape(bp, RWKV_FEAT), _block_diag_inv(gs_bd, GLA_H))):
            acc.append(s)

        ssd_slab, rw, gla_slab = _inproj(xs, mod_s[1], mod_s[0], w_pad, bs)
        y_ssd, y_rwkv, y_gla, conv_new, h_new, rs_new, gs_new = _mixer_step(
            ssd_slab[0], rw[0], gla_slab[0], state_conv[l].reshape(bs, -1), state_shift[l], state_ssd[l],
            state_rwkv[l], state_gla[l], ssd_p, rwkv_p, gla_p)
        xs = _tail(xs, mod_s, l, p, y_ssd[None], y_rwkv[None], y_gla[None], bs)
        for acc, s in zip(outs_s, (h_new, conv_new, rs_new, rw[0], gs_new)):
            acc.append(s)

    stack = lambda accs: tuple(jnp.stack(a, axis=0) for a in accs)
    return (xp, xs.reshape(bs, 1, D_MODEL)) + stack(outs_p) + stack(outs_s)


def kernel(x_prompt, x_sample, c_prompt, c_sample, state_ssd, state_ssd_conv, state_rwkv, state_rwkv_shift, state_gla, w_ada, b_ada, w_in, w_out, ssd_conv_w, ssd_conv_b, ssd_dt_bias, ssd_a_log, ssd_d, ssd_norm_g, rwkv_mu, rwkv_w0, rwkv_w2, rwkv_a0, rwkv_a2, rwkv_g2, rwkv_k_k, rwkv_k_a, rwkv_r_k, rwkv_ln_g, rwkv_ln_b, gla_w_gk2, gla_b_gk, gla_norm_g, ln_mix_g, ln_mix_b, ln_ffn_g, ln_ffn_b, ffn_w_gate, ffn_w_up, ffn_w_down, moe_router, moe_w_gate, moe_w_up, moe_w_down):
    p = dict(w_ada=w_ada, b_ada=b_ada, w_in=w_in, w_out=w_out, ssd_conv_w=ssd_conv_w, ssd_conv_b=ssd_conv_b,
             ssd_dt_bias=ssd_dt_bias, ssd_a_log=ssd_a_log, ssd_d=ssd_d, ssd_norm_g=ssd_norm_g,
             rwkv_mu=rwkv_mu, rwkv_w0=rwkv_w0, rwkv_w2=rwkv_w2, rwkv_a0=rwkv_a0, rwkv_a2=rwkv_a2,
             rwkv_g2=rwkv_g2, rwkv_k_k=rwkv_k_k, rwkv_k_a=rwkv_k_a, rwkv_r_k=rwkv_r_k,
             rwkv_ln_g=rwkv_ln_g, rwkv_ln_b=rwkv_ln_b, gla_w_gk2=gla_w_gk2, gla_b_gk=gla_b_gk,
             gla_norm_g=gla_norm_g, ln_mix_g=ln_mix_g, ln_mix_b=ln_mix_b, ln_ffn_g=ln_ffn_g,
             ln_ffn_b=ln_ffn_b, ffn_w_gate=ffn_w_gate, ffn_w_up=ffn_w_up, ffn_w_down=ffn_w_down,
             moe_router=moe_router, moe_w_gate=moe_w_gate, moe_w_up=moe_w_up, moe_w_down=moe_w_down)
    states = (state_ssd, state_ssd_conv, state_rwkv, state_rwkv_shift, state_gla)
    return _forward(x_prompt, x_sample, c_prompt, c_sample, states, p)
```

```python
import functools

import jax
import jax.numpy as jnp
from jax import lax
from jax.experimental import pallas as pl
from jax.experimental.pallas import tpu as pltpu

F32 = jnp.float32
BF16 = jnp.bfloat16

D_MODEL = 1024
DEPTH = 2
SSD_W = 512
SSD_H = 8
SSD_P = 64
SSD_N = 64
SSD_G = 2
SSD_CONV_W = 4
SSD_CONV_DIM = 768
RWKV_W = 256
RWKV_H = 4
RWKV_N = 64
RWKV_FEAT = 896
RWKV_GN_EPS = RWKV_N * 1e-5
GLA_W = 256
GLA_H = 4
GLA_DK = 32
GLA_DV = 64
GLA_HK = GLA_H * GLA_DK
GK_LORA = 16
GATE_NORMALIZER = 16.0
F_DENSE = 2816
N_EXPERTS = 8
ALPHA = (2.0 * DEPTH) ** 0.25
LN_EPS = 1e-5
RMS_EPS = 1e-6

LANES = 128
SMALL_W = LANES
DT_OFF = 0
GLO_OFF = 8
SSD_SLAB = SSD_W + SSD_CONV_DIM
GLA_SLAB = 2 * GLA_HK + 2 * GLA_W
IN_PAD = SSD_SLAB + SMALL_W + RWKV_FEAT + GLA_SLAB
SSD_CHUNK = 128
GLA_CHUNK = 128
GLA_SUB = 8
RWKV_CHUNK = 64
SSD_NB, RWKV_NB, GLA_NB = 2, 8, 4
RWKV_SKEW = 1
MOE_CAP_ROWS = 160
STEP_GRID = 32
VMEM_LIMIT = 56 * 1024 * 1024


def _cparams(*sem):
    return pltpu.CompilerParams(dimension_semantics=sem, vmem_limit_bytes=VMEM_LIMIT)


def _mm(a, b):
    return jnp.dot(a.astype(BF16), b.astype(BF16), preferred_element_type=F32)


def _mm_nt(a, b):
    return lax.dot_general(a.astype(BF16), b.astype(BF16), (((1,), (1,)), ((), ())),
                           preferred_element_type=F32)


def _split_bf16(x, terms):
    parts = []
    for _ in range(terms):
        p = x.astype(BF16)
        parts.append(p)
        x = x - p.astype(F32)
    return parts


def _mm_sel(sel, x, terms=3):
    return sum(jnp.dot(sel, p, preferred_element_type=F32) for p in _split_bf16(x, terms))


def _mm_xsel(x, sel, terms=2):
    return sum(jnp.dot(p, sel, preferred_element_type=F32) for p in _split_bf16(x, terms))


def _mm_hi(a, b):
    a_hi, a_lo = _split_bf16(a, 2)
    b_hi, b_lo = _split_bf16(b, 2)
    dot = lambda x, y: jnp.dot(x, y, preferred_element_type=F32)
    return dot(a_hi, b_hi) + (dot(a_hi, b_lo) + dot(a_lo, b_hi))


def _sigmoid(x):
    return 1.0 / (1.0 + jnp.exp(-x))


def _silu(x):
    return x * _sigmoid(x)


def _softplus(x):
    return jnp.maximum(x, 0.0) + jnp.log1p(jnp.exp(-jnp.abs(x)))


def _iota(shape, dim):
    return lax.broadcasted_iota(jnp.int32, shape, dim)


def _layernorm(u, g, b):
    mu = jnp.mean(u, axis=-1, keepdims=True)
    d = u - mu
    var = jnp.mean(d * d, axis=-1, keepdims=True)
    return d * lax.rsqrt(var + LN_EPS) * g + b


def _tri_incl(n):
    return (jnp.arange(n)[None, :] <= jnp.arange(n)[:, None]).astype(BF16)


def _seg_ones(n, seg, scale=1.0):
    idx = jnp.arange(n) // seg
    return jnp.where(idx[:, None] == idx[None, :], scale, 0.0).astype(BF16)


def _interleave(chains, skew=0):
    chains = list(chains)
    done = [False] * len(chains)
    rnd = 0
    while not all(done):
        for i, ch in enumerate(chains):
            if done[i] or rnd < i * skew:
                continue
            try:
                next(ch)
            except StopIteration:
                done[i] = True
        rnd += 1


def _const_spec(arr):
    nd = arr.ndim
    return pl.BlockSpec(arr.shape, lambda *_: (0,) * nd)


def _ada_kernel(c_ref, w_ref, b_ref, o_ref):
    o_ref[0] = _mm(_silu(c_ref[...]), w_ref[0]) + b_ref[0]


def _ada(c_all, w_ada, b_ada):
    rows = c_all.shape[0]
    tn = 1536
    return pl.pallas_call(
        _ada_kernel,
        grid=(DEPTH, 6 * D_MODEL // tn),
        in_specs=[pl.BlockSpec((rows, D_MODEL), lambda l, j: (0, 0)),
                  pl.BlockSpec((1, D_MODEL, tn), lambda l, j: (l, 0, j)),
                  pl.BlockSpec((1, 1, tn), lambda l, j: (l, 0, j))],
        out_specs=pl.BlockSpec((1, rows, tn), lambda l, j: (l, 0, j)),
        out_shape=jax.ShapeDtypeStruct((DEPTH, rows, 6 * D_MODEL), F32),
        compiler_params=_cparams("arbitrary", "arbitrary"),
        name="ada",
    )(c_all, w_ada, b_ada.reshape(DEPTH, 1, 6 * D_MODEL))


def _inproj_kernel(x_ref, sc_ref, sh_ref, w_ref, o_ssd, o_small, o_rwkv, o_gla):
    h = (x_ref[0] * (1.0 + sc_ref[0]) + sh_ref[0]).astype(BF16)
    off = 0
    for o_ref in (o_ssd, o_small, o_rwkv, o_gla):
        width = o_ref.shape[2]
        o_ref[0] = jnp.dot(h, w_ref[:, off:off + width], preferred_element_type=F32).astype(o_ref.dtype)
        off += width


def _mod_spec(mod, tm):
    if mod.shape[1] == 1:
        return pl.BlockSpec((1, 1, D_MODEL), lambda b, i, *_: (b, 0, 0))
    return pl.BlockSpec((1, tm, D_MODEL), lambda b, i, *_: (b, i, 0))


def _inproj(x, sc, sh, w_pad, tm):
    bsz, length, _ = x.shape
    tok = lambda width: pl.BlockSpec((1, tm, width), lambda b, i: (b, i, 0))
    return pl.pallas_call(
        _inproj_kernel,
        grid=(bsz, length // tm),
        in_specs=[tok(D_MODEL), _mod_spec(sc, tm), _mod_spec(sh, tm),
                  pl.BlockSpec((D_MODEL, IN_PAD), lambda b, i: (0, 0))],
        out_specs=[tok(SSD_SLAB), tok(SMALL_W), tok(RWKV_FEAT), tok(GLA_SLAB)],
        out_shape=[jax.ShapeDtypeStruct((bsz, length, SSD_SLAB), BF16),
                   jax.ShapeDtypeStruct((bsz, length, SMALL_W), F32),
                   jax.ShapeDtypeStruct((bsz, length, RWKV_FEAT), BF16),
                   jax.ShapeDtypeStruct((bsz, length, GLA_SLAB), BF16)],
        compiler_params=_cparams("arbitrary", "arbitrary"),
        name="inproj",
    )(x, sc, sh, w_pad)


def _ssd_token_math(xbc, small, conv_taps, cw, cb, dtb, alog):
    conv = cb + xbc * cw[3:4, :]
    for i in range(SSD_CONV_W - 1):
        conv = conv + conv_taps[i] * cw[i:i + 1, :]
    dt = _softplus(small + dtb)
    return _silu(conv), dt, dt * (-jnp.exp(alog))


def _ssd_finish(y, xs, z, dexp, ng):
    gated = (y + xs * dexp) * _silu(z)
    ms = jnp.mean(gated * gated, axis=-1, keepdims=True)
    return gated * lax.rsqrt(ms + RMS_EPS) * ng


def _ssd_kernel(slab_ref, small_ref, conv0_ref, h0_ref, cw_ref, cb_ref, dtb_ref, alog_ref, dexp_ref, ng_ref, tri_ref,
                head_x_ref, group_mask_ref, pair_mask_ref, y_ref, conv_out_ref, h_out_ref, ext_ref, h_ref):
    c = pl.program_id(1)
    n_chunks = pl.num_programs(1)
    ch = SSD_CHUNK
    causal = _iota((ch, ch), 1) <= _iota((ch, ch), 0)

    nb = slab_ref.shape[0]

    @pl.when(c == 0)
    def _():
        for j in range(nb):
            ext_ref[j, 0:8, :] = jnp.zeros((8, SSD_CONV_DIM), F32)
            ext_ref[j, 5:8, :] = conv0_ref[j]
            h_ref[j] = h0_ref[j]

    def chain(j):
        z = slab_ref[j, :, 0:SSD_W].astype(F32)
        xbc = slab_ref[j, :, SSD_W:].astype(F32)
        small = small_ref[j]

        ext_ref[j, 8:8 + ch, :] = xbc
        taps = [ext_ref[j, 5 + i:5 + i + ch, :] for i in range(SSD_CONV_W - 1)]
        act, dt, dta = _ssd_token_math(xbc, small, taps, cw_ref[...], cb_ref[...], dtb_ref[...], alog_ref[...])
        ext_ref[j, 0:8, :] = ext_ref[j, ch:ch + 8, :]

        xs = act[:, 0:SSD_W]
        bs = act[:, SSD_W:SSD_W + SSD_G * SSD_N]
        cs = act[:, SSD_W + SSD_G * SSD_N:]
        yield
        cum = _mm_sel(tri_ref[...], dta)
        cum_t = cum.T
        dt_t = dt.T
        cum_x = _mm_xsel(cum, head_x_ref[...], 3)
        dt_x = _mm_xsel(dt, head_x_ref[...], 3)
        yield
        last_x = cum_x[ch - 1:ch, :]
        xs_tail = (xs * (jnp.exp(last_x - cum_x) * dt_x)).astype(BF16)
        h_bd = h_ref[j]
        y_state = jnp.dot(cs.astype(BF16), h_bd.astype(BF16), preferred_element_type=F32) * jnp.exp(cum_x)
        upd = jnp.dot(bs.T.astype(BF16), xs_tail, preferred_element_type=F32)
        h_ref[j] = h_bd * jnp.exp(last_x) + upd * group_mask_ref[...]
        yield

        lane_group = _iota((1, SSD_G * SSD_N), 1) // SSD_N
        xs_b = xs.astype(BF16)
        ys = []
        for g in range(SSD_G):
            cb = _mm_nt(jnp.where(lane_group == g, cs, 0.0), bs)
            yield
            for pair in range(SSD_H // SSD_G // 2):
                scores = []
                for hh in range(2):
                    lane = DT_OFF + g * (SSD_H // SSD_G) + 2 * pair + hh
                    decay = jnp.exp(jnp.where(causal, cum[:, lane:lane + 1] - cum_t[lane:lane + 1, :], -jnp.inf))
                    scores.append((cb * decay * dt_t[lane:lane + 1, :]).astype(BF16))
                blk = g * (SSD_H // SSD_G) // 2 + pair
                x_pair = xs_b[:, blk * LANES:(blk + 1) * LANES]
                x_bd = jnp.concatenate([x_pair, x_pair], axis=0) * pair_mask_ref[...]
                ys.append(jnp.dot(jnp.concatenate(scores, axis=1), x_bd, preferred_element_type=F32))
                yield

        y = jnp.concatenate(ys, axis=-1) + y_state
        y_ref[j] = _ssd_finish(y, xs, z, dexp_ref[...], ng_ref[...]).astype(y_ref.dtype)

    _interleave([chain(j) for j in range(nb)])

    @pl.when(c == n_chunks - 1)
    def _():
        for j in range(nb):
            conv_out_ref[j] = ext_ref[j, 5:8, :]
            h_out_ref[j] = h_ref[j]


def _ssd_prompt(slab, small, conv0, h0, cw, cb, dtb, alog, dexp, ng):
    bsz, length, _ = slab.shape
    nb = SSD_NB if bsz % SSD_NB == 0 else 1
    gn, hp = SSD_G * SSD_N, SSD_H * SSD_P
    lane_head = jnp.arange(hp) // SSD_P
    head_x = (jnp.arange(SMALL_W)[:, None] == DT_OFF + lane_head[None, :]).astype(BF16)
    group_mask = (jnp.arange(gn)[:, None] // SSD_N == lane_head[None, :] // (SSD_H // SSD_G)).astype(F32)
    pair_mask = (jnp.arange(2 * SSD_CHUNK)[:, None] // SSD_CHUNK == jnp.arange(2 * SSD_P)[None, :] // SSD_P
                 ).astype(BF16)
    consts = (cw, cb, dtb, alog, dexp, ng, _tri_incl(SSD_CHUNK), head_x, group_mask, pair_mask)
    return pl.pallas_call(
        _ssd_kernel,
        grid=(bsz // nb, length // SSD_CHUNK),
        in_specs=[pl.BlockSpec((nb, SSD_CHUNK, SSD_SLAB), lambda b, c: (b, c, 0)),
                  pl.BlockSpec((nb, SSD_CHUNK, SMALL_W), lambda b, c: (b, c, 0)),
                  pl.BlockSpec((nb, SSD_CONV_W - 1, SSD_CONV_DIM), lambda b, c: (b, 0, 0)),
                  pl.BlockSpec((nb, gn, hp), lambda b, c: (b, 0, 0))]
                 + [_const_spec(a) for a in consts],
        out_specs=[pl.BlockSpec((nb, SSD_CHUNK, SSD_W), lambda b, c: (b, c, 0)),
                   pl.BlockSpec((nb, SSD_CONV_W - 1, SSD_CONV_DIM), lambda b, c: (b, 0, 0)),
                   pl.BlockSpec((nb, gn, hp), lambda b, c: (b, 0, 0))],
        out_shape=[jax.ShapeDtypeStruct((bsz, length, SSD_W), BF16),
                   jax.ShapeDtypeStruct((bsz, SSD_CONV_W - 1, SSD_CONV_DIM), F32),
                   jax.ShapeDtypeStruct((bsz, gn, hp), F32)],
        scratch_shapes=[pltpu.VMEM((nb, SSD_CHUNK + 8, SSD_CONV_DIM), F32),
                        pltpu.VMEM((nb, gn, hp), F32)],
        compiler_params=_cparams("arbitrary", "arbitrary"),
        name="ssd_scan",
    )(slab, small, conv0, h0, *consts)


def _rwkv_token_math(rw, prev, mu, lora_w, w0, a0, k_k, k_a, head_ones):
    mix = rw + mu * (prev - rw)
    r = mix[:, 0:RWKV_W]
    k = mix[:, RWKV_W:2 * RWKV_W]
    v = mix[:, 2 * RWKV_W:3 * RWKV_W]
    lora = mix[:, 3 * RWKV_W:]
    lane = _iota(lora.shape, 1)
    act = jnp.where(lane < 32, jnp.tanh(lora), jnp.where(lane < 64, lora, _sigmoid(lora)))
    lo = _mm(act, lora_w)
    w_log = -_softplus(-(w0 + lo[:, 0:RWKV_W])) - 0.5
    logw = -jnp.exp(w_log)
    aic = _sigmoid(a0 + lo[:, RWKV_W:2 * RWKV_W])
    gate = lo[:, 2 * RWKV_W:]
    kkf = k * k_k
    kk = kkf * lax.rsqrt(_mm_xsel(kkf * kkf, head_ones) + 1e-12)
    k2 = k * (1.0 + (aic - 1.0) * k_a)
    return r, k2, v, logw, aic, gate, kk


def _rwkv_finish(o, r, k2, v, gate, r_k, ln_g, ln_b, head_ones, head_avg):
    mu = _mm_xsel(o, head_avg)
    d = o - mu
    var = _mm_xsel(d * d, head_avg)
    on = d * lax.rsqrt(var + RWKV_GN_EPS) * ln_g + ln_b
    bonus = _mm_xsel(r * k2 * r_k, head_ones) * v
    return (on + bonus) * gate


def _rwkv_kernel(rw_ref, shift0_ref, s0_ref, mu_ref, lw_ref, w0_ref, a0_ref, kk_ref, ka_ref,
                 rk_ref, lng_ref, lnb_ref, ones_ref, avg_ref, tri_ref, strict_ref, incl_ref, eye_ref,
                 y_ref, shift_out_ref, s_out_ref, ext_ref, s_ref):
    c = pl.program_id(1)
    n_chunks = pl.num_programs(1)
    ch = RWKV_CHUNK
    nh = RWKV_H
    w = RWKV_W
    head_bd = ones_ref[...]
    head_bd_f = head_bd.astype(F32)
    tile4 = lambda x: jnp.concatenate([x] * nh, axis=0)

    nb = rw_ref.shape[0]

    @pl.when(c == 0)
    def _():
        for j in range(nb):
            ext_ref[j, 0:8, :] = jnp.zeros((8, RWKV_FEAT), F32)
            ext_ref[j, 7:8, :] = shift0_ref[j]
            s_ref[j] = s0_ref[j]

    def chain(j):
        rw = rw_ref[j].astype(F32)
        ext_ref[j, 8:8 + ch, :] = rw
        prev = ext_ref[j, 7:7 + ch, :]
        ext_ref[j, 7:8, :] = rw[ch - 1:ch, :]

        r, k2, v, logw, aic, gate, kk = _rwkv_token_math(
            rw, prev, mu_ref[...], lw_ref[...], w0_ref[...], a0_ref[...], kk_ref[...], ka_ref[...], head_bd)
        yield

        cumw = _mm_sel(tri_ref[...], logw)
        yield
        last = cumw[ch - 1:ch, :]
        inv_g = jnp.exp(-cumw)
        to_end = jnp.exp(last - cumw)
        kka = kk * aic
        a_t = (-kk * jnp.exp(cumw - logw)).astype(BF16)
        r_t = (r * jnp.exp(cumw)).astype(BF16)
        b_t = (kka * inv_g).astype(BF16)
        k_t = (k2 * inv_g).astype(BF16)
        bk_end = jnp.concatenate([kka * to_end, k2 * to_end], axis=0).astype(BF16)

        bd = lambda x: tile4(x.astype(BF16)) * head_bd
        nt = lambda x, y: lax.dot_general(x, y, (((1,), (1,)), ((), ())), preferred_element_type=F32)
        vb = v.astype(BF16)
        ar = jnp.concatenate([a_t, r_t], axis=0)
        prod = nt(ar, jnp.concatenate([bd(b_t), bd(k_t)], axis=0))
        a_ab = prod[0:ch, 0:w] * strict_ref[...]
        a_ak = prod[0:ch, w:2 * w] * strict_ref[...]
        a_rbk = (prod[ch:2 * ch, :] * incl_ref[...]).astype(BF16)
        yield
        s_bd = s_ref[j]
        uo = nt(ar, s_bd.astype(BF16))
        rhs = uo[0:ch, :] + jnp.dot(a_ak.astype(BF16), bd(vb), preferred_element_type=F32)
        yield

        t_inv = eye_ref[...] + a_ab
        x = a_ab
        x_bd = bd(x)
        power = 1
        while 2 * power < ch:
            x = jnp.dot(x.astype(BF16), x_bd, preferred_element_type=F32)
            x_bd = bd(x)
            yield
            t_inv = t_inv + jnp.dot(t_inv.astype(BF16), x_bd, preferred_element_type=F32)
            yield
            power *= 2

        p = jnp.dot(t_inv.astype(BF16), bd(rhs), preferred_element_type=F32)
        yield
        pb = p.astype(BF16)
        o = uo[ch:2 * ch, :] + jnp.dot(a_rbk, jnp.concatenate([bd(pb), bd(vb)], axis=0),
                                       preferred_element_type=F32)
        pv = jnp.concatenate([pb, vb], axis=0)
        yield

        upd = jnp.dot(pv.astype(F32).T.astype(BF16), bk_end, preferred_element_type=F32)
        s_ref[j] = s_bd * jnp.exp(last) + upd * head_bd_f
        yield

        y_ref[j] = _rwkv_finish(o, r, k2, v, gate, rk_ref[...], lng_ref[...], lnb_ref[...],
                                head_bd, avg_ref[...]).astype(y_ref.dtype)

    _interleave([chain(j) for j in range(nb)], skew=RWKV_SKEW)

    @pl.when(c == n_chunks - 1)
    def _():
        for j in range(nb):
            shift_out_ref[j] = ext_ref[j, 7:8, :]
            s_out_ref[j] = s_ref[j]


def _rwkv_consts():
    ch = RWKV_CHUNK
    assert ch == RWKV_N, "bd() reuses the head mask, which needs chunk rows == head width"
    t = jnp.arange(ch)[:, None]
    j = jnp.arange(2 * RWKV_W)[None, :] % ch
    strict = (j[:, :RWKV_W] < t).astype(F32)
    incl = (j <= t).astype(F32)
    eye = (j[:, :RWKV_W] == t).astype(F32)
    return (_seg_ones(RWKV_W, RWKV_N), _seg_ones(RWKV_W, RWKV_N, 1.0 / RWKV_N), _tri_incl(ch),
            strict, incl, eye)


def _rwkv_prompt(rw, shift0, s0_bd, *params):
    bsz, length, _ = rw.shape
    nb = RWKV_NB if bsz % RWKV_NB == 0 else 1
    consts = tuple(params) + _rwkv_consts()
    return pl.pallas_call(
        _rwkv_kernel,
        grid=(bsz // nb, length // RWKV_CHUNK),
        in_specs=[pl.BlockSpec((nb, RWKV_CHUNK, RWKV_FEAT), lambda b, c: (b, c, 0)),
                  pl.BlockSpec((nb, 1, RWKV_FEAT), lambda b, c: (b, 0, 0)),
                  pl.BlockSpec((nb, RWKV_W, RWKV_W), lambda b, c: (b, 0, 0))]
                 + [_const_spec(a) for a in consts],
        out_specs=[pl.BlockSpec((nb, RWKV_CHUNK, RWKV_W), lambda b, c: (b, c, 0)),
                   pl.BlockSpec((nb, 1, RWKV_FEAT), lambda b, c: (b, 0, 0)),
                   pl.BlockSpec((nb, RWKV_W, RWKV_W), lambda b, c: (b, 0, 0))],
        out_shape=[jax.ShapeDtypeStruct((bsz, length, RWKV_W), BF16),
                   jax.ShapeDtypeStruct((bsz, 1, RWKV_FEAT), F32),
                   jax.ShapeDtypeStruct((bsz, RWKV_W, RWKV_W), F32)],
        scratch_shapes=[pltpu.VMEM((nb, RWKV_CHUNK + 8, RWKV_FEAT), F32),
                        pltpu.VMEM((nb, RWKV_W, RWKV_W), F32)],
        compiler_params=_cparams("arbitrary", "arbitrary"),
        name="rwkv_scan",
    )(rw, shift0, s0_bd, *consts)


def _gla_gate_log(small, gk2_pad, b_gk):
    x = _mm_hi(small, gk2_pad) + b_gk
    return -_softplus(-x) / GATE_NORMALIZER


def _gla_finish(o, gg, norm_g, head_avg):
    ms = _mm_xsel(o * o, head_avg)
    return o * lax.rsqrt(ms + RMS_EPS) * norm_g * _silu(gg)


def _gla_kernel(slab_ref, small_ref, s0_ref, gk2_ref, bgk_ref, ng_ref, avg_ref, tri_ref, expand_ref,
                y_ref, s_out_ref, s_ref):
    c = pl.program_id(1)
    n_chunks = pl.num_programs(1)
    ch = GLA_CHUNK
    sub = GLA_SUB
    hk = GLA_HK
    expand = expand_ref[...]
    expand_f = expand.astype(F32)
    pos = _iota((ch, 1), 0) % sub
    t_idx = _iota((ch, 1), 0)
    hmask_k = (_iota((GLA_H * sub, hk), 0) // sub == _iota((GLA_H * sub, hk), 1) // GLA_DK).astype(F32)
    hmask_v = (_iota((GLA_H * sub, GLA_W), 0) // sub == _iota((GLA_H * sub, GLA_W), 1) // GLA_DV).astype(F32)

    nb = slab_ref.shape[0]

    @pl.when(c == 0)
    def _():
        s_ref[...] = s0_ref[...]

    def chain(j):
        q = slab_ref[j, :, 0:hk].astype(F32) * (GLA_DK ** -0.5)
        k = slab_ref[j, :, hk:2 * hk].astype(F32)
        v = slab_ref[j, :, 2 * hk:2 * hk + GLA_W].astype(F32)
        gg = slab_ref[j, :, 2 * hk + GLA_W:].astype(F32)
        small = small_ref[j]

        lg = _gla_gate_log(small, gk2_ref[...], bgk_ref[...])
        yield
        cum = _mm_sel(tri_ref[...], lg)
        yield
        last = cum[ch - 1:ch, :]
        s_bd = s_ref[j]

        o = _mm(q * jnp.exp(cum), s_bd)

        group = 4
        for d0 in range(0, sub, group):
            prods, v_shift = [], []
            for delta in range(d0, d0 + group):
                if delta == 0:
                    k_s, c_s, v_s = k, cum, v
                else:
                    k_s = pltpu.roll(k, delta, 0)
                    c_s = pltpu.roll(cum, delta, 0)
                    v_s = pltpu.roll(v, delta, 0)
                valid = pos >= delta
                w_pair = jnp.exp(jnp.where(valid, cum - c_s, 0.0))
                prods.append(jnp.where(valid, q * k_s * w_pair, 0.0).astype(BF16))
                v_shift.append(v_s)
            yield
            att = jnp.dot(jnp.concatenate(prods, axis=0), expand, preferred_element_type=F32)
            o = o + sum(att[g * ch:(g + 1) * ch, :] * v_shift[g] for g in range(group))
            yield

        vb = v.astype(BF16)
        atts = []
        for blk in range(1, ch // sub):
            lo = blk * sub
            ref_pt = cum[lo - 1:lo, :]
            qd = q[lo:lo + sub, :] * jnp.exp(cum[lo:lo + sub, :] - ref_pt)
            kp = k * jnp.exp(jnp.where(t_idx < lo, ref_pt - cum, -jnp.inf))
            q4 = jnp.concatenate([qd] * GLA_H, axis=0) * hmask_k
            atts.append(_mm_nt(q4, kp).astype(BF16))
            yield
        o4 = jnp.dot(jnp.concatenate(atts, axis=0), vb, preferred_element_type=F32)
        yield
        cross = [jnp.zeros((sub, GLA_W), F32)]
        for blk in range(ch // sub - 1):
            o4_b = o4[blk * GLA_H * sub:(blk + 1) * GLA_H * sub, :] * hmask_v
            cross.append(sum(o4_b[h * sub:(h + 1) * sub, :] for h in range(GLA_H)))
        o = o + jnp.concatenate(cross, axis=0)

        kt_t = (k * jnp.exp(last - cum)).T
        s_ref[j] = s_bd * _col(jnp.exp(last)) + _mm(kt_t, vb) * expand_f
        yield

        y_ref[j] = _gla_finish(o, gg, ng_ref[...], avg_ref[...]).astype(y_ref.dtype)

    _interleave([chain(j) for j in range(nb)])

    @pl.when(c == n_chunks - 1)
    def _():
        s_out_ref[...] = s_ref[...]


def _gla_prompt(slab, small, s0_bd, gk2_pad, b_gk, ng):
    bsz, length, _ = slab.shape
    nb = GLA_NB if bsz % GLA_NB == 0 else 1
    expand = (jnp.arange(GLA_HK)[:, None] // GLA_DK == jnp.arange(GLA_W)[None, :] // GLA_DV).astype(BF16)
    consts = (gk2_pad, b_gk, ng, _seg_ones(GLA_W, GLA_DV, 1.0 / GLA_DV), _tri_incl(GLA_CHUNK), expand)
    return pl.pallas_call(
        _gla_kernel,
        grid=(bsz // nb, length // GLA_CHUNK),
        in_specs=[pl.BlockSpec((nb, GLA_CHUNK, GLA_SLAB), lambda b, c: (b, c, 0)),
                  pl.BlockSpec((nb, GLA_CHUNK, SMALL_W), lambda b, c: (b, c, 0)),
                  pl.BlockSpec((nb, GLA_HK, GLA_W), lambda b, c: (b, 0, 0))]
                 + [_const_spec(a) for a in consts],
        out_specs=[pl.BlockSpec((nb, GLA_CHUNK, GLA_W), lambda b, c: (b, c, 0)),
                   pl.BlockSpec((nb, GLA_HK, GLA_W), lambda b, c: (b, 0, 0))],
        out_shape=[jax.ShapeDtypeStruct((bsz, length, GLA_W), BF16),
                   jax.ShapeDtypeStruct((bsz, GLA_HK, GLA_W), F32)],
        scratch_shapes=[pltpu.VMEM((nb, GLA_HK, GLA_W), F32)],
        compiler_params=_cparams("arbitrary", "arbitrary"),
        name="gla_scan",
    )(slab, small, s0_bd, *consts)


def _col(row):
    n = row.shape[1]
    eye = _iota((n, n), 0) == _iota((n, n), 1)
    return jnp.sum(jnp.where(eye, row, 0.0), axis=1, keepdims=True)


def _store_t(dst_ref, row0, x):
    for cblk in range(x.shape[1] // LANES):
        dst_ref[row0 + cblk * LANES:row0 + (cblk + 1) * LANES, :] = x[:, cblk * LANES:(cblk + 1) * LANES].T


def _load_t(src_ref, width):
    return jnp.concatenate([src_ref[cblk * LANES:(cblk + 1) * LANES, :].T for cblk in range(width // LANES)],
                           axis=1)


_TS_X, _TS_B, _TS_C, _TS_DT, _TS_DA = 0, SSD_W, SSD_W + SSD_G * SSD_N, SSD_CONV_DIM, SSD_CONV_DIM + SMALL_W
_TR_R, _TR_W, _TR_K, _TR_V, _TR_KK, _TR_KA = (i * RWKV_W for i in range(6))
_TG_Q, _TG_K, _TG_G, _TG_V = 0, GLA_HK, 2 * GLA_HK, 3 * GLA_HK


def _step_kernel(ssd_ref, small_ref, rw_ref, gla_ref, conv0_ref, shift0_ref, h0_ref, rs0_ref, gs0_ref,
                 cw_ref, cb_ref, dtb_ref, alog_ref, dexp_ref, sng_ref,
                 mu_ref, lw_ref, w0_ref, a0_ref, kk_ref, ka_ref, rk_ref, lng_ref, lnb_ref,
                 gk2_ref, bgk_ref, gng_ref, ones_ref, ravg_ref, gavg_ref,
                 y_ssd_ref, y_rwkv_ref, y_gla_ref, conv_out_ref, h_out_ref, rs_out_ref, gs_out_ref,
                 t_ssd, t_rwkv, t_gla, o_ssd_t, o_rwkv_t, o_gla_t):
    i = pl.program_id(0)
    hk = GLA_HK

    def ssd_tokens():
        xbc = ssd_ref[:, SSD_W:].astype(F32)
        taps = [conv0_ref[:, t * SSD_CONV_DIM:(t + 1) * SSD_CONV_DIM] for t in range(SSD_CONV_W - 1)]
        return xbc, _ssd_token_math(xbc, small_ref[...], taps, cw_ref[...], cb_ref[...],
                                    dtb_ref[...], alog_ref[...])

    def rwkv_tokens():
        return _rwkv_token_math(rw_ref[...].astype(F32), shift0_ref[...], mu_ref[...], lw_ref[...],
                                w0_ref[...], a0_ref[...], kk_ref[...], ka_ref[...], ones_ref[...])

    @pl.when(i == 0)
    def _():
        xbc, (act, dt, dta) = ssd_tokens()
        conv_out_ref[:, 0:2 * SSD_CONV_DIM] = conv0_ref[:, SSD_CONV_DIM:]
        conv_out_ref[:, 2 * SSD_CONV_DIM:] = xbc
        _store_t(t_ssd, _TS_X, act)
        _store_t(t_ssd, _TS_DT, dt)
        _store_t(t_ssd, _TS_DA, jnp.exp(dta))
        r, k2, v, logw, aic, gate, kk = rwkv_tokens()
        for off, x in ((_TR_R, r), (_TR_W, jnp.exp(logw)), (_TR_K, k2), (_TR_V, v), (_TR_KK, kk),
                       (_TR_KA, kk * aic)):
            _store_t(t_rwkv, off, x)
        lg = _gla_gate_log(small_ref[...], gk2_ref[...], bgk_ref[...])
        _store_t(t_gla, _TG_Q, gla_ref[:, 0:hk].astype(F32) * (GLA_DK ** -0.5))
        _store_t(t_gla, _TG_K, gla_ref[:, hk:2 * hk].astype(F32))
        _store_t(t_gla, _TG_G, jnp.exp(lg))
        _store_t(t_gla, _TG_V, gla_ref[:, 2 * hk:2 * hk + GLA_W].astype(F32))
        o_gla_t[...] = jnp.zeros_like(o_gla_t)

    def row(ref, idx):
        return ref[pl.ds(idx, 1), :]

    def rows(ref, idx, count):
        return ref[pl.ds(pl.multiple_of(idx, count), count), :]

    units = h0_ref.shape[0] // SSD_N
    per_head = SSD_P // units
    h = i // per_head
    p0 = (i % per_head) * units
    b_t = rows(t_ssd, _TS_B + (h // (SSD_H // SSD_G)) * SSD_N, SSD_N)
    c_t = rows(t_ssd, _TS_C + (h // (SSD_H // SSD_G)) * SSD_N, SSD_N)
    da_row = row(t_ssd, _TS_DA + DT_OFF + h)
    dt_row = row(t_ssd, _TS_DT + DT_OFF + h)
    for u in range(units):
        p_idx = h * SSD_P + p0 + u
        blk = slice(u * SSD_N, (u + 1) * SSD_N)
        h_new = h0_ref[blk, :] * da_row + (row(t_ssd, _TS_X + p_idx) * dt_row) * b_t
        o_ssd_t[pl.ds(p_idx, 1), :] = jnp.sum(h_new * c_t, axis=0, keepdims=True)
        h_out_ref[blk, :] = h_new

    units = rs0_ref.shape[0] // RWKV_N
    per_head = RWKV_N // units
    h = i // per_head
    v0 = (i % per_head) * units
    seg = lambda off: rows(t_rwkv, off + h * RWKV_N, RWKV_N)
    r_t, w_t, k_t, kk_t, ka_t = seg(_TR_R), seg(_TR_W), seg(_TR_K), seg(_TR_KK), seg(_TR_KA)
    for u in range(units):
        v_idx = h * RWKV_N + v0 + u
        blk = slice(u * RWKV_N, (u + 1) * RWKV_N)
        s = rs0_ref[blk, :]
        sa = jnp.sum(s * (-kk_t), axis=0, keepdims=True)
        s_new = s * w_t + sa * ka_t + row(t_rwkv, _TR_V + v_idx) * k_t
        o_rwkv_t[pl.ds(v_idx, 1), :] = jnp.sum(s_new * r_t, axis=0, keepdims=True)
        rs_out_ref[blk, :] = s_new

    units = gs0_ref.shape[0] // GLA_DV
    per_head = GLA_DK // units
    h = i // per_head
    k0 = (i % per_head) * units
    v_t = rows(t_gla, _TG_V + h * GLA_DV, GLA_DV)
    acc = jnp.zeros((GLA_DV, LANES), F32)
    for u in range(units):
        k_idx = h * GLA_DK + k0 + u
        blk = slice(u * GLA_DV, (u + 1) * GLA_DV)
        s_new = gs0_ref[blk, :] * row(t_gla, _TG_G + k_idx) + row(t_gla, _TG_K + k_idx) * v_t
        acc = acc + row(t_gla, _TG_Q + k_idx) * s_new
        gs_out_ref[blk, :] = s_new
    o_rows = pl.ds(pl.multiple_of(h * GLA_DV, GLA_DV), GLA_DV)
    o_gla_t[o_rows, :] = o_gla_t[o_rows, :] + acc

    @pl.when(i == pl.num_programs(0) - 1)
    def _():
        _, (act, _, _) = ssd_tokens()
        y_ssd_ref[...] = _ssd_finish(_load_t(o_ssd_t, SSD_W), act[:, 0:SSD_W], ssd_ref[:, 0:SSD_W].astype(F32),
                                     dexp_ref[...], sng_ref[...]).astype(y_ssd_ref.dtype)
        r, k2, v, _, _, gate, _ = rwkv_tokens()
        y_rwkv_ref[...] = _rwkv_finish(_load_t(o_rwkv_t, RWKV_W), r, k2, v, gate, rk_ref[...], lng_ref[...],
                                       lnb_ref[...], ones_ref[...], ravg_ref[...]).astype(y_rwkv_ref.dtype)
        y_gla_ref[...] = _gla_finish(_load_t(o_gla_t, GLA_W), gla_ref[:, 2 * hk + GLA_W:].astype(F32),
                                     gng_ref[...], gavg_ref[...]).astype(y_gla_ref.dtype)


def _mixer_step(ssd_slab, small, rw, gla_slab, conv0, shift0, h0, rs0, gs0, ssd_p, rwkv_p, gla_p):
    bsz = ssd_slab.shape[0]
    assert bsz == LANES, "the single-token mixer keeps exactly one vreg row of batch entries on the lanes"
    flat_t = lambda s: s.reshape(bsz, -1).T
    h0, rs0, gs0 = flat_t(h0), flat_t(rs0), flat_t(gs0)
    consts = (tuple(ssd_p) + tuple(rwkv_p) + tuple(gla_p)
              + (_seg_ones(RWKV_W, RWKV_N), _seg_ones(RWKV_W, RWKV_N, 1.0 / RWKV_N),
                 _seg_ones(GLA_W, GLA_DV, 1.0 / GLA_DV)))
    full = lambda a: pl.BlockSpec(a.shape, lambda i: (0,) * a.ndim)
    cols = lambda a: pl.BlockSpec((a.shape[0] // STEP_GRID, bsz), lambda i: (i, 0))
    tokens = (ssd_slab, small, rw, gla_slab, conv0, shift0)
    outs = pl.pallas_call(
        _step_kernel,
        grid=(STEP_GRID,),
        in_specs=[full(a) for a in tokens] + [cols(h0), cols(rs0), cols(gs0)] + [full(a) for a in consts],
        out_specs=[full(jax.ShapeDtypeStruct((bsz, w), BF16)) for w in (SSD_W, RWKV_W, GLA_W)]
                  + [full(conv0), cols(h0), cols(rs0), cols(gs0)],
        out_shape=[jax.ShapeDtypeStruct((bsz, SSD_W), BF16),
                   jax.ShapeDtypeStruct((bsz, RWKV_W), BF16),
                   jax.ShapeDtypeStruct((bsz, GLA_W), BF16),
                   jax.ShapeDtypeStruct(conv0.shape, F32),
                   jax.ShapeDtypeStruct(h0.shape, F32),
                   jax.ShapeDtypeStruct(rs0.shape, F32),
                   jax.ShapeDtypeStruct(gs0.shape, F32)],
        scratch_shapes=[pltpu.VMEM((SSD_CONV_DIM + 2 * SMALL_W, LANES), F32),
                        pltpu.VMEM((6 * RWKV_W, LANES), F32),
                        pltpu.VMEM((3 * GLA_HK + GLA_W, LANES), F32),
                        pltpu.VMEM((SSD_W, LANES), F32),
                        pltpu.VMEM((RWKV_W, LANES), F32),
                        pltpu.VMEM((GLA_W, LANES), F32)],
        compiler_params=_cparams("arbitrary"),
        name="mixer_step",
    )(*tokens, h0, rs0, gs0, *consts)
    y_ssd, y_rwkv, y_gla, conv_new, h_new, rs_new, gs_new = outs
    return (y_ssd, y_rwkv, y_gla, conv_new.reshape(bsz, SSD_CONV_W - 1, SSD_CONV_DIM),
            h_new.T.reshape(bsz, SSD_H, SSD_P, SSD_N), rs_new.T.reshape(bsz, RWKV_H, RWKV_N, RWKV_N),
            gs_new.T.reshape(bsz, GLA_H, GLA_DK, GLA_DV))


def _mod_rows(mod_ref, rows):
    return mod_ref[0] if mod_ref.shape[1] == 1 else mod_ref[0, rows, :]


def _row_parts(tm):
    return 2 if tm % 256 == 0 else 1


def _outproj_kernel(ys_ref, yr_ref, yg_ref, x_ref, gt_ref, w_ref, g_ref, b_ref, o_ref):
    tm = x_ref.shape[1]
    parts = _row_parts(tm)

    def chain(r):
        rows = slice(r * tm // parts, (r + 1) * tm // parts)
        m = (jnp.dot(ys_ref[0, rows, :], w_ref[0:SSD_W, :], preferred_element_type=F32)
             + jnp.dot(yr_ref[0, rows, :], w_ref[SSD_W:SSD_W + RWKV_W, :], preferred_element_type=F32)
             + jnp.dot(yg_ref[0, rows, :], w_ref[SSD_W + RWKV_W:, :], preferred_element_type=F32))
        yield
        u = ALPHA * x_ref[0, rows, :] + (1.0 + _mod_rows(gt_ref, rows)) * m
        o_ref[0, rows, :] = _layernorm(u, g_ref[...], b_ref[...])

    _interleave([chain(r) for r in range(parts)], skew=1)


def _outproj(y_ssd, y_rwkv, y_gla, x, gt, w_out, ln_g, ln_b, tm):
    bsz, length, _ = x.shape
    tok = lambda width: pl.BlockSpec((1, tm, width), lambda b, i: (b, i, 0))
    vec = pl.BlockSpec((1, D_MODEL), lambda b, i: (0, 0))
    return pl.pallas_call(
        _outproj_kernel,
        grid=(bsz, length // tm),
        in_specs=[tok(SSD_W), tok(RWKV_W), tok(GLA_W), tok(D_MODEL), _mod_spec(gt, tm),
                  pl.BlockSpec((D_MODEL, D_MODEL), lambda b, i: (0, 0)), vec, vec],
        out_specs=tok(D_MODEL),
        out_shape=jax.ShapeDtypeStruct((bsz, length, D_MODEL), F32),
        compiler_params=_cparams("arbitrary", "arbitrary"),
        name="outproj_ln",
    )(y_ssd, y_rwkv, y_gla, x, gt, w_out, ln_g, ln_b)


def _ffn_kernel(ys_ref, yr_ref, yg_ref, x_ref, gt1_ref, sc_ref, sh_ref, gt2_ref, wo_ref, g1_ref, b1_ref,
                wg_ref, wu_ref, wd_ref, g2_ref, b2_ref, o_ref):
    tm = x_ref.shape[1]
    parts = _row_parts(tm)

    def chain(r):
        rows = slice(r * tm // parts, (r + 1) * tm // parts)
        m = (jnp.dot(ys_ref[0, rows, :], wo_ref[0:SSD_W, :], preferred_element_type=F32)
             + jnp.dot(yr_ref[0, rows, :], wo_ref[SSD_W:SSD_W + RWKV_W, :], preferred_element_type=F32)
             + jnp.dot(yg_ref[0, rows, :], wo_ref[SSD_W + RWKV_W:, :], preferred_element_type=F32))
        yield
        x1 = _layernorm(ALPHA * x_ref[0, rows, :] + (1.0 + _mod_rows(gt1_ref, rows)) * m,
                        g1_ref[...], b1_ref[...])
        h = (x1 * (1.0 + _mod_rows(sc_ref, rows)) + _mod_rows(sh_ref, rows)).astype(BF16)
        yield
        gate = jnp.dot(h, wg_ref[...], preferred_element_type=F32)
        up = jnp.dot(h, wu_ref[...], preferred_element_type=F32)
        yield
        f = jnp.dot((_silu(gate) * up).astype(BF16), wd_ref[...], preferred_element_type=F32)
        yield
        u = ALPHA * x1 + (1.0 + _mod_rows(gt2_ref, rows)) * f
        o_ref[0, rows, :] = _layernorm(u, g2_ref[...], b2_ref[...])

    _interleave([chain(r) for r in range(parts)], skew=1)


def _outproj_ffn(y_ssd, y_rwkv, y_gla, x, gt1, sc2, sh2, gt2, w_out, ln1_g, ln1_b, wg, wu, wd, ln2_g, ln2_b, tm):
    bsz, length, _ = x.shape
    tok = lambda width: pl.BlockSpec((1, tm, width), lambda b, i: (b, i, 0))
    vec = pl.BlockSpec((1, D_MODEL), lambda b, i: (0, 0))
    resident = lambda a: pl.BlockSpec(a.shape, lambda b, i: (0, 0), pipeline_mode=pl.Buffered(1))
    return pl.pallas_call(
        _ffn_kernel,
        grid=(bsz, length // tm),
        in_specs=[tok(SSD_W), tok(RWKV_W), tok(GLA_W), tok(D_MODEL),
                  _mod_spec(gt1, tm), _mod_spec(sc2, tm), _mod_spec(sh2, tm), _mod_spec(gt2, tm),
                  resident(w_out), vec, vec, resident(wg), resident(wu), resident(wd), vec, vec],
        out_specs=tok(D_MODEL),
        out_shape=jax.ShapeDtypeStruct((bsz, length, D_MODEL), F32),
        compiler_params=_cparams("arbitrary", "arbitrary"),
        name="outproj_ffn_ln",
    )(y_ssd, y_rwkv, y_gla, x, gt1, sc2, sh2, gt2, w_out, ln1_g, ln1_b, wg, wu, wd, ln2_g, ln2_b)


def _moe_kernel(x_ref, sc_ref, sh_ref, gt_ref, rt_ref, wg_ref, wu_ref, wd_ref, g_ref, b_ref, tri_ref, o_ref,
                h_ref, comb_ref, rank_ref, rank_t_ref, cnt_ref, acc_ref, *, half, cap):
    e = pl.program_id(2)
    tm = 2 * half

    @pl.when(e == 0)
    def _():
        h = x_ref[0] * (1.0 + sc_ref[0]) + sh_ref[0]
        h_ref[...] = h.astype(BF16)
        acc_ref[...] = jnp.zeros_like(acc_ref)
        lane = _iota((tm, LANES), 1).astype(F32)
        logits = jnp.where(lane < N_EXPERTS, _mm_hi(h, rt_ref[...]), -jnp.inf)
        m1 = jnp.max(logits, axis=-1, keepdims=True)
        i1 = jnp.min(jnp.where(logits == m1, lane, float(LANES)), axis=-1, keepdims=True)
        rest = jnp.where(lane == i1, -jnp.inf, logits)
        m2 = jnp.max(rest, axis=-1, keepdims=True)
        i2 = jnp.min(jnp.where(rest == m2, lane, float(LANES)), axis=-1, keepdims=True)
        e2 = jnp.exp(m2 - m1)
        den = 1.0 + e2
        comb_ref[...] = jnp.where(lane == i1, 1.0 / den, 0.0) + jnp.where(lane == i2, e2 / den, 0.0)
        sel = (lane == i1) | (lane == i2)
        sel_f = sel.astype(F32)
        for hf in range(2):
            rows = slice(hf * half, (hf + 1) * half)
            s = sel_f[rows, :]
            before = jnp.dot(tri_ref[...], s.astype(BF16), preferred_element_type=F32)
            rank_ref[rows, :] = jnp.where(sel[rows, :], before, -1.0)
            cnt_ref[hf:hf + 1, :] = jnp.sum(s, axis=0, keepdims=True)
        for blk in range(tm // LANES):
            rows = slice(blk * LANES, (blk + 1) * LANES)
            rank_t_ref[:, rows] = rank_ref[rows, :].T

    on_lane = _iota((tm, LANES), 1) == e
    comb_e = jnp.sum(jnp.where(on_lane, comb_ref[...], 0.0), axis=-1, keepdims=True)
    rank_e = jnp.sum(jnp.where(on_lane, rank_ref[...], 0.0), axis=-1, keepdims=True)
    rank_t_e = rank_t_ref[pl.ds(e, 1), :]
    n_max = jnp.max(jnp.where(_iota((2, LANES), 1) == e, cnt_ref[...], 0.0))
    n_pass = (n_max.astype(jnp.int32) + (cap - 1)) // cap

    def one_pass(p, carry):
        off = (p * cap).astype(F32)
        slot_col = _iota((cap, 1), 0).astype(F32) + off
        slot_row = _iota((1, cap), 1).astype(F32) + off
        xs = []
        for hf in range(2):
            rows = slice(hf * half, (hf + 1) * half)
            pick = (rank_t_e[:, rows] == slot_col).astype(BF16)
            xs.append(jnp.dot(pick, h_ref[rows, :], preferred_element_type=F32).astype(BF16))
        xc = jnp.concatenate(xs, axis=0)
        a = _silu(jnp.dot(xc, wg_ref[0], preferred_element_type=F32)) * jnp.dot(
            xc, wu_ref[0], preferred_element_type=F32)
        out_e = jnp.dot(a.astype(BF16), wd_ref[0], preferred_element_type=F32)
        for hf in range(2):
            rows = slice(hf * half, (hf + 1) * half)
            put = (rank_e[rows, :] == slot_row).astype(BF16)
            back = jnp.dot(put, out_e[hf * cap:(hf + 1) * cap, :].astype(BF16), preferred_element_type=F32)
            acc_ref[rows, :] += comb_e[rows, :] * back
        return carry

    lax.fori_loop(0, n_pass, one_pass, 0)

    @pl.when(e == pl.num_programs(2) - 1)
    def _():
        u = ALPHA * x_ref[0] + (1.0 + gt_ref[0]) * acc_ref[...]
        o_ref[0] = _layernorm(u, g_ref[...], b_ref[...])


def _moe(x, sc, sh, gt, router_pad, wg, wu, wd, ln_g, ln_b, tm):
    bsz, length, _ = x.shape
    assert tm % LANES == 0
    half = tm // 2
    cap = MOE_CAP_ROWS if half >= 2 * MOE_CAP_ROWS else half
    tri = (jnp.arange(half)[None, :] < jnp.arange(half)[:, None]).astype(BF16)
    tok = pl.BlockSpec((1, tm, D_MODEL), lambda b, i, e: (b, i, 0))
    vec = pl.BlockSpec((1, D_MODEL), lambda b, i, e: (0, 0))
    wspec = pl.BlockSpec((1, D_MODEL, D_MODEL), lambda b, i, e: (e, 0, 0))
    return pl.pallas_call(
        functools.partial(_moe_kernel, half=half, cap=cap),
        grid=(bsz, length // tm, N_EXPERTS),
        in_specs=[tok, _mod_spec(sc, tm), _mod_spec(sh, tm), _mod_spec(gt, tm),
                  pl.BlockSpec((D_MODEL, LANES), lambda b, i, e: (0, 0)),
                  wspec, wspec, wspec, vec, vec, _const_spec(tri)],
        out_specs=tok,
        out_shape=jax.ShapeDtypeStruct((bsz, length, D_MODEL), F32),
        scratch_shapes=[pltpu.VMEM((tm, D_MODEL), BF16), pltpu.VMEM((tm, LANES), F32),
                        pltpu.VMEM((tm, LANES), F32), pltpu.VMEM((LANES, tm), F32),
                        pltpu.VMEM((2, LANES), F32), pltpu.VMEM((tm, D_MODEL), F32)],
        compiler_params=_cparams("arbitrary", "arbitrary", "arbitrary"),
        name="moe_ln",
    )(x, sc, sh, gt, router_pad, wg, wu, wd, ln_g, ln_b, tri)


def _pad_lanes(vec, offset, width=SMALL_W):
    out = jnp.zeros((1, width), F32)
    return out.at[0, offset:offset + vec.shape[0]].set(vec)


def _layer_params(p, l):
    w_in = p["w_in"][l]
    off = [0]
    for s in (SSD_W, SSD_CONV_DIM, SSD_H, RWKV_FEAT, GLA_HK, GLA_HK, GLA_W, GK_LORA, GLA_W):
        off.append(off[-1] + s)
    piece = lambda i: w_in[:, off[i]:off[i + 1]]
    small = jnp.zeros((D_MODEL, SMALL_W), F32)
    small = small.at[:, DT_OFF:DT_OFF + SSD_H].set(piece(2)).at[:, GLO_OFF:GLO_OFF + GK_LORA].set(piece(7))
    w_pad = jnp.concatenate([piece(0), piece(1), small, piece(3), piece(4), piece(5), piece(6), piece(8)],
                            axis=1).astype(BF16)
    ssd_p = (p["ssd_conv_w"][l], p["ssd_conv_b"][l][None, :],
             _pad_lanes(p["ssd_dt_bias"][l], DT_OFF), _pad_lanes(p["ssd_a_log"][l], DT_OFF),
             jnp.repeat(p["ssd_d"][l], SSD_P)[None, :], p["ssd_norm_g"][l][None, :])
    lora_w = jnp.zeros((LANES, 3 * RWKV_W), F32)
    lora_w = (lora_w.at[0:32, 0:RWKV_W].set(p["rwkv_w2"][l])
              .at[32:64, RWKV_W:2 * RWKV_W].set(p["rwkv_a2"][l])
              .at[64:128, 2 * RWKV_W:].set(p["rwkv_g2"][l])).astype(BF16)
    row = lambda name: p[name][l].reshape(1, -1)
    rwkv_p = (row("rwkv_mu"), lora_w, row("rwkv_w0"), row("rwkv_a0"), row("rwkv_k_k"), row("rwkv_k_a"),
              row("rwkv_r_k"), row("rwkv_ln_g"), row("rwkv_ln_b"))
    gk2_pad = jnp.zeros((SMALL_W, GLA_HK), F32).at[GLO_OFF:GLO_OFF + GK_LORA].set(p["gla_w_gk2"][l])
    gla_p = (gk2_pad, row("gla_b_gk"), jnp.tile(p["gla_norm_g"][l], GLA_H)[None, :])
    return w_pad, ssd_p, rwkv_p, gla_p


def _block_diag_inv(s_bd, h):
    b, hr, hc = s_bd.shape
    r, c = hr // h, hc // h
    s = s_bd.reshape(b, h, r, h, c)
    return jnp.stack([s[:, i, :, i, :] for i in range(h)], axis=1)


def _ssd_state_unpack(h_bd):
    b = h_bd.shape[0]
    s = h_bd.reshape(b, SSD_G, SSD_N, SSD_H, SSD_P)
    per_group = SSD_H // SSD_G
    heads = [s[:, h // per_group, :, h, :] for h in range(SSD_H)]
    return jnp.swapaxes(jnp.stack(heads, axis=1), 2, 3)


def _tail(x, mod_l, l, p, y_ssd, y_rwkv, y_gla, tm):
    sh1, sc1, gt1, sh2, sc2, gt2 = mod_l
    row = lambda name: p[name][l].reshape(1, -1)
    w_out = p["w_out"][l].astype(BF16)
    i = l // 2
    if l % 2 == 0:
        x = _outproj_ffn(y_ssd, y_rwkv, y_gla, x, gt1, sc2, sh2, gt2, w_out, row("ln_mix_g"), row("ln_mix_b"),
                         p["ffn_w_gate"][i].astype(BF16), p["ffn_w_up"][i].astype(BF16),
                         p["ffn_w_down"][i].astype(BF16), row("ln_ffn_g"), row("ln_ffn_b"), tm)
    else:
        x = _outproj(y_ssd, y_rwkv, y_gla, x, gt1, w_out, row("ln_mix_g"), row("ln_mix_b"), tm)
        router_pad = jnp.zeros((D_MODEL, LANES), F32).at[:, :N_EXPERTS].set(p["moe_router"][i])
        x = _moe(x, sc2, sh2, gt2, router_pad, p["moe_w_gate"][i].astype(BF16), p["moe_w_up"][i].astype(BF16),
                 p["moe_w_down"][i].astype(BF16), row("ln_ffn_g"), row("ln_ffn_b"), min(2 * tm, x.shape[1]))
    return x


def _forward(x_prompt, x_sample, c_prompt, c_sample, states, p):
    bp, seq, _ = x_prompt.shape
    bs = x_sample.shape[0]
    state_ssd, state_conv, state_rwkv, state_shift, state_gla = states
    mod = _ada(jnp.concatenate([c_prompt, c_sample], axis=0), p["w_ada"], p["b_ada"])

    xp = x_prompt
    xs = x_sample.reshape(1, bs, D_MODEL)
    tm_p = min(512, seq)
    outs_p = [[] for _ in range(5)]
    outs_s = [[] for _ in range(5)]
    for l in range(DEPTH):
        w_pad, ssd_p, rwkv_p, gla_p = _layer_params(p, l)
        mods = jnp.split(mod[l], 6, axis=-1)
        mod_p = [m[:bp, None, :] for m in mods]
        mod_s = [m[None, bp:, :] for m in mods]

        ssd_slab, small, rw, gla_slab = _inproj(xp, mod_p[1], mod_p[0], w_pad, tm_p)
        y_ssd, conv_new, h_new = _ssd_prompt(
            ssd_slab, small, jnp.zeros((bp, SSD_CONV_W - 1, SSD_CONV_DIM), F32),
            jnp.zeros((bp, SSD_G * SSD_N, SSD_H * SSD_P), F32), *ssd_p)
        h_new = _ssd_state_unpack(h_new)
        y_rwkv, shift_new, rs_bd = _rwkv_prompt(
            rw, jnp.zeros((bp, 1, RWKV_FEAT), F32), jnp.zeros((bp, RWKV_W, RWKV_W), F32), *rwkv_p)
        y_gla, gs_bd = _gla_prompt(gla_slab, small, jnp.zeros((bp, GLA_HK, GLA_W), F32), *gla_p)
        xp = _tail(xp, mod_p, l, p, y_ssd, y_rwkv, y_gla, tm_p)
        for acc, s in zip(outs_p, (h_new, conv_new, _block_diag_inv(rs_bd, RWKV_H),
                                   shift_new.reshape(bp, RWKV_FEAT), _block_diag_inv(gs_bd, GLA_H))):
            acc.append(s)

        ssd_slab, small, rw, gla_slab = _inproj(xs, mod_s[1], mod_s[0], w_pad, bs)
        y_ssd, y_rwkv, y_gla, conv_new, h_new, rs_new, gs_new = _mixer_step(
            ssd_slab[0], small[0], rw[0], gla_slab[0], state_conv[l].reshape(bs, -1), state_shift[l],
            state_ssd[l], state_rwkv[l], state_gla[l], ssd_p, rwkv_p, gla_p)
        xs = _tail(xs, mod_s, l, p, y_ssd[None], y_rwkv[None], y_gla[None], bs)
        for acc, s in zip(outs_s, (h_new, conv_new, rs_new, rw[0].astype(F32), gs_new)):
            acc.append(s)

    stack = lambda accs: tuple(jnp.stack(a, axis=0) for a in accs)
    return (xp, xs.reshape(bs, 1, D_MODEL)) + stack(outs_p) + stack(outs_s)


def kernel(x_prompt, x_sample, c_prompt, c_sample, state_ssd, state_ssd_conv, state_rwkv, state_rwkv_shift, state_gla, w_ada, b_ada, w_in, w_out, ssd_conv_w, ssd_conv_b, ssd_dt_bias, ssd_a_log, ssd_d, ssd_norm_g, rwkv_mu, rwkv_w0, rwkv_w2, rwkv_a0, rwkv_a2, rwkv_g2, rwkv_k_k, rwkv_k_a, rwkv_r_k, rwkv_ln_g, rwkv_ln_b, gla_w_gk2, gla_b_gk, gla_norm_g, ln_mix_g, ln_mix_b, ln_ffn_g, ln_ffn_b, ffn_w_gate, ffn_w_up, ffn_w_down, moe_router, moe_w_gate, moe_w_up, moe_w_down):
    p = dict(w_ada=w_ada, b_ada=b_ada, w_in=w_in, w_out=w_out, ssd_conv_w=ssd_conv_w, ssd_conv_b=ssd_conv_b,
             ssd_dt_bias=ssd_dt_bias, ssd_a_log=ssd_a_log, ssd_d=ssd_d, ssd_norm_g=ssd_norm_g,
             rwkv_mu=rwkv_mu, rwkv_w0=rwkv_w0, rwkv_w2=rwkv_w2, rwkv_a0=rwkv_a0, rwkv_a2=rwkv_a2,
             rwkv_g2=rwkv_g2, rwkv_k_k=rwkv_k_k, rwkv_k_a=rwkv_k_a, rwkv_r_k=rwkv_r_k,
             rwkv_ln_g=rwkv_ln_g, rwkv_ln_b=rwkv_ln_b, gla_w_gk2=gla_w_gk2, gla_b_gk=gla_b_gk,
             gla_norm_g=gla_norm_g, ln_mix_g=ln_mix_g, ln_mix_b=ln_mix_b, ln_ffn_g=ln_ffn_g,
             ln_ffn_b=ln_ffn_b, ffn_w_gate=ffn_w_gate, ffn_w_up=ffn_w_up, ffn_w_down=ffn_w_down,
             moe_router=moe_router, moe_w_gate=moe_w_gate, moe_w_up=moe_w_up, moe_w_down=moe_w_down)
    states = (state_ssd, state_ssd_conv, state_rwkv, state_rwkv_shift, state_gla)
    return _forward(x_prompt, x_sample, c_prompt, c_sample, states, p)
```

```python
import functools

import jax
import jax.numpy as jnp
from jax import lax
from jax.experimental import pallas as pl
from jax.experimental.pallas import tpu as pltpu

F32 = jnp.float32
BF16 = jnp.bfloat16

D_MODEL = 1024
DEPTH = 2
SSD_W = 512
SSD_H = 8
SSD_P = 64
SSD_N = 64
SSD_G = 2
SSD_CONV_W = 4
SSD_CONV_DIM = 768
RWKV_W = 256
RWKV_H = 4
RWKV_N = 64
RWKV_FEAT = 896
RWKV_GN_EPS = RWKV_N * 1e-5
GLA_W = 256
GLA_H = 4
GLA_DK = 32
GLA_DV = 64
GLA_HK = GLA_H * GLA_DK
GK_LORA = 16
GATE_NORMALIZER = 16.0
F_DENSE = 2816
N_EXPERTS = 8
ALPHA = (2.0 * DEPTH) ** 0.25
LN_EPS = 1e-5
RMS_EPS = 1e-6

LANES = 128
SMALL_W = LANES
DT_OFF = 0
GLO_OFF = 8
SSD_SLAB = SSD_W + SSD_CONV_DIM
GLA_SLAB = 2 * GLA_HK + 2 * GLA_W
IN_PAD = SSD_SLAB + SMALL_W + RWKV_FEAT + GLA_SLAB
SSD_CHUNK = 128
SSD_CARRY = 8
GLA_CHUNK = 128
GLA_SUB = 8
RWKV_CHUNK = 64
SSD_NB, RWKV_NB, GLA_NB = 2, 8, 4
RWKV_SKEW = 1
MOE_CAP_ROWS = 160
STEP_GRID = 32
VMEM_LIMIT = 56 * 1024 * 1024


def _cparams(*sem):
    return pltpu.CompilerParams(dimension_semantics=sem, vmem_limit_bytes=VMEM_LIMIT)


def _mm(a, b):
    return jnp.dot(a.astype(BF16), b.astype(BF16), preferred_element_type=F32)


def _mm_nt(a, b):
    return lax.dot_general(a.astype(BF16), b.astype(BF16), (((1,), (1,)), ((), ())),
                           preferred_element_type=F32)


def _split_bf16(x, terms):
    parts = []
    for _ in range(terms):
        p = x.astype(BF16)
        parts.append(p)
        x = x - p.astype(F32)
    return parts


def _mm_sel(sel, x, terms=3):
    return sum(jnp.dot(sel, p, preferred_element_type=F32) for p in _split_bf16(x, terms))


def _mm_xsel(x, sel, terms=2):
    return sum(jnp.dot(p, sel, preferred_element_type=F32) for p in _split_bf16(x, terms))


def _mm_hi(a, b):
    a_hi, a_lo = _split_bf16(a, 2)
    b_hi, b_lo = _split_bf16(b, 2)
    dot = lambda x, y: jnp.dot(x, y, preferred_element_type=F32)
    return dot(a_hi, b_hi) + (dot(a_hi, b_lo) + dot(a_lo, b_hi))


def _sigmoid(x):
    return 1.0 / (1.0 + jnp.exp(-x))


def _silu(x):
    return x * _sigmoid(x)


def _softplus(x):
    return jnp.maximum(x, 0.0) + jnp.log1p(jnp.exp(-jnp.abs(x)))


def _iota(shape, dim):
    return lax.broadcasted_iota(jnp.int32, shape, dim)


def _layernorm(u, g, b):
    mu = jnp.mean(u, axis=-1, keepdims=True)
    d = u - mu
    var = jnp.mean(d * d, axis=-1, keepdims=True)
    return d * lax.rsqrt(var + LN_EPS) * g + b


def _tri_incl(n):
    return (jnp.arange(n)[None, :] <= jnp.arange(n)[:, None]).astype(BF16)


def _seg_ones(n, seg, scale=1.0):
    idx = jnp.arange(n) // seg
    return jnp.where(idx[:, None] == idx[None, :], scale, 0.0).astype(BF16)


def _interleave(chains, skew=0):
    chains = list(chains)
    done = [False] * len(chains)
    rnd = 0
    while not all(done):
        for i, ch in enumerate(chains):
            if done[i] or rnd < i * skew:
                continue
            try:
                next(ch)
            except StopIteration:
                done[i] = True
        rnd += 1


def _const_spec(arr):
    nd = arr.ndim
    return pl.BlockSpec(arr.shape, lambda *_: (0,) * nd)


def _ada_kernel(c_ref, w_ref, b_ref, o_ref):
    o_ref[0] = _mm(_silu(c_ref[...]), w_ref[0]) + b_ref[0]


def _ada(c_all, w_ada, b_ada):
    rows = c_all.shape[0]
    tn = 1536
    return pl.pallas_call(
        _ada_kernel,
        grid=(DEPTH, 6 * D_MODEL // tn),
        in_specs=[pl.BlockSpec((rows, D_MODEL), lambda l, j: (0, 0)),
                  pl.BlockSpec((1, D_MODEL, tn), lambda l, j: (l, 0, j)),
                  pl.BlockSpec((1, 1, tn), lambda l, j: (l, 0, j))],
        out_specs=pl.BlockSpec((1, rows, tn), lambda l, j: (l, 0, j)),
        out_shape=jax.ShapeDtypeStruct((DEPTH, rows, 6 * D_MODEL), F32),
        compiler_params=_cparams("arbitrary", "arbitrary"),
        name="ada",
    )(c_all, w_ada, b_ada.reshape(DEPTH, 1, 6 * D_MODEL))


def _inproj_kernel(x_ref, sc_ref, sh_ref, w_ref, o_ssd, o_small, o_rwkv, o_gla):
    h = (x_ref[0] * (1.0 + sc_ref[0]) + sh_ref[0]).astype(BF16)
    off = 0
    for o_ref in (o_ssd, o_small, o_rwkv, o_gla):
        width = o_ref.shape[2]
        o_ref[0] = jnp.dot(h, w_ref[:, off:off + width], preferred_element_type=F32).astype(o_ref.dtype)
        off += width


def _mod_spec(mod, tm):
    if mod.shape[1] == 1:
        return pl.BlockSpec((1, 1, D_MODEL), lambda b, i, *_: (b, 0, 0))
    return pl.BlockSpec((1, tm, D_MODEL), lambda b, i, *_: (b, i, 0))


def _inproj(x, sc, sh, w_pad, tm):
    bsz, length, _ = x.shape
    tok = lambda width: pl.BlockSpec((1, tm, width), lambda b, i: (b, i, 0))
    return pl.pallas_call(
        _inproj_kernel,
        grid=(bsz, length // tm),
        in_specs=[tok(D_MODEL), _mod_spec(sc, tm), _mod_spec(sh, tm),
                  pl.BlockSpec((D_MODEL, IN_PAD), lambda b, i: (0, 0))],
        out_specs=[tok(SSD_SLAB), tok(SMALL_W), tok(RWKV_FEAT), tok(GLA_SLAB)],
        out_shape=[jax.ShapeDtypeStruct((bsz, length, SSD_SLAB), BF16),
                   jax.ShapeDtypeStruct((bsz, length, SMALL_W), F32),
                   jax.ShapeDtypeStruct((bsz, length, RWKV_FEAT), BF16),
                   jax.ShapeDtypeStruct((bsz, length, GLA_SLAB), BF16)],
        compiler_params=_cparams("arbitrary", "arbitrary"),
        name="inproj",
    )(x, sc, sh, w_pad)


def _ssd_token_math(xbc, small, conv_taps, cw, cb, dtb, alog):
    conv = cb + xbc * cw[3:4, :]
    for i in range(SSD_CONV_W - 1):
        conv = conv + conv_taps[i] * cw[i:i + 1, :]
    dt = _softplus(small + dtb)
    return _silu(conv), dt, dt * (-jnp.exp(alog))


def _ssd_finish(y, xs, z, dexp, ng):
    gated = (y + xs * dexp) * _silu(z)
    ms = jnp.mean(gated * gated, axis=-1, keepdims=True)
    return gated * lax.rsqrt(ms + RMS_EPS) * ng


def _ssd_kernel(slab_ref, small_ref, cw_ref, cb_ref, dtb_ref, alog_ref, dexp_ref, ng_ref, tri_ref,
                head_x_ref, group_mask_ref, pair_mask_ref, y_ref, conv_out_ref, h_out_ref, ext_ref, h_ref):
    c = pl.program_id(1)
    n_chunks = pl.num_programs(1)
    ch = SSD_CHUNK
    pad = ext_ref.shape[1] - ch
    causal = _iota((ch, ch), 1) <= _iota((ch, ch), 0)

    nb = slab_ref.shape[0]

    @pl.when(c == 0)
    def _():
        ext_ref[:, 0:pad, :] = jnp.zeros((nb, pad, SSD_CONV_DIM), F32)
        h_ref[...] = jnp.zeros_like(h_ref)

    def chain(j):
        z = slab_ref[j, :, 0:SSD_W].astype(F32)
        xbc = slab_ref[j, :, SSD_W:].astype(F32)
        small = small_ref[j]

        ext_ref[j, pad:pad + ch, :] = xbc
        first = pad - (SSD_CONV_W - 1)
        taps = [ext_ref[j, first + i:first + i + ch, :] for i in range(SSD_CONV_W - 1)]
        act, dt, dta = _ssd_token_math(xbc, small, taps, cw_ref[...], cb_ref[...], dtb_ref[...], alog_ref[...])
        ext_ref[j, 0:pad, :] = ext_ref[j, ch:ch + pad, :]

        xs = act[:, 0:SSD_W]
        bs = act[:, SSD_W:SSD_W + SSD_G * SSD_N]
        cs = act[:, SSD_W + SSD_G * SSD_N:]
        yield
        cum = _mm_sel(tri_ref[...], dta)
        cum_t = cum.T
        dt_t = dt.T
        cum_x = _mm_xsel(cum, head_x_ref[...], 3)
        dt_x = _mm_xsel(dt, head_x_ref[...], 3)
        yield
        last_x = cum_x[ch - 1:ch, :]
        xs_tail = (xs * (jnp.exp(last_x - cum_x) * dt_x)).astype(BF16)
        h_bd = h_ref[j]
        y_state = jnp.dot(cs.astype(BF16), h_bd.astype(BF16), preferred_element_type=F32) * jnp.exp(cum_x)
        upd = jnp.dot(bs.T.astype(BF16), xs_tail, preferred_element_type=F32)
        h_ref[j] = h_bd * jnp.exp(last_x) + upd * group_mask_ref[...]
        yield

        lane_group = _iota((1, SSD_G * SSD_N), 1) // SSD_N
        xs_b = xs.astype(BF16)
        ys = []
        for g in range(SSD_G):
            cb = _mm_nt(jnp.where(lane_group == g, cs, 0.0), bs)
            yield
            for pair in range(SSD_H // SSD_G // 2):
                scores = []
                for hh in range(2):
                    lane = DT_OFF + g * (SSD_H // SSD_G) + 2 * pair + hh
                    decay = jnp.exp(jnp.where(causal, cum[:, lane:lane + 1] - cum_t[lane:lane + 1, :], -jnp.inf))
                    scores.append((cb * decay * dt_t[lane:lane + 1, :]).astype(BF16))
                blk = g * (SSD_H // SSD_G) // 2 + pair
                x_pair = xs_b[:, blk * LANES:(blk + 1) * LANES]
                x_bd = jnp.concatenate([x_pair, x_pair], axis=0) * pair_mask_ref[...]
                ys.append(jnp.dot(jnp.concatenate(scores, axis=1), x_bd, preferred_element_type=F32))
                yield

        y = jnp.concatenate(ys, axis=-1) + y_state
        y_ref[j] = _ssd_finish(y, xs, z, dexp_ref[...], ng_ref[...]).astype(y_ref.dtype)

    _interleave([chain(j) for j in range(nb)])

    @pl.when(c == n_chunks - 1)
    def _():
        for j in range(nb):
            conv_out_ref[j] = ext_ref[j, pad - (SSD_CONV_W - 1):pad, :]
            h_out_ref[j] = h_ref[j]


def _ssd_prompt(slab, small, cw, cb, dtb, alog, dexp, ng):
    bsz, length, _ = slab.shape
    nb = SSD_NB if bsz % SSD_NB == 0 else 1
    gn, hp = SSD_G * SSD_N, SSD_H * SSD_P
    lane_head = jnp.arange(hp) // SSD_P
    head_x = (jnp.arange(SMALL_W)[:, None] == DT_OFF + lane_head[None, :]).astype(BF16)
    group_mask = (jnp.arange(gn)[:, None] // SSD_N == lane_head[None, :] // (SSD_H // SSD_G)).astype(F32)
    pair_mask = (jnp.arange(2 * SSD_CHUNK)[:, None] // SSD_CHUNK == jnp.arange(2 * SSD_P)[None, :] // SSD_P
                 ).astype(BF16)
    consts = (cw, cb, dtb, alog, dexp, ng, _tri_incl(SSD_CHUNK), head_x, group_mask, pair_mask)
    return pl.pallas_call(
        _ssd_kernel,
        grid=(bsz // nb, length // SSD_CHUNK),
        in_specs=[pl.BlockSpec((nb, SSD_CHUNK, SSD_SLAB), lambda b, c: (b, c, 0)),
                  pl.BlockSpec((nb, SSD_CHUNK, SMALL_W), lambda b, c: (b, c, 0))]
                 + [_const_spec(a) for a in consts],
        out_specs=[pl.BlockSpec((nb, SSD_CHUNK, SSD_W), lambda b, c: (b, c, 0)),
                   pl.BlockSpec((nb, SSD_CONV_W - 1, SSD_CONV_DIM), lambda b, c: (b, 0, 0)),
                   pl.BlockSpec((nb, gn, hp), lambda b, c: (b, 0, 0))],
        out_shape=[jax.ShapeDtypeStruct((bsz, length, SSD_W), BF16),
                   jax.ShapeDtypeStruct((bsz, SSD_CONV_W - 1, SSD_CONV_DIM), F32),
                   jax.ShapeDtypeStruct((bsz, gn, hp), F32)],
        scratch_shapes=[pltpu.VMEM((nb, SSD_CARRY + SSD_CHUNK, SSD_CONV_DIM), F32),
                        pltpu.VMEM((nb, gn, hp), F32)],
        compiler_params=_cparams("arbitrary", "arbitrary"),
        name="ssd_scan",
    )(slab, small, *consts)


def _rwkv_token_math(rw, prev, mu, lora_w, w0, a0, k_k, k_a, head_ones):
    mix = rw + mu * (prev - rw)
    r = mix[:, 0:RWKV_W]
    k = mix[:, RWKV_W:2 * RWKV_W]
    v = mix[:, 2 * RWKV_W:3 * RWKV_W]
    lora = mix[:, 3 * RWKV_W:]
    lane = _iota(lora.shape, 1)
    act = jnp.where(lane < 32, jnp.tanh(lora), jnp.where(lane < 64, lora, _sigmoid(lora)))
    lo = _mm(act, lora_w)
    w_log = -_softplus(-(w0 + lo[:, 0:RWKV_W])) - 0.5
    logw = -jnp.exp(w_log)
    aic = _sigmoid(a0 + lo[:, RWKV_W:2 * RWKV_W])
    gate = lo[:, 2 * RWKV_W:]
    kkf = k * k_k
    kk = kkf * lax.rsqrt(_mm_xsel(kkf * kkf, head_ones) + 1e-12)
    k2 = k * (1.0 + (aic - 1.0) * k_a)
    return r, k2, v, logw, aic, gate, kk


def _rwkv_finish(o, r, k2, v, gate, r_k, ln_g, ln_b, head_ones, head_avg):
    mu = _mm_xsel(o, head_avg)
    d = o - mu
    var = _mm_xsel(d * d, head_avg)
    on = d * lax.rsqrt(var + RWKV_GN_EPS) * ln_g + ln_b
    bonus = _mm_xsel(r * k2 * r_k, head_ones) * v
    return (on + bonus) * gate


def _rwkv_kernel(rw_ref, mu_ref, lw_ref, w0_ref, a0_ref, kk_ref, ka_ref,
                 rk_ref, lng_ref, lnb_ref, ones_ref, avg_ref, tri_ref, strict_ref, incl_ref, eye_ref,
                 y_ref, shift_out_ref, s_out_ref, ext_ref, s_ref):
    c = pl.program_id(1)
    n_chunks = pl.num_programs(1)
    ch = RWKV_CHUNK
    nh = RWKV_H
    w = RWKV_W
    head_bd = ones_ref[...]
    head_bd_f = head_bd.astype(F32)
    tile4 = lambda x: jnp.concatenate([x] * nh, axis=0)

    nb = rw_ref.shape[0]

    @pl.when(c == 0)
    def _():
        ext_ref[:, 0:8, :] = jnp.zeros((nb, 8, RWKV_FEAT), F32)
        s_ref[...] = jnp.zeros_like(s_ref)

    def chain(j):
        rw = rw_ref[j].astype(F32)
        ext_ref[j, 8:8 + ch, :] = rw
        prev = ext_ref[j, 7:7 + ch, :]
        ext_ref[j, 7:8, :] = rw[ch - 1:ch, :]

        r, k2, v, logw, aic, gate, kk = _rwkv_token_math(
            rw, prev, mu_ref[...], lw_ref[...], w0_ref[...], a0_ref[...], kk_ref[...], ka_ref[...], head_bd)
        yield

        cumw = _mm_sel(tri_ref[...], logw)
        yield
        last = cumw[ch - 1:ch, :]
        inv_g = jnp.exp(-cumw)
        to_end = jnp.exp(last - cumw)
        kka = kk * aic
        a_t = (-kk * jnp.exp(cumw - logw)).astype(BF16)
        r_t = (r * jnp.exp(cumw)).astype(BF16)
        b_t = (kka * inv_g).astype(BF16)
        k_t = (k2 * inv_g).astype(BF16)
        bk_end = jnp.concatenate([kka * to_end, k2 * to_end], axis=0).astype(BF16)

        bd = lambda x: tile4(x.astype(BF16)) * head_bd
        nt = lambda x, y: lax.dot_general(x, y, (((1,), (1,)), ((), ())), preferred_element_type=F32)
        vb = v.astype(BF16)
        ar = jnp.concatenate([a_t, r_t], axis=0)
        prod = nt(ar, jnp.concatenate([bd(b_t), bd(k_t)], axis=0))
        a_ab = prod[0:ch, 0:w] * strict_ref[...]
        a_ak = prod[0:ch, w:2 * w] * strict_ref[...]
        a_rbk = (prod[ch:2 * ch, :] * incl_ref[...]).astype(BF16)
        yield
        s_bd = s_ref[j]
        uo = nt(ar, s_bd.astype(BF16))
        rhs = uo[0:ch, :] + jnp.dot(a_ak.astype(BF16), bd(vb), preferred_element_type=F32)
        yield

        t_inv = eye_ref[...] + a_ab
        x = a_ab
        x_bd = bd(x)
        power = 1
        while 2 * power < ch:
            x = jnp.dot(x.astype(BF16), x_bd, preferred_element_type=F32)
            x_bd = bd(x)
            yield
            t_inv = t_inv + jnp.dot(t_inv.astype(BF16), x_bd, preferred_element_type=F32)
            yield
            power *= 2

        p = jnp.dot(t_inv.astype(BF16), bd(rhs), preferred_element_type=F32)
        yield
        pb = p.astype(BF16)
        o = uo[ch:2 * ch, :] + jnp.dot(a_rbk, jnp.concatenate([bd(pb), bd(vb)], axis=0),
                                       preferred_element_type=F32)
        pv = jnp.concatenate([pb, vb], axis=0)
        yield

        upd = jnp.dot(pv.astype(F32).T.astype(BF16), bk_end, preferred_element_type=F32)
        s_ref[j] = s_bd * jnp.exp(last) + upd * head_bd_f
        yield

        y_ref[j] = _rwkv_finish(o, r, k2, v, gate, rk_ref[...], lng_ref[...], lnb_ref[...],
                                head_bd, avg_ref[...]).astype(y_ref.dtype)

    _interleave([chain(j) for j in range(nb)], skew=RWKV_SKEW)

    @pl.when(c == n_chunks - 1)
    def _():
        for j in range(nb):
            shift_out_ref[j] = ext_ref[j, 7:8, :]
            s_out_ref[j] = s_ref[j]


def _rwkv_consts():
    ch = RWKV_CHUNK
    assert ch == RWKV_N, "bd() reuses the head mask, which needs chunk rows == head width"
    t = jnp.arange(ch)[:, None]
    j = jnp.arange(2 * RWKV_W)[None, :] % ch
    strict = (j[:, :RWKV_W] < t).astype(F32)
    incl = (j <= t).astype(F32)
    eye = (j[:, :RWKV_W] == t).astype(F32)
    return (_seg_ones(RWKV_W, RWKV_N), _seg_ones(RWKV_W, RWKV_N, 1.0 / RWKV_N), _tri_incl(ch),
            strict, incl, eye)


def _rwkv_prompt(rw, *params):
    bsz, length, _ = rw.shape
    nb = RWKV_NB if bsz % RWKV_NB == 0 else 1
    consts = tuple(params) + _rwkv_consts()
    return pl.pallas_call(
        _rwkv_kernel,
        grid=(bsz // nb, length // RWKV_CHUNK),
        in_specs=[pl.BlockSpec((nb, RWKV_CHUNK, RWKV_FEAT), lambda b, c: (b, c, 0))]
                 + [_const_spec(a) for a in consts],
        out_specs=[pl.BlockSpec((nb, RWKV_CHUNK, RWKV_W), lambda b, c: (b, c, 0)),
                   pl.BlockSpec((nb, 1, RWKV_FEAT), lambda b, c: (b, 0, 0)),
                   pl.BlockSpec((nb, RWKV_W, RWKV_W), lambda b, c: (b, 0, 0))],
        out_shape=[jax.ShapeDtypeStruct((bsz, length, RWKV_W), BF16),
                   jax.ShapeDtypeStruct((bsz, 1, RWKV_FEAT), F32),
                   jax.ShapeDtypeStruct((bsz, RWKV_W, RWKV_W), F32)],
        scratch_shapes=[pltpu.VMEM((nb, RWKV_CHUNK + 8, RWKV_FEAT), F32),
                        pltpu.VMEM((nb, RWKV_W, RWKV_W), F32)],
        compiler_params=_cparams("arbitrary", "arbitrary"),
        name="rwkv_scan",
    )(rw, *consts)


def _gla_gate_log(small, gk2_pad, b_gk):
    x = _mm_hi(small, gk2_pad) + b_gk
    return -_softplus(-x) / GATE_NORMALIZER


def _gla_finish(o, gg, norm_g, head_avg):
    ms = _mm_xsel(o * o, head_avg)
    return o * lax.rsqrt(ms + RMS_EPS) * norm_g * _silu(gg)


def _gla_kernel(slab_ref, small_ref, gk2_ref, bgk_ref, ng_ref, avg_ref, tri_ref, expand_ref,
                y_ref, s_out_ref, s_ref):
    c = pl.program_id(1)
    n_chunks = pl.num_programs(1)
    ch = GLA_CHUNK
    sub = GLA_SUB
    hk = GLA_HK
    expand = expand_ref[...]
    expand_f = expand.astype(F32)
    pos = _iota((ch, 1), 0) % sub
    t_idx = _iota((ch, 1), 0)
    hmask_k = (_iota((GLA_H * sub, hk), 0) // sub == _iota((GLA_H * sub, hk), 1) // GLA_DK).astype(F32)
    hmask_v = (_iota((GLA_H * sub, GLA_W), 0) // sub == _iota((GLA_H * sub, GLA_W), 1) // GLA_DV).astype(F32)

    nb = slab_ref.shape[0]

    @pl.when(c == 0)
    def _():
        s_ref[...] = jnp.zeros_like(s_ref)

    def chain(j):
        q = slab_ref[j, :, 0:hk].astype(F32) * (GLA_DK ** -0.5)
        k = slab_ref[j, :, hk:2 * hk].astype(F32)
        v = slab_ref[j, :, 2 * hk:2 * hk + GLA_W].astype(F32)
        gg = slab_ref[j, :, 2 * hk + GLA_W:].astype(F32)
        small = small_ref[j]

        lg = _gla_gate_log(small, gk2_ref[...], bgk_ref[...])
        yield
        cum = _mm_sel(tri_ref[...], lg)
        yield
        last = cum[ch - 1:ch, :]
        s_bd = s_ref[j]

        o = _mm(q * jnp.exp(cum), s_bd)

        group = 4
        for d0 in range(0, sub, group):
            prods, v_shift = [], []
            for delta in range(d0, d0 + group):
                if delta == 0:
                    k_s, c_s, v_s = k, cum, v
                else:
                    k_s = pltpu.roll(k, delta, 0)
                    c_s = pltpu.roll(cum, delta, 0)
                    v_s = pltpu.roll(v, delta, 0)
                valid = pos >= delta
                w_pair = jnp.exp(jnp.where(valid, cum - c_s, 0.0))
                prods.append(jnp.where(valid, q * k_s * w_pair, 0.0).astype(BF16))
                v_shift.append(v_s)
            yield
            att = jnp.dot(jnp.concatenate(prods, axis=0), expand, preferred_element_type=F32)
            o = o + sum(att[g * ch:(g + 1) * ch, :] * v_shift[g] for g in range(group))
            yield

        vb = v.astype(BF16)
        atts = []
        for blk in range(1, ch // sub):
            lo = blk * sub
            ref_pt = cum[lo - 1:lo, :]
            qd = q[lo:lo + sub, :] * jnp.exp(cum[lo:lo + sub, :] - ref_pt)
            kp = k * jnp.exp(jnp.where(t_idx < lo, ref_pt - cum, -jnp.inf))
            q4 = jnp.concatenate([qd] * GLA_H, axis=0) * hmask_k
            atts.append(_mm_nt(q4, kp).astype(BF16))
            yield
        o4 = jnp.dot(jnp.concatenate(atts, axis=0), vb, preferred_element_type=F32)
        yield
        cross = [jnp.zeros((sub, GLA_W), F32)]
        for blk in range(ch // sub - 1):
            o4_b = o4[blk * GLA_H * sub:(blk + 1) * GLA_H * sub, :] * hmask_v
            cross.append(sum(o4_b[h * sub:(h + 1) * sub, :] for h in range(GLA_H)))
        o = o + jnp.concatenate(cross, axis=0)

        kt_t = (k * jnp.exp(last - cum)).T
        s_ref[j] = s_bd * _col(jnp.exp(last)) + _mm(kt_t, vb) * expand_f
        yield

        y_ref[j] = _gla_finish(o, gg, ng_ref[...], avg_ref[...]).astype(y_ref.dtype)

    _interleave([chain(j) for j in range(nb)])

    @pl.when(c == n_chunks - 1)
    def _():
        s_out_ref[...] = s_ref[...]


def _gla_prompt(slab, small, gk2_pad, b_gk, ng):
    bsz, length, _ = slab.shape
    nb = GLA_NB if bsz % GLA_NB == 0 else 1
    expand = (jnp.arange(GLA_HK)[:, None] // GLA_DK == jnp.arange(GLA_W)[None, :] // GLA_DV).astype(BF16)
    consts = (gk2_pad, b_gk, ng, _seg_ones(GLA_W, GLA_DV, 1.0 / GLA_DV), _tri_incl(GLA_CHUNK), expand)
    return pl.pallas_call(
        _gla_kernel,
        grid=(bsz // nb, length // GLA_CHUNK),
        in_specs=[pl.BlockSpec((nb, GLA_CHUNK, GLA_SLAB), lambda b, c: (b, c, 0)),
                  pl.BlockSpec((nb, GLA_CHUNK, SMALL_W), lambda b, c: (b, c, 0))]
                 + [_const_spec(a) for a in consts],
        out_specs=[pl.BlockSpec((nb, GLA_CHUNK, GLA_W), lambda b, c: (b, c, 0)),
                   pl.BlockSpec((nb, GLA_HK, GLA_W), lambda b, c: (b, 0, 0))],
        out_shape=[jax.ShapeDtypeStruct((bsz, length, GLA_W), BF16),
                   jax.ShapeDtypeStruct((bsz, GLA_HK, GLA_W), F32)],
        scratch_shapes=[pltpu.VMEM((nb, GLA_HK, GLA_W), F32)],
        compiler_params=_cparams("arbitrary", "arbitrary"),
        name="gla_scan",
    )(slab, small, *consts)


def _col(row):
    n = row.shape[1]
    eye = _iota((n, n), 0) == _iota((n, n), 1)
    return jnp.sum(jnp.where(eye, row, 0.0), axis=1, keepdims=True)


def _store_t(dst_ref, row0, x):
    for cblk in range(x.shape[1] // LANES):
        dst_ref[row0 + cblk * LANES:row0 + (cblk + 1) * LANES, :] = x[:, cblk * LANES:(cblk + 1) * LANES].T


def _load_t(src_ref, width):
    return jnp.concatenate([src_ref[cblk * LANES:(cblk + 1) * LANES, :].T for cblk in range(width // LANES)],
                           axis=1)


_TS_X, _TS_B, _TS_C, _TS_DT, _TS_DA = 0, SSD_W, SSD_W + SSD_G * SSD_N, SSD_CONV_DIM, SSD_CONV_DIM + SMALL_W
_TR_R, _TR_W, _TR_K, _TR_V, _TR_KK, _TR_KA = (i * RWKV_W for i in range(6))
_TG_Q, _TG_K, _TG_G, _TG_V = 0, GLA_HK, 2 * GLA_HK, 3 * GLA_HK


def _step_kernel(ssd_ref, small_ref, rw_ref, gla_ref, conv0_ref, shift0_ref, h0_ref, rs0_ref, gs0_ref,
                 cw_ref, cb_ref, dtb_ref, alog_ref, dexp_ref, sng_ref,
                 mu_ref, lw_ref, w0_ref, a0_ref, kk_ref, ka_ref, rk_ref, lng_ref, lnb_ref,
                 gk2_ref, bgk_ref, gng_ref, ones_ref, ravg_ref, gavg_ref,
                 y_ssd_ref, y_rwkv_ref, y_gla_ref, conv_out_ref, h_out_ref, rs_out_ref, gs_out_ref,
                 t_ssd, t_rwkv, t_gla, o_ssd_t, o_rwkv_t, o_gla_t):
    i = pl.program_id(0)
    hk = GLA_HK

    def ssd_tokens():
        xbc = ssd_ref[:, SSD_W:].astype(F32)
        taps = [conv0_ref[:, t * SSD_CONV_DIM:(t + 1) * SSD_CONV_DIM] for t in range(SSD_CONV_W - 1)]
        return xbc, _ssd_token_math(xbc, small_ref[...], taps, cw_ref[...], cb_ref[...],
                                    dtb_ref[...], alog_ref[...])

    def rwkv_tokens():
        return _rwkv_token_math(rw_ref[...].astype(F32), shift0_ref[...], mu_ref[...], lw_ref[...],
                                w0_ref[...], a0_ref[...], kk_ref[...], ka_ref[...], ones_ref[...])

    @pl.when(i == 0)
    def _():
        xbc, (act, dt, dta) = ssd_tokens()
        conv_out_ref[:, 0:2 * SSD_CONV_DIM] = conv0_ref[:, SSD_CONV_DIM:]
        conv_out_ref[:, 2 * SSD_CONV_DIM:] = xbc
        _store_t(t_ssd, _TS_X, act)
        _store_t(t_ssd, _TS_DT, dt)
        _store_t(t_ssd, _TS_DA, jnp.exp(dta))
        r, k2, v, logw, aic, gate, kk = rwkv_tokens()
        for off, x in ((_TR_R, r), (_TR_W, jnp.exp(logw)), (_TR_K, k2), (_TR_V, v), (_TR_KK, kk),
                       (_TR_KA, kk * aic)):
            _store_t(t_rwkv, off, x)
        lg = _gla_gate_log(small_ref[...], gk2_ref[...], bgk_ref[...])
        _store_t(t_gla, _TG_Q, gla_ref[:, 0:hk].astype(F32) * (GLA_DK ** -0.5))
        _store_t(t_gla, _TG_K, gla_ref[:, hk:2 * hk].astype(F32))
        _store_t(t_gla, _TG_G, jnp.exp(lg))
        _store_t(t_gla, _TG_V, gla_ref[:, 2 * hk:2 * hk + GLA_W].astype(F32))
        o_gla_t[...] = jnp.zeros_like(o_gla_t)

    def row(ref, idx):
        return ref[pl.ds(idx, 1), :]

    def rows(ref, idx, count):
        return ref[pl.ds(pl.multiple_of(idx, count), count), :]

    units = h0_ref.shape[0] // SSD_N
    per_head = SSD_P // units
    h = i // per_head
    p0 = (i % per_head) * units
    b_t = rows(t_ssd, _TS_B + (h // (SSD_H // SSD_G)) * SSD_N, SSD_N)
    c_t = rows(t_ssd, _TS_C + (h // (SSD_H // SSD_G)) * SSD_N, SSD_N)
    da_row = row(t_ssd, _TS_DA + DT_OFF + h)
    dt_row = row(t_ssd, _TS_DT + DT_OFF + h)
    for u in range(units):
        p_idx = h * SSD_P + p0 + u
        blk = slice(u * SSD_N, (u + 1) * SSD_N)
        h_new = h0_ref[blk, :] * da_row + (row(t_ssd, _TS_X + p_idx) * dt_row) * b_t
        o_ssd_t[pl.ds(p_idx, 1), :] = jnp.sum(h_new * c_t, axis=0, keepdims=True)
        h_out_ref[blk, :] = h_new

    units = rs0_ref.shape[0] // RWKV_N
    per_head = RWKV_N // units
    h = i // per_head
    v0 = (i % per_head) * units
    seg = lambda off: rows(t_rwkv, off + h * RWKV_N, RWKV_N)
    r_t, w_t, k_t, kk_t, ka_t = seg(_TR_R), seg(_TR_W), seg(_TR_K), seg(_TR_KK), seg(_TR_KA)
    for u in range(units):
        v_idx = h * RWKV_N + v0 + u
        blk = slice(u * RWKV_N, (u + 1) * RWKV_N)
        s = rs0_ref[blk, :]
        sa = jnp.sum(s * (-kk_t), axis=0, keepdims=True)
        s_new = s * w_t + sa * ka_t + row(t_rwkv, _TR_V + v_idx) * k_t
        o_rwkv_t[pl.ds(v_idx, 1), :] = jnp.sum(s_new * r_t, axis=0, keepdims=True)
        rs_out_ref[blk, :] = s_new

    units = gs0_ref.shape[0] // GLA_DV
    per_head = GLA_DK // units
    h = i // per_head
    k0 = (i % per_head) * units
    v_t = rows(t_gla, _TG_V + h * GLA_DV, GLA_DV)
    acc = jnp.zeros((GLA_DV, LANES), F32)
    for u in range(units):
        k_idx = h * GLA_DK + k0 + u
        blk = slice(u * GLA_DV, (u + 1) * GLA_DV)
        s_new = gs0_ref[blk, :] * row(t_gla, _TG_G + k_idx) + row(t_gla, _TG_K + k_idx) * v_t
        acc = acc + row(t_gla, _TG_Q + k_idx) * s_new
        gs_out_ref[blk, :] = s_new
    o_rows = pl.ds(pl.multiple_of(h * GLA_DV, GLA_DV), GLA_DV)
    o_gla_t[o_rows, :] = o_gla_t[o_rows, :] + acc

    @pl.when(i == pl.num_programs(0) - 1)
    def _():
        _, (act, _, _) = ssd_tokens()
        y_ssd_ref[...] = _ssd_finish(_load_t(o_ssd_t, SSD_W), act[:, 0:SSD_W], ssd_ref[:, 0:SSD_W].astype(F32),
                                     dexp_ref[...], sng_ref[...]).astype(y_ssd_ref.dtype)
        r, k2, v, _, _, gate, _ = rwkv_tokens()
        y_rwkv_ref[...] = _rwkv_finish(_load_t(o_rwkv_t, RWKV_W), r, k2, v, gate, rk_ref[...], lng_ref[...],
                                       lnb_ref[...], ones_ref[...], ravg_ref[...]).astype(y_rwkv_ref.dtype)
        y_gla_ref[...] = _gla_finish(_load_t(o_gla_t, GLA_W), gla_ref[:, 2 * hk + GLA_W:].astype(F32),
                                     gng_ref[...], gavg_ref[...]).astype(y_gla_ref.dtype)


def _mixer_step(ssd_slab, small, rw, gla_slab, conv0, shift0, h0, rs0, gs0, ssd_p, rwkv_p, gla_p):
    bsz = ssd_slab.shape[0]
    assert bsz == LANES, "the single-token mixer keeps exactly one vreg row of batch entries on the lanes"
    flat_t = lambda s: s.reshape(bsz, -1).T
    h0, rs0, gs0 = flat_t(h0), flat_t(rs0), flat_t(gs0)
    consts = (tuple(ssd_p) + tuple(rwkv_p) + tuple(gla_p)
              + (_seg_ones(RWKV_W, RWKV_N), _seg_ones(RWKV_W, RWKV_N, 1.0 / RWKV_N),
                 _seg_ones(GLA_W, GLA_DV, 1.0 / GLA_DV)))
    full = lambda a: pl.BlockSpec(a.shape, lambda i: (0,) * a.ndim)
    cols = lambda a: pl.BlockSpec((a.shape[0] // STEP_GRID, bsz), lambda i: (i, 0))
    tokens = (ssd_slab, small, rw, gla_slab, conv0, shift0)
    outs = pl.pallas_call(
        _step_kernel,
        grid=(STEP_GRID,),
        in_specs=[full(a) for a in tokens] + [cols(h0), cols(rs0), cols(gs0)] + [full(a) for a in consts],
        out_specs=[full(jax.ShapeDtypeStruct((bsz, w), BF16)) for w in (SSD_W, RWKV_W, GLA_W)]
                  + [full(conv0), cols(h0), cols(rs0), cols(gs0)],
        out_shape=[jax.ShapeDtypeStruct((bsz, SSD_W), BF16),
                   jax.ShapeDtypeStruct((bsz, RWKV_W), BF16),
                   jax.ShapeDtypeStruct((bsz, GLA_W), BF16),
                   jax.ShapeDtypeStruct(conv0.shape, F32),
                   jax.ShapeDtypeStruct(h0.shape, F32),
                   jax.ShapeDtypeStruct(rs0.shape, F32),
                   jax.ShapeDtypeStruct(gs0.shape, F32)],
        scratch_shapes=[pltpu.VMEM((SSD_CONV_DIM + 2 * SMALL_W, LANES), F32),
                        pltpu.VMEM((6 * RWKV_W, LANES), F32),
                        pltpu.VMEM((3 * GLA_HK + GLA_W, LANES), F32),
                        pltpu.VMEM((SSD_W, LANES), F32),
                        pltpu.VMEM((RWKV_W, LANES), F32),
                        pltpu.VMEM((GLA_W, LANES), F32)],
        compiler_params=_cparams("arbitrary"),
        name="mixer_step",
    )(*tokens, h0, rs0, gs0, *consts)
    y_ssd, y_rwkv, y_gla, conv_new, h_new, rs_new, gs_new = outs
    return (y_ssd, y_rwkv, y_gla, conv_new.reshape(bsz, SSD_CONV_W - 1, SSD_CONV_DIM),
            h_new.T.reshape(bsz, SSD_H, SSD_P, SSD_N), rs_new.T.reshape(bsz, RWKV_H, RWKV_N, RWKV_N),
            gs_new.T.reshape(bsz, GLA_H, GLA_DK, GLA_DV))


def _mod_rows(mod_ref, rows):
    return mod_ref[0] if mod_ref.shape[1] == 1 else mod_ref[0, rows, :]


def _row_parts(tm):
    return 2 if tm % 256 == 0 else 1


def _outproj_kernel(ys_ref, yr_ref, yg_ref, x_ref, gt_ref, w_ref, g_ref, b_ref, o_ref):
    tm = x_ref.shape[1]
    parts = _row_parts(tm)

    def chain(r):
        rows = slice(r * tm // parts, (r + 1) * tm // parts)
        m = (jnp.dot(ys_ref[0, rows, :], w_ref[0:SSD_W, :], preferred_element_type=F32)
             + jnp.dot(yr_ref[0, rows, :], w_ref[SSD_W:SSD_W + RWKV_W, :], preferred_element_type=F32)
             + jnp.dot(yg_ref[0, rows, :], w_ref[SSD_W + RWKV_W:, :], preferred_element_type=F32))
        yield
        u = ALPHA * x_ref[0, rows, :] + (1.0 + _mod_rows(gt_ref, rows)) * m
        o_ref[0, rows, :] = _layernorm(u, g_ref[...], b_ref[...])

    _interleave([chain(r) for r in range(parts)], skew=1)


def _outproj(y_ssd, y_rwkv, y_gla, x, gt, w_out, ln_g, ln_b, tm):
    bsz, length, _ = x.shape
    tok = lambda width: pl.BlockSpec((1, tm, width), lambda b, i: (b, i, 0))
    vec = pl.BlockSpec((1, D_MODEL), lambda b, i: (0, 0))
    return pl.pallas_call(
        _outproj_kernel,
        grid=(bsz, length // tm),
        in_specs=[tok(SSD_W), tok(RWKV_W), tok(GLA_W), tok(D_MODEL), _mod_spec(gt, tm),
                  pl.BlockSpec((D_MODEL, D_MODEL), lambda b, i: (0, 0)), vec, vec],
        out_specs=tok(D_MODEL),
        out_shape=jax.ShapeDtypeStruct((bsz, length, D_MODEL), F32),
        compiler_params=_cparams("arbitrary", "arbitrary"),
        name="outproj_ln",
    )(y_ssd, y_rwkv, y_gla, x, gt, w_out, ln_g, ln_b)


def _ffn_kernel(ys_ref, yr_ref, yg_ref, x_ref, gt1_ref, sc_ref, sh_ref, gt2_ref, wo_ref, g1_ref, b1_ref,
                wg_ref, wu_ref, wd_ref, g2_ref, b2_ref, o_ref):
    tm = x_ref.shape[1]
    parts = _row_parts(tm)

    def chain(r):
        rows = slice(r * tm // parts, (r + 1) * tm // parts)
        m = (jnp.dot(ys_ref[0, rows, :], wo_ref[0:SSD_W, :], preferred_element_type=F32)
             + jnp.dot(yr_ref[0, rows, :], wo_ref[SSD_W:SSD_W + RWKV_W, :], preferred_element_type=F32)
             + jnp.dot(yg_ref[0, rows, :], wo_ref[SSD_W + RWKV_W:, :], preferred_element_type=F32))
        yield
        x1 = _layernorm(ALPHA * x_ref[0, rows, :] + (1.0 + _mod_rows(gt1_ref, rows)) * m,
                        g1_ref[...], b1_ref[...])
        h = (x1 * (1.0 + _mod_rows(sc_ref, rows)) + _mod_rows(sh_ref, rows)).astype(BF16)
        yield
        gate = jnp.dot(h, wg_ref[...], preferred_element_type=F32)
        up = jnp.dot(h, wu_ref[...], preferred_element_type=F32)
        yield
        f = jnp.dot((_silu(gate) * up).astype(BF16), wd_ref[...], preferred_element_type=F32)
        yield
        u = ALPHA * x1 + (1.0 + _mod_rows(gt2_ref, rows)) * f
        o_ref[0, rows, :] = _layernorm(u, g2_ref[...], b2_ref[...])

    _interleave([chain(r) for r in range(parts)], skew=1)


def _outproj_ffn(y_ssd, y_rwkv, y_gla, x, gt1, sc2, sh2, gt2, w_out, ln1_g, ln1_b, wg, wu, wd, ln2_g, ln2_b, tm):
    bsz, length, _ = x.shape
    tok = lambda width: pl.BlockSpec((1, tm, width), lambda b, i: (b, i, 0))
    vec = pl.BlockSpec((1, D_MODEL), lambda b, i: (0, 0))
    resident = lambda a: pl.BlockSpec(a.shape, lambda b, i: (0, 0), pipeline_mode=pl.Buffered(1))
    return pl.pallas_call(
        _ffn_kernel,
        grid=(bsz, length // tm),
        in_specs=[tok(SSD_W), tok(RWKV_W), tok(GLA_W), tok(D_MODEL),
                  _mod_spec(gt1, tm), _mod_spec(sc2, tm), _mod_spec(sh2, tm), _mod_spec(gt2, tm),
                  resident(w_out), vec, vec, resident(wg), resident(wu), resident(wd), vec, vec],
        out_specs=tok(D_MODEL),
        out_shape=jax.ShapeDtypeStruct((bsz, length, D_MODEL), F32),
        compiler_params=_cparams("arbitrary", "arbitrary"),
        name="outproj_ffn_ln",
    )(y_ssd, y_rwkv, y_gla, x, gt1, sc2, sh2, gt2, w_out, ln1_g, ln1_b, wg, wu, wd, ln2_g, ln2_b)


def _moe_kernel(x_ref, sc_ref, sh_ref, gt_ref, rt_ref, wg_ref, wu_ref, wd_ref, g_ref, b_ref, tri_ref, o_ref,
                h_ref, rank_t_ref, comb_t_ref, cnt_ref, acc_ref, *, half, cap):
    e = pl.program_id(2)
    tm = 2 * half

    @pl.when(e == 0)
    def _():
        h = x_ref[0] * (1.0 + sc_ref[0]) + sh_ref[0]
        h_ref[...] = h.astype(BF16)
        acc_ref[...] = jnp.zeros_like(acc_ref)
        lane = _iota((tm, LANES), 1).astype(F32)
        logits = jnp.where(lane < N_EXPERTS, _mm_hi(h, rt_ref[...]), -jnp.inf)
        m1 = jnp.max(logits, axis=-1, keepdims=True)
        i1 = jnp.min(jnp.where(logits == m1, lane, float(LANES)), axis=-1, keepdims=True)
        rest = jnp.where(lane == i1, -jnp.inf, logits)
        m2 = jnp.max(rest, axis=-1, keepdims=True)
        i2 = jnp.min(jnp.where(rest == m2, lane, float(LANES)), axis=-1, keepdims=True)
        e2 = jnp.exp(m2 - m1)
        den = 1.0 + e2
        comb = jnp.where(lane == i1, 1.0 / den, 0.0) + jnp.where(lane == i2, e2 / den, 0.0)
        sel = (lane == i1) | (lane == i2)
        sel_f = sel.astype(F32)
        ranks = []
        for hf in range(2):
            s = sel_f[hf * half:(hf + 1) * half, :]
            before = jnp.dot(tri_ref[...], s.astype(BF16), preferred_element_type=F32)
            ranks.append(jnp.where(sel[hf * half:(hf + 1) * half, :], before, -1.0))
            cnt_ref[hf:hf + 1, :] = jnp.sum(s, axis=0, keepdims=True)
        rank = jnp.concatenate(ranks, axis=0)
        for blk in range(tm // LANES):
            rows = slice(blk * LANES, (blk + 1) * LANES)
            rank_t_ref[:, rows] = rank[rows, :].T
            comb_t_ref[:, rows] = comb[rows, :].T

    rank_t_e = rank_t_ref[pl.ds(e, 1), :]
    comb_t_e = comb_t_ref[pl.ds(e, 1), :]
    n_max = jnp.max(jnp.where(_iota((2, LANES), 1) == e, cnt_ref[...], 0.0))
    n_pass = (n_max.astype(jnp.int32) + (cap - 1)) // cap

    def one_pass(p, carry):
        slot = _iota((cap, 1), 0).astype(F32) + (p * cap).astype(F32)
        picks, xs, gates = [], [], []
        for hf in range(2):
            rows = slice(hf * half, (hf + 1) * half)
            hit = rank_t_e[:, rows] == slot
            pick = hit.astype(BF16)
            picks.append(pick)
            xs.append(jnp.dot(pick, h_ref[rows, :], preferred_element_type=F32).astype(BF16))
            gates.append(jnp.sum(jnp.where(hit, comb_t_e[:, rows], 0.0), axis=1, keepdims=True))
        xc = jnp.concatenate(xs, axis=0)
        a = _silu(jnp.dot(xc, wg_ref[0], preferred_element_type=F32)) * jnp.dot(
            xc, wu_ref[0], preferred_element_type=F32)
        out_e = jnp.dot(a.astype(BF16), wd_ref[0], preferred_element_type=F32)
        for hf in range(2):
            rows = slice(hf * half, (hf + 1) * half)
            weighted = (out_e[hf * cap:(hf + 1) * cap, :] * gates[hf]).astype(BF16)
            acc_ref[rows, :] += lax.dot_general(picks[hf], weighted, (((0,), (0,)), ((), ())),
                                                preferred_element_type=F32)
        return carry

    lax.fori_loop(0, n_pass, one_pass, 0)

    @pl.when(e == pl.num_programs(2) - 1)
    def _():
        u = ALPHA * x_ref[0] + (1.0 + gt_ref[0]) * acc_ref[...]
        o_ref[0] = _layernorm(u, g_ref[...], b_ref[...])


def _moe(x, sc, sh, gt, router_pad, wg, wu, wd, ln_g, ln_b, tm):
    bsz, length, _ = x.shape
    assert tm % LANES == 0
    half = tm // 2
    cap = MOE_CAP_ROWS if half >= 2 * MOE_CAP_ROWS else half
    tri = (jnp.arange(half)[None, :] < jnp.arange(half)[:, None]).astype(BF16)
    tok = pl.BlockSpec((1, tm, D_MODEL), lambda b, i, e: (b, i, 0))
    vec = pl.BlockSpec((1, D_MODEL), lambda b, i, e: (0, 0))
    wspec = pl.BlockSpec((1, D_MODEL, D_MODEL), lambda b, i, e: (e, 0, 0))
    return pl.pallas_call(
        functools.partial(_moe_kernel, half=half, cap=cap),
        grid=(bsz, length // tm, N_EXPERTS),
        in_specs=[tok, _mod_spec(sc, tm), _mod_spec(sh, tm), _mod_spec(gt, tm),
                  pl.BlockSpec((D_MODEL, LANES), lambda b, i, e: (0, 0)),
                  wspec, wspec, wspec, vec, vec, _const_spec(tri)],
        out_specs=tok,
        out_shape=jax.ShapeDtypeStruct((bsz, length, D_MODEL), F32),
        scratch_shapes=[pltpu.VMEM((tm, D_MODEL), BF16), pltpu.VMEM((LANES, tm), F32),
                        pltpu.VMEM((LANES, tm), F32), pltpu.VMEM((2, LANES), F32),
                        pltpu.VMEM((tm, D_MODEL), F32)],
        compiler_params=_cparams("arbitrary", "arbitrary", "arbitrary"),
        name="moe_ln",
    )(x, sc, sh, gt, router_pad, wg, wu, wd, ln_g, ln_b, tri)


def _pad_lanes(vec, offset, width=SMALL_W):
    out = jnp.zeros((1, width), F32)
    return out.at[0, offset:offset + vec.shape[0]].set(vec)


def _layer_params(p, l):
    w_in = p["w_in"][l]
    off = [0]
    for s in (SSD_W, SSD_CONV_DIM, SSD_H, RWKV_FEAT, GLA_HK, GLA_HK, GLA_W, GK_LORA, GLA_W):
        off.append(off[-1] + s)
    piece = lambda i: w_in[:, off[i]:off[i + 1]]
    small = jnp.zeros((D_MODEL, SMALL_W), F32)
    small = small.at[:, DT_OFF:DT_OFF + SSD_H].set(piece(2)).at[:, GLO_OFF:GLO_OFF + GK_LORA].set(piece(7))
    w_pad = jnp.concatenate([piece(0), piece(1), small, piece(3), piece(4), piece(5), piece(6), piece(8)],
                            axis=1).astype(BF16)
    ssd_p = (p["ssd_conv_w"][l], p["ssd_conv_b"][l][None, :],
             _pad_lanes(p["ssd_dt_bias"][l], DT_OFF), _pad_lanes(p["ssd_a_log"][l], DT_OFF),
             jnp.repeat(p["ssd_d"][l], SSD_P)[None, :], p["ssd_norm_g"][l][None, :])
    lora_w = jnp.zeros((LANES, 3 * RWKV_W), F32)
    lora_w = (lora_w.at[0:32, 0:RWKV_W].set(p["rwkv_w2"][l])
              .at[32:64, RWKV_W:2 * RWKV_W].set(p["rwkv_a2"][l])
              .at[64:128, 2 * RWKV_W:].set(p["rwkv_g2"][l])).astype(BF16)
    row = lambda name: p[name][l].reshape(1, -1)
    rwkv_p = (row("rwkv_mu"), lora_w, row("rwkv_w0"), row("rwkv_a0"), row("rwkv_k_k"), row("rwkv_k_a"),
              row("rwkv_r_k"), row("rwkv_ln_g"), row("rwkv_ln_b"))
    gk2_pad = jnp.zeros((SMALL_W, GLA_HK), F32).at[GLO_OFF:GLO_OFF + GK_LORA].set(p["gla_w_gk2"][l])
    gla_p = (gk2_pad, row("gla_b_gk"), jnp.tile(p["gla_norm_g"][l], GLA_H)[None, :])
    return w_pad, ssd_p, rwkv_p, gla_p


def _block_diag_inv(s_bd, h):
    b, hr, hc = s_bd.shape
    r, c = hr // h, hc // h
    s = s_bd.reshape(b, h, r, h, c)
    return jnp.stack([s[:, i, :, i, :] for i in range(h)], axis=1)


def _ssd_state_unpack(h_bd):
    b = h_bd.shape[0]
    s = h_bd.reshape(b, SSD_G, SSD_N, SSD_H, SSD_P)
    per_group = SSD_H // SSD_G
    heads = [s[:, h // per_group, :, h, :] for h in range(SSD_H)]
    return jnp.swapaxes(jnp.stack(heads, axis=1), 2, 3)


def _tail(x, mod_l, l, p, y_ssd, y_rwkv, y_gla, tm):
    sh1, sc1, gt1, sh2, sc2, gt2 = mod_l
    row = lambda name: p[name][l].reshape(1, -1)
    w_out = p["w_out"][l].astype(BF16)
    i = l // 2
    if l % 2 == 0:
        x = _outproj_ffn(y_ssd, y_rwkv, y_gla, x, gt1, sc2, sh2, gt2, w_out, row("ln_mix_g"), row("ln_mix_b"),
                         p["ffn_w_gate"][i].astype(BF16), p["ffn_w_up"][i].astype(BF16),
                         p["ffn_w_down"][i].astype(BF16), row("ln_ffn_g"), row("ln_ffn_b"), tm)
    else:
        x = _outproj(y_ssd, y_rwkv, y_gla, x, gt1, w_out, row("ln_mix_g"), row("ln_mix_b"), tm)
        router_pad = jnp.zeros((D_MODEL, LANES), F32).at[:, :N_EXPERTS].set(p["moe_router"][i])
        x = _moe(x, sc2, sh2, gt2, router_pad, p["moe_w_gate"][i].astype(BF16), p["moe_w_up"][i].astype(BF16),
                 p["moe_w_down"][i].astype(BF16), row("ln_ffn_g"), row("ln_ffn_b"), min(2 * tm, x.shape[1]))
    return x


def _forward(x_prompt, x_sample, c_prompt, c_sample, states, p):
    bp, seq, _ = x_prompt.shape
    bs = x_sample.shape[0]
    state_ssd, state_conv, state_rwkv, state_shift, state_gla = states
    mod = _ada(jnp.concatenate([c_prompt, c_sample], axis=0), p["w_ada"], p["b_ada"])

    xp = x_prompt
    xs = x_sample.reshape(1, bs, D_MODEL)
    tm_p = min(512, seq)
    outs_p = [[] for _ in range(5)]
    outs_s = [[] for _ in range(5)]
    for l in range(DEPTH):
        w_pad, ssd_p, rwkv_p, gla_p = _layer_params(p, l)
        mods = jnp.split(mod[l], 6, axis=-1)
        mod_p = [m[:bp, None, :] for m in mods]
        mod_s = [m[None, bp:, :] for m in mods]

        ssd_slab, small, rw, gla_slab = _inproj(xp, mod_p[1], mod_p[0], w_pad, tm_p)
        y_ssd, conv_new, h_new = _ssd_prompt(ssd_slab, small, *ssd_p)
        h_new = _ssd_state_unpack(h_new)
        y_rwkv, shift_new, rs_bd = _rwkv_prompt(rw, *rwkv_p)
        y_gla, gs_bd = _gla_prompt(gla_slab, small, *gla_p)
        xp = _tail(xp, mod_p, l, p, y_ssd, y_rwkv, y_gla, tm_p)
        for acc, s in zip(outs_p, (h_new, conv_new, _block_diag_inv(rs_bd, RWKV_H),
                                   shift_new.reshape(bp, RWKV_FEAT), _block_diag_inv(gs_bd, GLA_H))):
            acc.append(s)

        ssd_slab, small, rw, gla_slab = _inproj(xs, mod_s[1], mod_s[0], w_pad, bs)
        y_ssd, y_rwkv, y_gla, conv_new, h_new, rs_new, gs_new = _mixer_step(
            ssd_slab[0], small[0], rw[0], gla_slab[0], state_conv[l].reshape(bs, -1), state_shift[l],
            state_ssd[l], state_rwkv[l], state_gla[l], ssd_p, rwkv_p, gla_p)
        xs = _tail(xs, mod_s, l, p, y_ssd[None], y_rwkv[None], y_gla[None], bs)
        for acc, s in zip(outs_s, (h_new, conv_new, rs_new, rw[0].astype(F32), gs_new)):
            acc.append(s)

    stack = lambda accs: tuple(jnp.stack(a, axis=0) for a in accs)
    return (xp, xs.reshape(bs, 1, D_MODEL)) + stack(outs_p) + stack(outs_s)


def kernel(x_prompt, x_sample, c_prompt, c_sample, state_ssd, state_ssd_conv, state_rwkv, state_rwkv_shift, state_gla, w_ada, b_ada, w_in, w_out, ssd_conv_w, ssd_conv_b, ssd_dt_bias, ssd_a_log, ssd_d, ssd_norm_g, rwkv_mu, rwkv_w0, rwkv_w2, rwkv_a0, rwkv_a2, rwkv_g2, rwkv_k_k, rwkv_k_a, rwkv_r_k, rwkv_ln_g, rwkv_ln_b, gla_w_gk2, gla_b_gk, gla_norm_g, ln_mix_g, ln_mix_b, ln_ffn_g, ln_ffn_b, ffn_w_gate, ffn_w_up, ffn_w_down, moe_router, moe_w_gate, moe_w_up, moe_w_down):
    p = dict(w_ada=w_ada, b_ada=b_ada, w_in=w_in, w_out=w_out, ssd_conv_w=ssd_conv_w, ssd_conv_b=ssd_conv_b,
             ssd_dt_bias=ssd_dt_bias, ssd_a_log=ssd_a_log, ssd_d=ssd_d, ssd_norm_g=ssd_norm_g,
             rwkv_mu=rwkv_mu, rwkv_w0=rwkv_w0, rwkv_w2=rwkv_w2, rwkv_a0=rwkv_a0, rwkv_a2=rwkv_a2,
             rwkv_g2=rwkv_g2, rwkv_k_k=rwkv_k_k, rwkv_k_a=rwkv_k_a, rwkv_r_k=rwkv_r_k,
             rwkv_ln_g=rwkv_ln_g, rwkv_ln_b=rwkv_ln_b, gla_w_gk2=gla_w_gk2, gla_b_gk=gla_b_gk,
             gla_norm_g=gla_norm_g, ln_mix_g=ln_mix_g, ln_mix_b=ln_mix_b, ln_ffn_g=ln_ffn_g,
             ln_ffn_b=ln_ffn_b, ffn_w_gate=ffn_w_gate, ffn_w_up=ffn_w_up, ffn_w_down=ffn_w_down,
             moe_router=moe_router, moe_w_gate=moe_w_gate, moe_w_up=moe_w_up, moe_w_down=moe_w_down)
    states = (state_ssd, state_ssd_conv, state_rwkv, state_rwkv_shift, state_gla)
    return _forward(x_prompt, x_sample, c_prompt, c_sample, states, p)
```

```python
import functools

import jax
import jax.numpy as jnp
from jax import lax
from jax.experimental import pallas as pl
from jax.experimental.pallas import tpu as pltpu

F32 = jnp.float32
BF16 = jnp.bfloat16

D_MODEL = 1024
DEPTH = 2
SSD_W = 512
SSD_H = 8
SSD_P = 64
SSD_N = 64
SSD_G = 2
SSD_CONV_W = 4
SSD_CONV_DIM = 768
RWKV_W = 256
RWKV_H = 4
RWKV_N = 64
RWKV_FEAT = 896
RWKV_GN_EPS = RWKV_N * 1e-5
GLA_W = 256
GLA_H = 4
GLA_DK = 32
GLA_DV = 64
GLA_HK = GLA_H * GLA_DK
GK_LORA = 16
GATE_NORMALIZER = 16.0
F_DENSE = 2816
N_EXPERTS = 8
ALPHA = (2.0 * DEPTH) ** 0.25
LN_EPS = 1e-5
RMS_EPS = 1e-6

LANES = 128
SMALL_W = LANES
DT_OFF = 0
GLO_OFF = 8
SSD_SLAB = SSD_W + SSD_CONV_DIM
GLA_SLAB = 2 * GLA_HK + 2 * GLA_W
IN_PAD = SSD_SLAB + SMALL_W + RWKV_FEAT + GLA_SLAB
SSD_CHUNK = 128
SSD_CARRY = 8
GLA_CHUNK = 128
GLA_SUB = 8
RWKV_CHUNK = 64
SSD_NB, RWKV_NB, GLA_NB = 2, 8, 4
RWKV_SKEW = 1
MOE_CAP_ROWS = 160
STEP_GRID = 32
VMEM_LIMIT = 56 * 1024 * 1024


def _cparams(*sem):
    return pltpu.CompilerParams(dimension_semantics=sem, vmem_limit_bytes=VMEM_LIMIT)


def _mm(a, b):
    return jnp.dot(a.astype(BF16), b.astype(BF16), preferred_element_type=F32)


def _mm_nt(a, b):
    return lax.dot_general(a.astype(BF16), b.astype(BF16), (((1,), (1,)), ((), ())),
                           preferred_element_type=F32)


def _split_bf16(x, terms):
    parts = []
    for _ in range(terms):
        p = x.astype(BF16)
        parts.append(p)
        x = x - p.astype(F32)
    return parts


def _mm_sel(sel, x, terms=3):
    return sum(jnp.dot(sel, p, preferred_element_type=F32) for p in _split_bf16(x, terms))


def _mm_xsel(x, sel, terms=2):
    return sum(jnp.dot(p, sel, preferred_element_type=F32) for p in _split_bf16(x, terms))


def _mm_hi(a, b):
    a_hi, a_lo = _split_bf16(a, 2)
    b_hi, b_lo = _split_bf16(b, 2)
    dot = lambda x, y: jnp.dot(x, y, preferred_element_type=F32)
    return dot(a_hi, b_hi) + (dot(a_hi, b_lo) + dot(a_lo, b_hi))


def _sigmoid(x):
    return 1.0 / (1.0 + jnp.exp(-x))


def _silu(x):
    return x * _sigmoid(x)


def _softplus(x):
    return jnp.maximum(x, 0.0) + jnp.log1p(jnp.exp(-jnp.abs(x)))


def _iota(shape, dim):
    return lax.broadcasted_iota(jnp.int32, shape, dim)


def _layernorm(u, g, b):
    mu = jnp.mean(u, axis=-1, keepdims=True)
    d = u - mu
    var = jnp.mean(d * d, axis=-1, keepdims=True)
    return d * lax.rsqrt(var + LN_EPS) * g + b


def _tri_incl(n):
    return (jnp.arange(n)[None, :] <= jnp.arange(n)[:, None]).astype(BF16)


def _seg_ones(n, seg, scale=1.0):
    idx = jnp.arange(n) // seg
    return jnp.where(idx[:, None] == idx[None, :], scale, 0.0).astype(BF16)


def _interleave(chains, skew=0):
    chains = list(chains)
    done = [False] * len(chains)
    rnd = 0
    while not all(done):
        for i, ch in enumerate(chains):
            if done[i] or rnd < i * skew:
                continue
            try:
                next(ch)
            except StopIteration:
                done[i] = True
        rnd += 1


def _const_spec(arr):
    nd = arr.ndim
    return pl.BlockSpec(arr.shape, lambda *_: (0,) * nd)


def _ada_kernel(c_ref, w_ref, b_ref, o_ref):
    o_ref[0] = _mm(_silu(c_ref[...]), w_ref[0]) + b_ref[0]


def _ada(c_all, w_ada, b_ada):
    rows = c_all.shape[0]
    tn = 1536
    return pl.pallas_call(
        _ada_kernel,
        grid=(DEPTH, 6 * D_MODEL // tn),
        in_specs=[pl.BlockSpec((rows, D_MODEL), lambda l, j: (0, 0)),
                  pl.BlockSpec((1, D_MODEL, tn), lambda l, j: (l, 0, j)),
                  pl.BlockSpec((1, 1, tn), lambda l, j: (l, 0, j))],
        out_specs=pl.BlockSpec((1, rows, tn), lambda l, j: (l, 0, j)),
        out_shape=jax.ShapeDtypeStruct((DEPTH, rows, 6 * D_MODEL), F32),
        compiler_params=_cparams("arbitrary", "arbitrary"),
        name="ada",
    )(c_all, w_ada, b_ada.reshape(DEPTH, 1, 6 * D_MODEL))


def _inproj_kernel(x_ref, sc_ref, sh_ref, w_ref, o_ssd, o_small, o_rwkv, o_gla):
    h = (x_ref[0] * (1.0 + sc_ref[0]) + sh_ref[0]).astype(BF16)
    off = 0
    for o_ref in (o_ssd, o_small, o_rwkv, o_gla):
        width = o_ref.shape[2]
        o_ref[0] = jnp.dot(h, w_ref[:, off:off + width], preferred_element_type=F32).astype(o_ref.dtype)
        off += width


def _mod_spec(mod, tm):
    if mod.shape[1] == 1:
        return pl.BlockSpec((1, 1, D_MODEL), lambda b, i, *_: (b, 0, 0))
    return pl.BlockSpec((1, tm, D_MODEL), lambda b, i, *_: (b, i, 0))


def _inproj(x, sc, sh, w_pad, tm):
    bsz, length, _ = x.shape
    tok = lambda width: pl.BlockSpec((1, tm, width), lambda b, i: (b, i, 0))
    return pl.pallas_call(
        _inproj_kernel,
        grid=(bsz, length // tm),
        in_specs=[tok(D_MODEL), _mod_spec(sc, tm), _mod_spec(sh, tm),
                  pl.BlockSpec((D_MODEL, IN_PAD), lambda b, i: (0, 0))],
        out_specs=[tok(SSD_SLAB), tok(SMALL_W), tok(RWKV_FEAT), tok(GLA_SLAB)],
        out_shape=[jax.ShapeDtypeStruct((bsz, length, SSD_SLAB), BF16),
                   jax.ShapeDtypeStruct((bsz, length, SMALL_W), F32),
                   jax.ShapeDtypeStruct((bsz, length, RWKV_FEAT), BF16),
                   jax.ShapeDtypeStruct((bsz, length, GLA_SLAB), BF16)],
        compiler_params=_cparams("arbitrary", "arbitrary"),
        name="inproj",
    )(x, sc, sh, w_pad)


def _ssd_token_math(xbc, small, conv_taps, cw, cb, dtb, alog):
    conv = cb + xbc * cw[3:4, :]
    for i in range(SSD_CONV_W - 1):
        conv = conv + conv_taps[i] * cw[i:i + 1, :]
    dt = _softplus(small + dtb)
    return _silu(conv), dt, dt * (-jnp.exp(alog))


def _ssd_finish(y, xs, z, dexp, ng):
    gated = (y + xs * dexp) * _silu(z)
    ms = jnp.mean(gated * gated, axis=-1, keepdims=True)
    return gated * lax.rsqrt(ms + RMS_EPS) * ng


def _ssd_kernel(slab_ref, small_ref, cw_ref, cb_ref, dtb_ref, alog_ref, dexp_ref, ng_ref, tri_ref,
                head_x_ref, group_mask_ref, pair_mask_ref, y_ref, conv_out_ref, h_out_ref, ext_ref, h_ref):
    c = pl.program_id(1)
    n_chunks = pl.num_programs(1)
    ch = SSD_CHUNK
    pad = ext_ref.shape[1] - ch
    causal = _iota((ch, ch), 1) <= _iota((ch, ch), 0)

    nb = slab_ref.shape[0]

    @pl.when(c == 0)
    def _():
        ext_ref[:, 0:pad, :] = jnp.zeros((nb, pad, SSD_CONV_DIM), F32)
        h_ref[...] = jnp.zeros_like(h_ref)

    def chain(j):
        z = slab_ref[j, :, 0:SSD_W].astype(F32)
        xbc = slab_ref[j, :, SSD_W:].astype(F32)
        small = small_ref[j]

        ext_ref[j, pad:pad + ch, :] = xbc
        first = pad - (SSD_CONV_W - 1)
        taps = [ext_ref[j, first + i:first + i + ch, :] for i in range(SSD_CONV_W - 1)]
        act, dt, dta = _ssd_token_math(xbc, small, taps, cw_ref[...], cb_ref[...], dtb_ref[...], alog_ref[...])
        ext_ref[j, 0:pad, :] = ext_ref[j, ch:ch + pad, :]

        xs = act[:, 0:SSD_W]
        bs = act[:, SSD_W:SSD_W + SSD_G * SSD_N]
        cs = act[:, SSD_W + SSD_G * SSD_N:]
        yield
        cum = _mm_sel(tri_ref[...], dta)
        cum_t = cum.T
        dt_t = dt.T
        cum_x = _mm_xsel(cum, head_x_ref[...], 3)
        dt_x = _mm_xsel(dt, head_x_ref[...], 3)
        yield
        last_x = cum_x[ch - 1:ch, :]
        xs_tail = (xs * (jnp.exp(last_x - cum_x) * dt_x)).astype(BF16)
        h_bd = h_ref[j]
        y_state = jnp.dot(cs.astype(BF16), h_bd.astype(BF16), preferred_element_type=F32) * jnp.exp(cum_x)
        upd = jnp.dot(bs.T.astype(BF16), xs_tail, preferred_element_type=F32)
        h_ref[j] = h_bd * jnp.exp(last_x) + upd * group_mask_ref[...]
        yield

        lane_group = _iota((1, SSD_G * SSD_N), 1) // SSD_N
        xs_b = xs.astype(BF16)
        ys = []
        for g in range(SSD_G):
            cb = _mm_nt(jnp.where(lane_group == g, cs, 0.0), bs)
            yield
            for pair in range(SSD_H // SSD_G // 2):
                scores = []
                for hh in range(2):
                    lane = DT_OFF + g * (SSD_H // SSD_G) + 2 * pair + hh
                    decay = jnp.exp(jnp.where(causal, cum[:, lane:lane + 1] - cum_t[lane:lane + 1, :], -jnp.inf))
                    scores.append((cb * decay * dt_t[lane:lane + 1, :]).astype(BF16))
                blk = g * (SSD_H // SSD_G) // 2 + pair
                x_pair = xs_b[:, blk * LANES:(blk + 1) * LANES]
                x_bd = jnp.concatenate([x_pair, x_pair], axis=0) * pair_mask_ref[...]
                ys.append(jnp.dot(jnp.concatenate(scores, axis=1), x_bd, preferred_element_type=F32))
                yield

        y = jnp.concatenate(ys, axis=-1) + y_state
        y_ref[j] = _ssd_finish(y, xs, z, dexp_ref[...], ng_ref[...]).astype(y_ref.dtype)

    _interleave([chain(j) for j in range(nb)])

    @pl.when(c == n_chunks - 1)
    def _():
        for j in range(nb):
            conv_out_ref[j] = ext_ref[j, pad - (SSD_CONV_W - 1):pad, :]
            h_out_ref[j] = h_ref[j]


def _ssd_prompt(slab, small, cw, cb, dtb, alog, dexp, ng):
    bsz, length, _ = slab.shape
    nb = SSD_NB if bsz % SSD_NB == 0 else 1
    gn, hp = SSD_G * SSD_N, SSD_H * SSD_P
    lane_head = jnp.arange(hp) // SSD_P
    head_x = (jnp.arange(SMALL_W)[:, None] == DT_OFF + lane_head[None, :]).astype(BF16)
    group_mask = (jnp.arange(gn)[:, None] // SSD_N == lane_head[None, :] // (SSD_H // SSD_G)).astype(F32)
    pair_mask = (jnp.arange(2 * SSD_CHUNK)[:, None] // SSD_CHUNK == jnp.arange(2 * SSD_P)[None, :] // SSD_P
                 ).astype(BF16)
    consts = (cw, cb, dtb, alog, dexp, ng, _tri_incl(SSD_CHUNK), head_x, group_mask, pair_mask)
    return pl.pallas_call(
        _ssd_kernel,
        grid=(bsz // nb, length // SSD_CHUNK),
        in_specs=[pl.BlockSpec((nb, SSD_CHUNK, SSD_SLAB), lambda b, c: (b, c, 0)),
                  pl.BlockSpec((nb, SSD_CHUNK, SMALL_W), lambda b, c: (b, c, 0))]
                 + [_const_spec(a) for a in consts],
        out_specs=[pl.BlockSpec((nb, SSD_CHUNK, SSD_W), lambda b, c: (b, c, 0)),
                   pl.BlockSpec((nb, SSD_CONV_W - 1, SSD_CONV_DIM), lambda b, c: (b, 0, 0)),
                   pl.BlockSpec((nb, gn, hp), lambda b, c: (b, 0, 0))],
        out_shape=[jax.ShapeDtypeStruct((bsz, length, SSD_W), BF16),
                   jax.ShapeDtypeStruct((bsz, SSD_CONV_W - 1, SSD_CONV_DIM), F32),
                   jax.ShapeDtypeStruct((bsz, gn, hp), F32)],
        scratch_shapes=[pltpu.VMEM((nb, SSD_CARRY + SSD_CHUNK, SSD_CONV_DIM), F32),
                        pltpu.VMEM((nb, gn, hp), F32)],
        compiler_params=_cparams("arbitrary", "arbitrary"),
        name="ssd_scan",
    )(slab, small, *consts)


def _rwkv_token_math(rw, prev, mu, lora_w, w0, a0, k_k, k_a, head_ones):
    mix = rw + mu * (prev - rw)
    r = mix[:, 0:RWKV_W]
    k = mix[:, RWKV_W:2 * RWKV_W]
    v = mix[:, 2 * RWKV_W:3 * RWKV_W]
    lora = mix[:, 3 * RWKV_W:]
    lane = _iota(lora.shape, 1)
    act = jnp.where(lane < 32, jnp.tanh(lora), jnp.where(lane < 64, lora, _sigmoid(lora)))
    lo = _mm(act, lora_w)
    w_log = -_softplus(-(w0 + lo[:, 0:RWKV_W])) - 0.5
    logw = -jnp.exp(w_log)
    aic = _sigmoid(a0 + lo[:, RWKV_W:2 * RWKV_W])
    gate = lo[:, 2 * RWKV_W:]
    kkf = k * k_k
    kk = kkf * lax.rsqrt(_mm_xsel(kkf * kkf, head_ones) + 1e-12)
    k2 = k * (1.0 + (aic - 1.0) * k_a)
    return r, k2, v, logw, aic, gate, kk


def _rwkv_finish(o, r, k2, v, gate, r_k, ln_g, ln_b, head_ones, head_avg):
    mu = _mm_xsel(o, head_avg)
    d = o - mu
    var = _mm_xsel(d * d, head_avg)
    on = d * lax.rsqrt(var + RWKV_GN_EPS) * ln_g + ln_b
    bonus = _mm_xsel(r * k2 * r_k, head_ones) * v
    return (on + bonus) * gate


def _rwkv_kernel(rw_ref, mu_ref, lw_ref, w0_ref, a0_ref, kk_ref, ka_ref,
                 rk_ref, lng_ref, lnb_ref, ones_ref, avg_ref, tri_ref, strict_ref, incl_ref, eye_ref,
                 y_ref, shift_out_ref, s_out_ref, ext_ref, s_ref):
    c = pl.program_id(1)
    n_chunks = pl.num_programs(1)
    ch = RWKV_CHUNK
    nh = RWKV_H
    w = RWKV_W
    head_bd = ones_ref[...]
    head_bd_f = head_bd.astype(F32)
    tile4 = lambda x: jnp.concatenate([x] * nh, axis=0)

    nb = rw_ref.shape[0]

    @pl.when(c == 0)
    def _():
        ext_ref[:, 0:8, :] = jnp.zeros((nb, 8, RWKV_FEAT), F32)
        s_ref[...] = jnp.zeros_like(s_ref)

    def chain(j):
        rw = rw_ref[j].astype(F32)
        ext_ref[j, 8:8 + ch, :] = rw
        prev = ext_ref[j, 7:7 + ch, :]
        ext_ref[j, 7:8, :] = rw[ch - 1:ch, :]

        r, k2, v, logw, aic, gate, kk = _rwkv_token_math(
            rw, prev, mu_ref[...], lw_ref[...], w0_ref[...], a0_ref[...], kk_ref[...], ka_ref[...], head_bd)
        yield

        cumw = _mm_sel(tri_ref[...], logw)
        yield
        last = cumw[ch - 1:ch, :]
        inv_g = jnp.exp(-cumw)
        to_end = jnp.exp(last - cumw)
        kka = kk * aic
        a_t = (-kk * jnp.exp(cumw - logw)).astype(BF16)
        r_t = (r * jnp.exp(cumw)).astype(BF16)
        b_t = (kka * inv_g).astype(BF16)
        k_t = (k2 * inv_g).astype(BF16)
        bk_end = jnp.concatenate([kka * to_end, k2 * to_end], axis=0).astype(BF16)

        bd = lambda x: tile4(x.astype(BF16)) * head_bd
        nt = lambda x, y: lax.dot_general(x, y, (((1,), (1,)), ((), ())), preferred_element_type=F32)
        vb = v.astype(BF16)
        ar = jnp.concatenate([a_t, r_t], axis=0)
        prod = nt(ar, jnp.concatenate([bd(b_t), bd(k_t)], axis=0))
        a_ab = prod[0:ch, 0:w] * strict_ref[...]
        a_ak = prod[0:ch, w:2 * w] * strict_ref[...]
        a_rbk = (prod[ch:2 * ch, :] * incl_ref[...]).astype(BF16)
        yield
        s_bd = s_ref[j]
        uo = nt(ar, s_bd.astype(BF16))
        rhs = uo[0:ch, :] + jnp.dot(a_ak.astype(BF16), bd(vb), preferred_element_type=F32)
        yield

        t_inv = eye_ref[...] + a_ab
        x = a_ab
        x_bd = bd(x)
        power = 1
        while 2 * power < ch:
            x = jnp.dot(x.astype(BF16), x_bd, preferred_element_type=F32)
            x_bd = bd(x)
            yield
            t_inv = t_inv + jnp.dot(t_inv.astype(BF16), x_bd, preferred_element_type=F32)
            yield
            power *= 2

        p = jnp.dot(t_inv.astype(BF16), bd(rhs), preferred_element_type=F32)
        yield
        pb = p.astype(BF16)
        o = uo[ch:2 * ch, :] + jnp.dot(a_rbk, jnp.concatenate([bd(pb), bd(vb)], axis=0),
                                       preferred_element_type=F32)
        pv = jnp.concatenate([pb, vb], axis=0)
        yield

        upd = jnp.dot(pv.astype(F32).T.astype(BF16), bk_end, preferred_element_type=F32)
        s_ref[j] = s_bd * jnp.exp(last) + upd * head_bd_f
        yield

        y_ref[j] = _rwkv_finish(o, r, k2, v, gate, rk_ref[...], lng_ref[...], lnb_ref[...],
                                head_bd, avg_ref[...]).astype(y_ref.dtype)

    _interleave([chain(j) for j in range(nb)], skew=RWKV_SKEW)

    @pl.when(c == n_chunks - 1)
    def _():
        for j in range(nb):
            shift_out_ref[j] = ext_ref[j, 7:8, :]
            s_out_ref[j] = s_ref[j]


def _rwkv_consts():
    ch = RWKV_CHUNK
    assert ch == RWKV_N, "bd() reuses the head mask, which needs chunk rows == head width"
    t = jnp.arange(ch)[:, None]
    j = jnp.arange(2 * RWKV_W)[None, :] % ch
    strict = (j[:, :RWKV_W] < t).astype(F32)
    incl = (j <= t).astype(F32)
    eye = (j[:, :RWKV_W] == t).astype(F32)
    return (_seg_ones(RWKV_W, RWKV_N), _seg_ones(RWKV_W, RWKV_N, 1.0 / RWKV_N), _tri_incl(ch),
            strict, incl, eye)


def _rwkv_prompt(rw, *params):
    bsz, length, _ = rw.shape
    nb = RWKV_NB if bsz % RWKV_NB == 0 else 1
    consts = tuple(params) + _rwkv_consts()
    return pl.pallas_call(
        _rwkv_kernel,
        grid=(bsz // nb, length // RWKV_CHUNK),
        in_specs=[pl.BlockSpec((nb, RWKV_CHUNK, RWKV_FEAT), lambda b, c: (b, c, 0))]
                 + [_const_spec(a) for a in consts],
        out_specs=[pl.BlockSpec((nb, RWKV_CHUNK, RWKV_W), lambda b, c: (b, c, 0)),
                   pl.BlockSpec((nb, 1, RWKV_FEAT), lambda b, c: (b, 0, 0)),
                   pl.BlockSpec((nb, RWKV_W, RWKV_W), lambda b, c: (b, 0, 0))],
        out_shape=[jax.ShapeDtypeStruct((bsz, length, RWKV_W), BF16),
                   jax.ShapeDtypeStruct((bsz, 1, RWKV_FEAT), F32),
                   jax.ShapeDtypeStruct((bsz, RWKV_W, RWKV_W), F32)],
        scratch_shapes=[pltpu.VMEM((nb, RWKV_CHUNK + 8, RWKV_FEAT), F32),
                        pltpu.VMEM((nb, RWKV_W, RWKV_W), F32)],
        compiler_params=_cparams("arbitrary", "arbitrary"),
        name="rwkv_scan",
    )(rw, *consts)


def _gla_gate_log(small, gk2_pad, b_gk):
    x = _mm_hi(small, gk2_pad) + b_gk
    return -_softplus(-x) / GATE_NORMALIZER


def _gla_finish(o, gg, norm_g, head_avg):
    ms = _mm_xsel(o * o, head_avg)
    return o * lax.rsqrt(ms + RMS_EPS) * norm_g * _silu(gg)


def _gla_kernel(slab_ref, small_ref, gk2_ref, bgk_ref, ng_ref, avg_ref, tri_ref, expand_ref,
                y_ref, s_out_ref, s_ref):
    c = pl.program_id(1)
    n_chunks = pl.num_programs(1)
    ch = GLA_CHUNK
    sub = GLA_SUB
    hk = GLA_HK
    expand = expand_ref[...]
    expand_f = expand.astype(F32)
    pos = _iota((ch, 1), 0) % sub
    t_idx = _iota((ch, 1), 0)
    hmask_k = (_iota((GLA_H * sub, hk), 0) // sub == _iota((GLA_H * sub, hk), 1) // GLA_DK).astype(F32)
    hmask_v = (_iota((GLA_H * sub, GLA_W), 0) // sub == _iota((GLA_H * sub, GLA_W), 1) // GLA_DV).astype(F32)

    nb = slab_ref.shape[0]

    @pl.when(c == 0)
    def _():
        s_ref[...] = jnp.zeros_like(s_ref)

    def chain(j):
        q = slab_ref[j, :, 0:hk].astype(F32) * (GLA_DK ** -0.5)
        k = slab_ref[j, :, hk:2 * hk].astype(F32)
        v = slab_ref[j, :, 2 * hk:2 * hk + GLA_W].astype(F32)
        gg = slab_ref[j, :, 2 * hk + GLA_W:].astype(F32)
        small = small_ref[j]

        lg = _gla_gate_log(small, gk2_ref[...], bgk_ref[...])
        yield
        cum = _mm_sel(tri_ref[...], lg)
        yield
        last = cum[ch - 1:ch, :]
        s_bd = s_ref[j]

        o = _mm(q * jnp.exp(cum), s_bd)

        group = 4
        for d0 in range(0, sub, group):
            prods, v_shift = [], []
            for delta in range(d0, d0 + group):
                if delta == 0:
                    k_s, c_s, v_s = k, cum, v
                else:
                    k_s = pltpu.roll(k, delta, 0)
                    c_s = pltpu.roll(cum, delta, 0)
                    v_s = pltpu.roll(v, delta, 0)
                valid = pos >= delta
                w_pair = jnp.exp(jnp.where(valid, cum - c_s, 0.0))
                prods.append(jnp.where(valid, q * k_s * w_pair, 0.0).astype(BF16))
                v_shift.append(v_s)
            yield
            att = jnp.dot(jnp.concatenate(prods, axis=0), expand, preferred_element_type=F32)
            o = o + sum(att[g * ch:(g + 1) * ch, :] * v_shift[g] for g in range(group))
            yield

        vb = v.astype(BF16)
        atts = []
        for blk in range(1, ch // sub):
            lo = blk * sub
            ref_pt = cum[lo - 1:lo, :]
            qd = q[lo:lo + sub, :] * jnp.exp(cum[lo:lo + sub, :] - ref_pt)
            kp = k * jnp.exp(jnp.where(t_idx < lo, ref_pt - cum, -jnp.inf))
            q4 = jnp.concatenate([qd] * GLA_H, axis=0) * hmask_k
            atts.append(_mm_nt(q4, kp).astype(BF16))
            yield
        o4 = jnp.dot(jnp.concatenate(atts, axis=0), vb, preferred_element_type=F32)
        yield
        cross = [jnp.zeros((sub, GLA_W), F32)]
        for blk in range(ch // sub - 1):
            o4_b = o4[blk * GLA_H * sub:(blk + 1) * GLA_H * sub, :] * hmask_v
            cross.append(sum(o4_b[h * sub:(h + 1) * sub, :] for h in range(GLA_H)))
        o = o + jnp.concatenate(cross, axis=0)

        kt_t = (k * jnp.exp(last - cum)).T
        s_ref[j] = s_bd * _col(jnp.exp(last)) + _mm(kt_t, vb) * expand_f
        yield

        y_ref[j] = _gla_finish(o, gg, ng_ref[...], avg_ref[...]).astype(y_ref.dtype)

    _interleave([chain(j) for j in range(nb)], skew=2)

    @pl.when(c == n_chunks - 1)
    def _():
        s_out_ref[...] = s_ref[...]


def _gla_prompt(slab, small, gk2_pad, b_gk, ng):
    bsz, length, _ = slab.shape
    nb = GLA_NB if bsz % GLA_NB == 0 else 1
    expand = (jnp.arange(GLA_HK)[:, None] // GLA_DK == jnp.arange(GLA_W)[None, :] // GLA_DV).astype(BF16)
    consts = (gk2_pad, b_gk, ng, _seg_ones(GLA_W, GLA_DV, 1.0 / GLA_DV), _tri_incl(GLA_CHUNK), expand)
    return pl.pallas_call(
        _gla_kernel,
        grid=(bsz // nb, length // GLA_CHUNK),
        in_specs=[pl.BlockSpec((nb, GLA_CHUNK, GLA_SLAB), lambda b, c: (b, c, 0)),
                  pl.BlockSpec((nb, GLA_CHUNK, SMALL_W), lambda b, c: (b, c, 0))]
                 + [_const_spec(a) for a in consts],
        out_specs=[pl.BlockSpec((nb, GLA_CHUNK, GLA_W), lambda b, c: (b, c, 0)),
                   pl.BlockSpec((nb, GLA_HK, GLA_W), lambda b, c: (b, 0, 0))],
        out_shape=[jax.ShapeDtypeStruct((bsz, length, GLA_W), BF16),
                   jax.ShapeDtypeStruct((bsz, GLA_HK, GLA_W), F32)],
        scratch_shapes=[pltpu.VMEM((nb, GLA_HK, GLA_W), F32)],
        compiler_params=_cparams("arbitrary", "arbitrary"),
        name="gla_scan",
    )(slab, small, *consts)


def _col(row):
    n = row.shape[1]
    eye = _iota((n, n), 0) == _iota((n, n), 1)
    return jnp.sum(jnp.where(eye, row, 0.0), axis=1, keepdims=True)


def _store_t(dst_ref, row0, x):
    for cblk in range(x.shape[1] // LANES):
        dst_ref[row0 + cblk * LANES:row0 + (cblk + 1) * LANES, :] = x[:, cblk * LANES:(cblk + 1) * LANES].T


def _load_t(src_ref, width):
    return jnp.concatenate([src_ref[cblk * LANES:(cblk + 1) * LANES, :].T for cblk in range(width // LANES)],
                           axis=1)


_TS_X, _TS_B, _TS_C, _TS_DT, _TS_DA = 0, SSD_W, SSD_W + SSD_G * SSD_N, SSD_CONV_DIM, SSD_CONV_DIM + SMALL_W
_TR_R, _TR_W, _TR_K, _TR_V, _TR_KK, _TR_KA = (i * RWKV_W for i in range(6))
_TG_Q, _TG_K, _TG_G, _TG_V = 0, GLA_HK, 2 * GLA_HK, 3 * GLA_HK


def _step_kernel(ssd_ref, small_ref, rw_ref, gla_ref, conv0_ref, shift0_ref, h0_ref, rs0_ref, gs0_ref,
                 cw_ref, cb_ref, dtb_ref, alog_ref, dexp_ref, sng_ref,
                 mu_ref, lw_ref, w0_ref, a0_ref, kk_ref, ka_ref, rk_ref, lng_ref, lnb_ref,
                 gk2_ref, bgk_ref, gng_ref, ones_ref, ravg_ref, gavg_ref,
                 y_ssd_ref, y_rwkv_ref, y_gla_ref, conv_out_ref, h_out_ref, rs_out_ref, gs_out_ref,
                 t_ssd, t_rwkv, t_gla, o_ssd_t, o_rwkv_t, o_gla_t):
    i = pl.program_id(0)
    hk = GLA_HK

    def ssd_tokens():
        xbc = ssd_ref[:, SSD_W:].astype(F32)
        taps = [conv0_ref[:, t * SSD_CONV_DIM:(t + 1) * SSD_CONV_DIM] for t in range(SSD_CONV_W - 1)]
        return xbc, _ssd_token_math(xbc, small_ref[...], taps, cw_ref[...], cb_ref[...],
                                    dtb_ref[...], alog_ref[...])

    def rwkv_tokens():
        return _rwkv_token_math(rw_ref[...].astype(F32), shift0_ref[...], mu_ref[...], lw_ref[...],
                                w0_ref[...], a0_ref[...], kk_ref[...], ka_ref[...], ones_ref[...])

    @pl.when(i == 0)
    def _():
        xbc, (act, dt, dta) = ssd_tokens()
        conv_out_ref[:, 0:2 * SSD_CONV_DIM] = conv0_ref[:, SSD_CONV_DIM:]
        conv_out_ref[:, 2 * SSD_CONV_DIM:] = xbc
        _store_t(t_ssd, _TS_X, act)
        _store_t(t_ssd, _TS_DT, dt)
        _store_t(t_ssd, _TS_DA, jnp.exp(dta))
        r, k2, v, logw, aic, gate, kk = rwkv_tokens()
        for off, x in ((_TR_R, r), (_TR_W, jnp.exp(logw)), (_TR_K, k2), (_TR_V, v), (_TR_KK, kk),
                       (_TR_KA, kk * aic)):
            _store_t(t_rwkv, off, x)
        lg = _gla_gate_log(small_ref[...], gk2_ref[...], bgk_ref[...])
        _store_t(t_gla, _TG_Q, gla_ref[:, 0:hk].astype(F32) * (GLA_DK ** -0.5))
        _store_t(t_gla, _TG_K, gla_ref[:, hk:2 * hk].astype(F32))
        _store_t(t_gla, _TG_G, jnp.exp(lg))
        _store_t(t_gla, _TG_V, gla_ref[:, 2 * hk:2 * hk + GLA_W].astype(F32))
        o_gla_t[...] = jnp.zeros_like(o_gla_t)

    def row(ref, idx):
        return ref[pl.ds(idx, 1), :]

    def rows(ref, idx, count):
        return ref[pl.ds(pl.multiple_of(idx, count), count), :]

    units = h0_ref.shape[0] // SSD_N
    per_head = SSD_P // units
    h = i // per_head
    p0 = (i % per_head) * units
    b_t = rows(t_ssd, _TS_B + (h // (SSD_H // SSD_G)) * SSD_N, SSD_N)
    c_t = rows(t_ssd, _TS_C + (h // (SSD_H // SSD_G)) * SSD_N, SSD_N)
    da_row = row(t_ssd, _TS_DA + DT_OFF + h)
    dt_row = row(t_ssd, _TS_DT + DT_OFF + h)
    for u in range(units):
        p_idx = h * SSD_P + p0 + u
        blk = slice(u * SSD_N, (u + 1) * SSD_N)
        h_new = h0_ref[blk, :] * da_row + (row(t_ssd, _TS_X + p_idx) * dt_row) * b_t
        o_ssd_t[pl.ds(p_idx, 1), :] = jnp.sum(h_new * c_t, axis=0, keepdims=True)
        h_out_ref[blk, :] = h_new

    units = rs0_ref.shape[0] // RWKV_N
    per_head = RWKV_N // units
    h = i // per_head
    v0 = (i % per_head) * units
    seg = lambda off: rows(t_rwkv, off + h * RWKV_N, RWKV_N)
    r_t, w_t, k_t, kk_t, ka_t = seg(_TR_R), seg(_TR_W), seg(_TR_K), seg(_TR_KK), seg(_TR_KA)
    for u in range(units):
        v_idx = h * RWKV_N + v0 + u
        blk = slice(u * RWKV_N, (u + 1) * RWKV_N)
        s = rs0_ref[blk, :]
        sa = jnp.sum(s * (-kk_t), axis=0, keepdims=True)
        s_new = s * w_t + sa * ka_t + row(t_rwkv, _TR_V + v_idx) * k_t
        o_rwkv_t[pl.ds(v_idx, 1), :] = jnp.sum(s_new * r_t, axis=0, keepdims=True)
        rs_out_ref[blk, :] = s_new

    units = gs0_ref.shape[0] // GLA_DV
    per_head = GLA_DK // units
    h = i // per_head
    k0 = (i % per_head) * units
    v_t = rows(t_gla, _TG_V + h * GLA_DV, GLA_DV)
    acc = jnp.zeros((GLA_DV, LANES), F32)
    for u in range(units):
        k_idx = h * GLA_DK + k0 + u
        blk = slice(u * GLA_DV, (u + 1) * GLA_DV)
        s_new = gs0_ref[blk, :] * row(t_gla, _TG_G + k_idx) + row(t_gla, _TG_K + k_idx) * v_t
        acc = acc + row(t_gla, _TG_Q + k_idx) * s_new
        gs_out_ref[blk, :] = s_new
    o_rows = pl.ds(pl.multiple_of(h * GLA_DV, GLA_DV), GLA_DV)
    o_gla_t[o_rows, :] = o_gla_t[o_rows, :] + acc

    @pl.when(i == pl.num_programs(0) - 1)
    def _():
        _, (act, _, _) = ssd_tokens()
        y_ssd_ref[...] = _ssd_finish(_load_t(o_ssd_t, SSD_W), act[:, 0:SSD_W], ssd_ref[:, 0:SSD_W].astype(F32),
                                     dexp_ref[...], sng_ref[...]).astype(y_ssd_ref.dtype)
        r, k2, v, _, _, gate, _ = rwkv_tokens()
        y_rwkv_ref[...] = _rwkv_finish(_load_t(o_rwkv_t, RWKV_W), r, k2, v, gate, rk_ref[...], lng_ref[...],
                                       lnb_ref[...], ones_ref[...], ravg_ref[...]).astype(y_rwkv_ref.dtype)
        y_gla_ref[...] = _gla_finish(_load_t(o_gla_t, GLA_W), gla_ref[:, 2 * hk + GLA_W:].astype(F32),
                                     gng_ref[...], gavg_ref[...]).astype(y_gla_ref.dtype)


def _mixer_step(ssd_slab, small, rw, gla_slab, conv0, shift0, h0, rs0, gs0, ssd_p, rwkv_p, gla_p):
    bsz = ssd_slab.shape[0]
    assert bsz == LANES, "the single-token mixer keeps exactly one vreg row of batch entries on the lanes"
    flat_t = lambda s: s.reshape(bsz, -1).T
    h0, rs0, gs0 = flat_t(h0), flat_t(rs0), flat_t(gs0)
    consts = (tuple(ssd_p) + tuple(rwkv_p) + tuple(gla_p)
              + (_seg_ones(RWKV_W, RWKV_N), _seg_ones(RWKV_W, RWKV_N, 1.0 / RWKV_N),
                 _seg_ones(GLA_W, GLA_DV, 1.0 / GLA_DV)))
    full = lambda a: pl.BlockSpec(a.shape, lambda i: (0,) * a.ndim)
    cols = lambda a: pl.BlockSpec((a.shape[0] // STEP_GRID, bsz), lambda i: (i, 0))
    tokens = (ssd_slab, small, rw, gla_slab, conv0, shift0)
    outs = pl.pallas_call(
        _step_kernel,
        grid=(STEP_GRID,),
        in_specs=[full(a) for a in tokens] + [cols(h0), cols(rs0), cols(gs0)] + [full(a) for a in consts],
        out_specs=[full(jax.ShapeDtypeStruct((bsz, w), BF16)) for w in (SSD_W, RWKV_W, GLA_W)]
                  + [full(conv0), cols(h0), cols(rs0), cols(gs0)],
        out_shape=[jax.ShapeDtypeStruct((bsz, SSD_W), BF16),
                   jax.ShapeDtypeStruct((bsz, RWKV_W), BF16),
                   jax.ShapeDtypeStruct((bsz, GLA_W), BF16),
                   jax.ShapeDtypeStruct(conv0.shape, F32),
                   jax.ShapeDtypeStruct(h0.shape, F32),
                   jax.ShapeDtypeStruct(rs0.shape, F32),
                   jax.ShapeDtypeStruct(gs0.shape, F32)],
        scratch_shapes=[pltpu.VMEM((SSD_CONV_DIM + 2 * SMALL_W, LANES), F32),
                        pltpu.VMEM((6 * RWKV_W, LANES), F32),
                        pltpu.VMEM((3 * GLA_HK + GLA_W, LANES), F32),
                        pltpu.VMEM((SSD_W, LANES), F32),
                        pltpu.VMEM((RWKV_W, LANES), F32),
                        pltpu.VMEM((GLA_W, LANES), F32)],
        compiler_params=_cparams("arbitrary"),
        name="mixer_step",
    )(*tokens, h0, rs0, gs0, *consts)
    y_ssd, y_rwkv, y_gla, conv_new, h_new, rs_new, gs_new = outs
    return (y_ssd, y_rwkv, y_gla, conv_new.reshape(bsz, SSD_CONV_W - 1, SSD_CONV_DIM),
            h_new.T.reshape(bsz, SSD_H, SSD_P, SSD_N), rs_new.T.reshape(bsz, RWKV_H, RWKV_N, RWKV_N),
            gs_new.T.reshape(bsz, GLA_H, GLA_DK, GLA_DV))


def _mod_rows(mod_ref, rows):
    return mod_ref[0] if mod_ref.shape[1] == 1 else mod_ref[0, rows, :]


def _row_parts(tm):
    return 2 if tm % 256 == 0 else 1


def _outproj_kernel(ys_ref, yr_ref, yg_ref, x_ref, gt_ref, sc_ref, sh_ref, w_ref, g_ref, b_ref, rt_ref,
                    o_ref, logit_ref):
    tm = x_ref.shape[1]
    parts = _row_parts(tm)

    def chain(r):
        rows = slice(r * tm // parts, (r + 1) * tm // parts)
        m = (jnp.dot(ys_ref[0, rows, :], w_ref[0:SSD_W, :], preferred_element_type=F32)
             + jnp.dot(yr_ref[0, rows, :], w_ref[SSD_W:SSD_W + RWKV_W, :], preferred_element_type=F32)
             + jnp.dot(yg_ref[0, rows, :], w_ref[SSD_W + RWKV_W:, :], preferred_element_type=F32))
        yield
        u = ALPHA * x_ref[0, rows, :] + (1.0 + _mod_rows(gt_ref, rows)) * m
        x1 = _layernorm(u, g_ref[...], b_ref[...])
        o_ref[0, rows, :] = x1
        h = x1 * (1.0 + _mod_rows(sc_ref, rows)) + _mod_rows(sh_ref, rows)
        logit_ref[0, rows, :] = _mm_hi(h, rt_ref[...])

    _interleave([chain(r) for r in range(parts)], skew=1)


def _outproj(y_ssd, y_rwkv, y_gla, x, gt, sc2, sh2, w_out, ln_g, ln_b, router_pad, tm):
    bsz, length, _ = x.shape
    tok = lambda width: pl.BlockSpec((1, tm, width), lambda b, i: (b, i, 0))
    vec = pl.BlockSpec((1, D_MODEL), lambda b, i: (0, 0))
    return pl.pallas_call(
        _outproj_kernel,
        grid=(bsz, length // tm),
        in_specs=[tok(SSD_W), tok(RWKV_W), tok(GLA_W), tok(D_MODEL),
                  _mod_spec(gt, tm), _mod_spec(sc2, tm), _mod_spec(sh2, tm),
                  pl.BlockSpec((D_MODEL, D_MODEL), lambda b, i: (0, 0)), vec, vec,
                  pl.BlockSpec((D_MODEL, LANES), lambda b, i: (0, 0))],
        out_specs=[tok(D_MODEL), tok(LANES)],
        out_shape=[jax.ShapeDtypeStruct((bsz, length, D_MODEL), F32),
                   jax.ShapeDtypeStruct((bsz, length, LANES), F32)],
        compiler_params=_cparams("arbitrary", "arbitrary"),
        name="outproj_ln",
    )(y_ssd, y_rwkv, y_gla, x, gt, sc2, sh2, w_out, ln_g, ln_b, router_pad)


def _ffn_kernel(ys_ref, yr_ref, yg_ref, x_ref, gt1_ref, sc_ref, sh_ref, gt2_ref, wo_ref, g1_ref, b1_ref,
                wg_ref, wu_ref, wd_ref, g2_ref, b2_ref, o_ref):
    tm = x_ref.shape[1]
    parts = _row_parts(tm)

    def chain(r):
        rows = slice(r * tm // parts, (r + 1) * tm // parts)
        m = (jnp.dot(ys_ref[0, rows, :], wo_ref[0:SSD_W, :], preferred_element_type=F32)
             + jnp.dot(yr_ref[0, rows, :], wo_ref[SSD_W:SSD_W + RWKV_W, :], preferred_element_type=F32)
             + jnp.dot(yg_ref[0, rows, :], wo_ref[SSD_W + RWKV_W:, :], preferred_element_type=F32))
        yield
        x1 = _layernorm(ALPHA * x_ref[0, rows, :] + (1.0 + _mod_rows(gt1_ref, rows)) * m,
                        g1_ref[...], b1_ref[...])
        h = (x1 * (1.0 + _mod_rows(sc_ref, rows)) + _mod_rows(sh_ref, rows)).astype(BF16)
        yield
        gate = jnp.dot(h, wg_ref[...], preferred_element_type=F32)
        up = jnp.dot(h, wu_ref[...], preferred_element_type=F32)
        yield
        f = jnp.dot((_silu(gate) * up).astype(BF16), wd_ref[...], preferred_element_type=F32)
        yield
        u = ALPHA * x1 + (1.0 + _mod_rows(gt2_ref, rows)) * f
        o_ref[0, rows, :] = _layernorm(u, g2_ref[...], b2_ref[...])

    _interleave([chain(r) for r in range(parts)], skew=1)


def _outproj_ffn(y_ssd, y_rwkv, y_gla, x, gt1, sc2, sh2, gt2, w_out, ln1_g, ln1_b, wg, wu, wd, ln2_g, ln2_b, tm):
    bsz, length, _ = x.shape
    tok = lambda width: pl.BlockSpec((1, tm, width), lambda b, i: (b, i, 0))
    vec = pl.BlockSpec((1, D_MODEL), lambda b, i: (0, 0))
    resident = lambda a: pl.BlockSpec(a.shape, lambda b, i: (0, 0), pipeline_mode=pl.Buffered(1))
    return pl.pallas_call(
        _ffn_kernel,
        grid=(bsz, length // tm),
        in_specs=[tok(SSD_W), tok(RWKV_W), tok(GLA_W), tok(D_MODEL),
                  _mod_spec(gt1, tm), _mod_spec(sc2, tm), _mod_spec(sh2, tm), _mod_spec(gt2, tm),
                  resident(w_out), vec, vec, resident(wg), resident(wu), resident(wd), vec, vec],
        out_specs=tok(D_MODEL),
        out_shape=jax.ShapeDtypeStruct((bsz, length, D_MODEL), F32),
        compiler_params=_cparams("arbitrary", "arbitrary"),
        name="outproj_ffn_ln",
    )(y_ssd, y_rwkv, y_gla, x, gt1, sc2, sh2, gt2, w_out, ln1_g, ln1_b, wg, wu, wd, ln2_g, ln2_b)


def _moe_kernel(x_ref, sc_ref, sh_ref, gt_ref, logit_ref, wg_ref, wu_ref, wd_ref, g_ref, b_ref, tri_ref, o_ref,
                h_ref, rank_t_ref, comb_t_ref, cnt_ref, acc_ref, *, half, cap):
    e = pl.program_id(2)
    tm = 2 * half

    @pl.when(e == 0)
    def _():
        h_ref[...] = (x_ref[0] * (1.0 + sc_ref[0]) + sh_ref[0]).astype(BF16)
        acc_ref[...] = jnp.zeros_like(acc_ref)
        lane = _iota((tm, LANES), 1).astype(F32)
        logits = jnp.where(lane < N_EXPERTS, logit_ref[0], -jnp.inf)
        m1 = jnp.max(logits, axis=-1, keepdims=True)
        i1 = jnp.min(jnp.where(logits == m1, lane, float(LANES)), axis=-1, keepdims=True)
        rest = jnp.where(lane == i1, -jnp.inf, logits)
        m2 = jnp.max(rest, axis=-1, keepdims=True)
        i2 = jnp.min(jnp.where(rest == m2, lane, float(LANES)), axis=-1, keepdims=True)
        e2 = jnp.exp(m2 - m1)
        den = 1.0 + e2
        comb = jnp.where(lane == i1, 1.0 / den, 0.0) + jnp.where(lane == i2, e2 / den, 0.0)
        sel = (lane == i1) | (lane == i2)
        sel_f = sel.astype(F32)
        ranks = []
        for hf in range(2):
            s = sel_f[hf * half:(hf + 1) * half, :]
            before = jnp.dot(tri_ref[...], s.astype(BF16), preferred_element_type=F32)
            ranks.append(jnp.where(sel[hf * half:(hf + 1) * half, :], before, -1.0))
            cnt_ref[hf:hf + 1, :] = jnp.sum(s, axis=0, keepdims=True)
        rank = jnp.concatenate(ranks, axis=0)
        for blk in range(tm // LANES):
            rows = slice(blk * LANES, (blk + 1) * LANES)
            rank_t_ref[:, rows] = rank[rows, :].T
            comb_t_ref[:, rows] = comb[rows, :].T

    rank_t_e = rank_t_ref[pl.ds(e, 1), :]
    comb_t_e = comb_t_ref[pl.ds(e, 1), :]
    n_max = jnp.max(jnp.where(_iota((2, LANES), 1) == e, cnt_ref[...], 0.0))
    n_pass = (n_max.astype(jnp.int32) + (cap - 1)) // cap

    def one_pass(p, carry):
        slot = _iota((cap, 1), 0).astype(F32) + (p * cap).astype(F32)
        picks, xs, gates = [], [], []
        for hf in range(2):
            rows = slice(hf * half, (hf + 1) * half)
            hit = rank_t_e[:, rows] == slot
            pick = hit.astype(BF16)
            picks.append(pick)
            xs.append(jnp.dot(pick, h_ref[rows, :], preferred_element_type=F32).astype(BF16))
            gates.append(jnp.sum(jnp.where(hit, comb_t_e[:, rows], 0.0), axis=1, keepdims=True))
        xc = jnp.concatenate(xs, axis=0)
        a = _silu(jnp.dot(xc, wg_ref[0], preferred_element_type=F32)) * jnp.dot(
            xc, wu_ref[0], preferred_element_type=F32)
        out_e = jnp.dot(a.astype(BF16), wd_ref[0], preferred_element_type=F32)
        for hf in range(2):
            rows = slice(hf * half, (hf + 1) * half)
            weighted = (out_e[hf * cap:(hf + 1) * cap, :] * gates[hf]).astype(BF16)
            acc_ref[rows, :] += lax.dot_general(picks[hf], weighted, (((0,), (0,)), ((), ())),
                                                preferred_element_type=F32)
        return carry

    one_pass(jnp.int32(0), 0)
    lax.fori_loop(1, n_pass, one_pass, 0)

    @pl.when(e == pl.num_programs(2) - 1)
    def _():
        u = ALPHA * x_ref[0] + (1.0 + gt_ref[0]) * acc_ref[...]
        o_ref[0] = _layernorm(u, g_ref[...], b_ref[...])


def _moe(x, sc, sh, gt, logits, wg, wu, wd, ln_g, ln_b, tm):
    bsz, length, _ = x.shape
    assert tm % LANES == 0
    half = tm // 2
    cap = MOE_CAP_ROWS if half >= 2 * MOE_CAP_ROWS else half
    tri = (jnp.arange(half)[None, :] < jnp.arange(half)[:, None]).astype(BF16)
    tok = pl.BlockSpec((1, tm, D_MODEL), lambda b, i, e: (b, i, 0))
    vec = pl.BlockSpec((1, D_MODEL), lambda b, i, e: (0, 0))
    wspec = pl.BlockSpec((1, D_MODEL, D_MODEL), lambda b, i, e: (e, 0, 0))
    return pl.pallas_call(
        functools.partial(_moe_kernel, half=half, cap=cap),
        grid=(bsz, length // tm, N_EXPERTS),
        in_specs=[tok, _mod_spec(sc, tm), _mod_spec(sh, tm), _mod_spec(gt, tm),
                  pl.BlockSpec((1, tm, LANES), lambda b, i, e: (b, i, 0)),
                  wspec, wspec, wspec, vec, vec, _const_spec(tri)],
        out_specs=tok,
        out_shape=jax.ShapeDtypeStruct((bsz, length, D_MODEL), F32),
        scratch_shapes=[pltpu.VMEM((tm, D_MODEL), BF16), pltpu.VMEM((LANES, tm), F32),
                        pltpu.VMEM((LANES, tm), F32), pltpu.VMEM((2, LANES), F32),
                        pltpu.VMEM((tm, D_MODEL), F32)],
        compiler_params=_cparams("arbitrary", "arbitrary", "arbitrary"),
        name="moe_ln",
    )(x, sc, sh, gt, logits, wg, wu, wd, ln_g, ln_b, tri)


def _pad_lanes(vec, offset, width=SMALL_W):
    out = jnp.zeros((1, width), F32)
    return out.at[0, offset:offset + vec.shape[0]].set(vec)


def _layer_params(p, l):
    w_in = p["w_in"][l]
    off = [0]
    for s in (SSD_W, SSD_CONV_DIM, SSD_H, RWKV_FEAT, GLA_HK, GLA_HK, GLA_W, GK_LORA, GLA_W):
        off.append(off[-1] + s)
    piece = lambda i: w_in[:, off[i]:off[i + 1]]
    small = jnp.zeros((D_MODEL, SMALL_W), F32)
    small = small.at[:, DT_OFF:DT_OFF + SSD_H].set(piece(2)).at[:, GLO_OFF:GLO_OFF + GK_LORA].set(piece(7))
    w_pad = jnp.concatenate([piece(0), piece(1), small, piece(3), piece(4), piece(5), piece(6), piece(8)],
                            axis=1).astype(BF16)
    ssd_p = (p["ssd_conv_w"][l], p["ssd_conv_b"][l][None, :],
             _pad_lanes(p["ssd_dt_bias"][l], DT_OFF), _pad_lanes(p["ssd_a_log"][l], DT_OFF),
             jnp.repeat(p["ssd_d"][l], SSD_P)[None, :], p["ssd_norm_g"][l][None, :])
    lora_w = jnp.zeros((LANES, 3 * RWKV_W), F32)
    lora_w = (lora_w.at[0:32, 0:RWKV_W].set(p["rwkv_w2"][l])
              .at[32:64, RWKV_W:2 * RWKV_W].set(p["rwkv_a2"][l])
              .at[64:128, 2 * RWKV_W:].set(p["rwkv_g2"][l])).astype(BF16)
    row = lambda name: p[name][l].reshape(1, -1)
    rwkv_p = (row("rwkv_mu"), lora_w, row("rwkv_w0"), row("rwkv_a0"), row("rwkv_k_k"), row("rwkv_k_a"),
              row("rwkv_r_k"), row("rwkv_ln_g"), row("rwkv_ln_b"))
    gk2_pad = jnp.zeros((SMALL_W, GLA_HK), F32).at[GLO_OFF:GLO_OFF + GK_LORA].set(p["gla_w_gk2"][l])
    gla_p = (gk2_pad, row("gla_b_gk"), jnp.tile(p["gla_norm_g"][l], GLA_H)[None, :])
    return w_pad, ssd_p, rwkv_p, gla_p


def _block_diag_inv(s_bd, h):
    b, hr, hc = s_bd.shape
    r, c = hr // h, hc // h
    s = s_bd.reshape(b, h, r, h, c)
    return jnp.stack([s[:, i, :, i, :] for i in range(h)], axis=1)


def _ssd_state_unpack(h_bd):
    b = h_bd.shape[0]
    s = h_bd.reshape(b, SSD_G, SSD_N, SSD_H, SSD_P)
    per_group = SSD_H // SSD_G
    heads = [s[:, h // per_group, :, h, :] for h in range(SSD_H)]
    return jnp.swapaxes(jnp.stack(heads, axis=1), 2, 3)


def _tail(x, mod_l, l, p, y_ssd, y_rwkv, y_gla, tm):
    sh1, sc1, gt1, sh2, sc2, gt2 = mod_l
    row = lambda name: p[name][l].reshape(1, -1)
    w_out = p["w_out"][l].astype(BF16)
    i = l // 2
    if l % 2 == 0:
        x = _outproj_ffn(y_ssd, y_rwkv, y_gla, x, gt1, sc2, sh2, gt2, w_out, row("ln_mix_g"), row("ln_mix_b"),
                         p["ffn_w_gate"][i].astype(BF16), p["ffn_w_up"][i].astype(BF16),
                         p["ffn_w_down"][i].astype(BF16), row("ln_ffn_g"), row("ln_ffn_b"), tm)
    else:
        router_pad = jnp.zeros((D_MODEL, LANES), F32).at[:, :N_EXPERTS].set(p["moe_router"][i])
        x, logits = _outproj(y_ssd, y_rwkv, y_gla, x, gt1, sc2, sh2, w_out, row("ln_mix_g"), row("ln_mix_b"),
                             router_pad, tm)
        x = _moe(x, sc2, sh2, gt2, logits, p["moe_w_gate"][i].astype(BF16), p["moe_w_up"][i].astype(BF16),
                 p["moe_w_down"][i].astype(BF16), row("ln_ffn_g"), row("ln_ffn_b"), min(2 * tm, x.shape[1]))
    return x


def _forward(x_prompt, x_sample, c_prompt, c_sample, states, p):
    bp, seq, _ = x_prompt.shape
    bs = x_sample.shape[0]
    state_ssd, state_conv, state_rwkv, state_shift, state_gla = states
    mod = _ada(jnp.concatenate([c_prompt, c_sample], axis=0), p["w_ada"], p["b_ada"])

    xp = x_prompt
    xs = x_sample.reshape(1, bs, D_MODEL)
    tm_p = min(512, seq)
    outs_p = [[] for _ in range(5)]
    outs_s = [[] for _ in range(5)]
    for l in range(DEPTH):
        w_pad, ssd_p, rwkv_p, gla_p = _layer_params(p, l)
        mods = jnp.split(mod[l], 6, axis=-1)
        mod_p = [m[:bp, None, :] for m in mods]
        mod_s = [m[None, bp:, :] for m in mods]

        ssd_slab, small, rw, gla_slab = _inproj(xp, mod_p[1], mod_p[0], w_pad, tm_p)
        y_ssd, conv_new, h_new = _ssd_prompt(ssd_slab, small, *ssd_p)
        h_new = _ssd_state_unpack(h_new)
        y_rwkv, shift_new, rs_bd = _rwkv_prompt(rw, *rwkv_p)
        y_gla, gs_bd = _gla_prompt(gla_slab, small, *gla_p)
        xp = _tail(xp, mod_p, l, p, y_ssd, y_rwkv, y_gla, tm_p)
        for acc, s in zip(outs_p, (h_new, conv_new, _block_diag_inv(rs_bd, RWKV_H),
                                   shift_new.reshape(bp, RWKV_FEAT), _block_diag_inv(gs_bd, GLA_H))):
            acc.append(s)

        ssd_slab, small, rw, gla_slab = _inproj(xs, mod_s[1], mod_s[0], w_pad, bs)
        y_ssd, y_rwkv, y_gla, conv_new, h_new, rs_new, gs_new = _mixer_step(
            ssd_slab[0], small[0], rw[0], gla_slab[0], state_conv[l].reshape(bs, -1), state_shift[l],
            state_ssd[l], state_rwkv[l], state_gla[l], ssd_p, rwkv_p, gla_p)
        xs = _tail(xs, mod_s, l, p, y_ssd[None], y_rwkv[None], y_gla[None], bs)
        for acc, s in zip(outs_s, (h_new, conv_new, rs_new, rw[0].astype(F32), gs_new)):
            acc.append(s)

    stack = lambda accs: tuple(jnp.stack(a, axis=0) for a in accs)
    return (xp, xs.reshape(bs, 1, D_MODEL)) + stack(outs_p) + stack(outs_s)


def kernel(x_prompt, x_sample, c_prompt, c_sample, state_ssd, state_ssd_conv, state_rwkv, state_rwkv_shift, state_gla, w_ada, b_ada, w_in, w_out, ssd_conv_w, ssd_conv_b, ssd_dt_bias, ssd_a_log, ssd_d, ssd_norm_g, rwkv_mu, rwkv_w0, rwkv_w2, rwkv_a0, rwkv_a2, rwkv_g2, rwkv_k_k, rwkv_k_a, rwkv_r_k, rwkv_ln_g, rwkv_ln_b, gla_w_gk2, gla_b_gk, gla_norm_g, ln_mix_g, ln_mix_b, ln_ffn_g, ln_ffn_b, ffn_w_gate, ffn_w_up, ffn_w_down, moe_router, moe_w_gate, moe_w_up, moe_w_down):
    p = dict(w_ada=w_ada, b_ada=b_ada, w_in=w_in, w_out=w_out, ssd_conv_w=ssd_conv_w, ssd_conv_b=ssd_conv_b,
             ssd_dt_bias=ssd_dt_bias, ssd_a_log=ssd_a_log, ssd_d=ssd_d, ssd_norm_g=ssd_norm_g,
             rwkv_mu=rwkv_mu, rwkv_w0=rwkv_w0, rwkv_w2=rwkv_w2, rwkv_a0=rwkv_a0, rwkv_a2=rwkv_a2,
             rwkv_g2=rwkv_g2, rwkv_k_k=rwkv_k_k, rwkv_k_a=rwkv_k_a, rwkv_r_k=rwkv_r_k,
             rwkv_ln_g=rwkv_ln_g, rwkv_ln_b=rwkv_ln_b, gla_w_gk2=gla_w_gk2, gla_b_gk=gla_b_gk,
             gla_norm_g=gla_norm_g, ln_mix_g=ln_mix_g, ln_mix_b=ln_mix_b, ln_ffn_g=ln_ffn_g,
             ln_ffn_b=ln_ffn_b, ffn_w_gate=ffn_w_gate, ffn_w_up=ffn_w_up, ffn_w_down=ffn_w_down,
             moe_router=moe_router, moe_w_gate=moe_w_gate, moe_w_up=moe_w_up, moe_w_down=moe_w_down)
    states = (state_ssd, state_ssd_conv, state_rwkv, state_rwkv_shift, state_gla)
    return _forward(x_prompt, x_sample, c_prompt, c_sample, states, p)
```

```python
import functools

import jax
import jax.numpy as jnp
from jax import lax
from jax.experimental import pallas as pl
from jax.experimental.pallas import tpu as pltpu

F32 = jnp.float32
BF16 = jnp.bfloat16

D_MODEL = 1024
DEPTH = 2
SSD_W = 512
SSD_H = 8
SSD_P = 64
SSD_N = 64
SSD_G = 2
SSD_CONV_W = 4
SSD_CONV_DIM = 768
RWKV_W = 256
RWKV_H = 4
RWKV_N = 64
RWKV_FEAT = 896
RWKV_GN_EPS = RWKV_N * 1e-5
GLA_W = 256
GLA_H = 4
GLA_DK = 32
GLA_DV = 64
GLA_HK = GLA_H * GLA_DK
GK_LORA = 16
GATE_NORMALIZER = 16.0
F_DENSE = 2816
N_EXPERTS = 8
ALPHA = (2.0 * DEPTH) ** 0.25
LN_EPS = 1e-5
RMS_EPS = 1e-6

LANES = 128
SMALL_W = LANES
DT_OFF = 0
GLO_OFF = 8
SSD_SLAB = SSD_W + SSD_CONV_DIM
GLA_SLAB = 2 * GLA_HK + 2 * GLA_W
IN_PAD = SSD_SLAB + SMALL_W + RWKV_FEAT + GLA_SLAB
SSD_CHUNK = 128
SSD_CARRY = 8
GLA_CHUNK = 128
GLA_SUB = 8
RWKV_CHUNK = 64
SSD_NB, RWKV_NB, GLA_NB = 2, 8, 4
RWKV_SKEW = 1
STAT_TERMS = 1
MOE_CAP_ROWS = 160
STEP_GRID = 32
VMEM_LIMIT = 56 * 1024 * 1024


def _cparams(*sem):
    return pltpu.CompilerParams(dimension_semantics=sem, vmem_limit_bytes=VMEM_LIMIT)


def _mm(a, b):
    return jnp.dot(a.astype(BF16), b.astype(BF16), preferred_element_type=F32)


def _mm_nt(a, b):
    return lax.dot_general(a.astype(BF16), b.astype(BF16), (((1,), (1,)), ((), ())),
                           preferred_element_type=F32)


def _split_bf16(x, terms):
    parts = []
    for _ in range(terms):
        p = x.astype(BF16)
        parts.append(p)
        x = x - p.astype(F32)
    return parts


def _mm_sel(sel, x, terms=3):
    return sum(jnp.dot(sel, p, preferred_element_type=F32) for p in _split_bf16(x, terms))


def _mm_xsel(x, sel, terms=2):
    return sum(jnp.dot(p, sel, preferred_element_type=F32) for p in _split_bf16(x, terms))


def _mm_hi(a, b):
    a_hi, a_lo = _split_bf16(a, 2)
    b_hi, b_lo = _split_bf16(b, 2)
    dot = lambda x, y: jnp.dot(x, y, preferred_element_type=F32)
    return dot(a_hi, b_hi) + (dot(a_hi, b_lo) + dot(a_lo, b_hi))


def _sigmoid(x):
    return 1.0 / (1.0 + jnp.exp(-x))


def _silu(x):
    return x * _sigmoid(x)


def _softplus(x):
    return jnp.maximum(x, 0.0) + jnp.log(1.0 + jnp.exp(-jnp.abs(x)))


def _iota(shape, dim):
    return lax.broadcasted_iota(jnp.int32, shape, dim)


def _layernorm(u, g, b):
    mu = jnp.mean(u, axis=-1, keepdims=True)
    d = u - mu
    var = jnp.mean(d * d, axis=-1, keepdims=True)
    return d * lax.rsqrt(var + LN_EPS) * g + b


def _tri_incl(n):
    return (jnp.arange(n)[None, :] <= jnp.arange(n)[:, None]).astype(BF16)


def _seg_ones(n, seg, scale=1.0):
    idx = jnp.arange(n) // seg
    return jnp.where(idx[:, None] == idx[None, :], scale, 0.0).astype(BF16)


def _interleave(chains, skew=0):
    chains = list(chains)
    done = [False] * len(chains)
    rnd = 0
    while not all(done):
        for i, ch in enumerate(chains):
            if done[i] or rnd < i * skew:
                continue
            try:
                next(ch)
            except StopIteration:
                done[i] = True
        rnd += 1


def _const_spec(arr):
    nd = arr.ndim
    return pl.BlockSpec(arr.shape, lambda *_: (0,) * nd)


def _ada_kernel(c_ref, w_ref, b_ref, o_ref):
    o_ref[0] = _mm(_silu(c_ref[...]), w_ref[0]) + b_ref[0]


def _ada(c_all, w_ada, b_ada):
    rows = c_all.shape[0]
    tn = 1536
    return pl.pallas_call(
        _ada_kernel,
        grid=(DEPTH, 6 * D_MODEL // tn),
        in_specs=[pl.BlockSpec((rows, D_MODEL), lambda l, j: (0, 0)),
                  pl.BlockSpec((1, D_MODEL, tn), lambda l, j: (l, 0, j)),
                  pl.BlockSpec((1, 1, tn), lambda l, j: (l, 0, j))],
        out_specs=pl.BlockSpec((1, rows, tn), lambda l, j: (l, 0, j)),
        out_shape=jax.ShapeDtypeStruct((DEPTH, rows, 6 * D_MODEL), F32),
        compiler_params=_cparams("arbitrary", "arbitrary"),
        name="ada",
    )(c_all, w_ada, b_ada.reshape(DEPTH, 1, 6 * D_MODEL))


def _inproj_kernel(x_ref, sc_ref, sh_ref, w_ref, o_ssd, o_small, o_rwkv, o_gla):
    h = (x_ref[0] * (1.0 + sc_ref[0]) + sh_ref[0]).astype(BF16)
    off = 0
    for o_ref in (o_ssd, o_small, o_rwkv, o_gla):
        width = o_ref.shape[2]
        o_ref[0] = jnp.dot(h, w_ref[:, off:off + width], preferred_element_type=F32).astype(o_ref.dtype)
        off += width


def _mod_spec(mod, tm):
    if mod.shape[1] == 1:
        return pl.BlockSpec((1, 1, D_MODEL), lambda b, i, *_: (b, 0, 0))
    return pl.BlockSpec((1, tm, D_MODEL), lambda b, i, *_: (b, i, 0))


def _inproj(x, sc, sh, w_pad, tm):
    bsz, length, _ = x.shape
    tok = lambda width: pl.BlockSpec((1, tm, width), lambda b, i: (b, i, 0))
    return pl.pallas_call(
        _inproj_kernel,
        grid=(bsz, length // tm),
        in_specs=[tok(D_MODEL), _mod_spec(sc, tm), _mod_spec(sh, tm),
                  pl.BlockSpec((D_MODEL, IN_PAD), lambda b, i: (0, 0))],
        out_specs=[tok(SSD_SLAB), tok(SMALL_W), tok(RWKV_FEAT), tok(GLA_SLAB)],
        out_shape=[jax.ShapeDtypeStruct((bsz, length, SSD_SLAB), BF16),
                   jax.ShapeDtypeStruct((bsz, length, SMALL_W), F32),
                   jax.ShapeDtypeStruct((bsz, length, RWKV_FEAT), BF16),
                   jax.ShapeDtypeStruct((bsz, length, GLA_SLAB), BF16)],
        compiler_params=_cparams("arbitrary", "arbitrary"),
        name="inproj",
    )(x, sc, sh, w_pad)


def _ssd_token_math(xbc, small, conv_taps, cw, cb, dtb, alog):
    conv = cb + xbc * cw[3:4, :]
    for i in range(SSD_CONV_W - 1):
        conv = conv + conv_taps[i] * cw[i:i + 1, :]
    dt = _softplus(small + dtb)
    return _silu(conv), dt, dt * (-jnp.exp(alog))


def _ssd_finish(y, xs, z, dexp, ng):
    gated = (y + xs * dexp) * _silu(z)
    ms = jnp.mean(gated * gated, axis=-1, keepdims=True)
    return gated * lax.rsqrt(ms + RMS_EPS) * ng


def _ssd_kernel(slab_ref, small_ref, cw_ref, cb_ref, dtb_ref, alog_ref, dexp_ref, ng_ref, tri_ref,
                head_x_ref, group_mask_ref, pair_mask_ref, y_ref, conv_out_ref, h_out_ref, ext_ref, h_ref):
    c = pl.program_id(1)
    n_chunks = pl.num_programs(1)
    ch = SSD_CHUNK
    pad = ext_ref.shape[1] - ch
    causal = _iota((ch, ch), 1) <= _iota((ch, ch), 0)

    nb = slab_ref.shape[0]

    @pl.when(c == 0)
    def _():
        ext_ref[:, 0:pad, :] = jnp.zeros((nb, pad, SSD_CONV_DIM), F32)
        h_ref[...] = jnp.zeros_like(h_ref)

    def chain(j):
        z = slab_ref[j, :, 0:SSD_W].astype(F32)
        xbc = slab_ref[j, :, SSD_W:].astype(F32)
        small = small_ref[j]

        ext_ref[j, pad:pad + ch, :] = xbc
        first = pad - (SSD_CONV_W - 1)
        taps = [ext_ref[j, first + i:first + i + ch, :] for i in range(SSD_CONV_W - 1)]
        act, dt, dta = _ssd_token_math(xbc, small, taps, cw_ref[...], cb_ref[...], dtb_ref[...], alog_ref[...])
        ext_ref[j, 0:pad, :] = ext_ref[j, ch:ch + pad, :]

        xs = act[:, 0:SSD_W]
        bs = act[:, SSD_W:SSD_W + SSD_G * SSD_N]
        cs = act[:, SSD_W + SSD_G * SSD_N:]
        yield
        cum = _mm_sel(tri_ref[...], dta)
        cum_t = cum.T
        dt_t = dt.T
        cum_x = _mm_xsel(cum, head_x_ref[...], 3)
        dt_x = _mm_xsel(dt, head_x_ref[...], 2)
        yield
        last_x = cum_x[ch - 1:ch, :]
        xs_tail = (xs * (jnp.exp(last_x - cum_x) * dt_x)).astype(BF16)
        h_bd = h_ref[j]
        y_state = jnp.dot(cs.astype(BF16), h_bd.astype(BF16), preferred_element_type=F32) * jnp.exp(cum_x)
        upd = jnp.dot(bs.T.astype(BF16), xs_tail, preferred_element_type=F32)
        h_ref[j] = h_bd * jnp.exp(last_x) + upd * group_mask_ref[...]
        yield

        lane_group = _iota((1, SSD_G * SSD_N), 1) // SSD_N
        xs_b = xs.astype(BF16)
        ys = []
        for g in range(SSD_G):
            cb = _mm_nt(jnp.where(lane_group == g, cs, 0.0), bs)
            yield
            for pair in range(SSD_H // SSD_G // 2):
                scores = []
                for hh in range(2):
                    lane = DT_OFF + g * (SSD_H // SSD_G) + 2 * pair + hh
                    decay = jnp.exp(jnp.where(causal, cum[:, lane:lane + 1] - cum_t[lane:lane + 1, :], -jnp.inf))
                    scores.append((cb * decay * dt_t[lane:lane + 1, :]).astype(BF16))
                blk = g * (SSD_H // SSD_G) // 2 + pair
                x_pair = xs_b[:, blk * LANES:(blk + 1) * LANES]
                x_bd = jnp.concatenate([x_pair, x_pair], axis=0) * pair_mask_ref[...]
                ys.append(jnp.dot(jnp.concatenate(scores, axis=1), x_bd, preferred_element_type=F32))
                yield

        y = jnp.concatenate(ys, axis=-1) + y_state
        y_ref[j] = _ssd_finish(y, xs, z, dexp_ref[...], ng_ref[...]).astype(y_ref.dtype)

    _interleave([chain(j) for j in range(nb)])

    @pl.when(c == n_chunks - 1)
    def _():
        for j in range(nb):
            conv_out_ref[j] = ext_ref[j, pad - (SSD_CONV_W - 1):pad, :]
            h_out_ref[j] = h_ref[j]


def _ssd_prompt(slab, small, cw, cb, dtb, alog, dexp, ng):
    bsz, length, _ = slab.shape
    nb = SSD_NB if bsz % SSD_NB == 0 else 1
    gn, hp = SSD_G * SSD_N, SSD_H * SSD_P
    lane_head = jnp.arange(hp) // SSD_P
    head_x = (jnp.arange(SMALL_W)[:, None] == DT_OFF + lane_head[None, :]).astype(BF16)
    group_mask = (jnp.arange(gn)[:, None] // SSD_N == lane_head[None, :] // (SSD_H // SSD_G)).astype(F32)
    pair_mask = (jnp.arange(2 * SSD_CHUNK)[:, None] // SSD_CHUNK == jnp.arange(2 * SSD_P)[None, :] // SSD_P
                 ).astype(BF16)
    consts = (cw, cb, dtb, alog, dexp, ng, _tri_incl(SSD_CHUNK), head_x, group_mask, pair_mask)
    return pl.pallas_call(
        _ssd_kernel,
        grid=(bsz // nb, length // SSD_CHUNK),
        in_specs=[pl.BlockSpec((nb, SSD_CHUNK, SSD_SLAB), lambda b, c: (b, c, 0)),
                  pl.BlockSpec((nb, SSD_CHUNK, SMALL_W), lambda b, c: (b, c, 0))]
                 + [_const_spec(a) for a in consts],
        out_specs=[pl.BlockSpec((nb, SSD_CHUNK, SSD_W), lambda b, c: (b, c, 0)),
                   pl.BlockSpec((nb, SSD_CONV_W - 1, SSD_CONV_DIM), lambda b, c: (b, 0, 0)),
                   pl.BlockSpec((nb, gn, hp), lambda b, c: (b, 0, 0))],
        out_shape=[jax.ShapeDtypeStruct((bsz, length, SSD_W), BF16),
                   jax.ShapeDtypeStruct((bsz, SSD_CONV_W - 1, SSD_CONV_DIM), F32),
                   jax.ShapeDtypeStruct((bsz, gn, hp), F32)],
        scratch_shapes=[pltpu.VMEM((nb, SSD_CARRY + SSD_CHUNK, SSD_CONV_DIM), F32),
                        pltpu.VMEM((nb, gn, hp), F32)],
        compiler_params=_cparams("arbitrary", "arbitrary"),
        name="ssd_scan",
    )(slab, small, *consts)


def _rwkv_token_math(rw, prev, mu, lora_w, w0, a0, k_k, k_a, head_ones):
    mix = rw + mu * (prev - rw)
    r = mix[:, 0:RWKV_W]
    k = mix[:, RWKV_W:2 * RWKV_W]
    v = mix[:, 2 * RWKV_W:3 * RWKV_W]
    lora = mix[:, 3 * RWKV_W:]
    lane = _iota(lora.shape, 1)
    act = jnp.where(lane < 32, jnp.tanh(lora), jnp.where(lane < 64, lora, _sigmoid(lora)))
    lo = _mm(act, lora_w)
    w_log = -_softplus(-(w0 + lo[:, 0:RWKV_W])) - 0.5
    logw = -jnp.exp(w_log)
    aic = _sigmoid(a0 + lo[:, RWKV_W:2 * RWKV_W])
    gate = lo[:, 2 * RWKV_W:]
    kkf = k * k_k
    kk = kkf * lax.rsqrt(_mm_xsel(kkf * kkf, head_ones, STAT_TERMS) + 1e-12)
    k2 = k * (1.0 + (aic - 1.0) * k_a)
    return r, k2, v, logw, aic, gate, kk


def _rwkv_finish(o, r, k2, v, gate, r_k, ln_g, ln_b, head_ones, head_avg):
    mu = _mm_xsel(o, head_avg, STAT_TERMS)
    d = o - mu
    var = _mm_xsel(d * d, head_avg, STAT_TERMS)
    on = d * lax.rsqrt(var + RWKV_GN_EPS) * ln_g + ln_b
    bonus = _mm_xsel(r * k2 * r_k, head_ones, STAT_TERMS) * v
    return (on + bonus) * gate


def _rwkv_kernel(rw_ref, mu_ref, lw_ref, w0_ref, a0_ref, kk_ref, ka_ref,
                 rk_ref, lng_ref, lnb_ref, ones_ref, avg_ref, tri_ref, strict_ref, incl_ref, eye_ref,
                 y_ref, shift_out_ref, s_out_ref, ext_ref, s_ref):
    c = pl.program_id(1)
    n_chunks = pl.num_programs(1)
    ch = RWKV_CHUNK
    nh = RWKV_H
    w = RWKV_W
    head_bd = ones_ref[...]
    head_bd_f = head_bd.astype(F32)
    tile4 = lambda x: jnp.concatenate([x] * nh, axis=0)

    nb = rw_ref.shape[0]

    @pl.when(c == 0)
    def _():
        ext_ref[:, 0:8, :] = jnp.zeros((nb, 8, RWKV_FEAT), F32)
        s_ref[...] = jnp.zeros_like(s_ref)

    def chain(j):
        rw = rw_ref[j].astype(F32)
        ext_ref[j, 8:8 + ch, :] = rw
        prev = ext_ref[j, 7:7 + ch, :]
        ext_ref[j, 7:8, :] = rw[ch - 1:ch, :]

        r, k2, v, logw, aic, gate, kk = _rwkv_token_math(
            rw, prev, mu_ref[...], lw_ref[...], w0_ref[...], a0_ref[...], kk_ref[...], ka_ref[...], head_bd)
        yield

        cumw = _mm_sel(tri_ref[...], logw)
        yield
        last = cumw[ch - 1:ch, :]
        inv_g = jnp.exp(-cumw)
        to_end = jnp.exp(last - cumw)
        kka = kk * aic
        a_t = (-kk * jnp.exp(cumw - logw)).astype(BF16)
        r_t = (r * jnp.exp(cumw)).astype(BF16)
        b_t = (kka * inv_g).astype(BF16)
        k_t = (k2 * inv_g).astype(BF16)
        bk_end = jnp.concatenate([kka * to_end, k2 * to_end], axis=0).astype(BF16)

        bd = lambda x: tile4(x.astype(BF16)) * head_bd
        nt = lambda x, y: lax.dot_general(x, y, (((1,), (1,)), ((), ())), preferred_element_type=F32)
        vb = v.astype(BF16)
        ar = jnp.concatenate([a_t, r_t], axis=0)
        prod = nt(ar, jnp.concatenate([bd(b_t), bd(k_t)], axis=0))
        a_ab = prod[0:ch, 0:w] * strict_ref[...]
        a_ak = prod[0:ch, w:2 * w] * strict_ref[...]
        a_rbk = (prod[ch:2 * ch, :] * incl_ref[...]).astype(BF16)
        yield
        s_bd = s_ref[j]
        uo = nt(ar, s_bd.astype(BF16))
        rhs = uo[0:ch, :] + jnp.dot(a_ak.astype(BF16), bd(vb), preferred_element_type=F32)
        yield

        t_inv = eye_ref[...] + a_ab
        x = a_ab
        x_bd = bd(x)
        power = 1
        while 2 * power < ch:
            x = jnp.dot(x.astype(BF16), x_bd, preferred_element_type=F32)
            x_bd = bd(x)
            yield
            t_inv = t_inv + jnp.dot(t_inv.astype(BF16), x_bd, preferred_element_type=F32)
            yield
            power *= 2

        p = jnp.dot(t_inv.astype(BF16), bd(rhs), preferred_element_type=F32)
        yield
        pb = p.astype(BF16)
        o = uo[ch:2 * ch, :] + jnp.dot(a_rbk, jnp.concatenate([bd(pb), bd(vb)], axis=0),
                                       preferred_element_type=F32)
        pv = jnp.concatenate([pb, vb], axis=0)
        yield

        upd = jnp.dot(pv.astype(F32).T.astype(BF16), bk_end, preferred_element_type=F32)
        s_ref[j] = s_bd * jnp.exp(last) + upd * head_bd_f
        yield

        y_ref[j] = _rwkv_finish(o, r, k2, v, gate, rk_ref[...], lng_ref[...], lnb_ref[...],
                                head_bd, avg_ref[...]).astype(y_ref.dtype)

    _interleave([chain(j) for j in range(nb)], skew=RWKV_SKEW)

    @pl.when(c == n_chunks - 1)
    def _():
        for j in range(nb):
            shift_out_ref[j] = ext_ref[j, 7:8, :]
            s_out_ref[j] = s_ref[j]


def _rwkv_consts():
    ch = RWKV_CHUNK
    assert ch == RWKV_N, "bd() reuses the head mask, which needs chunk rows == head width"
    t = jnp.arange(ch)[:, None]
    j = jnp.arange(2 * RWKV_W)[None, :] % ch
    strict = (j[:, :RWKV_W] < t).astype(F32)
    incl = (j <= t).astype(F32)
    eye = (j[:, :RWKV_W] == t).astype(F32)
    return (_seg_ones(RWKV_W, RWKV_N), _seg_ones(RWKV_W, RWKV_N, 1.0 / RWKV_N), _tri_incl(ch),
            strict, incl, eye)


def _rwkv_prompt(rw, *params):
    bsz, length, _ = rw.shape
    nb = RWKV_NB if bsz % RWKV_NB == 0 else 1
    consts = tuple(params) + _rwkv_consts()
    return pl.pallas_call(
        _rwkv_kernel,
        grid=(bsz // nb, length // RWKV_CHUNK),
        in_specs=[pl.BlockSpec((nb, RWKV_CHUNK, RWKV_FEAT), lambda b, c: (b, c, 0))]
                 + [_const_spec(a) for a in consts],
        out_specs=[pl.BlockSpec((nb, RWKV_CHUNK, RWKV_W), lambda b, c: (b, c, 0)),
                   pl.BlockSpec((nb, 1, RWKV_FEAT), lambda b, c: (b, 0, 0)),
                   pl.BlockSpec((nb, RWKV_W, RWKV_W), lambda b, c: (b, 0, 0))],
        out_shape=[jax.ShapeDtypeStruct((bsz, length, RWKV_W), BF16),
                   jax.ShapeDtypeStruct((bsz, 1, RWKV_FEAT), F32),
                   jax.ShapeDtypeStruct((bsz, RWKV_W, RWKV_W), F32)],
        scratch_shapes=[pltpu.VMEM((nb, RWKV_CHUNK + 8, RWKV_FEAT), F32),
                        pltpu.VMEM((nb, RWKV_W, RWKV_W), F32)],
        compiler_params=_cparams("arbitrary", "arbitrary"),
        name="rwkv_scan",
    )(rw, *consts)


def _gla_gate_log(small, gk2_pad, b_gk):
    x = _mm_hi(small, gk2_pad) + b_gk
    return -_softplus(-x) / GATE_NORMALIZER


def _gla_finish(o, gg, norm_g, head_avg):
    ms = _mm_xsel(o * o, head_avg, STAT_TERMS)
    return o * lax.rsqrt(ms + RMS_EPS) * norm_g * _silu(gg)


def _gla_kernel(slab_ref, small_ref, gk2_ref, bgk_ref, ng_ref, avg_ref, tri_ref, expand_ref,
                y_ref, s_out_ref, s_ref):
    c = pl.program_id(1)
    n_chunks = pl.num_programs(1)
    ch = GLA_CHUNK
    sub = GLA_SUB
    hk = GLA_HK
    expand = expand_ref[...]
    expand_f = expand.astype(F32)
    pos = _iota((ch, 1), 0) % sub
    t_idx = _iota((ch, 1), 0)
    hmask_k = (_iota((GLA_H * sub, hk), 0) // sub == _iota((GLA_H * sub, hk), 1) // GLA_DK).astype(F32)
    hmask_v = (_iota((GLA_H * sub, GLA_W), 0) // sub == _iota((GLA_H * sub, GLA_W), 1) // GLA_DV).astype(F32)

    nb = slab_ref.shape[0]

    @pl.when(c == 0)
    def _():
        s_ref[...] = jnp.zeros_like(s_ref)

    def chain(j):
        q = slab_ref[j, :, 0:hk].astype(F32) * (GLA_DK ** -0.5)
        k = slab_ref[j, :, hk:2 * hk].astype(F32)
        v = slab_ref[j, :, 2 * hk:2 * hk + GLA_W].astype(F32)
        gg = slab_ref[j, :, 2 * hk + GLA_W:].astype(F32)
        small = small_ref[j]

        lg = _gla_gate_log(small, gk2_ref[...], bgk_ref[...])
        yield
        cum = _mm_sel(tri_ref[...], lg)
        yield
        last = cum[ch - 1:ch, :]
        s_bd = s_ref[j]

        o = _mm(q * jnp.exp(cum), s_bd)

        group = 4
        for d0 in range(0, sub, group):
            prods, v_shift = [], []
            for delta in range(d0, d0 + group):
                if delta == 0:
                    k_s, c_s, v_s = k, cum, v
                else:
                    k_s = pltpu.roll(k, delta, 0)
                    c_s = pltpu.roll(cum, delta, 0)
                    v_s = pltpu.roll(v, delta, 0)
                valid = pos >= delta
                w_pair = jnp.exp(jnp.where(valid, cum - c_s, 0.0))
                prods.append(jnp.where(valid, q * k_s * w_pair, 0.0).astype(BF16))
                v_shift.append(v_s)
            yield
            att = jnp.dot(jnp.concatenate(prods, axis=0), expand, preferred_element_type=F32)
            o = o + sum(att[g * ch:(g + 1) * ch, :] * v_shift[g] for g in range(group))
            yield

        vb = v.astype(BF16)
        atts = []
        for blk in range(1, ch // sub):
            lo = blk * sub
            ref_pt = cum[lo - 1:lo, :]
            qd = q[lo:lo + sub, :] * jnp.exp(cum[lo:lo + sub, :] - ref_pt)
            kp = k * jnp.exp(jnp.where(t_idx < lo, ref_pt - cum, -jnp.inf))
            q4 = jnp.concatenate([qd] * GLA_H, axis=0) * hmask_k
            atts.append(_mm_nt(q4, kp).astype(BF16))
            yield
        o4 = jnp.dot(jnp.concatenate(atts, axis=0), vb, preferred_element_type=F32)
        yield
        cross = [jnp.zeros((sub, GLA_W), F32)]
        for blk in range(ch // sub - 1):
            o4_b = o4[blk * GLA_H * sub:(blk + 1) * GLA_H * sub, :] * hmask_v
            cross.append(sum(o4_b[h * sub:(h + 1) * sub, :] for h in range(GLA_H)))
        o = o + jnp.concatenate(cross, axis=0)

        kt_t = (k * jnp.exp(last - cum)).T
        s_ref[j] = s_bd * _col(jnp.exp(last)) + _mm(kt_t, vb) * expand_f
        yield

        y_ref[j] = _gla_finish(o, gg, ng_ref[...], avg_ref[...]).astype(y_ref.dtype)

    _interleave([chain(j) for j in range(nb)], skew=2)

    @pl.when(c == n_chunks - 1)
    def _():
        s_out_ref[...] = s_ref[...]


def _gla_prompt(slab, small, gk2_pad, b_gk, ng):
    bsz, length, _ = slab.shape
    nb = GLA_NB if bsz % GLA_NB == 0 else 1
    expand = (jnp.arange(GLA_HK)[:, None] // GLA_DK == jnp.arange(GLA_W)[None, :] // GLA_DV).astype(BF16)
    consts = (gk2_pad, b_gk, ng, _seg_ones(GLA_W, GLA_DV, 1.0 / GLA_DV), _tri_incl(GLA_CHUNK), expand)
    return pl.pallas_call(
        _gla_kernel,
        grid=(bsz // nb, length // GLA_CHUNK),
        in_specs=[pl.BlockSpec((nb, GLA_CHUNK, GLA_SLAB), lambda b, c: (b, c, 0)),
                  pl.BlockSpec((nb, GLA_CHUNK, SMALL_W), lambda b, c: (b, c, 0))]
                 + [_const_spec(a) for a in consts],
        out_specs=[pl.BlockSpec((nb, GLA_CHUNK, GLA_W), lambda b, c: (b, c, 0)),
                   pl.BlockSpec((nb, GLA_HK, GLA_W), lambda b, c: (b, 0, 0))],
        out_shape=[jax.ShapeDtypeStruct((bsz, length, GLA_W), BF16),
                   jax.ShapeDtypeStruct((bsz, GLA_HK, GLA_W), F32)],
        scratch_shapes=[pltpu.VMEM((nb, GLA_HK, GLA_W), F32)],
        compiler_params=_cparams("arbitrary", "arbitrary"),
        name="gla_scan",
    )(slab, small, *consts)


def _col(row):
    n = row.shape[1]
    eye = _iota((n, n), 0) == _iota((n, n), 1)
    return jnp.sum(jnp.where(eye, row, 0.0), axis=1, keepdims=True)


def _store_t(dst_ref, row0, x):
    for cblk in range(x.shape[1] // LANES):
        dst_ref[row0 + cblk * LANES:row0 + (cblk + 1) * LANES, :] = x[:, cblk * LANES:(cblk + 1) * LANES].T


def _load_t(src_ref, width):
    return jnp.concatenate([src_ref[cblk * LANES:(cblk + 1) * LANES, :].T for cblk in range(width // LANES)],
                           axis=1)


_TS_X, _TS_B, _TS_C, _TS_DT, _TS_DA = 0, SSD_W, SSD_W + SSD_G * SSD_N, SSD_CONV_DIM, SSD_CONV_DIM + SMALL_W
_TR_R, _TR_W, _TR_K, _TR_V, _TR_KK, _TR_KA = (i * RWKV_W for i in range(6))
_TG_Q, _TG_K, _TG_G, _TG_V = 0, GLA_HK, 2 * GLA_HK, 3 * GLA_HK


def _step_kernel(ssd_ref, small_ref, rw_ref, gla_ref, conv0_ref, shift0_ref, h0_ref, rs0_ref, gs0_ref,
                 cw_ref, cb_ref, dtb_ref, alog_ref, dexp_ref, sng_ref,
                 mu_ref, lw_ref, w0_ref, a0_ref, kk_ref, ka_ref, rk_ref, lng_ref, lnb_ref,
                 gk2_ref, bgk_ref, gng_ref, ones_ref, ravg_ref, gavg_ref,
                 y_ssd_ref, y_rwkv_ref, y_gla_ref, conv_out_ref, h_out_ref, rs_out_ref, gs_out_ref,
                 t_ssd, t_rwkv, t_gla, o_ssd_t, o_rwkv_t, o_gla_t):
    i = pl.program_id(0)
    hk = GLA_HK

    def ssd_tokens():
        xbc = ssd_ref[:, SSD_W:].astype(F32)
        taps = [conv0_ref[:, t * SSD_CONV_DIM:(t + 1) * SSD_CONV_DIM] for t in range(SSD_CONV_W - 1)]
        return xbc, _ssd_token_math(xbc, small_ref[...], taps, cw_ref[...], cb_ref[...],
                                    dtb_ref[...], alog_ref[...])

    def rwkv_tokens():
        return _rwkv_token_math(rw_ref[...].astype(F32), shift0_ref[...], mu_ref[...], lw_ref[...],
                                w0_ref[...], a0_ref[...], kk_ref[...], ka_ref[...], ones_ref[...])

    @pl.when(i == 0)
    def _():
        xbc, (act, dt, dta) = ssd_tokens()
        conv_out_ref[:, 0:2 * SSD_CONV_DIM] = conv0_ref[:, SSD_CONV_DIM:]
        conv_out_ref[:, 2 * SSD_CONV_DIM:] = xbc
        _store_t(t_ssd, _TS_X, act)
        _store_t(t_ssd, _TS_DT, dt)
        _store_t(t_ssd, _TS_DA, jnp.exp(dta))
        r, k2, v, logw, aic, gate, kk = rwkv_tokens()
        for off, x in ((_TR_R, r), (_TR_W, jnp.exp(logw)), (_TR_K, k2), (_TR_V, v), (_TR_KK, kk),
                       (_TR_KA, kk * aic)):
            _store_t(t_rwkv, off, x)
        lg = _gla_gate_log(small_ref[...], gk2_ref[...], bgk_ref[...])
        _store_t(t_gla, _TG_Q, gla_ref[:, 0:hk].astype(F32) * (GLA_DK ** -0.5))
        _store_t(t_gla, _TG_K, gla_ref[:, hk:2 * hk].astype(F32))
        _store_t(t_gla, _TG_G, jnp.exp(lg))
        _store_t(t_gla, _TG_V, gla_ref[:, 2 * hk:2 * hk + GLA_W].astype(F32))
        o_gla_t[...] = jnp.zeros_like(o_gla_t)

    def row(ref, idx):
        return ref[pl.ds(idx, 1), :]

    def rows(ref, idx, count):
        return ref[pl.ds(pl.multiple_of(idx, count), count), :]

    units = h0_ref.shape[0] // SSD_N
    per_head = SSD_P // units
    h = i // per_head
    p0 = (i % per_head) * units
    b_t = rows(t_ssd, _TS_B + (h // (SSD_H // SSD_G)) * SSD_N, SSD_N)
    c_t = rows(t_ssd, _TS_C + (h // (SSD_H // SSD_G)) * SSD_N, SSD_N)
    da_row = row(t_ssd, _TS_DA + DT_OFF + h)
    dt_row = row(t_ssd, _TS_DT + DT_OFF + h)
    for u in range(units):
        p_idx = h * SSD_P + p0 + u
        blk = slice(u * SSD_N, (u + 1) * SSD_N)
        h_new = h0_ref[blk, :] * da_row + (row(t_ssd, _TS_X + p_idx) * dt_row) * b_t
        o_ssd_t[pl.ds(p_idx, 1), :] = jnp.sum(h_new * c_t, axis=0, keepdims=True)
        h_out_ref[blk, :] = h_new

    units = rs0_ref.shape[0] // RWKV_N
    per_head = RWKV_N // units
    h = i // per_head
    v0 = (i % per_head) * units
    seg = lambda off: rows(t_rwkv, off + h * RWKV_N, RWKV_N)
    r_t, w_t, k_t, kk_t, ka_t = seg(_TR_R), seg(_TR_W), seg(_TR_K), seg(_TR_KK), seg(_TR_KA)
    for u in range(units):
        v_idx = h * RWKV_N + v0 + u
        blk = slice(u * RWKV_N, (u + 1) * RWKV_N)
        s = rs0_ref[blk, :]
        sa = jnp.sum(s * (-kk_t), axis=0, keepdims=True)
        s_new = s * w_t + sa * ka_t + row(t_rwkv, _TR_V + v_idx) * k_t
        o_rwkv_t[pl.ds(v_idx, 1), :] = jnp.sum(s_new * r_t, axis=0, keepdims=True)
        rs_out_ref[blk, :] = s_new

    units = gs0_ref.shape[0] // GLA_DV
    per_head = GLA_DK // units
    h = i // per_head
    k0 = (i % per_head) * units
    v_t = rows(t_gla, _TG_V + h * GLA_DV, GLA_DV)
    acc = jnp.zeros((GLA_DV, LANES), F32)
    for u in range(units):
        k_idx = h * GLA_DK + k0 + u
        blk = slice(u * GLA_DV, (u + 1) * GLA_DV)
        s_new = gs0_ref[blk, :] * row(t_gla, _TG_G + k_idx) + row(t_gla, _TG_K + k_idx) * v_t
        acc = acc + row(t_gla, _TG_Q + k_idx) * s_new
        gs_out_ref[blk, :] = s_new
    o_rows = pl.ds(pl.multiple_of(h * GLA_DV, GLA_DV), GLA_DV)
    o_gla_t[o_rows, :] = o_gla_t[o_rows, :] + acc

    @pl.when(i == pl.num_programs(0) - 1)
    def _():
        _, (act, _, _) = ssd_tokens()
        y_ssd_ref[...] = _ssd_finish(_load_t(o_ssd_t, SSD_W), act[:, 0:SSD_W], ssd_ref[:, 0:SSD_W].astype(F32),
                                     dexp_ref[...], sng_ref[...]).astype(y_ssd_ref.dtype)
        r, k2, v, _, _, gate, _ = rwkv_tokens()
        y_rwkv_ref[...] = _rwkv_finish(_load_t(o_rwkv_t, RWKV_W), r, k2, v, gate, rk_ref[...], lng_ref[...],
                                       lnb_ref[...], ones_ref[...], ravg_ref[...]).astype(y_rwkv_ref.dtype)
        y_gla_ref[...] = _gla_finish(_load_t(o_gla_t, GLA_W), gla_ref[:, 2 * hk + GLA_W:].astype(F32),
                                     gng_ref[...], gavg_ref[...]).astype(y_gla_ref.dtype)


def _mixer_step(ssd_slab, small, rw, gla_slab, conv0, shift0, h0, rs0, gs0, ssd_p, rwkv_p, gla_p):
    bsz = ssd_slab.shape[0]
    assert bsz == LANES, "the single-token mixer keeps exactly one vreg row of batch entries on the lanes"
    flat_t = lambda s: s.reshape(bsz, -1).T
    h0, rs0, gs0 = flat_t(h0), flat_t(rs0), flat_t(gs0)
    consts = (tuple(ssd_p) + tuple(rwkv_p) + tuple(gla_p)
              + (_seg_ones(RWKV_W, RWKV_N), _seg_ones(RWKV_W, RWKV_N, 1.0 / RWKV_N),
                 _seg_ones(GLA_W, GLA_DV, 1.0 / GLA_DV)))
    full = lambda a: pl.BlockSpec(a.shape, lambda i: (0,) * a.ndim)
    cols = lambda a: pl.BlockSpec((a.shape[0] // STEP_GRID, bsz), lambda i: (i, 0))
    tokens = (ssd_slab, small, rw, gla_slab, conv0, shift0)
    outs = pl.pallas_call(
        _step_kernel,
        grid=(STEP_GRID,),
        in_specs=[full(a) for a in tokens] + [cols(h0), cols(rs0), cols(gs0)] + [full(a) for a in consts],
        out_specs=[full(jax.ShapeDtypeStruct((bsz, w), BF16)) for w in (SSD_W, RWKV_W, GLA_W)]
                  + [full(conv0), cols(h0), cols(rs0), cols(gs0)],
        out_shape=[jax.ShapeDtypeStruct((bsz, SSD_W), BF16),
                   jax.ShapeDtypeStruct((bsz, RWKV_W), BF16),
                   jax.ShapeDtypeStruct((bsz, GLA_W), BF16),
                   jax.ShapeDtypeStruct(conv0.shape, F32),
                   jax.ShapeDtypeStruct(h0.shape, F32),
                   jax.ShapeDtypeStruct(rs0.shape, F32),
                   jax.ShapeDtypeStruct(gs0.shape, F32)],
        scratch_shapes=[pltpu.VMEM((SSD_CONV_DIM + 2 * SMALL_W, LANES), F32),
                        pltpu.VMEM((6 * RWKV_W, LANES), F32),
                        pltpu.VMEM((3 * GLA_HK + GLA_W, LANES), F32),
                        pltpu.VMEM((SSD_W, LANES), F32),
                        pltpu.VMEM((RWKV_W, LANES), F32),
                        pltpu.VMEM((GLA_W, LANES), F32)],
        compiler_params=_cparams("arbitrary"),
        name="mixer_step",
    )(*tokens, h0, rs0, gs0, *consts)
    y_ssd, y_rwkv, y_gla, conv_new, h_new, rs_new, gs_new = outs
    return (y_ssd, y_rwkv, y_gla, conv_new.reshape(bsz, SSD_CONV_W - 1, SSD_CONV_DIM),
            h_new.T.reshape(bsz, SSD_H, SSD_P, SSD_N), rs_new.T.reshape(bsz, RWKV_H, RWKV_N, RWKV_N),
            gs_new.T.reshape(bsz, GLA_H, GLA_DK, GLA_DV))


def _mod_rows(mod_ref, rows):
    return mod_ref[0] if mod_ref.shape[1] == 1 else mod_ref[0, rows, :]


def _row_parts(tm):
    return 2 if tm % 256 == 0 else 1


def _outproj_kernel(ys_ref, yr_ref, yg_ref, x_ref, gt_ref, sc_ref, sh_ref, w_ref, g_ref, b_ref, rt_ref,
                    o_ref, logit_ref):
    tm = x_ref.shape[1]
    parts = _row_parts(tm)

    def chain(r):
        rows = slice(r * tm // parts, (r + 1) * tm // parts)
        m = (jnp.dot(ys_ref[0, rows, :], w_ref[0:SSD_W, :], preferred_element_type=F32)
             + jnp.dot(yr_ref[0, rows, :], w_ref[SSD_W:SSD_W + RWKV_W, :], preferred_element_type=F32)
             + jnp.dot(yg_ref[0, rows, :], w_ref[SSD_W + RWKV_W:, :], preferred_element_type=F32))
        yield
        u = ALPHA * x_ref[0, rows, :] + (1.0 + _mod_rows(gt_ref, rows)) * m
        x1 = _layernorm(u, g_ref[...], b_ref[...])
        o_ref[0, rows, :] = x1
        h = x1 * (1.0 + _mod_rows(sc_ref, rows)) + _mod_rows(sh_ref, rows)
        logit_ref[0, rows, :] = _mm_hi(h, rt_ref[...])

    _interleave([chain(r) for r in range(parts)], skew=1)


def _outproj(y_ssd, y_rwkv, y_gla, x, gt, sc2, sh2, w_out, ln_g, ln_b, router_pad, tm):
    bsz, length, _ = x.shape
    tok = lambda width: pl.BlockSpec((1, tm, width), lambda b, i: (b, i, 0))
    vec = pl.BlockSpec((1, D_MODEL), lambda b, i: (0, 0))
    return pl.pallas_call(
        _outproj_kernel,
        grid=(bsz, length // tm),
        in_specs=[tok(SSD_W), tok(RWKV_W), tok(GLA_W), tok(D_MODEL),
                  _mod_spec(gt, tm), _mod_spec(sc2, tm), _mod_spec(sh2, tm),
                  pl.BlockSpec((D_MODEL, D_MODEL), lambda b, i: (0, 0)), vec, vec,
                  pl.BlockSpec((D_MODEL, LANES), lambda b, i: (0, 0))],
        out_specs=[tok(D_MODEL), tok(LANES)],
        out_shape=[jax.ShapeDtypeStruct((bsz, length, D_MODEL), F32),
                   jax.ShapeDtypeStruct((bsz, length, LANES), F32)],
        compiler_params=_cparams("arbitrary", "arbitrary"),
        name="outproj_ln",
    )(y_ssd, y_rwkv, y_gla, x, gt, sc2, sh2, w_out, ln_g, ln_b, router_pad)


def _ffn_kernel(ys_ref, yr_ref, yg_ref, x_ref, gt1_ref, sc_ref, sh_ref, gt2_ref, wo_ref, g1_ref, b1_ref,
                wg_ref, wu_ref, wd_ref, g2_ref, b2_ref, o_ref):
    tm = x_ref.shape[1]
    parts = _row_parts(tm)

    def chain(r):
        rows = slice(r * tm // parts, (r + 1) * tm // parts)
        m = (jnp.dot(ys_ref[0, rows, :], wo_ref[0:SSD_W, :], preferred_element_type=F32)
             + jnp.dot(yr_ref[0, rows, :], wo_ref[SSD_W:SSD_W + RWKV_W, :], preferred_element_type=F32)
             + jnp.dot(yg_ref[0, rows, :], wo_ref[SSD_W + RWKV_W:, :], preferred_element_type=F32))
        yield
        x1 = _layernorm(ALPHA * x_ref[0, rows, :] + (1.0 + _mod_rows(gt1_ref, rows)) * m,
                        g1_ref[...], b1_ref[...])
        h = (x1 * (1.0 + _mod_rows(sc_ref, rows)) + _mod_rows(sh_ref, rows)).astype(BF16)
        yield
        gate = jnp.dot(h, wg_ref[...], preferred_element_type=F32)
        up = jnp.dot(h, wu_ref[...], preferred_element_type=F32)
        yield
        f = jnp.dot((_silu(gate) * up).astype(BF16), wd_ref[...], preferred_element_type=F32)
        yield
        u = ALPHA * x1 + (1.0 + _mod_rows(gt2_ref, rows)) * f
        o_ref[0, rows, :] = _layernorm(u, g2_ref[...], b2_ref[...])

    _interleave([chain(r) for r in range(parts)], skew=1)


def _outproj_ffn(y_ssd, y_rwkv, y_gla, x, gt1, sc2, sh2, gt2, w_out, ln1_g, ln1_b, wg, wu, wd, ln2_g, ln2_b, tm):
    bsz, length, _ = x.shape
    tok = lambda width: pl.BlockSpec((1, tm, width), lambda b, i: (b, i, 0))
    vec = pl.BlockSpec((1, D_MODEL), lambda b, i: (0, 0))
    resident = lambda a: pl.BlockSpec(a.shape, lambda b, i: (0, 0), pipeline_mode=pl.Buffered(1))
    return pl.pallas_call(
        _ffn_kernel,
        grid=(bsz, length // tm),
        in_specs=[tok(SSD_W), tok(RWKV_W), tok(GLA_W), tok(D_MODEL),
                  _mod_spec(gt1, tm), _mod_spec(sc2, tm), _mod_spec(sh2, tm), _mod_spec(gt2, tm),
                  resident(w_out), vec, vec, resident(wg), resident(wu), resident(wd), vec, vec],
        out_specs=tok(D_MODEL),
        out_shape=jax.ShapeDtypeStruct((bsz, length, D_MODEL), F32),
        compiler_params=_cparams("arbitrary", "arbitrary"),
        name="outproj_ffn_ln",
    )(y_ssd, y_rwkv, y_gla, x, gt1, sc2, sh2, gt2, w_out, ln1_g, ln1_b, wg, wu, wd, ln2_g, ln2_b)


def _moe_kernel(x_ref, sc_ref, sh_ref, gt_ref, logit_ref, wg_ref, wu_ref, wd_ref, g_ref, b_ref, tri_ref, o_ref,
                h_ref, rank_t_ref, comb_t_ref, cnt_ref, acc_ref, *, half, cap):
    e = pl.program_id(2)
    tm = 2 * half

    @pl.when(e == 0)
    def _():
        h_ref[...] = (x_ref[0] * (1.0 + sc_ref[0]) + sh_ref[0]).astype(BF16)
        acc_ref[...] = jnp.zeros_like(acc_ref)
        lane = _iota((tm, LANES), 1).astype(F32)
        logits = jnp.where(lane < N_EXPERTS, logit_ref[0], -jnp.inf)
        m1 = jnp.max(logits, axis=-1, keepdims=True)
        i1 = jnp.min(jnp.where(logits == m1, lane, float(LANES)), axis=-1, keepdims=True)
        rest = jnp.where(lane == i1, -jnp.inf, logits)
        m2 = jnp.max(rest, axis=-1, keepdims=True)
        i2 = jnp.min(jnp.where(rest == m2, lane, float(LANES)), axis=-1, keepdims=True)
        e2 = jnp.exp(m2 - m1)
        den = 1.0 + e2
        comb = jnp.where(lane == i1, 1.0 / den, 0.0) + jnp.where(lane == i2, e2 / den, 0.0)
        sel = (lane == i1) | (lane == i2)
        sel_f = sel.astype(F32)
        ranks = []
        for hf in range(2):
            s = sel_f[hf * half:(hf + 1) * half, :]
            before = jnp.dot(tri_ref[...], s.astype(BF16), preferred_element_type=F32)
            ranks.append(jnp.where(sel[hf * half:(hf + 1) * half, :], before, -1.0))
            cnt_ref[hf:hf + 1, :] = jnp.sum(s, axis=0, keepdims=True)
        rank = jnp.concatenate(ranks, axis=0)
        for blk in range(tm // LANES):
            rows = slice(blk * LANES, (blk + 1) * LANES)
            rank_t_ref[:, rows] = rank[rows, :].T
            comb_t_ref[:, rows] = comb[rows, :].T

    rank_t_e = rank_t_ref[pl.ds(e, 1), :]
    comb_t_e = comb_t_ref[pl.ds(e, 1), :]
    n_max = jnp.max(jnp.where(_iota((2, LANES), 1) == e, cnt_ref[...], 0.0))
    n_pass = (n_max.astype(jnp.int32) + (cap - 1)) // cap

    def one_pass(p, carry):
        slot = _iota((cap, 1), 0).astype(F32) + (p * cap).astype(F32)
        picks, xs, gates = [], [], []
        for hf in range(2):
            rows = slice(hf * half, (hf + 1) * half)
            hit = rank_t_e[:, rows] == slot
            pick = hit.astype(BF16)
            picks.append(pick)
            xs.append(jnp.dot(pick, h_ref[rows, :], preferred_element_type=F32).astype(BF16))
            gates.append(jnp.sum(jnp.where(hit, comb_t_e[:, rows], 0.0), axis=1, keepdims=True))
        xc = jnp.concatenate(xs, axis=0)
        a = _silu(jnp.dot(xc, wg_ref[0], preferred_element_type=F32)) * jnp.dot(
            xc, wu_ref[0], preferred_element_type=F32)
        out_e = jnp.dot(a.astype(BF16), wd_ref[0], preferred_element_type=F32)
        for hf in range(2):
            rows = slice(hf * half, (hf + 1) * half)
            weighted = (out_e[hf * cap:(hf + 1) * cap, :] * gates[hf]).astype(BF16)
            acc_ref[rows, :] += lax.dot_general(picks[hf], weighted, (((0,), (0,)), ((), ())),
                                                preferred_element_type=F32)
        return carry

    one_pass(jnp.int32(0), 0)
    lax.fori_loop(1, n_pass, one_pass, 0)

    @pl.when(e == pl.num_programs(2) - 1)
    def _():
        u = ALPHA * x_ref[0] + (1.0 + gt_ref[0]) * acc_ref[...]
        o_ref[0] = _layernorm(u, g_ref[...], b_ref[...])


def _moe(x, sc, sh, gt, logits, wg, wu, wd, ln_g, ln_b, tm):
    bsz, length, _ = x.shape
    assert tm % LANES == 0
    half = tm // 2
    cap = MOE_CAP_ROWS if half >= 2 * MOE_CAP_ROWS else half
    tri = (jnp.arange(half)[None, :] < jnp.arange(half)[:, None]).astype(BF16)
    tok = pl.BlockSpec((1, tm, D_MODEL), lambda b, i, e: (b, i, 0))
    vec = pl.BlockSpec((1, D_MODEL), lambda b, i, e: (0, 0))
    wspec = pl.BlockSpec((1, D_MODEL, D_MODEL), lambda b, i, e: (e, 0, 0))
    return pl.pallas_call(
        functools.partial(_moe_kernel, half=half, cap=cap),
        grid=(bsz, length // tm, N_EXPERTS),
        in_specs=[tok, _mod_spec(sc, tm), _mod_spec(sh, tm), _mod_spec(gt, tm),
                  pl.BlockSpec((1, tm, LANES), lambda b, i, e: (b, i, 0)),
                  wspec, wspec, wspec, vec, vec, _const_spec(tri)],
        out_specs=tok,
        out_shape=jax.ShapeDtypeStruct((bsz, length, D_MODEL), F32),
        scratch_shapes=[pltpu.VMEM((tm, D_MODEL), BF16), pltpu.VMEM((LANES, tm), F32),
                        pltpu.VMEM((LANES, tm), F32), pltpu.VMEM((2, LANES), F32),
                        pltpu.VMEM((tm, D_MODEL), F32)],
        compiler_params=_cparams("arbitrary", "arbitrary", "arbitrary"),
        name="moe_ln",
    )(x, sc, sh, gt, logits, wg, wu, wd, ln_g, ln_b, tri)


def _pad_lanes(vec, offset, width=SMALL_W):
    out = jnp.zeros((1, width), F32)
    return out.at[0, offset:offset + vec.shape[0]].set(vec)


def _layer_params(p, l):
    w_in = p["w_in"][l]
    off = [0]
    for s in (SSD_W, SSD_CONV_DIM, SSD_H, RWKV_FEAT, GLA_HK, GLA_HK, GLA_W, GK_LORA, GLA_W):
        off.append(off[-1] + s)
    piece = lambda i: w_in[:, off[i]:off[i + 1]]
    small = jnp.zeros((D_MODEL, SMALL_W), F32)
    small = small.at[:, DT_OFF:DT_OFF + SSD_H].set(piece(2)).at[:, GLO_OFF:GLO_OFF + GK_LORA].set(piece(7))
    w_pad = jnp.concatenate([piece(0), piece(1), small, piece(3), piece(4), piece(5), piece(6), piece(8)],
                            axis=1).astype(BF16)
    ssd_p = (p["ssd_conv_w"][l], p["ssd_conv_b"][l][None, :],
             _pad_lanes(p["ssd_dt_bias"][l], DT_OFF), _pad_lanes(p["ssd_a_log"][l], DT_OFF),
             jnp.repeat(p["ssd_d"][l], SSD_P)[None, :], p["ssd_norm_g"][l][None, :])
    lora_w = jnp.zeros((LANES, 3 * RWKV_W), F32)
    lora_w = (lora_w.at[0:32, 0:RWKV_W].set(p["rwkv_w2"][l])
              .at[32:64, RWKV_W:2 * RWKV_W].set(p["rwkv_a2"][l])
              .at[64:128, 2 * RWKV_W:].set(p["rwkv_g2"][l])).astype(BF16)
    row = lambda name: p[name][l].reshape(1, -1)
    rwkv_p = (row("rwkv_mu"), lora_w, row("rwkv_w0"), row("rwkv_a0"), row("rwkv_k_k"), row("rwkv_k_a"),
              row("rwkv_r_k"), row("rwkv_ln_g"), row("rwkv_ln_b"))
    gk2_pad = jnp.zeros((SMALL_W, GLA_HK), F32).at[GLO_OFF:GLO_OFF + GK_LORA].set(p["gla_w_gk2"][l])
    gla_p = (gk2_pad, row("gla_b_gk"), jnp.tile(p["gla_norm_g"][l], GLA_H)[None, :])
    return w_pad, ssd_p, rwkv_p, gla_p


def _block_diag_inv(s_bd, h):
    b, hr, hc = s_bd.shape
    r, c = hr // h, hc // h
    s = s_bd.reshape(b, h, r, h, c)
    return jnp.stack([s[:, i, :, i, :] for i in range(h)], axis=1)


def _ssd_state_unpack(h_bd):
    b = h_bd.shape[0]
    s = h_bd.reshape(b, SSD_G, SSD_N, SSD_H, SSD_P)
    per_group = SSD_H // SSD_G
    heads = [s[:, h // per_group, :, h, :] for h in range(SSD_H)]
    return jnp.swapaxes(jnp.stack(heads, axis=1), 2, 3)


def _tail(x, mod_l, l, p, y_ssd, y_rwkv, y_gla, tm):
    sh1, sc1, gt1, sh2, sc2, gt2 = mod_l
    row = lambda name: p[name][l].reshape(1, -1)
    w_out = p["w_out"][l].astype(BF16)
    i = l // 2
    if l % 2 == 0:
        x = _outproj_ffn(y_ssd, y_rwkv, y_gla, x, gt1, sc2, sh2, gt2, w_out, row("ln_mix_g"), row("ln_mix_b"),
                         p["ffn_w_gate"][i].astype(BF16), p["ffn_w_up"][i].astype(BF16),
                         p["ffn_w_down"][i].astype(BF16), row("ln_ffn_g"), row("ln_ffn_b"), tm)
    else:
        router_pad = jnp.zeros((D_MODEL, LANES), F32).at[:, :N_EXPERTS].set(p["moe_router"][i])
        x, logits = _outproj(y_ssd, y_rwkv, y_gla, x, gt1, sc2, sh2, w_out, row("ln_mix_g"), row("ln_mix_b"),
                             router_pad, tm)
        x = _moe(x, sc2, sh2, gt2, logits, p["moe_w_gate"][i].astype(BF16), p["moe_w_up"][i].astype(BF16),
                 p["moe_w_down"][i].astype(BF16), row("ln_ffn_g"), row("ln_ffn_b"), min(2 * tm, x.shape[1]))
    return x


def _forward(x_prompt, x_sample, c_prompt, c_sample, states, p):
    bp, seq, _ = x_prompt.shape
    bs = x_sample.shape[0]
    state_ssd, state_conv, state_rwkv, state_shift, state_gla = states
    mod = _ada(jnp.concatenate([c_prompt, c_sample], axis=0), p["w_ada"], p["b_ada"])

    xp = x_prompt
    xs = x_sample.reshape(1, bs, D_MODEL)
    tm_p = min(512, seq)
    outs_p = [[] for _ in range(5)]
    outs_s = [[] for _ in range(5)]
    for l in range(DEPTH):
        w_pad, ssd_p, rwkv_p, gla_p = _layer_params(p, l)
        mods = jnp.split(mod[l], 6, axis=-1)
        mod_p = [m[:bp, None, :] for m in mods]
        mod_s = [m[None, bp:, :] for m in mods]

        ssd_slab, small, rw, gla_slab = _inproj(xp, mod_p[1], mod_p[0], w_pad, tm_p)
        y_ssd, conv_new, h_new = _ssd_prompt(ssd_slab, small, *ssd_p)
        h_new = _ssd_state_unpack(h_new)
        y_rwkv, shift_new, rs_bd = _rwkv_prompt(rw, *rwkv_p)
        y_gla, gs_bd = _gla_prompt(gla_slab, small, *gla_p)
        xp = _tail(xp, mod_p, l, p, y_ssd, y_rwkv, y_gla, tm_p)
        for acc, s in zip(outs_p, (h_new, conv_new, _block_diag_inv(rs_bd, RWKV_H),
                                   shift_new.reshape(bp, RWKV_FEAT), _block_diag_inv(gs_bd, GLA_H))):
            acc.append(s)

        ssd_slab, small, rw, gla_slab = _inproj(xs, mod_s[1], mod_s[0], w_pad, bs)
        y_ssd, y_rwkv, y_gla, conv_new, h_new, rs_new, gs_new = _mixer_step(
            ssd_slab[0], small[0], rw[0], gla_slab[0], state_conv[l].reshape(bs, -1), state_shift[l],
            state_ssd[l], state_rwkv[l], state_gla[l], ssd_p, rwkv_p, gla_p)
        xs = _tail(xs, mod_s, l, p, y_ssd[None], y_rwkv[None], y_gla[None], bs)
        for acc, s in zip(outs_s, (h_new, conv_new, rs_new, rw[0].astype(F32), gs_new)):
            acc.append(s)

    stack = lambda accs: tuple(jnp.stack(a, axis=0) for a in accs)
    return (xp, xs.reshape(bs, 1, D_MODEL)) + stack(outs_p) + stack(outs_s)


def kernel(x_prompt, x_sample, c_prompt, c_sample, state_ssd, state_ssd_conv, state_rwkv, state_rwkv_shift, state_gla, w_ada, b_ada, w_in, w_out, ssd_conv_w, ssd_conv_b, ssd_dt_bias, ssd_a_log, ssd_d, ssd_norm_g, rwkv_mu, rwkv_w0, rwkv_w2, rwkv_a0, rwkv_a2, rwkv_g2, rwkv_k_k, rwkv_k_a, rwkv_r_k, rwkv_ln_g, rwkv_ln_b, gla_w_gk2, gla_b_gk, gla_norm_g, ln_mix_g, ln_mix_b, ln_ffn_g, ln_ffn_b, ffn_w_gate, ffn_w_up, ffn_w_down, moe_router, moe_w_gate, moe_w_up, moe_w_down):
    p = dict(w_ada=w_ada, b_ada=b_ada, w_in=w_in, w_out=w_out, ssd_conv_w=ssd_conv_w, ssd_conv_b=ssd_conv_b,
             ssd_dt_bias=ssd_dt_bias, ssd_a_log=ssd_a_log, ssd_d=ssd_d, ssd_norm_g=ssd_norm_g,
             rwkv_mu=rwkv_mu, rwkv_w0=rwkv_w0, rwkv_w2=rwkv_w2, rwkv_a0=rwkv_a0, rwkv_a2=rwkv_a2,
             rwkv_g2=rwkv_g2, rwkv_k_k=rwkv_k_k, rwkv_k_a=rwkv_k_a, rwkv_r_k=rwkv_r_k,
             rwkv_ln_g=rwkv_ln_g, rwkv_ln_b=rwkv_ln_b, gla_w_gk2=gla_w_gk2, gla_b_gk=gla_b_gk,
             gla_norm_g=gla_norm_g, ln_mix_g=ln_mix_g, ln_mix_b=ln_mix_b, ln_ffn_g=ln_ffn_g,
             ln_ffn_b=ln_ffn_b, ffn_w_gate=ffn_w_gate, ffn_w_up=ffn_w_up, ffn_w_down=ffn_w_down,
             moe_router=moe_router, moe_w_gate=moe_w_gate, moe_w_up=moe_w_up, moe_w_down=moe_w_down)
    states = (state_ssd, state_ssd_conv, state_rwkv, state_rwkv_shift, state_gla)
    return _forward(x_prompt, x_sample, c_prompt, c_sample, states, p)
```

```python
import functools

import jax
import jax.numpy as jnp
from jax import lax
from jax.experimental import pallas as pl
from jax.experimental.pallas import tpu as pltpu

F32 = jnp.float32
BF16 = jnp.bfloat16

D_MODEL = 1024
DEPTH = 2
SSD_W = 512
SSD_H = 8
SSD_P = 64
SSD_N = 64
SSD_G = 2
SSD_CONV_W = 4
SSD_CONV_DIM = 768
RWKV_W = 256
RWKV_H = 4
RWKV_N = 64
RWKV_FEAT = 896
RWKV_GN_EPS = RWKV_N * 1e-5
GLA_W = 256
GLA_H = 4
GLA_DK = 32
GLA_DV = 64
GLA_HK = GLA_H * GLA_DK
GK_LORA = 16
GATE_NORMALIZER = 16.0
F_DENSE = 2816
N_EXPERTS = 8
ALPHA = (2.0 * DEPTH) ** 0.25
LN_EPS = 1e-5
RMS_EPS = 1e-6

LANES = 128
SMALL_W = LANES
DT_OFF = 0
GLO_OFF = 8
SSD_SLAB = SSD_W + SSD_CONV_DIM
GLA_SLAB = 2 * GLA_HK + 2 * GLA_W
IN_PAD = SSD_SLAB + SMALL_W + RWKV_FEAT + GLA_SLAB
SSD_CHUNK = 128
SSD_CARRY = 8
GLA_CHUNK = 128
GLA_SUB = 8
RWKV_CHUNK = 64
SSD_NB, RWKV_NB, GLA_NB = 4, 8, 8
RWKV_SKEW = 1
STAT_TERMS = 1
MOE_CAP_ROWS = 160
STEP_GRID = 32
VMEM_LIMIT = 56 * 1024 * 1024


def _cparams(*sem):
    return pltpu.CompilerParams(dimension_semantics=sem, vmem_limit_bytes=VMEM_LIMIT)


def _mm(a, b):
    return jnp.dot(a.astype(BF16), b.astype(BF16), preferred_element_type=F32)


def _mm_nt(a, b):
    return lax.dot_general(a.astype(BF16), b.astype(BF16), (((1,), (1,)), ((), ())),
                           preferred_element_type=F32)


def _split_bf16(x, terms):
    parts = []
    for _ in range(terms):
        p = x.astype(BF16)
        parts.append(p)
        x = x - p.astype(F32)
    return parts


def _mm_sel(sel, x, terms=3):
    return sum(jnp.dot(sel, p, preferred_element_type=F32) for p in _split_bf16(x, terms))


def _mm_xsel(x, sel, terms=2):
    return sum(jnp.dot(p, sel, preferred_element_type=F32) for p in _split_bf16(x, terms))


def _mm_hi(a, b):
    a_hi, a_lo = _split_bf16(a, 2)
    b_hi, b_lo = _split_bf16(b, 2)
    dot = lambda x, y: jnp.dot(x, y, preferred_element_type=F32)
    return dot(a_hi, b_hi) + (dot(a_hi, b_lo) + dot(a_lo, b_hi))


def _sigmoid(x):
    return 1.0 / (1.0 + jnp.exp(-x))


def _silu(x):
    return x * _sigmoid(x)


def _softplus(x):
    return jnp.maximum(x, 0.0) + jnp.log(1.0 + jnp.exp(-jnp.abs(x)))


def _iota(shape, dim):
    return lax.broadcasted_iota(jnp.int32, shape, dim)


def _layernorm(u, g, b):
    mu = jnp.mean(u, axis=-1, keepdims=True)
    d = u - mu
    var = jnp.mean(d * d, axis=-1, keepdims=True)
    return d * lax.rsqrt(var + LN_EPS) * g + b


def _tri_incl(n):
    return (jnp.arange(n)[None, :] <= jnp.arange(n)[:, None]).astype(BF16)


def _seg_ones(n, seg, scale=1.0):
    idx = jnp.arange(n) // seg
    return jnp.where(idx[:, None] == idx[None, :], scale, 0.0).astype(BF16)


def _interleave(chains, skew=0):
    chains = list(chains)
    done = [False] * len(chains)
    rnd = 0
    while not all(done):
        for i, ch in enumerate(chains):
            if done[i] or rnd < i * skew:
                continue
            try:
                next(ch)
            except StopIteration:
                done[i] = True
        rnd += 1


def _const_spec(arr):
    nd = arr.ndim
    return pl.BlockSpec(arr.shape, lambda *_: (0,) * nd)


def _ada_kernel(c_ref, w_ref, b_ref, o_ref):
    o_ref[0] = _mm(_silu(c_ref[...]), w_ref[0]) + b_ref[0]


def _ada(c_all, w_ada, b_ada):
    rows = c_all.shape[0]
    tn = 1536
    return pl.pallas_call(
        _ada_kernel,
        grid=(DEPTH, 6 * D_MODEL // tn),
        in_specs=[pl.BlockSpec((rows, D_MODEL), lambda l, j: (0, 0)),
                  pl.BlockSpec((1, D_MODEL, tn), lambda l, j: (l, 0, j)),
                  pl.BlockSpec((1, 1, tn), lambda l, j: (l, 0, j))],
        out_specs=pl.BlockSpec((1, rows, tn), lambda l, j: (l, 0, j)),
        out_shape=jax.ShapeDtypeStruct((DEPTH, rows, 6 * D_MODEL), F32),
        compiler_params=_cparams("arbitrary", "arbitrary"),
        name="ada",
    )(c_all, w_ada, b_ada.reshape(DEPTH, 1, 6 * D_MODEL))


def _inproj_kernel(x_ref, sc_ref, sh_ref, w_ref, o_ssd, o_small, o_rwkv, o_gla):
    h = (x_ref[0] * (1.0 + sc_ref[0]) + sh_ref[0]).astype(BF16)
    off = 0
    for o_ref in (o_ssd, o_small, o_rwkv, o_gla):
        width = o_ref.shape[2]
        o_ref[0] = jnp.dot(h, w_ref[:, off:off + width], preferred_element_type=F32).astype(o_ref.dtype)
        off += width


def _mod_spec(mod, tm):
    if mod.shape[1] == 1:
        return pl.BlockSpec((1, 1, D_MODEL), lambda b, i, *_: (b, 0, 0))
    return pl.BlockSpec((1, tm, D_MODEL), lambda b, i, *_: (b, i, 0))


def _inproj(x, sc, sh, w_pad, tm):
    bsz, length, _ = x.shape
    tok = lambda width: pl.BlockSpec((1, tm, width), lambda b, i: (b, i, 0))
    return pl.pallas_call(
        _inproj_kernel,
        grid=(bsz, length // tm),
        in_specs=[tok(D_MODEL), _mod_spec(sc, tm), _mod_spec(sh, tm),
                  pl.BlockSpec((D_MODEL, IN_PAD), lambda b, i: (0, 0))],
        out_specs=[tok(SSD_SLAB), tok(SMALL_W), tok(RWKV_FEAT), tok(GLA_SLAB)],
        out_shape=[jax.ShapeDtypeStruct((bsz, length, SSD_SLAB), BF16),
                   jax.ShapeDtypeStruct((bsz, length, SMALL_W), F32),
                   jax.ShapeDtypeStruct((bsz, length, RWKV_FEAT), BF16),
                   jax.ShapeDtypeStruct((bsz, length, GLA_SLAB), BF16)],
        compiler_params=_cparams("arbitrary", "arbitrary"),
        name="inproj",
    )(x, sc, sh, w_pad)


def _ssd_token_math(xbc, small, conv_taps, cw, cb, dtb, alog):
    conv = cb + xbc * cw[3:4, :]
    for i in range(SSD_CONV_W - 1):
        conv = conv + conv_taps[i] * cw[i:i + 1, :]
    dt = _softplus(small + dtb)
    return _silu(conv), dt, dt * (-jnp.exp(alog))


def _ssd_finish(y, xs, z, dexp, ng):
    gated = (y + xs * dexp) * _silu(z)
    ms = jnp.mean(gated * gated, axis=-1, keepdims=True)
    return gated * lax.rsqrt(ms + RMS_EPS) * ng


def _ssd_kernel(slab_ref, small_ref, cw_ref, cb_ref, dtb_ref, alog_ref, dexp_ref, ng_ref, tri_ref,
                head_x_ref, group_mask_ref, pair_mask_ref, y_ref, conv_out_ref, h_out_ref, ext_ref, h_ref):
    c = pl.program_id(1)
    n_chunks = pl.num_programs(1)
    ch = SSD_CHUNK
    pad = ext_ref.shape[1] - ch
    causal = _iota((ch, ch), 1) <= _iota((ch, ch), 0)

    nb = slab_ref.shape[0]

    @pl.when(c == 0)
    def _():
        ext_ref[:, 0:pad, :] = jnp.zeros((nb, pad, SSD_CONV_DIM), F32)
        h_ref[...] = jnp.zeros_like(h_ref)

    def chain(j):
        z = slab_ref[j, :, 0:SSD_W].astype(F32)
        xbc = slab_ref[j, :, SSD_W:].astype(F32)
        small = small_ref[j]

        ext_ref[j, pad:pad + ch, :] = xbc
        first = pad - (SSD_CONV_W - 1)
        taps = [ext_ref[j, first + i:first + i + ch, :] for i in range(SSD_CONV_W - 1)]
        act, dt, dta = _ssd_token_math(xbc, small, taps, cw_ref[...], cb_ref[...], dtb_ref[...], alog_ref[...])
        ext_ref[j, 0:pad, :] = ext_ref[j, ch:ch + pad, :]

        xs = act[:, 0:SSD_W]
        bs = act[:, SSD_W:SSD_W + SSD_G * SSD_N]
        cs = act[:, SSD_W + SSD_G * SSD_N:]
        yield
        cum = _mm_sel(tri_ref[...], dta)
        cum_t = cum.T
        dt_t = dt.T
        cum_x = _mm_xsel(cum, head_x_ref[...], 3)
        dt_x = _mm_xsel(dt, head_x_ref[...], 2)
        yield
        last_x = cum_x[ch - 1:ch, :]
        xs_tail = (xs * (jnp.exp(last_x - cum_x) * dt_x)).astype(BF16)
        h_bd = h_ref[j]
        y_state = jnp.dot(cs.astype(BF16), h_bd.astype(BF16), preferred_element_type=F32) * jnp.exp(cum_x)
        upd = jnp.dot(bs.T.astype(BF16), xs_tail, preferred_element_type=F32)
        h_ref[j] = h_bd * jnp.exp(last_x) + upd * group_mask_ref[...]
        yield

        lane_group = _iota((1, SSD_G * SSD_N), 1) // SSD_N
        xs_b = xs.astype(BF16)
        ys = []
        for g in range(SSD_G):
            cb = _mm_nt(jnp.where(lane_group == g, cs, 0.0), bs)
            yield
            for pair in range(SSD_H // SSD_G // 2):
                scores = []
                for hh in range(2):
                    lane = DT_OFF + g * (SSD_H // SSD_G) + 2 * pair + hh
                    decay = jnp.exp(jnp.where(causal, cum[:, lane:lane + 1] - cum_t[lane:lane + 1, :], -jnp.inf))
                    scores.append((cb * decay * dt_t[lane:lane + 1, :]).astype(BF16))
                blk = g * (SSD_H // SSD_G) // 2 + pair
                x_pair = xs_b[:, blk * LANES:(blk + 1) * LANES]
                x_bd = jnp.concatenate([x_pair, x_pair], axis=0) * pair_mask_ref[...]
                ys.append(jnp.dot(jnp.concatenate(scores, axis=1), x_bd, preferred_element_type=F32))
                yield

        y = jnp.concatenate(ys, axis=-1) + y_state
        y_ref[j] = _ssd_finish(y, xs, z, dexp_ref[...], ng_ref[...]).astype(y_ref.dtype)

    _interleave([chain(j) for j in range(nb)])

    @pl.when(c == n_chunks - 1)
    def _():
        for j in range(nb):
            conv_out_ref[j] = ext_ref[j, pad - (SSD_CONV_W - 1):pad, :]
            h_out_ref[j] = h_ref[j]


def _ssd_prompt(slab, small, cw, cb, dtb, alog, dexp, ng):
    bsz, length, _ = slab.shape
    nb = SSD_NB if bsz % SSD_NB == 0 else 1
    gn, hp = SSD_G * SSD_N, SSD_H * SSD_P
    lane_head = jnp.arange(hp) // SSD_P
    head_x = (jnp.arange(SMALL_W)[:, None] == DT_OFF + lane_head[None, :]).astype(BF16)
    group_mask = (jnp.arange(gn)[:, None] // SSD_N == lane_head[None, :] // (SSD_H // SSD_G)).astype(F32)
    pair_mask = (jnp.arange(2 * SSD_CHUNK)[:, None] // SSD_CHUNK == jnp.arange(2 * SSD_P)[None, :] // SSD_P
                 ).astype(BF16)
    consts = (cw, cb, dtb, alog, dexp, ng, _tri_incl(SSD_CHUNK), head_x, group_mask, pair_mask)
    return pl.pallas_call(
        _ssd_kernel,
        grid=(bsz // nb, length // SSD_CHUNK),
        in_specs=[pl.BlockSpec((nb, SSD_CHUNK, SSD_SLAB), lambda b, c: (b, c, 0)),
                  pl.BlockSpec((nb, SSD_CHUNK, SMALL_W), lambda b, c: (b, c, 0))]
                 + [_const_spec(a) for a in consts],
        out_specs=[pl.BlockSpec((nb, SSD_CHUNK, SSD_W), lambda b, c: (b, c, 0)),
                   pl.BlockSpec((nb, SSD_CONV_W - 1, SSD_CONV_DIM), lambda b, c: (b, 0, 0)),
                   pl.BlockSpec((nb, gn, hp), lambda b, c: (b, 0, 0))],
        out_shape=[jax.ShapeDtypeStruct((bsz, length, SSD_W), BF16),
                   jax.ShapeDtypeStruct((bsz, SSD_CONV_W - 1, SSD_CONV_DIM), F32),
                   jax.ShapeDtypeStruct((bsz, gn, hp), F32)],
        scratch_shapes=[pltpu.VMEM((nb, SSD_CARRY + SSD_CHUNK, SSD_CONV_DIM), F32),
                        pltpu.VMEM((nb, gn, hp), F32)],
        compiler_params=_cparams("arbitrary", "arbitrary"),
        name="ssd_scan",
    )(slab, small, *consts)


def _rwkv_token_math(rw, prev, mu, lora_w, w0, a0, k_k, k_a, head_ones):
    mix = rw + mu * (prev - rw)
    r = mix[:, 0:RWKV_W]
    k = mix[:, RWKV_W:2 * RWKV_W]
    v = mix[:, 2 * RWKV_W:3 * RWKV_W]
    lora = mix[:, 3 * RWKV_W:]
    lane = _iota(lora.shape, 1)
    act = jnp.where(lane < 32, jnp.tanh(lora), jnp.where(lane < 64, lora, _sigmoid(lora)))
    lo = _mm(act, lora_w)
    w_log = -_softplus(-(w0 + lo[:, 0:RWKV_W])) - 0.5
    logw = -jnp.exp(w_log)
    aic = _sigmoid(a0 + lo[:, RWKV_W:2 * RWKV_W])
    gate = lo[:, 2 * RWKV_W:]
    kkf = k * k_k
    kk = kkf * lax.rsqrt(_mm_xsel(kkf * kkf, head_ones, STAT_TERMS) + 1e-12)
    k2 = k * (1.0 + (aic - 1.0) * k_a)
    return r, k2, v, logw, aic, gate, kk


def _rwkv_finish(o, r, k2, v, gate, r_k, ln_g, ln_b, head_ones, head_avg):
    mu = _mm_xsel(o, head_avg, STAT_TERMS)
    d = o - mu
    var = _mm_xsel(d * d, head_avg, STAT_TERMS)
    on = d * lax.rsqrt(var + RWKV_GN_EPS) * ln_g + ln_b
    bonus = _mm_xsel(r * k2 * r_k, head_ones, STAT_TERMS) * v
    return (on + bonus) * gate


def _rwkv_kernel(rw_ref, mu_ref, lw_ref, w0_ref, a0_ref, kk_ref, ka_ref,
                 rk_ref, lng_ref, lnb_ref, ones_ref, avg_ref, tri_ref, strict_ref, incl_ref, eye_ref,
                 y_ref, shift_out_ref, s_out_ref, ext_ref, s_ref):
    c = pl.program_id(1)
    n_chunks = pl.num_programs(1)
    ch = RWKV_CHUNK
    nh = RWKV_H
    w = RWKV_W
    head_bd = ones_ref[...]
    head_bd_f = head_bd.astype(F32)
    tile4 = lambda x: jnp.concatenate([x] * nh, axis=0)

    nb = rw_ref.shape[0]

    @pl.when(c == 0)
    def _():
        ext_ref[:, 0:8, :] = jnp.zeros((nb, 8, RWKV_FEAT), F32)
        s_ref[...] = jnp.zeros_like(s_ref)

    def chain(j):
        rw = rw_ref[j].astype(F32)
        ext_ref[j, 8:8 + ch, :] = rw
        prev = ext_ref[j, 7:7 + ch, :]
        ext_ref[j, 7:8, :] = rw[ch - 1:ch, :]

        r, k2, v, logw, aic, gate, kk = _rwkv_token_math(
            rw, prev, mu_ref[...], lw_ref[...], w0_ref[...], a0_ref[...], kk_ref[...], ka_ref[...], head_bd)
        yield

        cumw = _mm_sel(tri_ref[...], logw)
        yield
        last = cumw[ch - 1:ch, :]
        inv_g = jnp.exp(-cumw)
        to_end = jnp.exp(last - cumw)
        kka = kk * aic
        a_t = (-kk * jnp.exp(cumw - logw)).astype(BF16)
        r_t = (r * jnp.exp(cumw)).astype(BF16)
        b_t = (kka * inv_g).astype(BF16)
        k_t = (k2 * inv_g).astype(BF16)
        bk_end = jnp.concatenate([kka * to_end, k2 * to_end], axis=0).astype(BF16)

        bd = lambda x: tile4(x.astype(BF16)) * head_bd
        nt = lambda x, y: lax.dot_general(x, y, (((1,), (1,)), ((), ())), preferred_element_type=F32)
        vb = v.astype(BF16)
        ar = jnp.concatenate([a_t, r_t], axis=0)
        prod = nt(ar, jnp.concatenate([bd(b_t), bd(k_t)], axis=0))
        a_ab = prod[0:ch, 0:w] * strict_ref[...]
        a_ak = prod[0:ch, w:2 * w] * strict_ref[...]
        a_rbk = (prod[ch:2 * ch, :] * incl_ref[...]).astype(BF16)
        yield
        s_bd = s_ref[j]
        uo = nt(ar, s_bd.astype(BF16))
        rhs = uo[0:ch, :] + jnp.dot(a_ak.astype(BF16), bd(vb), preferred_element_type=F32)
        yield

        t_inv = eye_ref[...] + a_ab
        x = a_ab
        x_bd = bd(x)
        power = 1
        while 2 * power < ch:
            x = jnp.dot(x.astype(BF16), x_bd, preferred_element_type=F32)
            x_bd = bd(x)
            yield
            t_inv = t_inv + jnp.dot(t_inv.astype(BF16), x_bd, preferred_element_type=F32)
            yield
            power *= 2

        p = jnp.dot(t_inv.astype(BF16), bd(rhs), preferred_element_type=F32)
        yield
        pb = p.astype(BF16)
        o = uo[ch:2 * ch, :] + jnp.dot(a_rbk, jnp.concatenate([bd(pb), bd(vb)], axis=0),
                                       preferred_element_type=F32)
        pv = jnp.concatenate([pb, vb], axis=0)
        yield

        upd = jnp.dot(pv.astype(F32).T.astype(BF16), bk_end, preferred_element_type=F32)
        s_ref[j] = s_bd * jnp.exp(last) + upd * head_bd_f
        yield

        y_ref[j] = _rwkv_finish(o, r, k2, v, gate, rk_ref[...], lng_ref[...], lnb_ref[...],
                                head_bd, avg_ref[...]).astype(y_ref.dtype)

    _interleave([chain(j) for j in range(nb)], skew=RWKV_SKEW)

    @pl.when(c == n_chunks - 1)
    def _():
        for j in range(nb):
            shift_out_ref[j] = ext_ref[j, 7:8, :]
            s_out_ref[j] = s_ref[j]


def _rwkv_consts():
    ch = RWKV_CHUNK
    assert ch == RWKV_N, "bd() reuses the head mask, which needs chunk rows == head width"
    t = jnp.arange(ch)[:, None]
    j = jnp.arange(2 * RWKV_W)[None, :] % ch
    strict = (j[:, :RWKV_W] < t).astype(F32)
    incl = (j <= t).astype(F32)
    eye = (j[:, :RWKV_W] == t).astype(F32)
    return (_seg_ones(RWKV_W, RWKV_N), _seg_ones(RWKV_W, RWKV_N, 1.0 / RWKV_N), _tri_incl(ch),
            strict, incl, eye)


def _rwkv_prompt(rw, *params):
    bsz, length, _ = rw.shape
    nb = RWKV_NB if bsz % RWKV_NB == 0 else 1
    consts = tuple(params) + _rwkv_consts()
    return pl.pallas_call(
        _rwkv_kernel,
        grid=(bsz // nb, length // RWKV_CHUNK),
        in_specs=[pl.BlockSpec((nb, RWKV_CHUNK, RWKV_FEAT), lambda b, c: (b, c, 0))]
                 + [_const_spec(a) for a in consts],
        out_specs=[pl.BlockSpec((nb, RWKV_CHUNK, RWKV_W), lambda b, c: (b, c, 0)),
                   pl.BlockSpec((nb, 1, RWKV_FEAT), lambda b, c: (b, 0, 0)),
                   pl.BlockSpec((nb, RWKV_W, RWKV_W), lambda b, c: (b, 0, 0))],
        out_shape=[jax.ShapeDtypeStruct((bsz, length, RWKV_W), BF16),
                   jax.ShapeDtypeStruct((bsz, 1, RWKV_FEAT), F32),
                   jax.ShapeDtypeStruct((bsz, RWKV_W, RWKV_W), F32)],
        scratch_shapes=[pltpu.VMEM((nb, RWKV_CHUNK + 8, RWKV_FEAT), F32),
                        pltpu.VMEM((nb, RWKV_W, RWKV_W), F32)],
        compiler_params=_cparams("arbitrary", "arbitrary"),
        name="rwkv_scan",
    )(rw, *consts)


def _gla_gate_log(small, gk2_pad, b_gk):
    x = _mm_hi(small, gk2_pad) + b_gk
    return -_softplus(-x) / GATE_NORMALIZER


def _gla_finish(o, gg, norm_g, head_avg):
    ms = _mm_xsel(o * o, head_avg, STAT_TERMS)
    return o * lax.rsqrt(ms + RMS_EPS) * norm_g * _silu(gg)


def _gla_kernel(slab_ref, small_ref, gk2_ref, bgk_ref, ng_ref, avg_ref, tri_ref, expand_ref,
                y_ref, s_out_ref, s_ref):
    c = pl.program_id(1)
    n_chunks = pl.num_programs(1)
    ch = GLA_CHUNK
    sub = GLA_SUB
    hk = GLA_HK
    expand = expand_ref[...]
    expand_f = expand.astype(F32)
    pos = _iota((ch, 1), 0) % sub
    t_idx = _iota((ch, 1), 0)
    hmask_k = (_iota((GLA_H * sub, hk), 0) // sub == _iota((GLA_H * sub, hk), 1) // GLA_DK).astype(F32)
    hmask_v = (_iota((GLA_H * sub, GLA_W), 0) // sub == _iota((GLA_H * sub, GLA_W), 1) // GLA_DV).astype(F32)

    nb = slab_ref.shape[0]

    @pl.when(c == 0)
    def _():
        s_ref[...] = jnp.zeros_like(s_ref)

    def chain(j):
        q = slab_ref[j, :, 0:hk].astype(F32) * (GLA_DK ** -0.5)
        k = slab_ref[j, :, hk:2 * hk].astype(F32)
        v = slab_ref[j, :, 2 * hk:2 * hk + GLA_W].astype(F32)
        gg = slab_ref[j, :, 2 * hk + GLA_W:].astype(F32)
        small = small_ref[j]

        lg = _gla_gate_log(small, gk2_ref[...], bgk_ref[...])
        yield
        cum = _mm_sel(tri_ref[...], lg)
        yield
        last = cum[ch - 1:ch, :]
        s_bd = s_ref[j]

        o = _mm(q * jnp.exp(cum), s_bd)

        group = 4
        for d0 in range(0, sub, group):
            prods, v_shift = [], []
            for delta in range(d0, d0 + group):
                if delta == 0:
                    k_s, c_s, v_s = k, cum, v
                else:
                    k_s = pltpu.roll(k, delta, 0)
                    c_s = pltpu.roll(cum, delta, 0)
                    v_s = pltpu.roll(v, delta, 0)
                valid = pos >= delta
                w_pair = jnp.exp(jnp.where(valid, cum - c_s, 0.0))
                prods.append(jnp.where(valid, q * k_s * w_pair, 0.0).astype(BF16))
                v_shift.append(v_s)
            yield
            att = jnp.dot(jnp.concatenate(prods, axis=0), expand, preferred_element_type=F32)
            o = o + sum(att[g * ch:(g + 1) * ch, :] * v_shift[g] for g in range(group))
            yield

        vb = v.astype(BF16)
        atts = []
        for blk in range(1, ch // sub):
            lo = blk * sub
            ref_pt = cum[lo - 1:lo, :]
            qd = q[lo:lo + sub, :] * jnp.exp(cum[lo:lo + sub, :] - ref_pt)
            kp = k * jnp.exp(jnp.where(t_idx < lo, ref_pt - cum, -jnp.inf))
            q4 = jnp.concatenate([qd] * GLA_H, axis=0) * hmask_k
            atts.append(_mm_nt(q4, kp).astype(BF16))
            yield
        o4 = jnp.dot(jnp.concatenate(atts, axis=0), vb, preferred_element_type=F32)
        yield
        cross = [jnp.zeros((sub, GLA_W), F32)]
        for blk in range(ch // sub - 1):
            o4_b = o4[blk * GLA_H * sub:(blk + 1) * GLA_H * sub, :] * hmask_v
            cross.append(sum(o4_b[h * sub:(h + 1) * sub, :] for h in range(GLA_H)))
        o = o + jnp.concatenate(cross, axis=0)

        kt_t = (k * jnp.exp(last - cum)).T
        s_ref[j] = s_bd * _col(jnp.exp(last)) + _mm(kt_t, vb) * expand_f
        yield

        y_ref[j] = _gla_finish(o, gg, ng_ref[...], avg_ref[...]).astype(y_ref.dtype)

    _interleave([chain(j) for j in range(nb)], skew=2)

    @pl.when(c == n_chunks - 1)
    def _():
        s_out_ref[...] = s_ref[...]


def _gla_prompt(slab, small, gk2_pad, b_gk, ng):
    bsz, length, _ = slab.shape
    nb = GLA_NB if bsz % GLA_NB == 0 else 1
    expand = (jnp.arange(GLA_HK)[:, None] // GLA_DK == jnp.arange(GLA_W)[None, :] // GLA_DV).astype(BF16)
    consts = (gk2_pad, b_gk, ng, _seg_ones(GLA_W, GLA_DV, 1.0 / GLA_DV), _tri_incl(GLA_CHUNK), expand)
    return pl.pallas_call(
        _gla_kernel,
        grid=(bsz // nb, length // GLA_CHUNK),
        in_specs=[pl.BlockSpec((nb, GLA_CHUNK, GLA_SLAB), lambda b, c: (b, c, 0)),
                  pl.BlockSpec((nb, GLA_CHUNK, SMALL_W), lambda b, c: (b, c, 0))]
                 + [_const_spec(a) for a in consts],
        out_specs=[pl.BlockSpec((nb, GLA_CHUNK, GLA_W), lambda b, c: (b, c, 0)),
                   pl.BlockSpec((nb, GLA_HK, GLA_W), lambda b, c: (b, 0, 0))],
        out_shape=[jax.ShapeDtypeStruct((bsz, length, GLA_W), BF16),
                   jax.ShapeDtypeStruct((bsz, GLA_HK, GLA_W), F32)],
        scratch_shapes=[pltpu.VMEM((nb, GLA_HK, GLA_W), F32)],
        compiler_params=_cparams("arbitrary", "arbitrary"),
        name="gla_scan",
    )(slab, small, *consts)


def _col(row):
    n = row.shape[1]
    eye = _iota((n, n), 0) == _iota((n, n), 1)
    return jnp.sum(jnp.where(eye, row, 0.0), axis=1, keepdims=True)


def _store_t(dst_ref, row0, x):
    for cblk in range(x.shape[1] // LANES):
        dst_ref[row0 + cblk * LANES:row0 + (cblk + 1) * LANES, :] = x[:, cblk * LANES:(cblk + 1) * LANES].T


def _load_t(src_ref, width):
    return jnp.concatenate([src_ref[cblk * LANES:(cblk + 1) * LANES, :].T for cblk in range(width // LANES)],
                           axis=1)


_TS_X, _TS_B, _TS_C, _TS_DT, _TS_DA = 0, SSD_W, SSD_W + SSD_G * SSD_N, SSD_CONV_DIM, SSD_CONV_DIM + SMALL_W
_TR_R, _TR_W, _TR_K, _TR_V, _TR_KK, _TR_KA = (i * RWKV_W for i in range(6))
_TG_Q, _TG_K, _TG_G, _TG_V = 0, GLA_HK, 2 * GLA_HK, 3 * GLA_HK


def _step_kernel(ssd_ref, small_ref, rw_ref, gla_ref, conv0_ref, shift0_ref, h0_ref, rs0_ref, gs0_ref,
                 cw_ref, cb_ref, dtb_ref, alog_ref, dexp_ref, sng_ref,
                 mu_ref, lw_ref, w0_ref, a0_ref, kk_ref, ka_ref, rk_ref, lng_ref, lnb_ref,
                 gk2_ref, bgk_ref, gng_ref, ones_ref, ravg_ref, gavg_ref,
                 y_ssd_ref, y_rwkv_ref, y_gla_ref, conv_out_ref, h_out_ref, rs_out_ref, gs_out_ref,
                 t_ssd, t_rwkv, t_gla, o_ssd_t, o_rwkv_t, o_gla_t):
    i = pl.program_id(0)
    hk = GLA_HK

    def ssd_tokens():
        xbc = ssd_ref[:, SSD_W:].astype(F32)
        taps = [conv0_ref[:, t * SSD_CONV_DIM:(t + 1) * SSD_CONV_DIM] for t in range(SSD_CONV_W - 1)]
        return xbc, _ssd_token_math(xbc, small_ref[...], taps, cw_ref[...], cb_ref[...],
                                    dtb_ref[...], alog_ref[...])

    def rwkv_tokens():
        return _rwkv_token_math(rw_ref[...].astype(F32), shift0_ref[...], mu_ref[...], lw_ref[...],
                                w0_ref[...], a0_ref[...], kk_ref[...], ka_ref[...], ones_ref[...])

    @pl.when(i == 0)
    def _():
        xbc, (act, dt, dta) = ssd_tokens()
        conv_out_ref[:, 0:2 * SSD_CONV_DIM] = conv0_ref[:, SSD_CONV_DIM:]
        conv_out_ref[:, 2 * SSD_CONV_DIM:] = xbc
        _store_t(t_ssd, _TS_X, act)
        _store_t(t_ssd, _TS_DT, dt)
        _store_t(t_ssd, _TS_DA, jnp.exp(dta))
        r, k2, v, logw, aic, gate, kk = rwkv_tokens()
        for off, x in ((_TR_R, r), (_TR_W, jnp.exp(logw)), (_TR_K, k2), (_TR_V, v), (_TR_KK, kk),
                       (_TR_KA, kk * aic)):
            _store_t(t_rwkv, off, x)
        lg = _gla_gate_log(small_ref[...], gk2_ref[...], bgk_ref[...])
        _store_t(t_gla, _TG_Q, gla_ref[:, 0:hk].astype(F32) * (GLA_DK ** -0.5))
        _store_t(t_gla, _TG_K, gla_ref[:, hk:2 * hk].astype(F32))
        _store_t(t_gla, _TG_G, jnp.exp(lg))
        _store_t(t_gla, _TG_V, gla_ref[:, 2 * hk:2 * hk + GLA_W].astype(F32))
        o_gla_t[...] = jnp.zeros_like(o_gla_t)

    def row(ref, idx):
        return ref[pl.ds(idx, 1), :]

    def rows(ref, idx, count):
        return ref[pl.ds(pl.multiple_of(idx, count), count), :]

    units = h0_ref.shape[0] // SSD_N
    per_head = SSD_P // units
    h = i // per_head
    p0 = (i % per_head) * units
    b_t = rows(t_ssd, _TS_B + (h // (SSD_H // SSD_G)) * SSD_N, SSD_N)
    c_t = rows(t_ssd, _TS_C + (h // (SSD_H // SSD_G)) * SSD_N, SSD_N)
    da_row = row(t_ssd, _TS_DA + DT_OFF + h)
    dt_row = row(t_ssd, _TS_DT + DT_OFF + h)
    for u in range(units):
        p_idx = h * SSD_P + p0 + u
        blk = slice(u * SSD_N, (u + 1) * SSD_N)
        h_new = h0_ref[blk, :] * da_row + (row(t_ssd, _TS_X + p_idx) * dt_row) * b_t
        o_ssd_t[pl.ds(p_idx, 1), :] = jnp.sum(h_new * c_t, axis=0, keepdims=True)
        h_out_ref[blk, :] = h_new

    units = rs0_ref.shape[0] // RWKV_N
    per_head = RWKV_N // units
    h = i // per_head
    v0 = (i % per_head) * units
    seg = lambda off: rows(t_rwkv, off + h * RWKV_N, RWKV_N)
    r_t, w_t, k_t, kk_t, ka_t = seg(_TR_R), seg(_TR_W), seg(_TR_K), seg(_TR_KK), seg(_TR_KA)
    for u in range(units):
        v_idx = h * RWKV_N + v0 + u
        blk = slice(u * RWKV_N, (u + 1) * RWKV_N)
        s = rs0_ref[blk, :]
        sa = jnp.sum(s * (-kk_t), axis=0, keepdims=True)
        s_new = s * w_t + sa * ka_t + row(t_rwkv, _TR_V + v_idx) * k_t
        o_rwkv_t[pl.ds(v_idx, 1), :] = jnp.sum(s_new * r_t, axis=0, keepdims=True)
        rs_out_ref[blk, :] = s_new

    units = gs0_ref.shape[0] // GLA_DV
    per_head = GLA_DK // units
    h = i // per_head
    k0 = (i % per_head) * units
    v_t = rows(t_gla, _TG_V + h * GLA_DV, GLA_DV)
    acc = jnp.zeros((GLA_DV, LANES), F32)
    for u in range(units):
        k_idx = h * GLA_DK + k0 + u
        blk = slice(u * GLA_DV, (u + 1) * GLA_DV)
        s_new = gs0_ref[blk, :] * row(t_gla, _TG_G + k_idx) + row(t_gla, _TG_K + k_idx) * v_t
        acc = acc + row(t_gla, _TG_Q + k_idx) * s_new
        gs_out_ref[blk, :] = s_new
    o_rows = pl.ds(pl.multiple_of(h * GLA_DV, GLA_DV), GLA_DV)
    o_gla_t[o_rows, :] = o_gla_t[o_rows, :] + acc

    @pl.when(i == pl.num_programs(0) - 1)
    def _():
        _, (act, _, _) = ssd_tokens()
        y_ssd_ref[...] = _ssd_finish(_load_t(o_ssd_t, SSD_W), act[:, 0:SSD_W], ssd_ref[:, 0:SSD_W].astype(F32),
                                     dexp_ref[...], sng_ref[...]).astype(y_ssd_ref.dtype)
        r, k2, v, _, _, gate, _ = rwkv_tokens()
        y_rwkv_ref[...] = _rwkv_finish(_load_t(o_rwkv_t, RWKV_W), r, k2, v, gate, rk_ref[...], lng_ref[...],
                                       lnb_ref[...], ones_ref[...], ravg_ref[...]).astype(y_rwkv_ref.dtype)
        y_gla_ref[...] = _gla_finish(_load_t(o_gla_t, GLA_W), gla_ref[:, 2 * hk + GLA_W:].astype(F32),
                                     gng_ref[...], gavg_ref[...]).astype(y_gla_ref.dtype)


def _mixer_step(ssd_slab, small, rw, gla_slab, conv0, shift0, h0, rs0, gs0, ssd_p, rwkv_p, gla_p):
    bsz = ssd_slab.shape[0]
    assert bsz == LANES, "the single-token mixer keeps exactly one vreg row of batch entries on the lanes"
    flat_t = lambda s: s.reshape(bsz, -1).T
    h0, rs0, gs0 = flat_t(h0), flat_t(rs0), flat_t(gs0)
    consts = (tuple(ssd_p) + tuple(rwkv_p) + tuple(gla_p)
              + (_seg_ones(RWKV_W, RWKV_N), _seg_ones(RWKV_W, RWKV_N, 1.0 / RWKV_N),
                 _seg_ones(GLA_W, GLA_DV, 1.0 / GLA_DV)))
    full = lambda a: pl.BlockSpec(a.shape, lambda i: (0,) * a.ndim)
    cols = lambda a: pl.BlockSpec((a.shape[0] // STEP_GRID, bsz), lambda i: (i, 0))
    tokens = (ssd_slab, small, rw, gla_slab, conv0, shift0)
    outs = pl.pallas_call(
        _step_kernel,
        grid=(STEP_GRID,),
        in_specs=[full(a) for a in tokens] + [cols(h0), cols(rs0), cols(gs0)] + [full(a) for a in consts],
        out_specs=[full(jax.ShapeDtypeStruct((bsz, w), BF16)) for w in (SSD_W, RWKV_W, GLA_W)]
                  + [full(conv0), cols(h0), cols(rs0), cols(gs0)],
        out_shape=[jax.ShapeDtypeStruct((bsz, SSD_W), BF16),
                   jax.ShapeDtypeStruct((bsz, RWKV_W), BF16),
                   jax.ShapeDtypeStruct((bsz, GLA_W), BF16),
                   jax.ShapeDtypeStruct(conv0.shape, F32),
                   jax.ShapeDtypeStruct(h0.shape, F32),
                   jax.ShapeDtypeStruct(rs0.shape, F32),
                   jax.ShapeDtypeStruct(gs0.shape, F32)],
        scratch_shapes=[pltpu.VMEM((SSD_CONV_DIM + 2 * SMALL_W, LANES), F32),
                        pltpu.VMEM((6 * RWKV_W, LANES), F32),
                        pltpu.VMEM((3 * GLA_HK + GLA_W, LANES), F32),
                        pltpu.VMEM((SSD_W, LANES), F32),
                        pltpu.VMEM((RWKV_W, LANES), F32),
                        pltpu.VMEM((GLA_W, LANES), F32)],
        compiler_params=_cparams("arbitrary"),
        name="mixer_step",
    )(*tokens, h0, rs0, gs0, *consts)
    y_ssd, y_rwkv, y_gla, conv_new, h_new, rs_new, gs_new = outs
    return (y_ssd, y_rwkv, y_gla, conv_new.reshape(bsz, SSD_CONV_W - 1, SSD_CONV_DIM),
            h_new.T.reshape(bsz, SSD_H, SSD_P, SSD_N), rs_new.T.reshape(bsz, RWKV_H, RWKV_N, RWKV_N),
            gs_new.T.reshape(bsz, GLA_H, GLA_DK, GLA_DV))


def _mod_rows(mod_ref, rows):
    return mod_ref[0] if mod_ref.shape[1] == 1 else mod_ref[0, rows, :]


def _row_parts(tm):
    return 2 if tm % 256 == 0 else 1


def _outproj_kernel(ys_ref, yr_ref, yg_ref, x_ref, gt_ref, sc_ref, sh_ref, w_ref, g_ref, b_ref, rt_ref,
                    o_ref, logit_ref):
    tm = x_ref.shape[1]
    parts = _row_parts(tm)

    def chain(r):
        rows = slice(r * tm // parts, (r + 1) * tm // parts)
        m = (jnp.dot(ys_ref[0, rows, :], w_ref[0:SSD_W, :], preferred_element_type=F32)
             + jnp.dot(yr_ref[0, rows, :], w_ref[SSD_W:SSD_W + RWKV_W, :], preferred_element_type=F32)
             + jnp.dot(yg_ref[0, rows, :], w_ref[SSD_W + RWKV_W:, :], preferred_element_type=F32))
        yield
        u = ALPHA * x_ref[0, rows, :] + (1.0 + _mod_rows(gt_ref, rows)) * m
        x1 = _layernorm(u, g_ref[...], b_ref[...])
        o_ref[0, rows, :] = x1
        h = x1 * (1.0 + _mod_rows(sc_ref, rows)) + _mod_rows(sh_ref, rows)
        logit_ref[0, rows, :] = _mm_hi(h, rt_ref[...])

    _interleave([chain(r) for r in range(parts)], skew=1)


def _outproj(y_ssd, y_rwkv, y_gla, x, gt, sc2, sh2, w_out, ln_g, ln_b, router_pad, tm):
    bsz, length, _ = x.shape
    tok = lambda width: pl.BlockSpec((1, tm, width), lambda b, i: (b, i, 0))
    vec = pl.BlockSpec((1, D_MODEL), lambda b, i: (0, 0))
    return pl.pallas_call(
        _outproj_kernel,
        grid=(bsz, length // tm),
        in_specs=[tok(SSD_W), tok(RWKV_W), tok(GLA_W), tok(D_MODEL),
                  _mod_spec(gt, tm), _mod_spec(sc2, tm), _mod_spec(sh2, tm),
                  pl.BlockSpec((D_MODEL, D_MODEL), lambda b, i: (0, 0)), vec, vec,
                  pl.BlockSpec((D_MODEL, LANES), lambda b, i: (0, 0))],
        out_specs=[tok(D_MODEL), tok(LANES)],
        out_shape=[jax.ShapeDtypeStruct((bsz, length, D_MODEL), F32),
                   jax.ShapeDtypeStruct((bsz, length, LANES), F32)],
        compiler_params=_cparams("arbitrary", "arbitrary"),
        name="outproj_ln",
    )(y_ssd, y_rwkv, y_gla, x, gt, sc2, sh2, w_out, ln_g, ln_b, router_pad)


def _ffn_kernel(ys_ref, yr_ref, yg_ref, x_ref, gt1_ref, sc_ref, sh_ref, gt2_ref, wo_ref, g1_ref, b1_ref,
                wg_ref, wu_ref, wd_ref, g2_ref, b2_ref, o_ref):
    tm = x_ref.shape[1]
    parts = _row_parts(tm)

    def chain(r):
        rows = slice(r * tm // parts, (r + 1) * tm // parts)
        m = (jnp.dot(ys_ref[0, rows, :], wo_ref[0:SSD_W, :], preferred_element_type=F32)
             + jnp.dot(yr_ref[0, rows, :], wo_ref[SSD_W:SSD_W + RWKV_W, :], preferred_element_type=F32)
             + jnp.dot(yg_ref[0, rows, :], wo_ref[SSD_W + RWKV_W:, :], preferred_element_type=F32))
        yield
        x1 = _layernorm(ALPHA * x_ref[0, rows, :] + (1.0 + _mod_rows(gt1_ref, rows)) * m,
                        g1_ref[...], b1_ref[...])
        h = (x1 * (1.0 + _mod_rows(sc_ref, rows)) + _mod_rows(sh_ref, rows)).astype(BF16)
        yield
        gate = jnp.dot(h, wg_ref[...], preferred_element_type=F32)
        up = jnp.dot(h, wu_ref[...], preferred_element_type=F32)
        yield
        f = jnp.dot((_silu(gate) * up).astype(BF16), wd_ref[...], preferred_element_type=F32)
        yield
        u = ALPHA * x1 + (1.0 + _mod_rows(gt2_ref, rows)) * f
        o_ref[0, rows, :] = _layernorm(u, g2_ref[...], b2_ref[...])

    _interleave([chain(r) for r in range(parts)], skew=1)


def _outproj_ffn(y_ssd, y_rwkv, y_gla, x, gt1, sc2, sh2, gt2, w_out, ln1_g, ln1_b, wg, wu, wd, ln2_g, ln2_b, tm):
    bsz, length, _ = x.shape
    tok = lambda width: pl.BlockSpec((1, tm, width), lambda b, i: (b, i, 0))
    vec = pl.BlockSpec((1, D_MODEL), lambda b, i: (0, 0))
    resident = lambda a: pl.BlockSpec(a.shape, lambda b, i: (0, 0), pipeline_mode=pl.Buffered(1))
    return pl.pallas_call(
        _ffn_kernel,
        grid=(bsz, length // tm),
        in_specs=[tok(SSD_W), tok(RWKV_W), tok(GLA_W), tok(D_MODEL),
                  _mod_spec(gt1, tm), _mod_spec(sc2, tm), _mod_spec(sh2, tm), _mod_spec(gt2, tm),
                  resident(w_out), vec, vec, resident(wg), resident(wu), resident(wd), vec, vec],
        out_specs=tok(D_MODEL),
        out_shape=jax.ShapeDtypeStruct((bsz, length, D_MODEL), F32),
        compiler_params=_cparams("arbitrary", "arbitrary"),
        name="outproj_ffn_ln",
    )(y_ssd, y_rwkv, y_gla, x, gt1, sc2, sh2, gt2, w_out, ln1_g, ln1_b, wg, wu, wd, ln2_g, ln2_b)


def _moe_kernel(x_ref, sc_ref, sh_ref, gt_ref, logit_ref, wg_ref, wu_ref, wd_ref, g_ref, b_ref, tri_ref, o_ref,
                h_ref, rank_t_ref, comb_t_ref, cnt_ref, acc_ref, *, half, cap):
    e = pl.program_id(2)
    tm = 2 * half

    @pl.when(e == 0)
    def _():
        h_ref[...] = (x_ref[0] * (1.0 + sc_ref[0]) + sh_ref[0]).astype(BF16)
        acc_ref[...] = jnp.zeros_like(acc_ref)
        lane = _iota((tm, LANES), 1).astype(F32)
        logits = jnp.where(lane < N_EXPERTS, logit_ref[0], -jnp.inf)
        m1 = jnp.max(logits, axis=-1, keepdims=True)
        i1 = jnp.min(jnp.where(logits == m1, lane, float(LANES)), axis=-1, keepdims=True)
        rest = jnp.where(lane == i1, -jnp.inf, logits)
        m2 = jnp.max(rest, axis=-1, keepdims=True)
        i2 = jnp.min(jnp.where(rest == m2, lane, float(LANES)), axis=-1, keepdims=True)
        e2 = jnp.exp(m2 - m1)
        den = 1.0 + e2
        comb = jnp.where(lane == i1, 1.0 / den, 0.0) + jnp.where(lane == i2, e2 / den, 0.0)
        sel = (lane == i1) | (lane == i2)
        sel_f = sel.astype(F32)
        ranks = []
        for hf in range(2):
            s = sel_f[hf * half:(hf + 1) * half, :]
            before = jnp.dot(tri_ref[...], s.astype(BF16), preferred_element_type=F32)
            ranks.append(jnp.where(sel[hf * half:(hf + 1) * half, :], before, -1.0))
            cnt_ref[hf:hf + 1, :] = jnp.sum(s, axis=0, keepdims=True)
        rank = jnp.concatenate(ranks, axis=0)
        for blk in range(tm // LANES):
            rows = slice(blk * LANES, (blk + 1) * LANES)
            rank_t_ref[:, rows] = rank[rows, :].T
            comb_t_ref[:, rows] = comb[rows, :].T

    rank_t_e = rank_t_ref[pl.ds(e, 1), :]
    comb_t_e = comb_t_ref[pl.ds(e, 1), :]
    n_max = jnp.max(jnp.where(_iota((2, LANES), 1) == e, cnt_ref[...], 0.0))
    n_pass = (n_max.astype(jnp.int32) + (cap - 1)) // cap

    def one_pass(p, carry):
        slot = _iota((cap, 1), 0).astype(F32) + (p * cap).astype(F32)
        picks, xs, gates = [], [], []
        for hf in range(2):
            rows = slice(hf * half, (hf + 1) * half)
            hit = rank_t_e[:, rows] == slot
            pick = hit.astype(BF16)
            picks.append(pick)
            xs.append(jnp.dot(pick, h_ref[rows, :], preferred_element_type=F32).astype(BF16))
            gates.append(jnp.sum(jnp.where(hit, comb_t_e[:, rows], 0.0), axis=1, keepdims=True))
        xc = jnp.concatenate(xs, axis=0)
        a = _silu(jnp.dot(xc, wg_ref[0], preferred_element_type=F32)) * jnp.dot(
            xc, wu_ref[0], preferred_element_type=F32)
        out_e = jnp.dot(a.astype(BF16), wd_ref[0], preferred_element_type=F32)
        for hf in range(2):
            rows = slice(hf * half, (hf + 1) * half)
            weighted = (out_e[hf * cap:(hf + 1) * cap, :] * gates[hf]).astype(BF16)
            acc_ref[rows, :] += lax.dot_general(picks[hf], weighted, (((0,), (0,)), ((), ())),
                                                preferred_element_type=F32)
        return carry

    one_pass(jnp.int32(0), 0)
    lax.fori_loop(1, n_pass, one_pass, 0)

    @pl.when(e == pl.num_programs(2) - 1)
    def _():
        u = ALPHA * x_ref[0] + (1.0 + gt_ref[0]) * acc_ref[...]
        o_ref[0] = _layernorm(u, g_ref[...], b_ref[...])


def _moe(x, sc, sh, gt, logits, wg, wu, wd, ln_g, ln_b, tm):
    bsz, length, _ = x.shape
    assert tm % LANES == 0
    half = tm // 2
    cap = MOE_CAP_ROWS if half >= 2 * MOE_CAP_ROWS else half
    tri = (jnp.arange(half)[None, :] < jnp.arange(half)[:, None]).astype(BF16)
    tok = pl.BlockSpec((1, tm, D_MODEL), lambda b, i, e: (b, i, 0))
    vec = pl.BlockSpec((1, D_MODEL), lambda b, i, e: (0, 0))
    wspec = pl.BlockSpec((1, D_MODEL, D_MODEL), lambda b, i, e: (e, 0, 0))
    return pl.pallas_call(
        functools.partial(_moe_kernel, half=half, cap=cap),
        grid=(bsz, length // tm, N_EXPERTS),
        in_specs=[tok, _mod_spec(sc, tm), _mod_spec(sh, tm), _mod_spec(gt, tm),
                  pl.BlockSpec((1, tm, LANES), lambda b, i, e: (b, i, 0)),
                  wspec, wspec, wspec, vec, vec, _const_spec(tri)],
        out_specs=tok,
        out_shape=jax.ShapeDtypeStruct((bsz, length, D_MODEL), F32),
        scratch_shapes=[pltpu.VMEM((tm, D_MODEL), BF16), pltpu.VMEM((LANES, tm), F32),
                        pltpu.VMEM((LANES, tm), F32), pltpu.VMEM((2, LANES), F32),
                        pltpu.VMEM((tm, D_MODEL), F32)],
        compiler_params=_cparams("arbitrary", "arbitrary", "arbitrary"),
        name="moe_ln",
    )(x, sc, sh, gt, logits, wg, wu, wd, ln_g, ln_b, tri)


def _pad_lanes(vec, offset, width=SMALL_W):
    out = jnp.zeros((1, width), F32)
    return out.at[0, offset:offset + vec.shape[0]].set(vec)


def _layer_params(p, l):
    w_in = p["w_in"][l]
    off = [0]
    for s in (SSD_W, SSD_CONV_DIM, SSD_H, RWKV_FEAT, GLA_HK, GLA_HK, GLA_W, GK_LORA, GLA_W):
        off.append(off[-1] + s)
    piece = lambda i: w_in[:, off[i]:off[i + 1]]
    small = jnp.zeros((D_MODEL, SMALL_W), F32)
    small = small.at[:, DT_OFF:DT_OFF + SSD_H].set(piece(2)).at[:, GLO_OFF:GLO_OFF + GK_LORA].set(piece(7))
    w_pad = jnp.concatenate([piece(0), piece(1), small, piece(3), piece(4), piece(5), piece(6), piece(8)],
                            axis=1).astype(BF16)
    ssd_p = (p["ssd_conv_w"][l], p["ssd_conv_b"][l][None, :],
             _pad_lanes(p["ssd_dt_bias"][l], DT_OFF), _pad_lanes(p["ssd_a_log"][l], DT_OFF),
             jnp.repeat(p["ssd_d"][l], SSD_P)[None, :], p["ssd_norm_g"][l][None, :])
    lora_w = jnp.zeros((LANES, 3 * RWKV_W), F32)
    lora_w = (lora_w.at[0:32, 0:RWKV_W].set(p["rwkv_w2"][l])
              .at[32:64, RWKV_W:2 * RWKV_W].set(p["rwkv_a2"][l])
              .at[64:128, 2 * RWKV_W:].set(p["rwkv_g2"][l])).astype(BF16)
    row = lambda name: p[name][l].reshape(1, -1)
    rwkv_p = (row("rwkv_mu"), lora_w, row("rwkv_w0"), row("rwkv_a0"), row("rwkv_k_k"), row("rwkv_k_a"),
              row("rwkv_r_k"), row("rwkv_ln_g"), row("rwkv_ln_b"))
    gk2_pad = jnp.zeros((SMALL_W, GLA_HK), F32).at[GLO_OFF:GLO_OFF + GK_LORA].set(p["gla_w_gk2"][l])
    gla_p = (gk2_pad, row("gla_b_gk"), jnp.tile(p["gla_norm_g"][l], GLA_H)[None, :])
    return w_pad, ssd_p, rwkv_p, gla_p


def _block_diag_inv(s_bd, h):
    b, hr, hc = s_bd.shape
    r, c = hr // h, hc // h
    s = s_bd.reshape(b, h, r, h, c)
    return jnp.stack([s[:, i, :, i, :] for i in range(h)], axis=1)


def _ssd_state_unpack(h_bd):
    b = h_bd.shape[0]
    s = h_bd.reshape(b, SSD_G, SSD_N, SSD_H, SSD_P)
    per_group = SSD_H // SSD_G
    heads = [s[:, h // per_group, :, h, :] for h in range(SSD_H)]
    return jnp.swapaxes(jnp.stack(heads, axis=1), 2, 3)


def _tail(x, mod_l, l, p, y_ssd, y_rwkv, y_gla, tm):
    sh1, sc1, gt1, sh2, sc2, gt2 = mod_l
    row = lambda name: p[name][l].reshape(1, -1)
    w_out = p["w_out"][l].astype(BF16)
    i = l // 2
    if l % 2 == 0:
        x = _outproj_ffn(y_ssd, y_rwkv, y_gla, x, gt1, sc2, sh2, gt2, w_out, row("ln_mix_g"), row("ln_mix_b"),
                         p["ffn_w_gate"][i].astype(BF16), p["ffn_w_up"][i].astype(BF16),
                         p["ffn_w_down"][i].astype(BF16), row("ln_ffn_g"), row("ln_ffn_b"), tm)
    else:
        router_pad = jnp.zeros((D_MODEL, LANES), F32).at[:, :N_EXPERTS].set(p["moe_router"][i])
        x, logits = _outproj(y_ssd, y_rwkv, y_gla, x, gt1, sc2, sh2, w_out, row("ln_mix_g"), row("ln_mix_b"),
                             router_pad, tm)
        x = _moe(x, sc2, sh2, gt2, logits, p["moe_w_gate"][i].astype(BF16), p["moe_w_up"][i].astype(BF16),
                 p["moe_w_down"][i].astype(BF16), row("ln_ffn_g"), row("ln_ffn_b"), min(2 * tm, x.shape[1]))
    return x


def _forward(x_prompt, x_sample, c_prompt, c_sample, states, p):
    bp, seq, _ = x_prompt.shape
    bs = x_sample.shape[0]
    state_ssd, state_conv, state_rwkv, state_shift, state_gla = states
    mod = _ada(jnp.concatenate([c_prompt, c_sample], axis=0), p["w_ada"], p["b_ada"])

    xp = x_prompt
    xs = x_sample.reshape(1, bs, D_MODEL)
    tm_p = min(512, seq)
    outs_p = [[] for _ in range(5)]
    outs_s = [[] for _ in range(5)]
    for l in range(DEPTH):
        w_pad, ssd_p, rwkv_p, gla_p = _layer_params(p, l)
        mods = jnp.split(mod[l], 6, axis=-1)
        mod_p = [m[:bp, None, :] for m in mods]
        mod_s = [m[None, bp:, :] for m in mods]

        ssd_slab, small, rw, gla_slab = _inproj(xp, mod_p[1], mod_p[0], w_pad, tm_p)
        y_ssd, conv_new, h_new = _ssd_prompt(ssd_slab, small, *ssd_p)
        h_new = _ssd_state_unpack(h_new)
        y_rwkv, shift_new, rs_bd = _rwkv_prompt(rw, *rwkv_p)
        y_gla, gs_bd = _gla_prompt(gla_slab, small, *gla_p)
        xp = _tail(xp, mod_p, l, p, y_ssd, y_rwkv, y_gla, tm_p)
        for acc, s in zip(outs_p, (h_new, conv_new, _block_diag_inv(rs_bd, RWKV_H),
                                   shift_new.reshape(bp, RWKV_FEAT), _block_diag_inv(gs_bd, GLA_H))):
            acc.append(s)

        ssd_slab, small, rw, gla_slab = _inproj(xs, mod_s[1], mod_s[0], w_pad, bs)
        y_ssd, y_rwkv, y_gla, conv_new, h_new, rs_new, gs_new = _mixer_step(
            ssd_slab[0], small[0], rw[0], gla_slab[0], state_conv[l].reshape(bs, -1), state_shift[l],
            state_ssd[l], state_rwkv[l], state_gla[l], ssd_p, rwkv_p, gla_p)
        xs = _tail(xs, mod_s, l, p, y_ssd[None], y_rwkv[None], y_gla[None], bs)
        for acc, s in zip(outs_s, (h_new, conv_new, rs_new, rw[0].astype(F32), gs_new)):
            acc.append(s)

    stack = lambda accs: tuple(jnp.stack(a, axis=0) for a in accs)
    return (xp, xs.reshape(bs, 1, D_MODEL)) + stack(outs_p) + stack(outs_s)


def kernel(x_prompt, x_sample, c_prompt, c_sample, state_ssd, state_ssd_conv, state_rwkv, state_rwkv_shift, state_gla, w_ada, b_ada, w_in, w_out, ssd_conv_w, ssd_conv_b, ssd_dt_bias, ssd_a_log, ssd_d, ssd_norm_g, rwkv_mu, rwkv_w0, rwkv_w2, rwkv_a0, rwkv_a2, rwkv_g2, rwkv_k_k, rwkv_k_a, rwkv_r_k, rwkv_ln_g, rwkv_ln_b, gla_w_gk2, gla_b_gk, gla_norm_g, ln_mix_g, ln_mix_b, ln_ffn_g, ln_ffn_b, ffn_w_gate, ffn_w_up, ffn_w_down, moe_router, moe_w_gate, moe_w_up, moe_w_down):
    p = dict(w_ada=w_ada, b_ada=b_ada, w_in=w_in, w_out=w_out, ssd_conv_w=ssd_conv_w, ssd_conv_b=ssd_conv_b,
             ssd_dt_bias=ssd_dt_bias, ssd_a_log=ssd_a_log, ssd_d=ssd_d, ssd_norm_g=ssd_norm_g,
             rwkv_mu=rwkv_mu, rwkv_w0=rwkv_w0, rwkv_w2=rwkv_w2, rwkv_a0=rwkv_a0, rwkv_a2=rwkv_a2,
             rwkv_g2=rwkv_g2, rwkv_k_k=rwkv_k_k, rwkv_k_a=rwkv_k_a, rwkv_r_k=rwkv_r_k,
             rwkv_ln_g=rwkv_ln_g, rwkv_ln_b=rwkv_ln_b, gla_w_gk2=gla_w_gk2, gla_b_gk=gla_b_gk,
             gla_norm_g=gla_norm_g, ln_mix_g=ln_mix_g, ln_mix_b=ln_mix_b, ln_ffn_g=ln_ffn_g,
             ln_ffn_b=ln_ffn_b, ffn_w_gate=ffn_w_gate, ffn_w_up=ffn_w_up, ffn_w_down=ffn_w_down,
             moe_router=moe_router, moe_w_gate=moe_w_gate, moe_w_up=moe_w_up, moe_w_down=moe_w_down)
    states = (state_ssd, state_ssd_conv, state_rwkv, state_rwkv_shift, state_gla)
    return _forward(x_prompt, x_sample, c_prompt, c_sample, states, p)
```

```python
import functools

import jax
import jax.numpy as jnp
from jax import lax
from jax.experimental import pallas as pl
from jax.experimental.pallas import tpu as pltpu

F32 = jnp.float32
BF16 = jnp.bfloat16

D_MODEL = 1024
DEPTH = 2
SSD_W = 512
SSD_H = 8
SSD_P = 64
SSD_N = 64
SSD_G = 2
SSD_CONV_W = 4
SSD_CONV_DIM = 768
RWKV_W = 256
RWKV_H = 4
RWKV_N = 64
RWKV_FEAT = 896
RWKV_GN_EPS = RWKV_N * 1e-5
GLA_W = 256
GLA_H = 4
GLA_DK = 32
GLA_DV = 64
GLA_HK = GLA_H * GLA_DK
GK_LORA = 16
GATE_NORMALIZER = 16.0
F_DENSE = 2816
N_EXPERTS = 8
ALPHA = (2.0 * DEPTH) ** 0.25
LN_EPS = 1e-5
RMS_EPS = 1e-6

LANES = 128
SMALL_W = LANES
DT_OFF = 0
GLO_OFF = 8
SSD_SLAB = SSD_W + SSD_CONV_DIM
GLA_SLAB = 2 * GLA_HK + 2 * GLA_W
IN_PAD = SSD_SLAB + SMALL_W + RWKV_FEAT + GLA_SLAB
SSD_CHUNK = 128
SSD_CARRY = 8
GLA_CHUNK = 128
GLA_SUB = 8
RWKV_CHUNK = 64
SSD_NB, RWKV_NB, GLA_NB = 4, 8, 8
RWKV_SKEW = 1
STAT_TERMS = 1
MOE_CAP_ROWS = 160
STEP_GRID = 16
VMEM_LIMIT = 56 * 1024 * 1024


def _cparams(*sem):
    return pltpu.CompilerParams(dimension_semantics=sem, vmem_limit_bytes=VMEM_LIMIT)


def _mm(a, b):
    return jnp.dot(a.astype(BF16), b.astype(BF16), preferred_element_type=F32)


def _mm_nt(a, b):
    return lax.dot_general(a.astype(BF16), b.astype(BF16), (((1,), (1,)), ((), ())),
                           preferred_element_type=F32)


def _split_bf16(x, terms):
    parts = []
    for _ in range(terms):
        p = x.astype(BF16)
        parts.append(p)
        x = x - p.astype(F32)
    return parts


def _mm_sel(sel, x, terms=3):
    return sum(jnp.dot(sel, p, preferred_element_type=F32) for p in _split_bf16(x, terms))


def _mm_xsel(x, sel, terms=2):
    return sum(jnp.dot(p, sel, preferred_element_type=F32) for p in _split_bf16(x, terms))


def _mm_hi(a, b):
    a_hi, a_lo = _split_bf16(a, 2)
    b_hi, b_lo = _split_bf16(b, 2)
    dot = lambda x, y: jnp.dot(x, y, preferred_element_type=F32)
    return dot(a_hi, b_hi) + (dot(a_hi, b_lo) + dot(a_lo, b_hi))


def _sigmoid(x):
    return 1.0 / (1.0 + jnp.exp(-x))


def _silu(x):
    return x * _sigmoid(x)


def _softplus(x):
    return jnp.maximum(x, 0.0) + jnp.log(1.0 + jnp.exp(-jnp.abs(x)))


def _iota(shape, dim):
    return lax.broadcasted_iota(jnp.int32, shape, dim)


def _layernorm(u, g, b):
    mu = jnp.mean(u, axis=-1, keepdims=True)
    d = u - mu
    var = jnp.mean(d * d, axis=-1, keepdims=True)
    return d * lax.rsqrt(var + LN_EPS) * g + b


def _tri_incl(n):
    return (jnp.arange(n)[None, :] <= jnp.arange(n)[:, None]).astype(BF16)


def _seg_ones(n, seg, scale=1.0):
    idx = jnp.arange(n) // seg
    return jnp.where(idx[:, None] == idx[None, :], scale, 0.0).astype(BF16)


def _interleave(chains, skew=0):
    chains = list(chains)
    done = [False] * len(chains)
    rnd = 0
    while not all(done):
        for i, ch in enumerate(chains):
            if done[i] or rnd < i * skew:
                continue
            try:
                next(ch)
            except StopIteration:
                done[i] = True
        rnd += 1


def _const_spec(arr):
    nd = arr.ndim
    return pl.BlockSpec(arr.shape, lambda *_: (0,) * nd)


def _ada_kernel(c_ref, w_ref, b_ref, o_ref):
    o_ref[0] = _mm(_silu(c_ref[...]), w_ref[0]) + b_ref[0]


def _ada(c_all, w_ada, b_ada):
    rows = c_all.shape[0]
    tn = 1536
    return pl.pallas_call(
        _ada_kernel,
        grid=(DEPTH, 6 * D_MODEL // tn),
        in_specs=[pl.BlockSpec((rows, D_MODEL), lambda l, j: (0, 0)),
                  pl.BlockSpec((1, D_MODEL, tn), lambda l, j: (l, 0, j)),
                  pl.BlockSpec((1, 1, tn), lambda l, j: (l, 0, j))],
        out_specs=pl.BlockSpec((1, rows, tn), lambda l, j: (l, 0, j)),
        out_shape=jax.ShapeDtypeStruct((DEPTH, rows, 6 * D_MODEL), F32),
        compiler_params=_cparams("arbitrary", "arbitrary"),
        name="ada",
    )(c_all, w_ada, b_ada.reshape(DEPTH, 1, 6 * D_MODEL))


def _inproj_kernel(x_ref, sc_ref, sh_ref, w_ref, o_ssd, o_small, o_rwkv, o_gla):
    h = (x_ref[0] * (1.0 + sc_ref[0]) + sh_ref[0]).astype(BF16)
    off = 0
    for o_ref in (o_ssd, o_small, o_rwkv, o_gla):
        width = o_ref.shape[2]
        o_ref[0] = jnp.dot(h, w_ref[:, off:off + width], preferred_element_type=F32).astype(o_ref.dtype)
        off += width


def _mod_spec(mod, tm):
    if mod.shape[1] == 1:
        return pl.BlockSpec((1, 1, D_MODEL), lambda b, i, *_: (b, 0, 0))
    return pl.BlockSpec((1, tm, D_MODEL), lambda b, i, *_: (b, i, 0))


def _inproj(x, sc, sh, w_pad, tm):
    bsz, length, _ = x.shape
    tok = lambda width: pl.BlockSpec((1, tm, width), lambda b, i: (b, i, 0))
    return pl.pallas_call(
        _inproj_kernel,
        grid=(bsz, length // tm),
        in_specs=[tok(D_MODEL), _mod_spec(sc, tm), _mod_spec(sh, tm),
                  pl.BlockSpec((D_MODEL, IN_PAD), lambda b, i: (0, 0))],
        out_specs=[tok(SSD_SLAB), tok(SMALL_W), tok(RWKV_FEAT), tok(GLA_SLAB)],
        out_shape=[jax.ShapeDtypeStruct((bsz, length, SSD_SLAB), BF16),
                   jax.ShapeDtypeStruct((bsz, length, SMALL_W), F32),
                   jax.ShapeDtypeStruct((bsz, length, RWKV_FEAT), BF16),
                   jax.ShapeDtypeStruct((bsz, length, GLA_SLAB), BF16)],
        compiler_params=_cparams("arbitrary", "arbitrary"),
        name="inproj",
    )(x, sc, sh, w_pad)


def _ssd_conv_act(xbc, conv_taps, cw, cb):
    conv = cb + xbc * cw[SSD_CONV_W - 1:SSD_CONV_W, :]
    for i in range(SSD_CONV_W - 1):
        conv = conv + conv_taps[i] * cw[i:i + 1, :]
    return _silu(conv)


def _ssd_dt(small, dtb, alog):
    dt = _softplus(small + dtb)
    return dt, dt * (-jnp.exp(alog))


def _ssd_finish(y, xs, z, dexp, ng):
    gated = (y + xs * dexp) * _silu(z)
    ms = jnp.mean(gated * gated, axis=-1, keepdims=True)
    return gated * lax.rsqrt(ms + RMS_EPS) * ng


def _ssd_kernel(slab_ref, small_ref, cw_ref, cb_ref, dtb_ref, alog_ref, dexp_ref, ng_ref, tri_ref,
                head_x_ref, group_mask_ref, pair_mask_ref, y_ref, conv_out_ref, h_out_ref, ext_ref, h_ref):
    c = pl.program_id(1)
    n_chunks = pl.num_programs(1)
    ch = SSD_CHUNK
    pad = ext_ref.shape[1] - ch
    causal = _iota((ch, ch), 1) <= _iota((ch, ch), 0)

    nb = slab_ref.shape[0]

    @pl.when(c == 0)
    def _():
        ext_ref[:, 0:pad, :] = jnp.zeros((nb, pad, SSD_CONV_DIM), F32)
        h_ref[...] = jnp.zeros_like(h_ref)

    def chain(j):
        z = slab_ref[j, :, 0:SSD_W].astype(F32)
        xbc = slab_ref[j, :, SSD_W:].astype(F32)

        ext_ref[j, pad:pad + ch, :] = xbc
        first = pad - (SSD_CONV_W - 1)
        taps = [ext_ref[j, first + i:first + i + ch, :] for i in range(SSD_CONV_W - 1)]
        act = _ssd_conv_act(xbc, taps, cw_ref[...], cb_ref[...])
        ext_ref[j, 0:pad, :] = ext_ref[j, ch:ch + pad, :]
        dt, dta = _ssd_dt(small_ref[j], dtb_ref[...], alog_ref[...])

        xs = act[:, 0:SSD_W]
        bs = act[:, SSD_W:SSD_W + SSD_G * SSD_N]
        cs = act[:, SSD_W + SSD_G * SSD_N:]
        yield
        cum = _mm_sel(tri_ref[...], dta)
        cum_t = cum.T
        dt_t = dt.T
        cum_x = _mm_xsel(cum, head_x_ref[...], 3)
        dt_x = _mm_xsel(dt, head_x_ref[...], 2)
        yield
        last_x = cum_x[ch - 1:ch, :]
        xs_tail = (xs * (jnp.exp(last_x - cum_x) * dt_x)).astype(BF16)
        h_bd = h_ref[j]
        y_state = jnp.dot(cs.astype(BF16), h_bd.astype(BF16), preferred_element_type=F32) * jnp.exp(cum_x)
        upd = jnp.dot(bs.T.astype(BF16), xs_tail, preferred_element_type=F32)
        h_ref[j] = h_bd * jnp.exp(last_x) + upd * group_mask_ref[...]
        yield

        lane_group = _iota((1, SSD_G * SSD_N), 1) // SSD_N
        xs_b = xs.astype(BF16)
        ys = []
        for g in range(SSD_G):
            cb = _mm_nt(jnp.where(lane_group == g, cs, 0.0), bs)
            yield
            for pair in range(SSD_H // SSD_G // 2):
                scores = []
                for hh in range(2):
                    lane = DT_OFF + g * (SSD_H // SSD_G) + 2 * pair + hh
                    decay = jnp.exp(jnp.where(causal, cum[:, lane:lane + 1] - cum_t[lane:lane + 1, :], -jnp.inf))
                    scores.append((cb * decay * dt_t[lane:lane + 1, :]).astype(BF16))
                blk = g * (SSD_H // SSD_G) // 2 + pair
                x_pair = xs_b[:, blk * LANES:(blk + 1) * LANES]
                x_bd = jnp.concatenate([x_pair, x_pair], axis=0) * pair_mask_ref[...]
                ys.append(jnp.dot(jnp.concatenate(scores, axis=1), x_bd, preferred_element_type=F32))
                yield

        y = jnp.concatenate(ys, axis=-1) + y_state
        y_ref[j] = _ssd_finish(y, xs, z, dexp_ref[...], ng_ref[...]).astype(y_ref.dtype)

    _interleave([chain(j) for j in range(nb)])

    @pl.when(c == n_chunks - 1)
    def _():
        for j in range(nb):
            conv_out_ref[j] = ext_ref[j, pad - (SSD_CONV_W - 1):pad, :]
            h_out_ref[j] = h_ref[j]


def _ssd_prompt(slab, small, cw, cb, dtb, alog, dexp, ng):
    bsz, length, _ = slab.shape
    nb = SSD_NB if bsz % SSD_NB == 0 else 1
    gn, hp = SSD_G * SSD_N, SSD_H * SSD_P
    lane_head = jnp.arange(hp) // SSD_P
    head_x = (jnp.arange(SMALL_W)[:, None] == DT_OFF + lane_head[None, :]).astype(BF16)
    group_mask = (jnp.arange(gn)[:, None] // SSD_N == lane_head[None, :] // (SSD_H // SSD_G)).astype(F32)
    pair_mask = (jnp.arange(2 * SSD_CHUNK)[:, None] // SSD_CHUNK == jnp.arange(2 * SSD_P)[None, :] // SSD_P
                 ).astype(BF16)
    consts = (cw, cb, dtb, alog, dexp, ng, _tri_incl(SSD_CHUNK), head_x, group_mask, pair_mask)
    return pl.pallas_call(
        _ssd_kernel,
        grid=(bsz // nb, length // SSD_CHUNK),
        in_specs=[pl.BlockSpec((nb, SSD_CHUNK, SSD_SLAB), lambda b, c: (b, c, 0)),
                  pl.BlockSpec((nb, SSD_CHUNK, SMALL_W), lambda b, c: (b, c, 0))]
                 + [_const_spec(a) for a in consts],
        out_specs=[pl.BlockSpec((nb, SSD_CHUNK, SSD_W), lambda b, c: (b, c, 0)),
                   pl.BlockSpec((nb, SSD_CONV_W - 1, SSD_CONV_DIM), lambda b, c: (b, 0, 0)),
                   pl.BlockSpec((nb, gn, hp), lambda b, c: (b, 0, 0))],
        out_shape=[jax.ShapeDtypeStruct((bsz, length, SSD_W), BF16),
                   jax.ShapeDtypeStruct((bsz, SSD_CONV_W - 1, SSD_CONV_DIM), F32),
                   jax.ShapeDtypeStruct((bsz, gn, hp), F32)],
        scratch_shapes=[pltpu.VMEM((nb, SSD_CARRY + SSD_CHUNK, SSD_CONV_DIM), F32),
                        pltpu.VMEM((nb, gn, hp), F32)],
        compiler_params=_cparams("arbitrary", "arbitrary"),
        name="ssd_scan",
    )(slab, small, *consts)


def _rwkv_token_math(rw, prev, mu, lora_w, w0, a0, k_k, k_a, head_ones):
    mix = rw + mu * (prev - rw)
    r = mix[:, 0:RWKV_W]
    k = mix[:, RWKV_W:2 * RWKV_W]
    v = mix[:, 2 * RWKV_W:3 * RWKV_W]
    lora = mix[:, 3 * RWKV_W:]
    lane = _iota(lora.shape, 1)
    act = jnp.where(lane < 32, jnp.tanh(lora), jnp.where(lane < 64, lora, _sigmoid(lora)))
    lo = _mm(act, lora_w)
    w_log = -_softplus(-(w0 + lo[:, 0:RWKV_W])) - 0.5
    logw = -jnp.exp(w_log)
    aic = _sigmoid(a0 + lo[:, RWKV_W:2 * RWKV_W])
    gate = lo[:, 2 * RWKV_W:]
    kkf = k * k_k
    kk = kkf * lax.rsqrt(_mm_xsel(kkf * kkf, head_ones, STAT_TERMS) + 1e-12)
    k2 = k * (1.0 + (aic - 1.0) * k_a)
    return r, k2, v, logw, aic, gate, kk


def _rwkv_finish(o, r, k2, v, gate, r_k, ln_g, ln_b, head_ones, head_avg):
    mu = _mm_xsel(o, head_avg, STAT_TERMS)
    d = o - mu
    var = _mm_xsel(d * d, head_avg, STAT_TERMS)
    on = d * lax.rsqrt(var + RWKV_GN_EPS) * ln_g + ln_b
    bonus = _mm_xsel(r * k2 * r_k, head_ones, STAT_TERMS) * v
    return (on + bonus) * gate


def _rwkv_kernel(rw_ref, mu_ref, lw_ref, w0_ref, a0_ref, kk_ref, ka_ref,
                 rk_ref, lng_ref, lnb_ref, ones_ref, avg_ref, tri_ref, strict_ref, incl_ref, eye_ref,
                 y_ref, shift_out_ref, s_out_ref, ext_ref, s_ref):
    c = pl.program_id(1)
    n_chunks = pl.num_programs(1)
    ch = RWKV_CHUNK
    nh = RWKV_H
    w = RWKV_W
    head_bd = ones_ref[...]
    head_bd_f = head_bd.astype(F32)
    tile4 = lambda x: jnp.concatenate([x] * nh, axis=0)

    nb = rw_ref.shape[0]

    @pl.when(c == 0)
    def _():
        ext_ref[:, 0:8, :] = jnp.zeros((nb, 8, RWKV_FEAT), F32)
        s_ref[...] = jnp.zeros_like(s_ref)

    def chain(j):
        rw = rw_ref[j].astype(F32)
        ext_ref[j, 8:8 + ch, :] = rw
        prev = ext_ref[j, 7:7 + ch, :]
        ext_ref[j, 7:8, :] = rw[ch - 1:ch, :]

        r, k2, v, logw, aic, gate, kk = _rwkv_token_math(
            rw, prev, mu_ref[...], lw_ref[...], w0_ref[...], a0_ref[...], kk_ref[...], ka_ref[...], head_bd)
        yield

        cumw = _mm_sel(tri_ref[...], logw)
        yield
        last = cumw[ch - 1:ch, :]
        inv_g = jnp.exp(-cumw)
        to_end = jnp.exp(last - cumw)
        kka = kk * aic
        a_t = (-kk * jnp.exp(cumw - logw)).astype(BF16)
        r_t = (r * jnp.exp(cumw)).astype(BF16)
        b_t = (kka * inv_g).astype(BF16)
        k_t = (k2 * inv_g).astype(BF16)
        bk_end = jnp.concatenate([kka * to_end, k2 * to_end], axis=0).astype(BF16)

        bd = lambda x: tile4(x.astype(BF16)) * head_bd
        nt = lambda x, y: lax.dot_general(x, y, (((1,), (1,)), ((), ())), preferred_element_type=F32)
        vb = v.astype(BF16)
        ar = jnp.concatenate([a_t, r_t], axis=0)
        prod = nt(ar, jnp.concatenate([bd(b_t), bd(k_t)], axis=0))
        a_ab = prod[0:ch, 0:w] * strict_ref[...]
        a_ak = prod[0:ch, w:2 * w] * strict_ref[...]
        a_rbk = (prod[ch:2 * ch, :] * incl_ref[...]).astype(BF16)
        yield
        s_bd = s_ref[j]
        uo = nt(ar, s_bd.astype(BF16))
        rhs = uo[0:ch, :] + jnp.dot(a_ak.astype(BF16), bd(vb), preferred_element_type=F32)
        yield

        t_inv = eye_ref[...] + a_ab
        x = a_ab
        x_bd = bd(x)
        power = 1
        while 2 * power < ch:
            x = jnp.dot(x.astype(BF16), x_bd, preferred_element_type=F32)
            x_bd = bd(x)
            yield
            t_inv = t_inv + jnp.dot(t_inv.astype(BF16), x_bd, preferred_element_type=F32)
            yield
            power *= 2

        p = jnp.dot(t_inv.astype(BF16), bd(rhs), preferred_element_type=F32)
        yield
        pb = p.astype(BF16)
        o = uo[ch:2 * ch, :] + jnp.dot(a_rbk, jnp.concatenate([bd(pb), bd(vb)], axis=0),
                                       preferred_element_type=F32)
        pv = jnp.concatenate([pb, vb], axis=0)
        yield

        upd = jnp.dot(pv.astype(F32).T.astype(BF16), bk_end, preferred_element_type=F32)
        s_ref[j] = s_bd * jnp.exp(last) + upd * head_bd_f
        yield

        y_ref[j] = _rwkv_finish(o, r, k2, v, gate, rk_ref[...], lng_ref[...], lnb_ref[...],
                                head_bd, avg_ref[...]).astype(y_ref.dtype)

    _interleave([chain(j) for j in range(nb)], skew=RWKV_SKEW)

    @pl.when(c == n_chunks - 1)
    def _():
        for j in range(nb):
            shift_out_ref[j] = ext_ref[j, 7:8, :]
            s_out_ref[j] = s_ref[j]


def _rwkv_consts():
    ch = RWKV_CHUNK
    assert ch == RWKV_N, "bd() reuses the head mask, which needs chunk rows == head width"
    t = jnp.arange(ch)[:, None]
    j = jnp.arange(2 * RWKV_W)[None, :] % ch
    strict = (j[:, :RWKV_W] < t).astype(F32)
    incl = (j <= t).astype(F32)
    eye = (j[:, :RWKV_W] == t).astype(F32)
    return (_seg_ones(RWKV_W, RWKV_N), _seg_ones(RWKV_W, RWKV_N, 1.0 / RWKV_N), _tri_incl(ch),
            strict, incl, eye)


def _rwkv_prompt(rw, *params):
    bsz, length, _ = rw.shape
    nb = RWKV_NB if bsz % RWKV_NB == 0 else 1
    consts = tuple(params) + _rwkv_consts()
    return pl.pallas_call(
        _rwkv_kernel,
        grid=(bsz // nb, length // RWKV_CHUNK),
        in_specs=[pl.BlockSpec((nb, RWKV_CHUNK, RWKV_FEAT), lambda b, c: (b, c, 0))]
                 + [_const_spec(a) for a in consts],
        out_specs=[pl.BlockSpec((nb, RWKV_CHUNK, RWKV_W), lambda b, c: (b, c, 0)),
                   pl.BlockSpec((nb, 1, RWKV_FEAT), lambda b, c: (b, 0, 0)),
                   pl.BlockSpec((nb, RWKV_W, RWKV_W), lambda b, c: (b, 0, 0))],
        out_shape=[jax.ShapeDtypeStruct((bsz, length, RWKV_W), BF16),
                   jax.ShapeDtypeStruct((bsz, 1, RWKV_FEAT), F32),
                   jax.ShapeDtypeStruct((bsz, RWKV_W, RWKV_W), F32)],
        scratch_shapes=[pltpu.VMEM((nb, RWKV_CHUNK + 8, RWKV_FEAT), F32),
                        pltpu.VMEM((nb, RWKV_W, RWKV_W), F32)],
        compiler_params=_cparams("arbitrary", "arbitrary"),
        name="rwkv_scan",
    )(rw, *consts)


def _gla_gate_log(small, gk2_pad, b_gk):
    x = _mm_hi(small, gk2_pad) + b_gk
    return -_softplus(-x) / GATE_NORMALIZER


def _gla_finish(o, gg, norm_g, head_avg):
    ms = _mm_xsel(o * o, head_avg, STAT_TERMS)
    return o * lax.rsqrt(ms + RMS_EPS) * norm_g * _silu(gg)


def _gla_kernel(slab_ref, small_ref, gk2_ref, bgk_ref, ng_ref, avg_ref, tri_ref, expand_ref,
                y_ref, s_out_ref, s_ref):
    c = pl.program_id(1)
    n_chunks = pl.num_programs(1)
    ch = GLA_CHUNK
    sub = GLA_SUB
    hk = GLA_HK
    expand = expand_ref[...]
    expand_f = expand.astype(F32)
    pos = _iota((ch, 1), 0) % sub
    t_idx = _iota((ch, 1), 0)
    hmask_k = (_iota((GLA_H * sub, hk), 0) // sub == _iota((GLA_H * sub, hk), 1) // GLA_DK).astype(F32)
    hmask_v = (_iota((GLA_H * sub, GLA_W), 0) // sub == _iota((GLA_H * sub, GLA_W), 1) // GLA_DV).astype(F32)

    nb = slab_ref.shape[0]

    @pl.when(c == 0)
    def _():
        s_ref[...] = jnp.zeros_like(s_ref)

    def chain(j):
        q = slab_ref[j, :, 0:hk].astype(F32) * (GLA_DK ** -0.5)
        k = slab_ref[j, :, hk:2 * hk].astype(F32)
        v = slab_ref[j, :, 2 * hk:2 * hk + GLA_W].astype(F32)
        gg = slab_ref[j, :, 2 * hk + GLA_W:].astype(F32)
        small = small_ref[j]

        lg = _gla_gate_log(small, gk2_ref[...], bgk_ref[...])
        yield
        cum = _mm_sel(tri_ref[...], lg)
        yield
        last = cum[ch - 1:ch, :]
        s_bd = s_ref[j]

        o = _mm(q * jnp.exp(cum), s_bd)

        group = 4
        for d0 in range(0, sub, group):
            prods, v_shift = [], []
            for delta in range(d0, d0 + group):
                if delta == 0:
                    k_s, c_s, v_s = k, cum, v
                else:
                    k_s = pltpu.roll(k, delta, 0)
                    c_s = pltpu.roll(cum, delta, 0)
                    v_s = pltpu.roll(v, delta, 0)
                valid = pos >= delta
                w_pair = jnp.exp(jnp.where(valid, cum - c_s, 0.0))
                prods.append(jnp.where(valid, q * k_s * w_pair, 0.0).astype(BF16))
                v_shift.append(v_s)
            yield
            att = jnp.dot(jnp.concatenate(prods, axis=0), expand, preferred_element_type=F32)
            o = o + sum(att[g * ch:(g + 1) * ch, :] * v_shift[g] for g in range(group))
            yield

        vb = v.astype(BF16)
        atts = []
        for blk in range(1, ch // sub):
            lo = blk * sub
            ref_pt = cum[lo - 1:lo, :]
            qd = q[lo:lo + sub, :] * jnp.exp(cum[lo:lo + sub, :] - ref_pt)
            kp = k * jnp.exp(jnp.where(t_idx < lo, ref_pt - cum, -jnp.inf))
            q4 = jnp.concatenate([qd] * GLA_H, axis=0) * hmask_k
            atts.append(_mm_nt(q4, kp).astype(BF16))
            yield
        o4 = jnp.dot(jnp.concatenate(atts, axis=0), vb, preferred_element_type=F32)
        yield
        cross = [jnp.zeros((sub, GLA_W), F32)]
        for blk in range(ch // sub - 1):
            o4_b = o4[blk * GLA_H * sub:(blk + 1) * GLA_H * sub, :] * hmask_v
            cross.append(sum(o4_b[h * sub:(h + 1) * sub, :] for h in range(GLA_H)))
        o = o + jnp.concatenate(cross, axis=0)

        kt_t = (k * jnp.exp(last - cum)).T
        s_ref[j] = s_bd * _col(jnp.exp(last)) + _mm(kt_t, vb) * expand_f
        yield

        y_ref[j] = _gla_finish(o, gg, ng_ref[...], avg_ref[...]).astype(y_ref.dtype)

    _interleave([chain(j) for j in range(nb)], skew=2)

    @pl.when(c == n_chunks - 1)
    def _():
        s_out_ref[...] = s_ref[...]


def _gla_prompt(slab, small, gk2_pad, b_gk, ng):
    bsz, length, _ = slab.shape
    nb = GLA_NB if bsz % GLA_NB == 0 else 1
    expand = (jnp.arange(GLA_HK)[:, None] // GLA_DK == jnp.arange(GLA_W)[None, :] // GLA_DV).astype(BF16)
    consts = (gk2_pad, b_gk, ng, _seg_ones(GLA_W, GLA_DV, 1.0 / GLA_DV), _tri_incl(GLA_CHUNK), expand)
    return pl.pallas_call(
        _gla_kernel,
        grid=(bsz // nb, length // GLA_CHUNK),
        in_specs=[pl.BlockSpec((nb, GLA_CHUNK, GLA_SLAB), lambda b, c: (b, c, 0)),
                  pl.BlockSpec((nb, GLA_CHUNK, SMALL_W), lambda b, c: (b, c, 0))]
                 + [_const_spec(a) for a in consts],
        out_specs=[pl.BlockSpec((nb, GLA_CHUNK, GLA_W), lambda b, c: (b, c, 0)),
                   pl.BlockSpec((nb, GLA_HK, GLA_W), lambda b, c: (b, 0, 0))],
        out_shape=[jax.ShapeDtypeStruct((bsz, length, GLA_W), BF16),
                   jax.ShapeDtypeStruct((bsz, GLA_HK, GLA_W), F32)],
        scratch_shapes=[pltpu.VMEM((nb, GLA_HK, GLA_W), F32)],
        compiler_params=_cparams("arbitrary", "arbitrary"),
        name="gla_scan",
    )(slab, small, *consts)


def _col(row):
    n = row.shape[1]
    eye = _iota((n, n), 0) == _iota((n, n), 1)
    return jnp.sum(jnp.where(eye, row, 0.0), axis=1, keepdims=True)


def _store_t(dst_ref, row0, x):
    for cblk in range(x.shape[1] // LANES):
        dst_ref[row0 + cblk * LANES:row0 + (cblk + 1) * LANES, :] = x[:, cblk * LANES:(cblk + 1) * LANES].T


def _load_t(src_ref, width):
    return jnp.concatenate([src_ref[cblk * LANES:(cblk + 1) * LANES, :].T for cblk in range(width // LANES)],
                           axis=1)


_TS_X, _TS_B, _TS_C, _TS_DT, _TS_DA = 0, SSD_W, SSD_W + SSD_G * SSD_N, SSD_CONV_DIM, SSD_CONV_DIM + SMALL_W
_TR_R, _TR_W, _TR_K, _TR_V, _TR_KK, _TR_KA = (i * RWKV_W for i in range(6))
_TG_Q, _TG_K, _TG_G, _TG_V = 0, GLA_HK, 2 * GLA_HK, 3 * GLA_HK


def _step_kernel(ssd_ref, small_ref, rw_ref, gla_ref, conv0_ref, shift0_ref, h0_ref, rs0_ref, gs0_ref,
                 cw_ref, cb_ref, dtb_ref, alog_ref, dexp_ref, sng_ref,
                 mu_ref, lw_ref, w0_ref, a0_ref, kk_ref, ka_ref, rk_ref, lng_ref, lnb_ref,
                 gk2_ref, bgk_ref, gng_ref, ones_ref, ravg_ref, gavg_ref,
                 y_ssd_ref, y_rwkv_ref, y_gla_ref, conv_out_ref, h_out_ref, rs_out_ref, gs_out_ref,
                 t_ssd, t_rwkv, t_gla, o_ssd_t, o_rwkv_t, o_gla_t):
    i = pl.program_id(0)
    hk = GLA_HK

    def ssd_tokens():
        xbc = ssd_ref[:, SSD_W:].astype(F32)
        taps = [conv0_ref[:, t * SSD_CONV_DIM:(t + 1) * SSD_CONV_DIM] for t in range(SSD_CONV_W - 1)]
        dt, dta = _ssd_dt(small_ref[...], dtb_ref[...], alog_ref[...])
        return xbc, (_ssd_conv_act(xbc, taps, cw_ref[...], cb_ref[...]), dt, dta)

    def rwkv_tokens():
        return _rwkv_token_math(rw_ref[...].astype(F32), shift0_ref[...], mu_ref[...], lw_ref[...],
                                w0_ref[...], a0_ref[...], kk_ref[...], ka_ref[...], ones_ref[...])

    @pl.when(i == 0)
    def _():
        xbc, (act, dt, dta) = ssd_tokens()
        conv_out_ref[:, 0:2 * SSD_CONV_DIM] = conv0_ref[:, SSD_CONV_DIM:]
        conv_out_ref[:, 2 * SSD_CONV_DIM:] = xbc
        _store_t(t_ssd, _TS_X, act)
        _store_t(t_ssd, _TS_DT, dt)
        _store_t(t_ssd, _TS_DA, jnp.exp(dta))
        r, k2, v, logw, aic, gate, kk = rwkv_tokens()
        for off, x in ((_TR_R, r), (_TR_W, jnp.exp(logw)), (_TR_K, k2), (_TR_V, v), (_TR_KK, kk),
                       (_TR_KA, kk * aic)):
            _store_t(t_rwkv, off, x)
        lg = _gla_gate_log(small_ref[...], gk2_ref[...], bgk_ref[...])
        _store_t(t_gla, _TG_Q, gla_ref[:, 0:hk].astype(F32) * (GLA_DK ** -0.5))
        _store_t(t_gla, _TG_K, gla_ref[:, hk:2 * hk].astype(F32))
        _store_t(t_gla, _TG_G, jnp.exp(lg))
        _store_t(t_gla, _TG_V, gla_ref[:, 2 * hk:2 * hk + GLA_W].astype(F32))
        o_gla_t[...] = jnp.zeros_like(o_gla_t)

    def row(ref, idx):
        return ref[pl.ds(idx, 1), :]

    def rows(ref, idx, count):
        return ref[pl.ds(pl.multiple_of(idx, count), count), :]

    units = h0_ref.shape[0] // SSD_N
    per_head = SSD_P // units
    h = i // per_head
    p0 = (i % per_head) * units
    b_t = rows(t_ssd, _TS_B + (h // (SSD_H // SSD_G)) * SSD_N, SSD_N)
    c_t = rows(t_ssd, _TS_C + (h // (SSD_H // SSD_G)) * SSD_N, SSD_N)
    da_row = row(t_ssd, _TS_DA + DT_OFF + h)
    dt_row = row(t_ssd, _TS_DT + DT_OFF + h)
    for u in range(units):
        p_idx = h * SSD_P + p0 + u
        blk = slice(u * SSD_N, (u + 1) * SSD_N)
        h_new = h0_ref[blk, :] * da_row + (row(t_ssd, _TS_X + p_idx) * dt_row) * b_t
        o_ssd_t[pl.ds(p_idx, 1), :] = jnp.sum(h_new * c_t, axis=0, keepdims=True)
        h_out_ref[blk, :] = h_new

    units = rs0_ref.shape[0] // RWKV_N
    per_head = RWKV_N // units
    h = i // per_head
    v0 = (i % per_head) * units
    seg = lambda off: rows(t_rwkv, off + h * RWKV_N, RWKV_N)
    r_t, w_t, k_t, kk_t, ka_t = seg(_TR_R), seg(_TR_W), seg(_TR_K), seg(_TR_KK), seg(_TR_KA)
    for u in range(units):
        v_idx = h * RWKV_N + v0 + u
        blk = slice(u * RWKV_N, (u + 1) * RWKV_N)
        s = rs0_ref[blk, :]
        sa = jnp.sum(s * (-kk_t), axis=0, keepdims=True)
        s_new = s * w_t + sa * ka_t + row(t_rwkv, _TR_V + v_idx) * k_t
        o_rwkv_t[pl.ds(v_idx, 1), :] = jnp.sum(s_new * r_t, axis=0, keepdims=True)
        rs_out_ref[blk, :] = s_new

    units = gs0_ref.shape[0] // GLA_DV
    per_head = GLA_DK // units
    h = i // per_head
    k0 = (i % per_head) * units
    v_t = rows(t_gla, _TG_V + h * GLA_DV, GLA_DV)
    acc = jnp.zeros((GLA_DV, LANES), F32)
    for u in range(units):
        k_idx = h * GLA_DK + k0 + u
        blk = slice(u * GLA_DV, (u + 1) * GLA_DV)
        s_new = gs0_ref[blk, :] * row(t_gla, _TG_G + k_idx) + row(t_gla, _TG_K + k_idx) * v_t
        acc = acc + row(t_gla, _TG_Q + k_idx) * s_new
        gs_out_ref[blk, :] = s_new
    o_rows = pl.ds(pl.multiple_of(h * GLA_DV, GLA_DV), GLA_DV)
    o_gla_t[o_rows, :] = o_gla_t[o_rows, :] + acc

    @pl.when(i == pl.num_programs(0) - 1)
    def _():
        _, (act, _, _) = ssd_tokens()
        y_ssd_ref[...] = _ssd_finish(_load_t(o_ssd_t, SSD_W), act[:, 0:SSD_W], ssd_ref[:, 0:SSD_W].astype(F32),
                                     dexp_ref[...], sng_ref[...]).astype(y_ssd_ref.dtype)
        r, k2, v, _, _, gate, _ = rwkv_tokens()
        y_rwkv_ref[...] = _rwkv_finish(_load_t(o_rwkv_t, RWKV_W), r, k2, v, gate, rk_ref[...], lng_ref[...],
                                       lnb_ref[...], ones_ref[...], ravg_ref[...]).astype(y_rwkv_ref.dtype)
        y_gla_ref[...] = _gla_finish(_load_t(o_gla_t, GLA_W), gla_ref[:, 2 * hk + GLA_W:].astype(F32),
                                     gng_ref[...], gavg_ref[...]).astype(y_gla_ref.dtype)


def _mixer_step(ssd_slab, small, rw, gla_slab, conv0, shift0, h0, rs0, gs0, ssd_p, rwkv_p, gla_p):
    bsz = ssd_slab.shape[0]
    assert bsz == LANES, "the single-token mixer keeps exactly one vreg row of batch entries on the lanes"
    flat_t = lambda s: s.reshape(bsz, -1).T
    h0, rs0, gs0 = flat_t(h0), flat_t(rs0), flat_t(gs0)
    consts = (tuple(ssd_p) + tuple(rwkv_p) + tuple(gla_p)
              + (_seg_ones(RWKV_W, RWKV_N), _seg_ones(RWKV_W, RWKV_N, 1.0 / RWKV_N),
                 _seg_ones(GLA_W, GLA_DV, 1.0 / GLA_DV)))
    full = lambda a: pl.BlockSpec(a.shape, lambda i: (0,) * a.ndim)
    cols = lambda a: pl.BlockSpec((a.shape[0] // STEP_GRID, bsz), lambda i: (i, 0))
    tokens = (ssd_slab, small, rw, gla_slab, conv0, shift0)
    outs = pl.pallas_call(
        _step_kernel,
        grid=(STEP_GRID,),
        in_specs=[full(a) for a in tokens] + [cols(h0), cols(rs0), cols(gs0)] + [full(a) for a in consts],
        out_specs=[full(jax.ShapeDtypeStruct((bsz, w), BF16)) for w in (SSD_W, RWKV_W, GLA_W)]
                  + [full(conv0), cols(h0), cols(rs0), cols(gs0)],
        out_shape=[jax.ShapeDtypeStruct((bsz, SSD_W), BF16),
                   jax.ShapeDtypeStruct((bsz, RWKV_W), BF16),
                   jax.ShapeDtypeStruct((bsz, GLA_W), BF16),
                   jax.ShapeDtypeStruct(conv0.shape, F32),
                   jax.ShapeDtypeStruct(h0.shape, F32),
                   jax.ShapeDtypeStruct(rs0.shape, F32),
                   jax.ShapeDtypeStruct(gs0.shape, F32)],
        scratch_shapes=[pltpu.VMEM((SSD_CONV_DIM + 2 * SMALL_W, LANES), F32),
                        pltpu.VMEM((6 * RWKV_W, LANES), F32),
                        pltpu.VMEM((3 * GLA_HK + GLA_W, LANES), F32),
                        pltpu.VMEM((SSD_W, LANES), F32),
                        pltpu.VMEM((RWKV_W, LANES), F32),
                        pltpu.VMEM((GLA_W, LANES), F32)],
        compiler_params=_cparams("arbitrary"),
        name="mixer_step",
    )(*tokens, h0, rs0, gs0, *consts)
    y_ssd, y_rwkv, y_gla, conv_new, h_new, rs_new, gs_new = outs
    return (y_ssd, y_rwkv, y_gla, conv_new.reshape(bsz, SSD_CONV_W - 1, SSD_CONV_DIM),
            h_new.T.reshape(bsz, SSD_H, SSD_P, SSD_N), rs_new.T.reshape(bsz, RWKV_H, RWKV_N, RWKV_N),
            gs_new.T.reshape(bsz, GLA_H, GLA_DK, GLA_DV))


def _mod_rows(mod_ref, rows):
    return mod_ref[0] if mod_ref.shape[1] == 1 else mod_ref[0, rows, :]


def _row_parts(tm):
    return 2 if tm % 256 == 0 else 1


def _outproj_kernel(ys_ref, yr_ref, yg_ref, x_ref, gt_ref, sc_ref, sh_ref, w_ref, g_ref, b_ref, rt_ref,
                    o_ref, logit_ref):
    tm = x_ref.shape[1]
    parts = _row_parts(tm)

    def chain(r):
        rows = slice(r * tm // parts, (r + 1) * tm // parts)
        m = (jnp.dot(ys_ref[0, rows, :], w_ref[0:SSD_W, :], preferred_element_type=F32)
             + jnp.dot(yr_ref[0, rows, :], w_ref[SSD_W:SSD_W + RWKV_W, :], preferred_element_type=F32)
             + jnp.dot(yg_ref[0, rows, :], w_ref[SSD_W + RWKV_W:, :], preferred_element_type=F32))
        yield
        u = ALPHA * x_ref[0, rows, :] + (1.0 + _mod_rows(gt_ref, rows)) * m
        x1 = _layernorm(u, g_ref[...], b_ref[...])
        o_ref[0, rows, :] = x1
        h = x1 * (1.0 + _mod_rows(sc_ref, rows)) + _mod_rows(sh_ref, rows)
        logit_ref[0, rows, :] = _mm_hi(h, rt_ref[...])

    _interleave([chain(r) for r in range(parts)], skew=1)


def _outproj(y_ssd, y_rwkv, y_gla, x, gt, sc2, sh2, w_out, ln_g, ln_b, router_pad, tm):
    bsz, length, _ = x.shape
    tok = lambda width: pl.BlockSpec((1, tm, width), lambda b, i: (b, i, 0))
    vec = pl.BlockSpec((1, D_MODEL), lambda b, i: (0, 0))
    return pl.pallas_call(
        _outproj_kernel,
        grid=(bsz, length // tm),
        in_specs=[tok(SSD_W), tok(RWKV_W), tok(GLA_W), tok(D_MODEL),
                  _mod_spec(gt, tm), _mod_spec(sc2, tm), _mod_spec(sh2, tm),
                  pl.BlockSpec((D_MODEL, D_MODEL), lambda b, i: (0, 0)), vec, vec,
                  pl.BlockSpec((D_MODEL, LANES), lambda b, i: (0, 0))],
        out_specs=[tok(D_MODEL), tok(LANES)],
        out_shape=[jax.ShapeDtypeStruct((bsz, length, D_MODEL), F32),
                   jax.ShapeDtypeStruct((bsz, length, LANES), F32)],
        compiler_params=_cparams("arbitrary", "arbitrary"),
        name="outproj_ln",
    )(y_ssd, y_rwkv, y_gla, x, gt, sc2, sh2, w_out, ln_g, ln_b, router_pad)


def _ffn_kernel(ys_ref, yr_ref, yg_ref, x_ref, gt1_ref, sc_ref, sh_ref, gt2_ref, wo_ref, g1_ref, b1_ref,
                wg_ref, wu_ref, wd_ref, g2_ref, b2_ref, o_ref):
    tm = x_ref.shape[1]
    parts = _row_parts(tm)

    def chain(r):
        rows = slice(r * tm // parts, (r + 1) * tm // parts)
        m = (jnp.dot(ys_ref[0, rows, :], wo_ref[0:SSD_W, :], preferred_element_type=F32)
             + jnp.dot(yr_ref[0, rows, :], wo_ref[SSD_W:SSD_W + RWKV_W, :], preferred_element_type=F32)
             + jnp.dot(yg_ref[0, rows, :], wo_ref[SSD_W + RWKV_W:, :], preferred_element_type=F32))
        yield
        x1 = _layernorm(ALPHA * x_ref[0, rows, :] + (1.0 + _mod_rows(gt1_ref, rows)) * m,
                        g1_ref[...], b1_ref[...])
        h = (x1 * (1.0 + _mod_rows(sc_ref, rows)) + _mod_rows(sh_ref, rows)).astype(BF16)
        yield
        gate = jnp.dot(h, wg_ref[...], preferred_element_type=F32)
        up = jnp.dot(h, wu_ref[...], preferred_element_type=F32)
        yield
        f = jnp.dot((_silu(gate) * up).astype(BF16), wd_ref[...], preferred_element_type=F32)
        yield
        u = ALPHA * x1 + (1.0 + _mod_rows(gt2_ref, rows)) * f
        o_ref[0, rows, :] = _layernorm(u, g2_ref[...], b2_ref[...])

    _interleave([chain(r) for r in range(parts)], skew=1)


def _outproj_ffn(y_ssd, y_rwkv, y_gla, x, gt1, sc2, sh2, gt2, w_out, ln1_g, ln1_b, wg, wu, wd, ln2_g, ln2_b, tm):
    bsz, length, _ = x.shape
    tok = lambda width: pl.BlockSpec((1, tm, width), lambda b, i: (b, i, 0))
    vec = pl.BlockSpec((1, D_MODEL), lambda b, i: (0, 0))
    resident = lambda a: pl.BlockSpec(a.shape, lambda b, i: (0, 0), pipeline_mode=pl.Buffered(1))
    return pl.pallas_call(
        _ffn_kernel,
        grid=(bsz, length // tm),
        in_specs=[tok(SSD_W), tok(RWKV_W), tok(GLA_W), tok(D_MODEL),
                  _mod_spec(gt1, tm), _mod_spec(sc2, tm), _mod_spec(sh2, tm), _mod_spec(gt2, tm),
                  resident(w_out), vec, vec, resident(wg), resident(wu), resident(wd), vec, vec],
        out_specs=tok(D_MODEL),
        out_shape=jax.ShapeDtypeStruct((bsz, length, D_MODEL), F32),
        compiler_params=_cparams("arbitrary", "arbitrary"),
        name="outproj_ffn_ln",
    )(y_ssd, y_rwkv, y_gla, x, gt1, sc2, sh2, gt2, w_out, ln1_g, ln1_b, wg, wu, wd, ln2_g, ln2_b)


def _moe_kernel(x_ref, sc_ref, sh_ref, gt_ref, logit_ref, wg_ref, wu_ref, wd_ref, g_ref, b_ref, tri_ref, o_ref,
                h_ref, rank_t_ref, comb_t_ref, cnt_ref, acc_ref, *, half, cap):
    e = pl.program_id(2)
    tm = 2 * half

    @pl.when(e == 0)
    def _():
        h_ref[...] = (x_ref[0] * (1.0 + sc_ref[0]) + sh_ref[0]).astype(BF16)
        acc_ref[...] = jnp.zeros_like(acc_ref)
        lane = _iota((tm, LANES), 1).astype(F32)
        logits = jnp.where(lane < N_EXPERTS, logit_ref[0], -jnp.inf)
        m1 = jnp.max(logits, axis=-1, keepdims=True)
        i1 = jnp.min(jnp.where(logits == m1, lane, float(LANES)), axis=-1, keepdims=True)
        rest = jnp.where(lane == i1, -jnp.inf, logits)
        m2 = jnp.max(rest, axis=-1, keepdims=True)
        i2 = jnp.min(jnp.where(rest == m2, lane, float(LANES)), axis=-1, keepdims=True)
        e2 = jnp.exp(m2 - m1)
        den = 1.0 + e2
        comb = jnp.where(lane == i1, 1.0 / den, 0.0) + jnp.where(lane == i2, e2 / den, 0.0)
        sel = (lane == i1) | (lane == i2)
        sel_f = sel.astype(F32)
        ranks = []
        for hf in range(2):
            s = sel_f[hf * half:(hf + 1) * half, :]
            before = jnp.dot(tri_ref[...], s.astype(BF16), preferred_element_type=F32)
            ranks.append(jnp.where(sel[hf * half:(hf + 1) * half, :], before, -1.0))
            cnt_ref[hf:hf + 1, :] = jnp.sum(s, axis=0, keepdims=True)
        rank = jnp.concatenate(ranks, axis=0)
        for blk in range(tm // LANES):
            rows = slice(blk * LANES, (blk + 1) * LANES)
            rank_t_ref[:, rows] = rank[rows, :].T
            comb_t_ref[:, rows] = comb[rows, :].T

    rank_t_e = rank_t_ref[pl.ds(e, 1), :]
    comb_t_e = comb_t_ref[pl.ds(e, 1), :]
    n_max = jnp.max(jnp.where(_iota((2, LANES), 1) == e, cnt_ref[...], 0.0))
    n_pass = (n_max.astype(jnp.int32) + (cap - 1)) // cap

    def one_pass(p, carry):
        slot = _iota((cap, 1), 0).astype(F32) + (p * cap).astype(F32)
        picks, xs, gates = [], [], []
        for hf in range(2):
            rows = slice(hf * half, (hf + 1) * half)
            hit = rank_t_e[:, rows] == slot
            pick = hit.astype(BF16)
            picks.append(pick)
            xs.append(jnp.dot(pick, h_ref[rows, :], preferred_element_type=F32).astype(BF16))
            gates.append(jnp.sum(jnp.where(hit, comb_t_e[:, rows], 0.0), axis=1, keepdims=True))
        xc = jnp.concatenate(xs, axis=0)
        a = _silu(jnp.dot(xc, wg_ref[0], preferred_element_type=F32)) * jnp.dot(
            xc, wu_ref[0], preferred_element_type=F32)
        out_e = jnp.dot(a.astype(BF16), wd_ref[0], preferred_element_type=F32)
        for hf in range(2):
            rows = slice(hf * half, (hf + 1) * half)
            weighted = (out_e[hf * cap:(hf + 1) * cap, :] * gates[hf]).astype(BF16)
            acc_ref[rows, :] += lax.dot_general(picks[hf], weighted, (((0,), (0,)), ((), ())),
                                                preferred_element_type=F32)
        return carry

    one_pass(jnp.int32(0), 0)
    lax.fori_loop(1, n_pass, one_pass, 0)

    @pl.when(e == pl.num_programs(2) - 1)
    def _():
        u = ALPHA * x_ref[0] + (1.0 + gt_ref[0]) * acc_ref[...]
        o_ref[0] = _layernorm(u, g_ref[...], b_ref[...])


def _moe(x, sc, sh, gt, logits, wg, wu, wd, ln_g, ln_b, tm):
    bsz, length, _ = x.shape
    assert tm % LANES == 0
    half = tm // 2
    cap = MOE_CAP_ROWS if half >= 2 * MOE_CAP_ROWS else half
    tri = (jnp.arange(half)[None, :] < jnp.arange(half)[:, None]).astype(BF16)
    tok = pl.BlockSpec((1, tm, D_MODEL), lambda b, i, e: (b, i, 0))
    vec = pl.BlockSpec((1, D_MODEL), lambda b, i, e: (0, 0))
    wspec = pl.BlockSpec((1, D_MODEL, D_MODEL), lambda b, i, e: (e, 0, 0))
    return pl.pallas_call(
        functools.partial(_moe_kernel, half=half, cap=cap),
        grid=(bsz, length // tm, N_EXPERTS),
        in_specs=[tok, _mod_spec(sc, tm), _mod_spec(sh, tm), _mod_spec(gt, tm),
                  pl.BlockSpec((1, tm, LANES), lambda b, i, e: (b, i, 0)),
                  wspec, wspec, wspec, vec, vec, _const_spec(tri)],
        out_specs=tok,
        out_shape=jax.ShapeDtypeStruct((bsz, length, D_MODEL), F32),
        scratch_shapes=[pltpu.VMEM((tm, D_MODEL), BF16), pltpu.VMEM((LANES, tm), F32),
                        pltpu.VMEM((LANES, tm), F32), pltpu.VMEM((2, LANES), F32),
                        pltpu.VMEM((tm, D_MODEL), F32)],
        compiler_params=_cparams("arbitrary", "arbitrary", "arbitrary"),
        name="moe_ln",
    )(x, sc, sh, gt, logits, wg, wu, wd, ln_g, ln_b, tri)


def _pad_lanes(vec, offset, width=SMALL_W):
    out = jnp.zeros((1, width), F32)
    return out.at[0, offset:offset + vec.shape[0]].set(vec)


def _layer_params(p, l):
    w_in = p["w_in"][l]
    off = [0]
    for s in (SSD_W, SSD_CONV_DIM, SSD_H, RWKV_FEAT, GLA_HK, GLA_HK, GLA_W, GK_LORA, GLA_W):
        off.append(off[-1] + s)
    piece = lambda i: w_in[:, off[i]:off[i + 1]]
    small = jnp.zeros((D_MODEL, SMALL_W), F32)
    small = small.at[:, DT_OFF:DT_OFF + SSD_H].set(piece(2)).at[:, GLO_OFF:GLO_OFF + GK_LORA].set(piece(7))
    w_pad = jnp.concatenate([piece(0), piece(1), small, piece(3), piece(4), piece(5), piece(6), piece(8)],
                            axis=1).astype(BF16)
    ssd_p = (p["ssd_conv_w"][l], p["ssd_conv_b"][l][None, :],
             _pad_lanes(p["ssd_dt_bias"][l], DT_OFF), _pad_lanes(p["ssd_a_log"][l], DT_OFF),
             jnp.repeat(p["ssd_d"][l], SSD_P)[None, :], p["ssd_norm_g"][l][None, :])
    lora_w = jnp.zeros((LANES, 3 * RWKV_W), F32)
    lora_w = (lora_w.at[0:32, 0:RWKV_W].set(p["rwkv_w2"][l])
              .at[32:64, RWKV_W:2 * RWKV_W].set(p["rwkv_a2"][l])
              .at[64:128, 2 * RWKV_W:].set(p["rwkv_g2"][l])).astype(BF16)
    row = lambda name: p[name][l].reshape(1, -1)
    rwkv_p = (row("rwkv_mu"), lora_w, row("rwkv_w0"), row("rwkv_a0"), row("rwkv_k_k"), row("rwkv_k_a"),
              row("rwkv_r_k"), row("rwkv_ln_g"), row("rwkv_ln_b"))
    gk2_pad = jnp.zeros((SMALL_W, GLA_HK), F32).at[GLO_OFF:GLO_OFF + GK_LORA].set(p["gla_w_gk2"][l])
    gla_p = (gk2_pad, row("gla_b_gk"), jnp.tile(p["gla_norm_g"][l], GLA_H)[None, :])
    return w_pad, ssd_p, rwkv_p, gla_p


def _block_diag_inv(s_bd, h):
    b, hr, hc = s_bd.shape
    r, c = hr // h, hc // h
    s = s_bd.reshape(b, h, r, h, c)
    return jnp.stack([s[:, i, :, i, :] for i in range(h)], axis=1)


def _ssd_state_unpack(h_bd):
    b = h_bd.shape[0]
    s = h_bd.reshape(b, SSD_G, SSD_N, SSD_H, SSD_P)
    per_group = SSD_H // SSD_G
    heads = [s[:, h // per_group, :, h, :] for h in range(SSD_H)]
    return jnp.swapaxes(jnp.stack(heads, axis=1), 2, 3)


def _tail(x, mod_l, l, p, y_ssd, y_rwkv, y_gla, tm):
    sh1, sc1, gt1, sh2, sc2, gt2 = mod_l
    row = lambda name: p[name][l].reshape(1, -1)
    w_out = p["w_out"][l].astype(BF16)
    i = l // 2
    if l % 2 == 0:
        x = _outproj_ffn(y_ssd, y_rwkv, y_gla, x, gt1, sc2, sh2, gt2, w_out, row("ln_mix_g"), row("ln_mix_b"),
                         p["ffn_w_gate"][i].astype(BF16), p["ffn_w_up"][i].astype(BF16),
                         p["ffn_w_down"][i].astype(BF16), row("ln_ffn_g"), row("ln_ffn_b"), tm)
    else:
        router_pad = jnp.zeros((D_MODEL, LANES), F32).at[:, :N_EXPERTS].set(p["moe_router"][i])
        x, logits = _outproj(y_ssd, y_rwkv, y_gla, x, gt1, sc2, sh2, w_out, row("ln_mix_g"), row("ln_mix_b"),
                             router_pad, tm)
        x = _moe(x, sc2, sh2, gt2, logits, p["moe_w_gate"][i].astype(BF16), p["moe_w_up"][i].astype(BF16),
                 p["moe_w_down"][i].astype(BF16), row("ln_ffn_g"), row("ln_ffn_b"), min(2 * tm, x.shape[1]))
    return x


def _forward(x_prompt, x_sample, c_prompt, c_sample, states, p):
    bp, seq, _ = x_prompt.shape
    bs = x_sample.shape[0]
    state_ssd, state_conv, state_rwkv, state_shift, state_gla = states
    mod = _ada(jnp.concatenate([c_prompt, c_sample], axis=0), p["w_ada"], p["b_ada"])

    xp = x_prompt
    xs = x_sample.reshape(1, bs, D_MODEL)
    tm_p = min(512, seq)
    outs_p = [[] for _ in range(5)]
    outs_s = [[] for _ in range(5)]
    for l in range(DEPTH):
        w_pad, ssd_p, rwkv_p, gla_p = _layer_params(p, l)
        mods = jnp.split(mod[l], 6, axis=-1)
        mod_p = [m[:bp, None, :] for m in mods]
        mod_s = [m[None, bp:, :] for m in mods]

        ssd_slab, small, rw, gla_slab = _inproj(xp, mod_p[1], mod_p[0], w_pad, tm_p)
        y_ssd, conv_new, h_new = _ssd_prompt(ssd_slab, small, *ssd_p)
        h_new = _ssd_state_unpack(h_new)
        y_rwkv, shift_new, rs_bd = _rwkv_prompt(rw, *rwkv_p)
        y_gla, gs_bd = _gla_prompt(gla_slab, small, *gla_p)
        xp = _tail(xp, mod_p, l, p, y_ssd, y_rwkv, y_gla, tm_p)
        for acc, s in zip(outs_p, (h_new, conv_new, _block_diag_inv(rs_bd, RWKV_H),
                                   shift_new.reshape(bp, RWKV_FEAT), _block_diag_inv(gs_bd, GLA_H))):
            acc.append(s)

        ssd_slab, small, rw, gla_slab = _inproj(xs, mod_s[1], mod_s[0], w_pad, bs)
        y_ssd, y_rwkv, y_gla, conv_new, h_new, rs_new, gs_new = _mixer_step(
            ssd_slab[0], small[0], rw[0], gla_slab[0], state_conv[l].reshape(bs, -1), state_shift[l],
            state_ssd[l], state_rwkv[l], state_gla[l], ssd_p, rwkv_p, gla_p)
        xs = _tail(xs, mod_s, l, p, y_ssd[None], y_rwkv[None], y_gla[None], bs)
        for acc, s in zip(outs_s, (h_new, conv_new, rs_new, rw[0].astype(F32), gs_new)):
            acc.append(s)

    stack = lambda accs: tuple(jnp.stack(a, axis=0) for a in accs)
    return (xp, xs.reshape(bs, 1, D_MODEL)) + stack(outs_p) + stack(outs_s)


def kernel(x_prompt, x_sample, c_prompt, c_sample, state_ssd, state_ssd_conv, state_rwkv, state_rwkv_shift, state_gla, w_ada, b_ada, w_in, w_out, ssd_conv_w, ssd_conv_b, ssd_dt_bias, ssd_a_log, ssd_d, ssd_norm_g, rwkv_mu, rwkv_w0, rwkv_w2, rwkv_a0, rwkv_a2, rwkv_g2, rwkv_k_k, rwkv_k_a, rwkv_r_k, rwkv_ln_g, rwkv_ln_b, gla_w_gk2, gla_b_gk, gla_norm_g, ln_mix_g, ln_mix_b, ln_ffn_g, ln_ffn_b, ffn_w_gate, ffn_w_up, ffn_w_down, moe_router, moe_w_gate, moe_w_up, moe_w_down):
    p = dict(w_ada=w_ada, b_ada=b_ada, w_in=w_in, w_out=w_out, ssd_conv_w=ssd_conv_w, ssd_conv_b=ssd_conv_b,
             ssd_dt_bias=ssd_dt_bias, ssd_a_log=ssd_a_log, ssd_d=ssd_d, ssd_norm_g=ssd_norm_g,
             rwkv_mu=rwkv_mu, rwkv_w0=rwkv_w0, rwkv_w2=rwkv_w2, rwkv_a0=rwkv_a0, rwkv_a2=rwkv_a2,
             rwkv_g2=rwkv_g2, rwkv_k_k=rwkv_k_k, rwkv_k_a=rwkv_k_a, rwkv_r_k=rwkv_r_k,
             rwkv_ln_g=rwkv_ln_g, rwkv_ln_b=rwkv_ln_b, gla_w_gk2=gla_w_gk2, gla_b_gk=gla_b_gk,
             gla_norm_g=gla_norm_g, ln_mix_g=ln_mix_g, ln_mix_b=ln_mix_b, ln_ffn_g=ln_ffn_g,
             ln_ffn_b=ln_ffn_b, ffn_w_gate=ffn_w_gate, ffn_w_up=ffn_w_up, ffn_w_down=ffn_w_down,
             moe_router=moe_router, moe_w_gate=moe_w_gate, moe_w_up=moe_w_up, moe_w_down=moe_w_down)
    states = (state_ssd, state_ssd_conv, state_rwkv, state_rwkv_shift, state_gla)
    return _forward(x_prompt, x_sample, c_prompt, c_sample, states, p)
```

```python
import functools

import jax
import jax.numpy as jnp
from jax import lax
from jax.experimental import pallas as pl
from jax.experimental.pallas import tpu as pltpu

F32 = jnp.float32
BF16 = jnp.bfloat16

D_MODEL = 1024
DEPTH = 2
SSD_W = 512
SSD_H = 8
SSD_P = 64
SSD_N = 64
SSD_G = 2
SSD_CONV_W = 4
SSD_CONV_DIM = 768
RWKV_W = 256
RWKV_H = 4
RWKV_N = 64
RWKV_FEAT = 896
RWKV_GN_EPS = RWKV_N * 1e-5
GLA_W = 256
GLA_H = 4
GLA_DK = 32
GLA_DV = 64
GLA_HK = GLA_H * GLA_DK
GK_LORA = 16
GATE_NORMALIZER = 16.0
F_DENSE = 2816
N_EXPERTS = 8
ALPHA = (2.0 * DEPTH) ** 0.25
LN_EPS = 1e-5
RMS_EPS = 1e-6

LANES = 128
SMALL_W = LANES
DT_OFF = 0
GLO_OFF = 8
SSD_SLAB = SSD_W + SSD_CONV_DIM
GLA_SLAB = 2 * GLA_HK + 2 * GLA_W
IN_PAD = SSD_SLAB + SMALL_W + RWKV_FEAT + GLA_SLAB
SSD_CHUNK = 128
SSD_CARRY = 8
GLA_CHUNK = 128
GLA_SUB = 8
RWKV_CHUNK = 64
SSD_NB, RWKV_NB, GLA_NB = 4, 8, 8
RWKV_SKEW = 1
STAT_TERMS = 1
MOE_CAP_ROWS = 160
STEP_GRID = 8
VMEM_LIMIT = 56 * 1024 * 1024


def _cparams(*sem):
    return pltpu.CompilerParams(dimension_semantics=sem, vmem_limit_bytes=VMEM_LIMIT)


def _mm(a, b):
    return jnp.dot(a.astype(BF16), b.astype(BF16), preferred_element_type=F32)


def _mm_nt(a, b):
    return lax.dot_general(a.astype(BF16), b.astype(BF16), (((1,), (1,)), ((), ())),
                           preferred_element_type=F32)


def _split_bf16(x, terms):
    parts = []
    for _ in range(terms):
        p = x.astype(BF16)
        parts.append(p)
        x = x - p.astype(F32)
    return parts


def _mm_sel(sel, x, terms=3):
    return sum(jnp.dot(sel, p, preferred_element_type=F32) for p in _split_bf16(x, terms))


def _mm_xsel(x, sel, terms=2):
    return sum(jnp.dot(p, sel, preferred_element_type=F32) for p in _split_bf16(x, terms))


def _mm_hi(a, b):
    a_hi, a_lo = _split_bf16(a, 2)
    b_hi, b_lo = _split_bf16(b, 2)
    dot = lambda x, y: jnp.dot(x, y, preferred_element_type=F32)
    return dot(a_hi, b_hi) + (dot(a_hi, b_lo) + dot(a_lo, b_hi))


def _sigmoid(x):
    return 1.0 / (1.0 + jnp.exp(-x))


def _silu(x):
    return x * _sigmoid(x)


def _softplus(x):
    return jnp.maximum(x, 0.0) + jnp.log(1.0 + jnp.exp(-jnp.abs(x)))


def _iota(shape, dim):
    return lax.broadcasted_iota(jnp.int32, shape, dim)


def _layernorm(u, g, b):
    mu = jnp.mean(u, axis=-1, keepdims=True)
    d = u - mu
    var = jnp.mean(d * d, axis=-1, keepdims=True)
    return d * lax.rsqrt(var + LN_EPS) * g + b


def _tri_incl(n):
    return (jnp.arange(n)[None, :] <= jnp.arange(n)[:, None]).astype(BF16)


def _seg_ones(n, seg, scale=1.0):
    idx = jnp.arange(n) // seg
    return jnp.where(idx[:, None] == idx[None, :], scale, 0.0).astype(BF16)


def _interleave(chains, skew=0):
    chains = list(chains)
    done = [False] * len(chains)
    rnd = 0
    while not all(done):
        for i, ch in enumerate(chains):
            if done[i] or rnd < i * skew:
                continue
            try:
                next(ch)
            except StopIteration:
                done[i] = True
        rnd += 1


def _const_spec(arr):
    nd = arr.ndim
    return pl.BlockSpec(arr.shape, lambda *_: (0,) * nd)


def _ada_kernel(c_ref, w_ref, b_ref, o_ref):
    o_ref[0] = _mm(_silu(c_ref[...]), w_ref[0]) + b_ref[0]


def _ada(c_all, w_ada, b_ada):
    rows = c_all.shape[0]
    tn = 1536
    return pl.pallas_call(
        _ada_kernel,
        grid=(DEPTH, 6 * D_MODEL // tn),
        in_specs=[pl.BlockSpec((rows, D_MODEL), lambda l, j: (0, 0)),
                  pl.BlockSpec((1, D_MODEL, tn), lambda l, j: (l, 0, j)),
                  pl.BlockSpec((1, 1, tn), lambda l, j: (l, 0, j))],
        out_specs=pl.BlockSpec((1, rows, tn), lambda l, j: (l, 0, j)),
        out_shape=jax.ShapeDtypeStruct((DEPTH, rows, 6 * D_MODEL), F32),
        compiler_params=_cparams("arbitrary", "arbitrary"),
        name="ada",
    )(c_all, w_ada, b_ada.reshape(DEPTH, 1, 6 * D_MODEL))


def _inproj_kernel(x_ref, sc_ref, sh_ref, w_ref, o_ssd, o_small, o_rwkv, o_gla):
    h = (x_ref[0] * (1.0 + sc_ref[0]) + sh_ref[0]).astype(BF16)
    off = 0
    for o_ref in (o_ssd, o_small, o_rwkv, o_gla):
        width = o_ref.shape[2]
        o_ref[0] = jnp.dot(h, w_ref[:, off:off + width], preferred_element_type=F32).astype(o_ref.dtype)
        off += width


def _mod_spec(mod, tm):
    if mod.shape[1] == 1:
        return pl.BlockSpec((1, 1, D_MODEL), lambda b, i, *_: (b, 0, 0))
    return pl.BlockSpec((1, tm, D_MODEL), lambda b, i, *_: (b, i, 0))


def _inproj(x, sc, sh, w_pad, tm):
    bsz, length, _ = x.shape
    tok = lambda width: pl.BlockSpec((1, tm, width), lambda b, i: (b, i, 0))
    return pl.pallas_call(
        _inproj_kernel,
        grid=(bsz, length // tm),
        in_specs=[tok(D_MODEL), _mod_spec(sc, tm), _mod_spec(sh, tm),
                  pl.BlockSpec((D_MODEL, IN_PAD), lambda b, i: (0, 0))],
        out_specs=[tok(SSD_SLAB), tok(SMALL_W), tok(RWKV_FEAT), tok(GLA_SLAB)],
        out_shape=[jax.ShapeDtypeStruct((bsz, length, SSD_SLAB), BF16),
                   jax.ShapeDtypeStruct((bsz, length, SMALL_W), F32),
                   jax.ShapeDtypeStruct((bsz, length, RWKV_FEAT), BF16),
                   jax.ShapeDtypeStruct((bsz, length, GLA_SLAB), BF16)],
        compiler_params=_cparams("arbitrary", "arbitrary"),
        name="inproj",
    )(x, sc, sh, w_pad)


def _ssd_conv_act(xbc, conv_taps, cw, cb):
    conv = cb + xbc * cw[SSD_CONV_W - 1:SSD_CONV_W, :]
    for i in range(SSD_CONV_W - 1):
        conv = conv + conv_taps[i] * cw[i:i + 1, :]
    return _silu(conv)


def _ssd_dt(small, dtb, alog):
    dt = _softplus(small + dtb)
    return dt, dt * (-jnp.exp(alog))


def _ssd_finish(y, xs, z, dexp, ng):
    gated = (y + xs * dexp) * _silu(z)
    ms = jnp.mean(gated * gated, axis=-1, keepdims=True)
    return gated * lax.rsqrt(ms + RMS_EPS) * ng


def _ssd_kernel(slab_ref, small_ref, cw_ref, cb_ref, dtb_ref, alog_ref, dexp_ref, ng_ref, tri_ref,
                head_x_ref, group_mask_ref, pair_mask_ref, y_ref, conv_out_ref, h_out_ref, ext_ref, h_ref):
    c = pl.program_id(1)
    n_chunks = pl.num_programs(1)
    ch = SSD_CHUNK
    pad = ext_ref.shape[1] - ch
    causal = _iota((ch, ch), 1) <= _iota((ch, ch), 0)

    nb = slab_ref.shape[0]

    @pl.when(c == 0)
    def _():
        ext_ref[:, 0:pad, :] = jnp.zeros((nb, pad, SSD_CONV_DIM), F32)
        h_ref[...] = jnp.zeros_like(h_ref)

    def chain(j):
        z = slab_ref[j, :, 0:SSD_W].astype(F32)
        xbc = slab_ref[j, :, SSD_W:].astype(F32)

        ext_ref[j, pad:pad + ch, :] = xbc
        first = pad - (SSD_CONV_W - 1)
        taps = [ext_ref[j, first + i:first + i + ch, :] for i in range(SSD_CONV_W - 1)]
        act = _ssd_conv_act(xbc, taps, cw_ref[...], cb_ref[...])
        ext_ref[j, 0:pad, :] = ext_ref[j, ch:ch + pad, :]
        dt, dta = _ssd_dt(small_ref[j], dtb_ref[...], alog_ref[...])

        xs = act[:, 0:SSD_W]
        bs = act[:, SSD_W:SSD_W + SSD_G * SSD_N]
        cs = act[:, SSD_W + SSD_G * SSD_N:]
        yield
        cum = _mm_sel(tri_ref[...], dta)
        cum_t = cum.T
        dt_t = dt.T
        cum_x = _mm_xsel(cum, head_x_ref[...], 3)
        dt_x = _mm_xsel(dt, head_x_ref[...], 2)
        yield
        last_x = cum_x[ch - 1:ch, :]
        xs_tail = (xs * (jnp.exp(last_x - cum_x) * dt_x)).astype(BF16)
        h_bd = h_ref[j]
        y_state = jnp.dot(cs.astype(BF16), h_bd.astype(BF16), preferred_element_type=F32) * jnp.exp(cum_x)
        upd = jnp.dot(bs.T.astype(BF16), xs_tail, preferred_element_type=F32)
        h_ref[j] = h_bd * jnp.exp(last_x) + upd * group_mask_ref[...]
        yield

        lane_group = _iota((1, SSD_G * SSD_N), 1) // SSD_N
        xs_b = xs.astype(BF16)
        ys = []
        for g in range(SSD_G):
            cb = _mm_nt(jnp.where(lane_group == g, cs, 0.0), bs)
            yield
            for pair in range(SSD_H // SSD_G // 2):
                scores = []
                for hh in range(2):
                    lane = DT_OFF + g * (SSD_H // SSD_G) + 2 * pair + hh
                    decay = jnp.exp(jnp.where(causal, cum[:, lane:lane + 1] - cum_t[lane:lane + 1, :], -jnp.inf))
                    scores.append((cb * decay * dt_t[lane:lane + 1, :]).astype(BF16))
                blk = g * (SSD_H // SSD_G) // 2 + pair
                x_pair = xs_b[:, blk * LANES:(blk + 1) * LANES]
                x_bd = jnp.concatenate([x_pair, x_pair], axis=0) * pair_mask_ref[...]
                ys.append(jnp.dot(jnp.concatenate(scores, axis=1), x_bd, preferred_element_type=F32))
                yield

        y = jnp.concatenate(ys, axis=-1) + y_state
        y_ref[j] = _ssd_finish(y, xs, z, dexp_ref[...], ng_ref[...]).astype(y_ref.dtype)

    _interleave([chain(j) for j in range(nb)])

    @pl.when(c == n_chunks - 1)
    def _():
        for j in range(nb):
            conv_out_ref[j] = ext_ref[j, pad - (SSD_CONV_W - 1):pad, :]
            h_out_ref[j] = h_ref[j]


def _ssd_prompt(slab, small, cw, cb, dtb, alog, dexp, ng):
    bsz, length, _ = slab.shape
    nb = SSD_NB if bsz % SSD_NB == 0 else 1
    gn, hp = SSD_G * SSD_N, SSD_H * SSD_P
    lane_head = jnp.arange(hp) // SSD_P
    head_x = (jnp.arange(SMALL_W)[:, None] == DT_OFF + lane_head[None, :]).astype(BF16)
    group_mask = (jnp.arange(gn)[:, None] // SSD_N == lane_head[None, :] // (SSD_H // SSD_G)).astype(F32)
    pair_mask = (jnp.arange(2 * SSD_CHUNK)[:, None] // SSD_CHUNK == jnp.arange(2 * SSD_P)[None, :] // SSD_P
                 ).astype(BF16)
    consts = (cw, cb, dtb, alog, dexp, ng, _tri_incl(SSD_CHUNK), head_x, group_mask, pair_mask)
    return pl.pallas_call(
        _ssd_kernel,
        grid=(bsz // nb, length // SSD_CHUNK),
        in_specs=[pl.BlockSpec((nb, SSD_CHUNK, SSD_SLAB), lambda b, c: (b, c, 0)),
                  pl.BlockSpec((nb, SSD_CHUNK, SMALL_W), lambda b, c: (b, c, 0))]
                 + [_const_spec(a) for a in consts],
        out_specs=[pl.BlockSpec((nb, SSD_CHUNK, SSD_W), lambda b, c: (b, c, 0)),
                   pl.BlockSpec((nb, SSD_CONV_W - 1, SSD_CONV_DIM), lambda b, c: (b, 0, 0)),
                   pl.BlockSpec((nb, gn, hp), lambda b, c: (b, 0, 0))],
        out_shape=[jax.ShapeDtypeStruct((bsz, length, SSD_W), BF16),
                   jax.ShapeDtypeStruct((bsz, SSD_CONV_W - 1, SSD_CONV_DIM), F32),
                   jax.ShapeDtypeStruct((bsz, gn, hp), F32)],
        scratch_shapes=[pltpu.VMEM((nb, SSD_CARRY + SSD_CHUNK, SSD_CONV_DIM), F32),
                        pltpu.VMEM((nb, gn, hp), F32)],
        compiler_params=_cparams("arbitrary", "arbitrary"),
        name="ssd_scan",
    )(slab, small, *consts)


def _rwkv_token_math(rw, prev, mu, lora_w, w0, a0, k_k, k_a, head_ones):
    mix = rw + mu * (prev - rw)
    r = mix[:, 0:RWKV_W]
    k = mix[:, RWKV_W:2 * RWKV_W]
    v = mix[:, 2 * RWKV_W:3 * RWKV_W]
    lora = mix[:, 3 * RWKV_W:]
    lane = _iota(lora.shape, 1)
    act = jnp.where(lane < 32, jnp.tanh(lora), jnp.where(lane < 64, lora, _sigmoid(lora)))
    lo = _mm(act, lora_w)
    w_log = -_softplus(-(w0 + lo[:, 0:RWKV_W])) - 0.5
    logw = -jnp.exp(w_log)
    aic = _sigmoid(a0 + lo[:, RWKV_W:2 * RWKV_W])
    gate = lo[:, 2 * RWKV_W:]
    kkf = k * k_k
    kk = kkf * lax.rsqrt(_mm_xsel(kkf * kkf, head_ones, STAT_TERMS) + 1e-12)
    k2 = k * (1.0 + (aic - 1.0) * k_a)
    return r, k2, v, logw, aic, gate, kk


def _rwkv_finish(o, r, k2, v, gate, r_k, ln_g, ln_b, head_ones, head_avg):
    mu = _mm_xsel(o, head_avg, STAT_TERMS)
    d = o - mu
    var = _mm_xsel(d * d, head_avg, STAT_TERMS)
    on = d * lax.rsqrt(var + RWKV_GN_EPS) * ln_g + ln_b
    bonus = _mm_xsel(r * k2 * r_k, head_ones, STAT_TERMS) * v
    return (on + bonus) * gate


def _rwkv_kernel(rw_ref, mu_ref, lw_ref, w0_ref, a0_ref, kk_ref, ka_ref,
                 rk_ref, lng_ref, lnb_ref, ones_ref, avg_ref, tri_ref, strict_ref, incl_ref, eye_ref,
                 y_ref, shift_out_ref, s_out_ref, ext_ref, s_ref):
    c = pl.program_id(1)
    n_chunks = pl.num_programs(1)
    ch = RWKV_CHUNK
    nh = RWKV_H
    w = RWKV_W
    head_bd = ones_ref[...]
    head_bd_f = head_bd.astype(F32)
    tile4 = lambda x: jnp.concatenate([x] * nh, axis=0)

    nb = rw_ref.shape[0]

    @pl.when(c == 0)
    def _():
        ext_ref[:, 0:8, :] = jnp.zeros((nb, 8, RWKV_FEAT), F32)
        s_ref[...] = jnp.zeros_like(s_ref)

    def chain(j):
        rw = rw_ref[j].astype(F32)
        ext_ref[j, 8:8 + ch, :] = rw
        prev = ext_ref[j, 7:7 + ch, :]
        ext_ref[j, 7:8, :] = rw[ch - 1:ch, :]

        r, k2, v, logw, aic, gate, kk = _rwkv_token_math(
            rw, prev, mu_ref[...], lw_ref[...], w0_ref[...], a0_ref[...], kk_ref[...], ka_ref[...], head_bd)
        yield

        cumw = _mm_sel(tri_ref[...], logw)
        yield
        last = cumw[ch - 1:ch, :]
        inv_g = jnp.exp(-cumw)
        to_end = jnp.exp(last - cumw)
        kka = kk * aic
        a_t = (-kk * jnp.exp(cumw - logw)).astype(BF16)
        r_t = (r * jnp.exp(cumw)).astype(BF16)
        b_t = (kka * inv_g).astype(BF16)
        k_t = (k2 * inv_g).astype(BF16)
        bk_end = jnp.concatenate([kka * to_end, k2 * to_end], axis=0).astype(BF16)

        bd = lambda x: tile4(x.astype(BF16)) * head_bd
        nt = lambda x, y: lax.dot_general(x, y, (((1,), (1,)), ((), ())), preferred_element_type=F32)
        vb = v.astype(BF16)
        ar = jnp.concatenate([a_t, r_t], axis=0)
        prod = nt(ar, jnp.concatenate([bd(b_t), bd(k_t)], axis=0))
        a_ab = prod[0:ch, 0:w] * strict_ref[...]
        a_ak = prod[0:ch, w:2 * w] * strict_ref[...]
        a_rbk = (prod[ch:2 * ch, :] * incl_ref[...]).astype(BF16)
        yield
        s_bd = s_ref[j]
        uo = nt(ar, s_bd.astype(BF16))
        rhs = uo[0:ch, :] + jnp.dot(a_ak.astype(BF16), bd(vb), preferred_element_type=F32)
        yield

        t_inv = eye_ref[...] + a_ab
        x = a_ab
        x_bd = bd(x)
        power = 1
        while 2 * power < ch:
            x = jnp.dot(x.astype(BF16), x_bd, preferred_element_type=F32)
            x_bd = bd(x)
            yield
            t_inv = t_inv + jnp.dot(t_inv.astype(BF16), x_bd, preferred_element_type=F32)
            yield
            power *= 2

        p = jnp.dot(t_inv.astype(BF16), bd(rhs), preferred_element_type=F32)
        yield
        pb = p.astype(BF16)
        o = uo[ch:2 * ch, :] + jnp.dot(a_rbk, jnp.concatenate([bd(pb), bd(vb)], axis=0),
                                       preferred_element_type=F32)
        pv = jnp.concatenate([pb, vb], axis=0)
        yield

        upd = jnp.dot(pv.astype(F32).T.astype(BF16), bk_end, preferred_element_type=F32)
        s_ref[j] = s_bd * jnp.exp(last) + upd * head_bd_f
        yield

        y_ref[j] = _rwkv_finish(o, r, k2, v, gate, rk_ref[...], lng_ref[...], lnb_ref[...],
                                head_bd, avg_ref[...]).astype(y_ref.dtype)

    _interleave([chain(j) for j in range(nb)], skew=RWKV_SKEW)

    @pl.when(c == n_chunks - 1)
    def _():
        for j in range(nb):
            shift_out_ref[j] = ext_ref[j, 7:8, :]
            s_out_ref[j] = s_ref[j]


def _rwkv_consts():
    ch = RWKV_CHUNK
    assert ch == RWKV_N, "bd() reuses the head mask, which needs chunk rows == head width"
    t = jnp.arange(ch)[:, None]
    j = jnp.arange(2 * RWKV_W)[None, :] % ch
    strict = (j[:, :RWKV_W] < t).astype(F32)
    incl = (j <= t).astype(F32)
    eye = (j[:, :RWKV_W] == t).astype(F32)
    return (_seg_ones(RWKV_W, RWKV_N), _seg_ones(RWKV_W, RWKV_N, 1.0 / RWKV_N), _tri_incl(ch),
            strict, incl, eye)


def _rwkv_prompt(rw, *params):
    bsz, length, _ = rw.shape
    nb = RWKV_NB if bsz % RWKV_NB == 0 else 1
    consts = tuple(params) + _rwkv_consts()
    return pl.pallas_call(
        _rwkv_kernel,
        grid=(bsz // nb, length // RWKV_CHUNK),
        in_specs=[pl.BlockSpec((nb, RWKV_CHUNK, RWKV_FEAT), lambda b, c: (b, c, 0))]
                 + [_const_spec(a) for a in consts],
        out_specs=[pl.BlockSpec((nb, RWKV_CHUNK, RWKV_W), lambda b, c: (b, c, 0)),
                   pl.BlockSpec((nb, 1, RWKV_FEAT), lambda b, c: (b, 0, 0)),
                   pl.BlockSpec((nb, RWKV_W, RWKV_W), lambda b, c: (b, 0, 0))],
        out_shape=[jax.ShapeDtypeStruct((bsz, length, RWKV_W), BF16),
                   jax.ShapeDtypeStruct((bsz, 1, RWKV_FEAT), F32),
                   jax.ShapeDtypeStruct((bsz, RWKV_W, RWKV_W), F32)],
        scratch_shapes=[pltpu.VMEM((nb, RWKV_CHUNK + 8, RWKV_FEAT), F32),
                        pltpu.VMEM((nb, RWKV_W, RWKV_W), F32)],
        compiler_params=_cparams("arbitrary", "arbitrary"),
        name="rwkv_scan",
    )(rw, *consts)


def _gla_gate_log(small, gk2_pad, b_gk):
    x = _mm_hi(small, gk2_pad) + b_gk
    return -_softplus(-x) / GATE_NORMALIZER


def _gla_finish(o, gg, norm_g, head_avg):
    ms = _mm_xsel(o * o, head_avg, STAT_TERMS)
    return o * lax.rsqrt(ms + RMS_EPS) * norm_g * _silu(gg)


def _gla_kernel(slab_ref, small_ref, gk2_ref, bgk_ref, ng_ref, avg_ref, tri_ref, expand_ref,
                y_ref, s_out_ref, s_ref):
    c = pl.program_id(1)
    n_chunks = pl.num_programs(1)
    ch = GLA_CHUNK
    sub = GLA_SUB
    hk = GLA_HK
    expand = expand_ref[...]
    expand_f = expand.astype(F32)
    pos = _iota((ch, 1), 0) % sub
    t_idx = _iota((ch, 1), 0)
    hmask_k = (_iota((GLA_H * sub, hk), 0) // sub == _iota((GLA_H * sub, hk), 1) // GLA_DK).astype(F32)
    hmask_v = (_iota((GLA_H * sub, GLA_W), 0) // sub == _iota((GLA_H * sub, GLA_W), 1) // GLA_DV).astype(F32)

    nb = slab_ref.shape[0]

    @pl.when(c == 0)
    def _():
        s_ref[...] = jnp.zeros_like(s_ref)

    def chain(j):
        q = slab_ref[j, :, 0:hk].astype(F32) * (GLA_DK ** -0.5)
        k = slab_ref[j, :, hk:2 * hk].astype(F32)
        v = slab_ref[j, :, 2 * hk:2 * hk + GLA_W].astype(F32)
        gg = slab_ref[j, :, 2 * hk + GLA_W:].astype(F32)
        small = small_ref[j]

        lg = _gla_gate_log(small, gk2_ref[...], bgk_ref[...])
        yield
        cum = _mm_sel(tri_ref[...], lg)
        yield
        last = cum[ch - 1:ch, :]
        s_bd = s_ref[j]

        o = _mm(q * jnp.exp(cum), s_bd)

        group = 4
        for d0 in range(0, sub, group):
            prods, v_shift = [], []
            for delta in range(d0, d0 + group):
                if delta == 0:
                    k_s, c_s, v_s = k, cum, v
                else:
                    k_s = pltpu.roll(k, delta, 0)
                    c_s = pltpu.roll(cum, delta, 0)
                    v_s = pltpu.roll(v, delta, 0)
                valid = pos >= delta
                w_pair = jnp.exp(jnp.where(valid, cum - c_s, 0.0))
                prods.append(jnp.where(valid, q * k_s * w_pair, 0.0).astype(BF16))
                v_shift.append(v_s)
            yield
            att = jnp.dot(jnp.concatenate(prods, axis=0), expand, preferred_element_type=F32)
            o = o + sum(att[g * ch:(g + 1) * ch, :] * v_shift[g] for g in range(group))
            yield

        vb = v.astype(BF16)
        atts = []
        for blk in range(1, ch // sub):
            lo = blk * sub
            ref_pt = cum[lo - 1:lo, :]
            qd = q[lo:lo + sub, :] * jnp.exp(cum[lo:lo + sub, :] - ref_pt)
            kp = k * jnp.exp(jnp.where(t_idx < lo, ref_pt - cum, -jnp.inf))
            q4 = jnp.concatenate([qd] * GLA_H, axis=0) * hmask_k
            atts.append(_mm_nt(q4, kp).astype(BF16))
            yield
        o4 = jnp.dot(jnp.concatenate(atts, axis=0), vb, preferred_element_type=F32)
        yield
        cross = [jnp.zeros((sub, GLA_W), F32)]
        for blk in range(ch // sub - 1):
            o4_b = o4[blk * GLA_H * sub:(blk + 1) * GLA_H * sub, :] * hmask_v
            cross.append(sum(o4_b[h * sub:(h + 1) * sub, :] for h in range(GLA_H)))
        o = o + jnp.concatenate(cross, axis=0)

        kt_t = (k * jnp.exp(last - cum)).T
        s_ref[j] = s_bd * _col(jnp.exp(last)) + _mm(kt_t, vb) * expand_f
        yield

        y_ref[j] = _gla_finish(o, gg, ng_ref[...], avg_ref[...]).astype(y_ref.dtype)

    _interleave([chain(j) for j in range(nb)], skew=2)

    @pl.when(c == n_chunks - 1)
    def _():
        s_out_ref[...] = s_ref[...]


def _gla_prompt(slab, small, gk2_pad, b_gk, ng):
    bsz, length, _ = slab.shape
    nb = GLA_NB if bsz % GLA_NB == 0 else 1
    expand = (jnp.arange(GLA_HK)[:, None] // GLA_DK == jnp.arange(GLA_W)[None, :] // GLA_DV).astype(BF16)
    consts = (gk2_pad, b_gk, ng, _seg_ones(GLA_W, GLA_DV, 1.0 / GLA_DV), _tri_incl(GLA_CHUNK), expand)
    return pl.pallas_call(
        _gla_kernel,
        grid=(bsz // nb, length // GLA_CHUNK),
        in_specs=[pl.BlockSpec((nb, GLA_CHUNK, GLA_SLAB), lambda b, c: (b, c, 0)),
                  pl.BlockSpec((nb, GLA_CHUNK, SMALL_W), lambda b, c: (b, c, 0))]
                 + [_const_spec(a) for a in consts],
        out_specs=[pl.BlockSpec((nb, GLA_CHUNK, GLA_W), lambda b, c: (b, c, 0)),
                   pl.BlockSpec((nb, GLA_HK, GLA_W), lambda b, c: (b, 0, 0))],
        out_shape=[jax.ShapeDtypeStruct((bsz, length, GLA_W), BF16),
                   jax.ShapeDtypeStruct((bsz, GLA_HK, GLA_W), F32)],
        scratch_shapes=[pltpu.VMEM((nb, GLA_HK, GLA_W), F32)],
        compiler_params=_cparams("arbitrary", "arbitrary"),
        name="gla_scan",
    )(slab, small, *consts)


def _col(row):
    n = row.shape[1]
    eye = _iota((n, n), 0) == _iota((n, n), 1)
    return jnp.sum(jnp.where(eye, row, 0.0), axis=1, keepdims=True)


def _store_t(dst_ref, row0, x):
    for cblk in range(x.shape[1] // LANES):
        dst_ref[row0 + cblk * LANES:row0 + (cblk + 1) * LANES, :] = x[:, cblk * LANES:(cblk + 1) * LANES].T


def _load_t(src_ref, width):
    return jnp.concatenate([src_ref[cblk * LANES:(cblk + 1) * LANES, :].T for cblk in range(width // LANES)],
                           axis=1)


_TS_X, _TS_B, _TS_C, _TS_DT, _TS_DA = 0, SSD_W, SSD_W + SSD_G * SSD_N, SSD_CONV_DIM, SSD_CONV_DIM + SMALL_W
_TR_R, _TR_W, _TR_K, _TR_V, _TR_KK, _TR_KA = (i * RWKV_W for i in range(6))
_TG_Q, _TG_K, _TG_G, _TG_V = 0, GLA_HK, 2 * GLA_HK, 3 * GLA_HK


def _step_kernel(ssd_ref, small_ref, rw_ref, gla_ref, conv0_ref, shift0_ref, h0_ref, rs0_ref, gs0_ref,
                 cw_ref, cb_ref, dtb_ref, alog_ref, dexp_ref, sng_ref,
                 mu_ref, lw_ref, w0_ref, a0_ref, kk_ref, ka_ref, rk_ref, lng_ref, lnb_ref,
                 gk2_ref, bgk_ref, gng_ref, ones_ref, ravg_ref, gavg_ref,
                 y_ssd_ref, y_rwkv_ref, y_gla_ref, conv_out_ref, h_out_ref, rs_out_ref, gs_out_ref,
                 t_ssd, t_rwkv, t_gla, o_ssd_t, o_rwkv_t, o_gla_t):
    i = pl.program_id(0)
    hk = GLA_HK

    def ssd_tokens():
        xbc = ssd_ref[:, SSD_W:].astype(F32)
        taps = [conv0_ref[:, t * SSD_CONV_DIM:(t + 1) * SSD_CONV_DIM] for t in range(SSD_CONV_W - 1)]
        dt, dta = _ssd_dt(small_ref[...], dtb_ref[...], alog_ref[...])
        return xbc, (_ssd_conv_act(xbc, taps, cw_ref[...], cb_ref[...]), dt, dta)

    def rwkv_tokens():
        return _rwkv_token_math(rw_ref[...].astype(F32), shift0_ref[...], mu_ref[...], lw_ref[...],
                                w0_ref[...], a0_ref[...], kk_ref[...], ka_ref[...], ones_ref[...])

    @pl.when(i == 0)
    def _():
        xbc, (act, dt, dta) = ssd_tokens()
        conv_out_ref[:, 0:2 * SSD_CONV_DIM] = conv0_ref[:, SSD_CONV_DIM:]
        conv_out_ref[:, 2 * SSD_CONV_DIM:] = xbc
        _store_t(t_ssd, _TS_X, act)
        _store_t(t_ssd, _TS_DT, dt)
        _store_t(t_ssd, _TS_DA, jnp.exp(dta))
        r, k2, v, logw, aic, gate, kk = rwkv_tokens()
        for off, x in ((_TR_R, r), (_TR_W, jnp.exp(logw)), (_TR_K, k2), (_TR_V, v), (_TR_KK, kk),
                       (_TR_KA, kk * aic)):
            _store_t(t_rwkv, off, x)
        lg = _gla_gate_log(small_ref[...], gk2_ref[...], bgk_ref[...])
        _store_t(t_gla, _TG_Q, gla_ref[:, 0:hk].astype(F32) * (GLA_DK ** -0.5))
        _store_t(t_gla, _TG_K, gla_ref[:, hk:2 * hk].astype(F32))
        _store_t(t_gla, _TG_G, jnp.exp(lg))
        _store_t(t_gla, _TG_V, gla_ref[:, 2 * hk:2 * hk + GLA_W].astype(F32))
        o_gla_t[...] = jnp.zeros_like(o_gla_t)

    def row(ref, idx):
        return ref[pl.ds(idx, 1), :]

    def rows(ref, idx, count):
        return ref[pl.ds(pl.multiple_of(idx, count), count), :]

    units = h0_ref.shape[0] // SSD_N
    per_head = SSD_P // units
    h = i // per_head
    p0 = (i % per_head) * units
    b_t = rows(t_ssd, _TS_B + (h // (SSD_H // SSD_G)) * SSD_N, SSD_N)
    c_t = rows(t_ssd, _TS_C + (h // (SSD_H // SSD_G)) * SSD_N, SSD_N)
    da_row = row(t_ssd, _TS_DA + DT_OFF + h)
    dt_row = row(t_ssd, _TS_DT + DT_OFF + h)
    for u in range(units):
        p_idx = h * SSD_P + p0 + u
        blk = slice(u * SSD_N, (u + 1) * SSD_N)
        h_new = h0_ref[blk, :] * da_row + (row(t_ssd, _TS_X + p_idx) * dt_row) * b_t
        o_ssd_t[pl.ds(p_idx, 1), :] = jnp.sum(h_new * c_t, axis=0, keepdims=True)
        h_out_ref[blk, :] = h_new

    units = rs0_ref.shape[0] // RWKV_N
    per_head = RWKV_N // units
    h = i // per_head
    v0 = (i % per_head) * units
    seg = lambda off: rows(t_rwkv, off + h * RWKV_N, RWKV_N)
    r_t, w_t, k_t, kk_t, ka_t = seg(_TR_R), seg(_TR_W), seg(_TR_K), seg(_TR_KK), seg(_TR_KA)
    for u in range(units):
        v_idx = h * RWKV_N + v0 + u
        blk = slice(u * RWKV_N, (u + 1) * RWKV_N)
        s = rs0_ref[blk, :]
        sa = jnp.sum(s * (-kk_t), axis=0, keepdims=True)
        s_new = s * w_t + sa * ka_t + row(t_rwkv, _TR_V + v_idx) * k_t
        o_rwkv_t[pl.ds(v_idx, 1), :] = jnp.sum(s_new * r_t, axis=0, keepdims=True)
        rs_out_ref[blk, :] = s_new

    units = gs0_ref.shape[0] // GLA_DV
    per_head = GLA_DK // units
    h = i // per_head
    k0 = (i % per_head) * units
    v_t = rows(t_gla, _TG_V + h * GLA_DV, GLA_DV)
    acc = jnp.zeros((GLA_DV, LANES), F32)
    for u in range(units):
        k_idx = h * GLA_DK + k0 + u
        blk = slice(u * GLA_DV, (u + 1) * GLA_DV)
        s_new = gs0_ref[blk, :] * row(t_gla, _TG_G + k_idx) + row(t_gla, _TG_K + k_idx) * v_t
        acc = acc + row(t_gla, _TG_Q + k_idx) * s_new
        gs_out_ref[blk, :] = s_new
    o_rows = pl.ds(pl.multiple_of(h * GLA_DV, GLA_DV), GLA_DV)
    o_gla_t[o_rows, :] = o_gla_t[o_rows, :] + acc

    @pl.when(i == pl.num_programs(0) - 1)
    def _():
        _, (act, _, _) = ssd_tokens()
        y_ssd_ref[...] = _ssd_finish(_load_t(o_ssd_t, SSD_W), act[:, 0:SSD_W], ssd_ref[:, 0:SSD_W].astype(F32),
                                     dexp_ref[...], sng_ref[...]).astype(y_ssd_ref.dtype)
        r, k2, v, _, _, gate, _ = rwkv_tokens()
        y_rwkv_ref[...] = _rwkv_finish(_load_t(o_rwkv_t, RWKV_W), r, k2, v, gate, rk_ref[...], lng_ref[...],
                                       lnb_ref[...], ones_ref[...], ravg_ref[...]).astype(y_rwkv_ref.dtype)
        y_gla_ref[...] = _gla_finish(_load_t(o_gla_t, GLA_W), gla_ref[:, 2 * hk + GLA_W:].astype(F32),
                                     gng_ref[...], gavg_ref[...]).astype(y_gla_ref.dtype)


def _mixer_step(ssd_slab, small, rw, gla_slab, conv0, shift0, h0, rs0, gs0, ssd_p, rwkv_p, gla_p):
    bsz = ssd_slab.shape[0]
    assert bsz == LANES, "the single-token mixer keeps exactly one vreg row of batch entries on the lanes"
    flat_t = lambda s: s.reshape(bsz, -1).T
    h0, rs0, gs0 = flat_t(h0), flat_t(rs0), flat_t(gs0)
    consts = (tuple(ssd_p) + tuple(rwkv_p) + tuple(gla_p)
              + (_seg_ones(RWKV_W, RWKV_N), _seg_ones(RWKV_W, RWKV_N, 1.0 / RWKV_N),
                 _seg_ones(GLA_W, GLA_DV, 1.0 / GLA_DV)))
    full = lambda a: pl.BlockSpec(a.shape, lambda i: (0,) * a.ndim)
    cols = lambda a: pl.BlockSpec((a.shape[0] // STEP_GRID, bsz), lambda i: (i, 0))
    tokens = (ssd_slab, small, rw, gla_slab, conv0, shift0)
    outs = pl.pallas_call(
        _step_kernel,
        grid=(STEP_GRID,),
        in_specs=[full(a) for a in tokens] + [cols(h0), cols(rs0), cols(gs0)] + [full(a) for a in consts],
        out_specs=[full(jax.ShapeDtypeStruct((bsz, w), BF16)) for w in (SSD_W, RWKV_W, GLA_W)]
                  + [full(conv0), cols(h0), cols(rs0), cols(gs0)],
        out_shape=[jax.ShapeDtypeStruct((bsz, SSD_W), BF16),
                   jax.ShapeDtypeStruct((bsz, RWKV_W), BF16),
                   jax.ShapeDtypeStruct((bsz, GLA_W), BF16),
                   jax.ShapeDtypeStruct(conv0.shape, F32),
                   jax.ShapeDtypeStruct(h0.shape, F32),
                   jax.ShapeDtypeStruct(rs0.shape, F32),
                   jax.ShapeDtypeStruct(gs0.shape, F32)],
        scratch_shapes=[pltpu.VMEM((SSD_CONV_DIM + 2 * SMALL_W, LANES), F32),
                        pltpu.VMEM((6 * RWKV_W, LANES), F32),
                        pltpu.VMEM((3 * GLA_HK + GLA_W, LANES), F32),
                        pltpu.VMEM((SSD_W, LANES), F32),
                        pltpu.VMEM((RWKV_W, LANES), F32),
                        pltpu.VMEM((GLA_W, LANES), F32)],
        compiler_params=_cparams("arbitrary"),
        name="mixer_step",
    )(*tokens, h0, rs0, gs0, *consts)
    y_ssd, y_rwkv, y_gla, conv_new, h_new, rs_new, gs_new = outs
    return (y_ssd, y_rwkv, y_gla, conv_new.reshape(bsz, SSD_CONV_W - 1, SSD_CONV_DIM),
            h_new.T.reshape(bsz, SSD_H, SSD_P, SSD_N), rs_new.T.reshape(bsz, RWKV_H, RWKV_N, RWKV_N),
            gs_new.T.reshape(bsz, GLA_H, GLA_DK, GLA_DV))


def _mod_rows(mod_ref, rows):
    return mod_ref[0] if mod_ref.shape[1] == 1 else mod_ref[0, rows, :]


def _row_parts(tm):
    return 2 if tm % 256 == 0 else 1


def _outproj_kernel(ys_ref, yr_ref, yg_ref, x_ref, gt_ref, sc_ref, sh_ref, w_ref, g_ref, b_ref, rt_ref,
                    o_ref, logit_ref):
    tm = x_ref.shape[1]
    parts = _row_parts(tm)

    def chain(r):
        rows = slice(r * tm // parts, (r + 1) * tm // parts)
        m = (jnp.dot(ys_ref[0, rows, :], w_ref[0:SSD_W, :], preferred_element_type=F32)
             + jnp.dot(yr_ref[0, rows, :], w_ref[SSD_W:SSD_W + RWKV_W, :], preferred_element_type=F32)
             + jnp.dot(yg_ref[0, rows, :], w_ref[SSD_W + RWKV_W:, :], preferred_element_type=F32))
        yield
        u = ALPHA * x_ref[0, rows, :] + (1.0 + _mod_rows(gt_ref, rows)) * m
        x1 = _layernorm(u, g_ref[...], b_ref[...])
        o_ref[0, rows, :] = x1
        h = x1 * (1.0 + _mod_rows(sc_ref, rows)) + _mod_rows(sh_ref, rows)
        logit_ref[0, rows, :] = _mm_hi(h, rt_ref[...])

    _interleave([chain(r) for r in range(parts)], skew=1)


def _outproj(y_ssd, y_rwkv, y_gla, x, gt, sc2, sh2, w_out, ln_g, ln_b, router_pad, tm):
    bsz, length, _ = x.shape
    tok = lambda width: pl.BlockSpec((1, tm, width), lambda b, i: (b, i, 0))
    vec = pl.BlockSpec((1, D_MODEL), lambda b, i: (0, 0))
    return pl.pallas_call(
        _outproj_kernel,
        grid=(bsz, length // tm),
        in_specs=[tok(SSD_W), tok(RWKV_W), tok(GLA_W), tok(D_MODEL),
                  _mod_spec(gt, tm), _mod_spec(sc2, tm), _mod_spec(sh2, tm),
                  pl.BlockSpec((D_MODEL, D_MODEL), lambda b, i: (0, 0)), vec, vec,
                  pl.BlockSpec((D_MODEL, LANES), lambda b, i: (0, 0))],
        out_specs=[tok(D_MODEL), tok(LANES)],
        out_shape=[jax.ShapeDtypeStruct((bsz, length, D_MODEL), F32),
                   jax.ShapeDtypeStruct((bsz, length, LANES), F32)],
        compiler_params=_cparams("arbitrary", "arbitrary"),
        name="outproj_ln",
    )(y_ssd, y_rwkv, y_gla, x, gt, sc2, sh2, w_out, ln_g, ln_b, router_pad)


def _ffn_kernel(ys_ref, yr_ref, yg_ref, x_ref, gt1_ref, sc_ref, sh_ref, gt2_ref, wo_ref, g1_ref, b1_ref,
                wg_ref, wu_ref, wd_ref, g2_ref, b2_ref, o_ref):
    tm = x_ref.shape[1]
    parts = _row_parts(tm)

    def chain(r):
        rows = slice(r * tm // parts, (r + 1) * tm // parts)
        m = (jnp.dot(ys_ref[0, rows, :], wo_ref[0:SSD_W, :], preferred_element_type=F32)
             + jnp.dot(yr_ref[0, rows, :], wo_ref[SSD_W:SSD_W + RWKV_W, :], preferred_element_type=F32)
             + jnp.dot(yg_ref[0, rows, :], wo_ref[SSD_W + RWKV_W:, :], preferred_element_type=F32))
        yield
        x1 = _layernorm(ALPHA * x_ref[0, rows, :] + (1.0 + _mod_rows(gt1_ref, rows)) * m,
                        g1_ref[...], b1_ref[...])
        h = (x1 * (1.0 + _mod_rows(sc_ref, rows)) + _mod_rows(sh_ref, rows)).astype(BF16)
        yield
        gate = jnp.dot(h, wg_ref[...], preferred_element_type=F32)
        up = jnp.dot(h, wu_ref[...], preferred_element_type=F32)
        yield
        f = jnp.dot((_silu(gate) * up).astype(BF16), wd_ref[...], preferred_element_type=F32)
        yield
        u = ALPHA * x1 + (1.0 + _mod_rows(gt2_ref, rows)) * f
        o_ref[0, rows, :] = _layernorm(u, g2_ref[...], b2_ref[...])

    _interleave([chain(r) for r in range(parts)], skew=1)


def _outproj_ffn(y_ssd, y_rwkv, y_gla, x, gt1, sc2, sh2, gt2, w_out, ln1_g, ln1_b, wg, wu, wd, ln2_g, ln2_b, tm):
    bsz, length, _ = x.shape
    tok = lambda width: pl.BlockSpec((1, tm, width), lambda b, i: (b, i, 0))
    vec = pl.BlockSpec((1, D_MODEL), lambda b, i: (0, 0))
    resident = lambda a: pl.BlockSpec(a.shape, lambda b, i: (0, 0), pipeline_mode=pl.Buffered(1))
    return pl.pallas_call(
        _ffn_kernel,
        grid=(bsz, length // tm),
        in_specs=[tok(SSD_W), tok(RWKV_W), tok(GLA_W), tok(D_MODEL),
                  _mod_spec(gt1, tm), _mod_spec(sc2, tm), _mod_spec(sh2, tm), _mod_spec(gt2, tm),
                  resident(w_out), vec, vec, resident(wg), resident(wu), resident(wd), vec, vec],
        out_specs=tok(D_MODEL),
        out_shape=jax.ShapeDtypeStruct((bsz, length, D_MODEL), F32),
        compiler_params=_cparams("arbitrary", "arbitrary"),
        name="outproj_ffn_ln",
    )(y_ssd, y_rwkv, y_gla, x, gt1, sc2, sh2, gt2, w_out, ln1_g, ln1_b, wg, wu, wd, ln2_g, ln2_b)


def _moe_kernel(x_ref, sc_ref, sh_ref, gt_ref, logit_ref, wg_ref, wu_ref, wd_ref, g_ref, b_ref, tri_ref, o_ref,
                h_ref, rank_t_ref, comb_t_ref, cnt_ref, acc_ref, *, half, cap):
    e = pl.program_id(2)
    tm = 2 * half

    @pl.when(e == 0)
    def _():
        h_ref[...] = (x_ref[0] * (1.0 + sc_ref[0]) + sh_ref[0]).astype(BF16)
        acc_ref[...] = jnp.zeros_like(acc_ref)
        lane = _iota((tm, LANES), 1).astype(F32)
        logits = jnp.where(lane < N_EXPERTS, logit_ref[0], -jnp.inf)
        m1 = jnp.max(logits, axis=-1, keepdims=True)
        i1 = jnp.min(jnp.where(logits == m1, lane, float(LANES)), axis=-1, keepdims=True)
        rest = jnp.where(lane == i1, -jnp.inf, logits)
        m2 = jnp.max(rest, axis=-1, keepdims=True)
        i2 = jnp.min(jnp.where(rest == m2, lane, float(LANES)), axis=-1, keepdims=True)
        e2 = jnp.exp(m2 - m1)
        den = 1.0 + e2
        comb = jnp.where(lane == i1, 1.0 / den, 0.0) + jnp.where(lane == i2, e2 / den, 0.0)
        sel = (lane == i1) | (lane == i2)
        sel_f = sel.astype(F32)
        ranks = []
        for hf in range(2):
            s = sel_f[hf * half:(hf + 1) * half, :]
            before = jnp.dot(tri_ref[...], s.astype(BF16), preferred_element_type=F32)
            ranks.append(jnp.where(sel[hf * half:(hf + 1) * half, :], before, -1.0))
            cnt_ref[hf:hf + 1, :] = jnp.sum(s, axis=0, keepdims=True)
        rank = jnp.concatenate(ranks, axis=0)
        for blk in range(tm // LANES):
            rows = slice(blk * LANES, (blk + 1) * LANES)
            rank_t_ref[:, rows] = rank[rows, :].T
            comb_t_ref[:, rows] = comb[rows, :].T

    rank_t_e = rank_t_ref[pl.ds(e, 1), :]
    comb_t_e = comb_t_ref[pl.ds(e, 1), :]
    n_max = jnp.max(jnp.where(_iota((2, LANES), 1) == e, cnt_ref[...], 0.0))
    n_pass = (n_max.astype(jnp.int32) + (cap - 1)) // cap

    def one_pass(p, carry):
        slot = _iota((cap, 1), 0).astype(F32) + (p * cap).astype(F32)
        picks, xs, gates = [], [], []
        for hf in range(2):
            rows = slice(hf * half, (hf + 1) * half)
            hit = rank_t_e[:, rows] == slot
            pick = hit.astype(BF16)
            picks.append(pick)
            xs.append(jnp.dot(pick, h_ref[rows, :], preferred_element_type=F32).astype(BF16))
            gates.append(jnp.sum(jnp.where(hit, comb_t_e[:, rows], 0.0), axis=1, keepdims=True))
        xc = jnp.concatenate(xs, axis=0)
        a = _silu(jnp.dot(xc, wg_ref[0], preferred_element_type=F32)) * jnp.dot(
            xc, wu_ref[0], preferred_element_type=F32)
        out_e = jnp.dot(a.astype(BF16), wd_ref[0], preferred_element_type=F32)
        for hf in range(2):
            rows = slice(hf * half, (hf + 1) * half)
            weighted = (out_e[hf * cap:(hf + 1) * cap, :] * gates[hf]).astype(BF16)
            acc_ref[rows, :] += lax.dot_general(picks[hf], weighted, (((0,), (0,)), ((), ())),
                                                preferred_element_type=F32)
        return carry

    one_pass(jnp.int32(0), 0)
    lax.fori_loop(1, n_pass, one_pass, 0)

    @pl.when(e == pl.num_programs(2) - 1)
    def _():
        u = ALPHA * x_ref[0] + (1.0 + gt_ref[0]) * acc_ref[...]
        o_ref[0] = _layernorm(u, g_ref[...], b_ref[...])


def _moe(x, sc, sh, gt, logits, wg, wu, wd, ln_g, ln_b, tm):
    bsz, length, _ = x.shape
    assert tm % LANES == 0
    half = tm // 2
    cap = MOE_CAP_ROWS if half >= 2 * MOE_CAP_ROWS else half
    tri = (jnp.arange(half)[None, :] < jnp.arange(half)[:, None]).astype(BF16)
    tok = pl.BlockSpec((1, tm, D_MODEL), lambda b, i, e: (b, i, 0))
    vec = pl.BlockSpec((1, D_MODEL), lambda b, i, e: (0, 0))
    wspec = pl.BlockSpec((1, D_MODEL, D_MODEL), lambda b, i, e: (e, 0, 0))
    return pl.pallas_call(
        functools.partial(_moe_kernel, half=half, cap=cap),
        grid=(bsz, length // tm, N_EXPERTS),
        in_specs=[tok, _mod_spec(sc, tm), _mod_spec(sh, tm), _mod_spec(gt, tm),
                  pl.BlockSpec((1, tm, LANES), lambda b, i, e: (b, i, 0)),
                  wspec, wspec, wspec, vec, vec, _const_spec(tri)],
        out_specs=tok,
        out_shape=jax.ShapeDtypeStruct((bsz, length, D_MODEL), F32),
        scratch_shapes=[pltpu.VMEM((tm, D_MODEL), BF16), pltpu.VMEM((LANES, tm), F32),
                        pltpu.VMEM((LANES, tm), F32), pltpu.VMEM((2, LANES), F32),
                        pltpu.VMEM((tm, D_MODEL), F32)],
        compiler_params=_cparams("arbitrary", "arbitrary", "arbitrary"),
        name="moe_ln",
    )(x, sc, sh, gt, logits, wg, wu, wd, ln_g, ln_b, tri)


def _pad_lanes(vec, offset, width=SMALL_W):
    out = jnp.zeros((1, width), F32)
    return out.at[0, offset:offset + vec.shape[0]].set(vec)


def _layer_params(p, l):
    w_in = p["w_in"][l]
    off = [0]
    for s in (SSD_W, SSD_CONV_DIM, SSD_H, RWKV_FEAT, GLA_HK, GLA_HK, GLA_W, GK_LORA, GLA_W):
        off.append(off[-1] + s)
    piece = lambda i: w_in[:, off[i]:off[i + 1]]
    small = jnp.zeros((D_MODEL, SMALL_W), F32)
    small = small.at[:, DT_OFF:DT_OFF + SSD_H].set(piece(2)).at[:, GLO_OFF:GLO_OFF + GK_LORA].set(piece(7))
    w_pad = jnp.concatenate([piece(0), piece(1), small, piece(3), piece(4), piece(5), piece(6), piece(8)],
                            axis=1).astype(BF16)
    ssd_p = (p["ssd_conv_w"][l], p["ssd_conv_b"][l][None, :],
             _pad_lanes(p["ssd_dt_bias"][l], DT_OFF), _pad_lanes(p["ssd_a_log"][l], DT_OFF),
             jnp.repeat(p["ssd_d"][l], SSD_P)[None, :], p["ssd_norm_g"][l][None, :])
    lora_w = jnp.zeros((LANES, 3 * RWKV_W), F32)
    lora_w = (lora_w.at[0:32, 0:RWKV_W].set(p["rwkv_w2"][l])
              .at[32:64, RWKV_W:2 * RWKV_W].set(p["rwkv_a2"][l])
              .at[64:128, 2 * RWKV_W:].set(p["rwkv_g2"][l])).astype(BF16)
    row = lambda name: p[name][l].reshape(1, -1)
    rwkv_p = (row("rwkv_mu"), lora_w, row("rwkv_w0"), row("rwkv_a0"), row("rwkv_k_k"), row("rwkv_k_a"),
              row("rwkv_r_k"), row("rwkv_ln_g"), row("rwkv_ln_b"))
    gk2_pad = jnp.zeros((SMALL_W, GLA_HK), F32).at[GLO_OFF:GLO_OFF + GK_LORA].set(p["gla_w_gk2"][l])
    gla_p = (gk2_pad, row("gla_b_gk"), jnp.tile(p["gla_norm_g"][l], GLA_H)[None, :])
    return w_pad, ssd_p, rwkv_p, gla_p


def _block_diag_inv(s_bd, h):
    b, hr, hc = s_bd.shape
    r, c = hr // h, hc // h
    s = s_bd.reshape(b, h, r, h, c)
    return jnp.stack([s[:, i, :, i, :] for i in range(h)], axis=1)


def _ssd_state_unpack(h_bd):
    b = h_bd.shape[0]
    s = h_bd.reshape(b, SSD_G, SSD_N, SSD_H, SSD_P)
    per_group = SSD_H // SSD_G
    heads = [s[:, h // per_group, :, h, :] for h in range(SSD_H)]
    return jnp.swapaxes(jnp.stack(heads, axis=1), 2, 3)


def _tail(x, mod_l, l, p, y_ssd, y_rwkv, y_gla, tm):
    sh1, sc1, gt1, sh2, sc2, gt2 = mod_l
    row = lambda name: p[name][l].reshape(1, -1)
    w_out = p["w_out"][l].astype(BF16)
    i = l // 2
    if l % 2 == 0:
        x = _outproj_ffn(y_ssd, y_rwkv, y_gla, x, gt1, sc2, sh2, gt2, w_out, row("ln_mix_g"), row("ln_mix_b"),
                         p["ffn_w_gate"][i].astype(BF16), p["ffn_w_up"][i].astype(BF16),
                         p["ffn_w_down"][i].astype(BF16), row("ln_ffn_g"), row("ln_ffn_b"), tm)
    else:
        router_pad = jnp.zeros((D_MODEL, LANES), F32).at[:, :N_EXPERTS].set(p["moe_router"][i])
        x, logits = _outproj(y_ssd, y_rwkv, y_gla, x, gt1, sc2, sh2, w_out, row("ln_mix_g"), row("ln_mix_b"),
                             router_pad, tm)
        x = _moe(x, sc2, sh2, gt2, logits, p["moe_w_gate"][i].astype(BF16), p["moe_w_up"][i].astype(BF16),
                 p["moe_w_down"][i].astype(BF16), row("ln_ffn_g"), row("ln_ffn_b"), min(2 * tm, x.shape[1]))
    return x


def _forward(x_prompt, x_sample, c_prompt, c_sample, states, p):
    bp, seq, _ = x_prompt.shape
    bs = x_sample.shape[0]
    state_ssd, state_conv, state_rwkv, state_shift, state_gla = states
    mod = _ada(jnp.concatenate([c_prompt, c_sample], axis=0), p["w_ada"], p["b_ada"])

    xp = x_prompt
    xs = x_sample.reshape(1, bs, D_MODEL)
    tm_p = min(512, seq)
    outs_p = [[] for _ in range(5)]
    outs_s = [[] for _ in range(5)]
    for l in range(DEPTH):
        w_pad, ssd_p, rwkv_p, gla_p = _layer_params(p, l)
        mods = jnp.split(mod[l], 6, axis=-1)
        mod_p = [m[:bp, None, :] for m in mods]
        mod_s = [m[None, bp:, :] for m in mods]

        ssd_slab, small, rw, gla_slab = _inproj(xp, mod_p[1], mod_p[0], w_pad, min(2 * tm_p, seq))
        y_ssd, conv_new, h_new = _ssd_prompt(ssd_slab, small, *ssd_p)
        h_new = _ssd_state_unpack(h_new)
        y_rwkv, shift_new, rs_bd = _rwkv_prompt(rw, *rwkv_p)
        y_gla, gs_bd = _gla_prompt(gla_slab, small, *gla_p)
        xp = _tail(xp, mod_p, l, p, y_ssd, y_rwkv, y_gla, tm_p)
        for acc, s in zip(outs_p, (h_new, conv_new, _block_diag_inv(rs_bd, RWKV_H),
                                   shift_new.reshape(bp, RWKV_FEAT), _block_diag_inv(gs_bd, GLA_H))):
            acc.append(s)

        ssd_slab, small, rw, gla_slab = _inproj(xs, mod_s[1], mod_s[0], w_pad, bs)
        y_ssd, y_rwkv, y_gla, conv_new, h_new, rs_new, gs_new = _mixer_step(
            ssd_slab[0], small[0], rw[0], gla_slab[0], state_conv[l].reshape(bs, -1), state_shift[l],
            state_ssd[l], state_rwkv[l], state_gla[l], ssd_p, rwkv_p, gla_p)
        xs = _tail(xs, mod_s, l, p, y_ssd[None], y_rwkv[None], y_gla[None], bs)
        for acc, s in zip(outs_s, (h_new, conv_new, rs_new, rw[0].astype(F32), gs_new)):
            acc.append(s)

    stack = lambda accs: tuple(jnp.stack(a, axis=0) for a in accs)
    return (xp, xs.reshape(bs, 1, D_MODEL)) + stack(outs_p) + stack(outs_s)


def kernel(x_prompt, x_sample, c_prompt, c_sample, state_ssd, state_ssd_conv, state_rwkv, state_rwkv_shift, state_gla, w_ada, b_ada, w_in, w_out, ssd_conv_w, ssd_conv_b, ssd_dt_bias, ssd_a_log, ssd_d, ssd_norm_g, rwkv_mu, rwkv_w0, rwkv_w2, rwkv_a0, rwkv_a2, rwkv_g2, rwkv_k_k, rwkv_k_a, rwkv_r_k, rwkv_ln_g, rwkv_ln_b, gla_w_gk2, gla_b_gk, gla_norm_g, ln_mix_g, ln_mix_b, ln_ffn_g, ln_ffn_b, ffn_w_gate, ffn_w_up, ffn_w_down, moe_router, moe_w_gate, moe_w_up, moe_w_down):
    p = dict(w_ada=w_ada, b_ada=b_ada, w_in=w_in, w_out=w_out, ssd_conv_w=ssd_conv_w, ssd_conv_b=ssd_conv_b,
             ssd_dt_bias=ssd_dt_bias, ssd_a_log=ssd_a_log, ssd_d=ssd_d, ssd_norm_g=ssd_norm_g,
             rwkv_mu=rwkv_mu, rwkv_w0=rwkv_w0, rwkv_w2=rwkv_w2, rwkv_a0=rwkv_a0, rwkv_a2=rwkv_a2,
             rwkv_g2=rwkv_g2, rwkv_k_k=rwkv_k_k, rwkv_k_a=rwkv_k_a, rwkv_r_k=rwkv_r_k,
             rwkv_ln_g=rwkv_ln_g, rwkv_ln_b=rwkv_ln_b, gla_w_gk2=gla_w_gk2, gla_b_gk=gla_b_gk,
             gla_norm_g=gla_norm_g, ln_mix_g=ln_mix_g, ln_mix_b=ln_mix_b, ln_ffn_g=ln_ffn_g,
             ln_ffn_b=ln_ffn_b, ffn_w_gate=ffn_w_gate, ffn_w_up=ffn_w_up, ffn_w_down=ffn_w_down,
             moe_router=moe_router, moe_w_gate=moe_w_gate, moe_w_up=moe_w_up, moe_w_down=moe_w_down)
    states = (state_ssd, state_ssd_conv, state_rwkv, state_rwkv_shift, state_gla)
    return _forward(x_prompt, x_sample, c_prompt, c_sample, states, p)
```

```python
import functools

import jax
import jax.numpy as jnp
from jax import lax
from jax.experimental import pallas as pl
from jax.experimental.pallas import tpu as pltpu

F32 = jnp.float32
BF16 = jnp.bfloat16

D_MODEL = 1024
DEPTH = 2
SSD_W = 512
SSD_H = 8
SSD_P = 64
SSD_N = 64
SSD_G = 2
SSD_CONV_W = 4
SSD_CONV_DIM = 768
RWKV_W = 256
RWKV_H = 4
RWKV_N = 64
RWKV_FEAT = 896
RWKV_GN_EPS = RWKV_N * 1e-5
GLA_W = 256
GLA_H = 4
GLA_DK = 32
GLA_DV = 64
GLA_HK = GLA_H * GLA_DK
GK_LORA = 16
GATE_NORMALIZER = 16.0
F_DENSE = 2816
N_EXPERTS = 8
ALPHA = (2.0 * DEPTH) ** 0.25
LN_EPS = 1e-5
RMS_EPS = 1e-6

LANES = 128
SMALL_W = LANES
DT_OFF = 0
GLO_OFF = 8
SSD_SLAB = SSD_W + SSD_CONV_DIM
GLA_SLAB = 2 * GLA_HK + 2 * GLA_W
IN_PAD = SSD_SLAB + SMALL_W + RWKV_FEAT + GLA_SLAB
SSD_CHUNK = 128
SSD_CARRY = 8
GLA_CHUNK = 128
GLA_SUB = 8
RWKV_CHUNK = 64
SSD_NB, RWKV_NB, GLA_NB = 4, 8, 8
RWKV_SKEW = 1
STAT_TERMS = 1
MOE_CAP_ROWS = 160
STEP_GRID = 8
VMEM_LIMIT = 56 * 1024 * 1024


def _cparams(*sem):
    return pltpu.CompilerParams(dimension_semantics=sem, vmem_limit_bytes=VMEM_LIMIT)


def _mm(a, b):
    return jnp.dot(a.astype(BF16), b.astype(BF16), preferred_element_type=F32)


def _mm_nt(a, b):
    return lax.dot_general(a.astype(BF16), b.astype(BF16), (((1,), (1,)), ((), ())),
                           preferred_element_type=F32)


def _split_bf16(x, terms):
    parts = []
    for _ in range(terms):
        p = x.astype(BF16)
        parts.append(p)
        x = x - p.astype(F32)
    return parts


def _mm_sel(sel, x, terms=3):
    return sum(jnp.dot(sel, p, preferred_element_type=F32) for p in _split_bf16(x, terms))


def _mm_xsel(x, sel, terms=2):
    return sum(jnp.dot(p, sel, preferred_element_type=F32) for p in _split_bf16(x, terms))


def _mm_hi(a, b):
    a_hi, a_lo = _split_bf16(a, 2)
    b_hi, b_lo = _split_bf16(b, 2)
    dot = lambda x, y: jnp.dot(x, y, preferred_element_type=F32)
    return dot(a_hi, b_hi) + (dot(a_hi, b_lo) + dot(a_lo, b_hi))


def _sigmoid(x):
    return 1.0 / (1.0 + jnp.exp(-x))


def _silu(x):
    return x * _sigmoid(x)


def _softplus(x):
    return jnp.maximum(x, 0.0) + jnp.log(1.0 + jnp.exp(-jnp.abs(x)))


def _iota(shape, dim):
    return lax.broadcasted_iota(jnp.int32, shape, dim)


def _layernorm(u, g, b):
    mu = jnp.mean(u, axis=-1, keepdims=True)
    d = u - mu
    var = jnp.mean(d * d, axis=-1, keepdims=True)
    return d * lax.rsqrt(var + LN_EPS) * g + b


def _tri_incl(n):
    return (jnp.arange(n)[None, :] <= jnp.arange(n)[:, None]).astype(BF16)


def _seg_ones(n, seg, scale=1.0):
    idx = jnp.arange(n) // seg
    return jnp.where(idx[:, None] == idx[None, :], scale, 0.0).astype(BF16)


def _interleave(chains, skew=0):
    chains = list(chains)
    done = [False] * len(chains)
    rnd = 0
    while not all(done):
        for i, ch in enumerate(chains):
            if done[i] or rnd < i * skew:
                continue
            try:
                next(ch)
            except StopIteration:
                done[i] = True
        rnd += 1


def _const_spec(arr):
    nd = arr.ndim
    return pl.BlockSpec(arr.shape, lambda *_: (0,) * nd)


def _ada_kernel(c_ref, w_ref, b_ref, o_ref):
    o_ref[0] = _mm(_silu(c_ref[...]), w_ref[0]) + b_ref[0]


def _ada(c_all, w_ada, b_ada):
    rows = c_all.shape[0]
    tn = 1536
    return pl.pallas_call(
        _ada_kernel,
        grid=(DEPTH, 6 * D_MODEL // tn),
        in_specs=[pl.BlockSpec((rows, D_MODEL), lambda l, j: (0, 0)),
                  pl.BlockSpec((1, D_MODEL, tn), lambda l, j: (l, 0, j)),
                  pl.BlockSpec((1, 1, tn), lambda l, j: (l, 0, j))],
        out_specs=pl.BlockSpec((1, rows, tn), lambda l, j: (l, 0, j)),
        out_shape=jax.ShapeDtypeStruct((DEPTH, rows, 6 * D_MODEL), F32),
        compiler_params=_cparams("arbitrary", "arbitrary"),
        name="ada",
    )(c_all, w_ada, b_ada.reshape(DEPTH, 1, 6 * D_MODEL))


def _inproj_kernel(x_ref, sc_ref, sh_ref, w_ref, o_ssd, o_small, o_rwkv, o_gla):
    h = (x_ref[0] * (1.0 + sc_ref[0]) + sh_ref[0]).astype(BF16)
    off = 0
    for o_ref in (o_ssd, o_small, o_rwkv, o_gla):
        width = o_ref.shape[2]
        o_ref[0] = jnp.dot(h, w_ref[:, off:off + width], preferred_element_type=F32).astype(o_ref.dtype)
        off += width


def _mod_spec(mod, tm):
    if mod.shape[1] == 1:
        return pl.BlockSpec((1, 1, D_MODEL), lambda b, i, *_: (b, 0, 0))
    return pl.BlockSpec((1, tm, D_MODEL), lambda b, i, *_: (b, i, 0))


def _inproj(x, sc, sh, w_pad, tm):
    bsz, length, _ = x.shape
    tok = lambda width: pl.BlockSpec((1, tm, width), lambda b, i: (b, i, 0))
    return pl.pallas_call(
        _inproj_kernel,
        grid=(bsz, length // tm),
        in_specs=[tok(D_MODEL), _mod_spec(sc, tm), _mod_spec(sh, tm),
                  pl.BlockSpec((D_MODEL, IN_PAD), lambda b, i: (0, 0))],
        out_specs=[tok(SSD_SLAB), tok(SMALL_W), tok(RWKV_FEAT), tok(GLA_SLAB)],
        out_shape=[jax.ShapeDtypeStruct((bsz, length, SSD_SLAB), BF16),
                   jax.ShapeDtypeStruct((bsz, length, SMALL_W), F32),
                   jax.ShapeDtypeStruct((bsz, length, RWKV_FEAT), BF16),
                   jax.ShapeDtypeStruct((bsz, length, GLA_SLAB), BF16)],
        compiler_params=_cparams("arbitrary", "arbitrary"),
        name="inproj",
    )(x, sc, sh, w_pad)


def _ssd_conv_act(xbc, conv_taps, cw, cb):
    conv = cb + xbc * cw[SSD_CONV_W - 1:SSD_CONV_W, :]
    for i in range(SSD_CONV_W - 1):
        conv = conv + conv_taps[i] * cw[i:i + 1, :]
    return _silu(conv)


def _ssd_dt(small, dtb, alog):
    dt = _softplus(small + dtb)
    return dt, dt * (-jnp.exp(alog))


def _ssd_finish(y, xs, z, dexp, ng):
    gated = (y + xs * dexp) * _silu(z)
    ms = jnp.mean(gated * gated, axis=-1, keepdims=True)
    return gated * lax.rsqrt(ms + RMS_EPS) * ng


def _ssd_kernel(slab_ref, small_ref, cw_ref, cb_ref, dtb_ref, alog_ref, dexp_ref, ng_ref, tri_ref,
                head_x_ref, group_mask_ref, pair_mask_ref, y_ref, conv_out_ref, h_out_ref, ext_ref, h_ref):
    c = pl.program_id(1)
    n_chunks = pl.num_programs(1)
    ch = SSD_CHUNK
    pad = ext_ref.shape[1] - ch
    causal = _iota((ch, ch), 1) <= _iota((ch, ch), 0)

    nb = slab_ref.shape[0]

    @pl.when(c == 0)
    def _():
        ext_ref[:, 0:pad, :] = jnp.zeros((nb, pad, SSD_CONV_DIM), F32)
        h_ref[...] = jnp.zeros_like(h_ref)

    def chain(j):
        z = slab_ref[j, :, 0:SSD_W].astype(F32)
        xbc = slab_ref[j, :, SSD_W:].astype(F32)

        ext_ref[j, pad:pad + ch, :] = xbc
        first = pad - (SSD_CONV_W - 1)
        taps = [ext_ref[j, first + i:first + i + ch, :] for i in range(SSD_CONV_W - 1)]
        act = _ssd_conv_act(xbc, taps, cw_ref[...], cb_ref[...])
        ext_ref[j, 0:pad, :] = ext_ref[j, ch:ch + pad, :]
        dt, dta = _ssd_dt(small_ref[j], dtb_ref[...], alog_ref[...])

        xs = act[:, 0:SSD_W]
        bs = act[:, SSD_W:SSD_W + SSD_G * SSD_N]
        cs = act[:, SSD_W + SSD_G * SSD_N:]
        yield
        cum = _mm_sel(tri_ref[...], dta)
        cum_t = cum.T
        dt_t = dt.T
        cum_x = _mm_xsel(cum, head_x_ref[...], 3)
        dt_x = _mm_xsel(dt, head_x_ref[...], 2)
        yield
        last_x = cum_x[ch - 1:ch, :]
        xs_tail = (xs * (jnp.exp(last_x - cum_x) * dt_x)).astype(BF16)
        h_bd = h_ref[j]
        y_state = jnp.dot(cs.astype(BF16), h_bd.astype(BF16), preferred_element_type=F32) * jnp.exp(cum_x)
        upd = jnp.dot(bs.T.astype(BF16), xs_tail, preferred_element_type=F32)
        h_ref[j] = h_bd * jnp.exp(last_x) + upd * group_mask_ref[...]
        yield

        lane_group = _iota((1, SSD_G * SSD_N), 1) // SSD_N
        xs_b = xs.astype(BF16)
        ys = []
        for g in range(SSD_G):
            cb = _mm_nt(jnp.where(lane_group == g, cs, 0.0), bs)
            yield
            for pair in range(SSD_H // SSD_G // 2):
                scores = []
                for hh in range(2):
                    lane = DT_OFF + g * (SSD_H // SSD_G) + 2 * pair + hh
                    decay = jnp.exp(jnp.where(causal, cum[:, lane:lane + 1] - cum_t[lane:lane + 1, :], -jnp.inf))
                    scores.append((cb * decay * dt_t[lane:lane + 1, :]).astype(BF16))
                blk = g * (SSD_H // SSD_G) // 2 + pair
                x_pair = xs_b[:, blk * LANES:(blk + 1) * LANES]
                x_bd = jnp.concatenate([x_pair, x_pair], axis=0) * pair_mask_ref[...]
                ys.append(jnp.dot(jnp.concatenate(scores, axis=1), x_bd, preferred_element_type=F32))
                yield

        y = jnp.concatenate(ys, axis=-1) + y_state
        y_ref[j] = _ssd_finish(y, xs, z, dexp_ref[...], ng_ref[...]).astype(y_ref.dtype)

    _interleave([chain(j) for j in range(nb)])

    @pl.when(c == n_chunks - 1)
    def _():
        for j in range(nb):
            conv_out_ref[j] = ext_ref[j, pad - (SSD_CONV_W - 1):pad, :]
            h_out_ref[j] = h_ref[j]


def _ssd_prompt(slab, small, cw, cb, dtb, alog, dexp, ng):
    bsz, length, _ = slab.shape
    nb = SSD_NB if bsz % SSD_NB == 0 else 1
    gn, hp = SSD_G * SSD_N, SSD_H * SSD_P
    lane_head = jnp.arange(hp) // SSD_P
    head_x = (jnp.arange(SMALL_W)[:, None] == DT_OFF + lane_head[None, :]).astype(BF16)
    group_mask = (jnp.arange(gn)[:, None] // SSD_N == lane_head[None, :] // (SSD_H // SSD_G)).astype(F32)
    pair_mask = (jnp.arange(2 * SSD_CHUNK)[:, None] // SSD_CHUNK == jnp.arange(2 * SSD_P)[None, :] // SSD_P
                 ).astype(BF16)
    consts = (cw, cb, dtb, alog, dexp, ng, _tri_incl(SSD_CHUNK), head_x, group_mask, pair_mask)
    return pl.pallas_call(
        _ssd_kernel,
        grid=(bsz // nb, length // SSD_CHUNK),
        in_specs=[pl.BlockSpec((nb, SSD_CHUNK, SSD_SLAB), lambda b, c: (b, c, 0)),
                  pl.BlockSpec((nb, SSD_CHUNK, SMALL_W), lambda b, c: (b, c, 0))]
                 + [_const_spec(a) for a in consts],
        out_specs=[pl.BlockSpec((nb, SSD_CHUNK, SSD_W), lambda b, c: (b, c, 0)),
                   pl.BlockSpec((nb, SSD_CONV_W - 1, SSD_CONV_DIM), lambda b, c: (b, 0, 0)),
                   pl.BlockSpec((nb, gn, hp), lambda b, c: (b, 0, 0))],
        out_shape=[jax.ShapeDtypeStruct((bsz, length, SSD_W), BF16),
                   jax.ShapeDtypeStruct((bsz, SSD_CONV_W - 1, SSD_CONV_DIM), F32),
                   jax.ShapeDtypeStruct((bsz, gn, hp), F32)],
        scratch_shapes=[pltpu.VMEM((nb, SSD_CARRY + SSD_CHUNK, SSD_CONV_DIM), F32),
                        pltpu.VMEM((nb, gn, hp), F32)],
        compiler_params=_cparams("arbitrary", "arbitrary"),
        name="ssd_scan",
    )(slab, small, *consts)


def _rwkv_token_math(rw, prev, mu, lora_w, w0, a0, k_k, k_a, head_ones):
    mix = rw + mu * (prev - rw)
    r = mix[:, 0:RWKV_W]
    k = mix[:, RWKV_W:2 * RWKV_W]
    v = mix[:, 2 * RWKV_W:3 * RWKV_W]
    lora = mix[:, 3 * RWKV_W:]
    lane = _iota(lora.shape, 1)
    act = jnp.where(lane < 32, jnp.tanh(lora), jnp.where(lane < 64, lora, _sigmoid(lora)))
    lo = _mm(act, lora_w)
    w_log = -_softplus(-(w0 + lo[:, 0:RWKV_W])) - 0.5
    logw = -jnp.exp(w_log)
    aic = _sigmoid(a0 + lo[:, RWKV_W:2 * RWKV_W])
    gate = lo[:, 2 * RWKV_W:]
    kkf = k * k_k
    kk = kkf * lax.rsqrt(_mm_xsel(kkf * kkf, head_ones, STAT_TERMS) + 1e-12)
    k2 = k * (1.0 + (aic - 1.0) * k_a)
    return r, k2, v, logw, aic, gate, kk


def _rwkv_finish(o, r, k2, v, gate, r_k, ln_g, ln_b, head_ones, head_avg):
    mu = _mm_xsel(o, head_avg, STAT_TERMS)
    d = o - mu
    var = _mm_xsel(d * d, head_avg, STAT_TERMS)
    on = d * lax.rsqrt(var + RWKV_GN_EPS) * ln_g + ln_b
    bonus = _mm_xsel(r * k2 * r_k, head_ones, STAT_TERMS) * v
    return (on + bonus) * gate


def _rwkv_kernel(rw_ref, mu_ref, lw_ref, w0_ref, a0_ref, kk_ref, ka_ref,
                 rk_ref, lng_ref, lnb_ref, ones_ref, avg_ref, tri_ref, strict_ref, incl_ref, eye_ref,
                 y_ref, shift_out_ref, s_out_ref, ext_ref, s_ref):
    c = pl.program_id(1)
    n_chunks = pl.num_programs(1)
    ch = RWKV_CHUNK
    nh = RWKV_H
    w = RWKV_W
    head_bd = ones_ref[...]
    head_bd_f = head_bd.astype(F32)
    tile4 = lambda x: jnp.concatenate([x] * nh, axis=0)

    nb = rw_ref.shape[0]

    @pl.when(c == 0)
    def _():
        ext_ref[:, 0:8, :] = jnp.zeros((nb, 8, RWKV_FEAT), F32)
        s_ref[...] = jnp.zeros_like(s_ref)

    def chain(j):
        rw = rw_ref[j].astype(F32)
        ext_ref[j, 8:8 + ch, :] = rw
        prev = ext_ref[j, 7:7 + ch, :]
        ext_ref[j, 7:8, :] = rw[ch - 1:ch, :]

        r, k2, v, logw, aic, gate, kk = _rwkv_token_math(
            rw, prev, mu_ref[...], lw_ref[...], w0_ref[...], a0_ref[...], kk_ref[...], ka_ref[...], head_bd)
        yield

        cumw = _mm_sel(tri_ref[...], logw)
        yield
        last = cumw[ch - 1:ch, :]
        inv_g = jnp.exp(-cumw)
        to_end = jnp.exp(last - cumw)
        kka = kk * aic
        a_t = (-kk * jnp.exp(cumw - logw)).astype(BF16)
        r_t = (r * jnp.exp(cumw)).astype(BF16)
        b_t = (kka * inv_g).astype(BF16)
        k_t = (k2 * inv_g).astype(BF16)
        bk_end = jnp.concatenate([kka * to_end, k2 * to_end], axis=0).astype(BF16)

        bd = lambda x: tile4(x.astype(BF16)) * head_bd
        nt = lambda x, y: lax.dot_general(x, y, (((1,), (1,)), ((), ())), preferred_element_type=F32)
        vb = v.astype(BF16)
        ar = jnp.concatenate([a_t, r_t], axis=0)
        prod = nt(ar, jnp.concatenate([bd(b_t), bd(k_t)], axis=0))
        a_ab = prod[0:ch, 0:w] * strict_ref[...]
        a_ak = prod[0:ch, w:2 * w] * strict_ref[...]
        a_rbk = (prod[ch:2 * ch, :] * incl_ref[...]).astype(BF16)
        yield
        s_bd = s_ref[j]
        uo = nt(ar, s_bd.astype(BF16))
        rhs = uo[0:ch, :] + jnp.dot(a_ak.astype(BF16), bd(vb), preferred_element_type=F32)
        yield

        t_inv = eye_ref[...] + a_ab
        x = a_ab
        x_bd = bd(x)
        power = 1
        while 2 * power < ch:
            x = jnp.dot(x.astype(BF16), x_bd, preferred_element_type=F32)
            x_bd = bd(x)
            yield
            t_inv = t_inv + jnp.dot(t_inv.astype(BF16), x_bd, preferred_element_type=F32)
            yield
            power *= 2

        p = jnp.dot(t_inv.astype(BF16), bd(rhs), preferred_element_type=F32)
        yield
        pb = p.astype(BF16)
        o = uo[ch:2 * ch, :] + jnp.dot(a_rbk, jnp.concatenate([bd(pb), bd(vb)], axis=0),
                                       preferred_element_type=F32)
        pv = jnp.concatenate([pb, vb], axis=0)
        yield

        upd = jnp.dot(pv.astype(F32).T.astype(BF16), bk_end, preferred_element_type=F32)
        s_ref[j] = s_bd * jnp.exp(last) + upd * head_bd_f
        yield

        y_ref[j] = _rwkv_finish(o, r, k2, v, gate, rk_ref[...], lng_ref[...], lnb_ref[...],
                                head_bd, avg_ref[...]).astype(y_ref.dtype)

    _interleave([chain(j) for j in range(nb)], skew=RWKV_SKEW)

    @pl.when(c == n_chunks - 1)
    def _():
        for j in range(nb):
            shift_out_ref[j] = ext_ref[j, 7:8, :]
            s_out_ref[j] = s_ref[j]


def _rwkv_consts():
    ch = RWKV_CHUNK
    assert ch == RWKV_N, "bd() reuses the head mask, which needs chunk rows == head width"
    t = jnp.arange(ch)[:, None]
    j = jnp.arange(2 * RWKV_W)[None, :] % ch
    strict = (j[:, :RWKV_W] < t).astype(F32)
    incl = (j <= t).astype(F32)
    eye = (j[:, :RWKV_W] == t).astype(F32)
    return (_seg_ones(RWKV_W, RWKV_N), _seg_ones(RWKV_W, RWKV_N, 1.0 / RWKV_N), _tri_incl(ch),
            strict, incl, eye)


def _rwkv_prompt(rw, *params):
    bsz, length, _ = rw.shape
    nb = RWKV_NB if bsz % RWKV_NB == 0 else 1
    consts = tuple(params) + _rwkv_consts()
    return pl.pallas_call(
        _rwkv_kernel,
        grid=(bsz // nb, length // RWKV_CHUNK),
        in_specs=[pl.BlockSpec((nb, RWKV_CHUNK, RWKV_FEAT), lambda b, c: (b, c, 0))]
                 + [_const_spec(a) for a in consts],
        out_specs=[pl.BlockSpec((nb, RWKV_CHUNK, RWKV_W), lambda b, c: (b, c, 0)),
                   pl.BlockSpec((nb, 1, RWKV_FEAT), lambda b, c: (b, 0, 0)),
                   pl.BlockSpec((nb, RWKV_W, RWKV_W), lambda b, c: (b, 0, 0))],
        out_shape=[jax.ShapeDtypeStruct((bsz, length, RWKV_W), BF16),
                   jax.ShapeDtypeStruct((bsz, 1, RWKV_FEAT), F32),
                   jax.ShapeDtypeStruct((bsz, RWKV_W, RWKV_W), F32)],
        scratch_shapes=[pltpu.VMEM((nb, RWKV_CHUNK + 8, RWKV_FEAT), F32),
                        pltpu.VMEM((nb, RWKV_W, RWKV_W), F32)],
        compiler_params=_cparams("arbitrary", "arbitrary"),
        name="rwkv_scan",
    )(rw, *consts)


def _gla_gate_log(small, gk2_pad, b_gk):
    x = _mm_hi(small, gk2_pad) + b_gk
    return -_softplus(-x) / GATE_NORMALIZER


def _gla_finish(o, gg, norm_g, head_avg):
    ms = _mm_xsel(o * o, head_avg, STAT_TERMS)
    return o * lax.rsqrt(ms + RMS_EPS) * norm_g * _silu(gg)


def _gla_kernel(slab_ref, small_ref, gk2_ref, bgk_ref, ng_ref, avg_ref, tri_ref, expand_ref,
                y_ref, s_out_ref, s_ref):
    c = pl.program_id(1)
    n_chunks = pl.num_programs(1)
    ch = GLA_CHUNK
    sub = GLA_SUB
    hk = GLA_HK
    expand = expand_ref[...]
    expand_f = expand.astype(F32)
    pos = _iota((ch, 1), 0) % sub
    t_idx = _iota((ch, 1), 0)
    hmask_k = (_iota((GLA_H * sub, hk), 0) // sub == _iota((GLA_H * sub, hk), 1) // GLA_DK).astype(F32)
    hmask_v = (_iota((GLA_H * sub, GLA_W), 0) // sub == _iota((GLA_H * sub, GLA_W), 1) // GLA_DV).astype(F32)

    nb = slab_ref.shape[0]

    @pl.when(c == 0)
    def _():
        s_ref[...] = jnp.zeros_like(s_ref)

    def chain(j):
        q = slab_ref[j, :, 0:hk].astype(F32) * (GLA_DK ** -0.5)
        k = slab_ref[j, :, hk:2 * hk].astype(F32)
        v = slab_ref[j, :, 2 * hk:2 * hk + GLA_W].astype(F32)
        gg = slab_ref[j, :, 2 * hk + GLA_W:].astype(F32)
        small = small_ref[j]

        lg = _gla_gate_log(small, gk2_ref[...], bgk_ref[...])
        yield
        cum = _mm_sel(tri_ref[...], lg)
        yield
        last = cum[ch - 1:ch, :]
        s_bd = s_ref[j]

        o = _mm(q * jnp.exp(cum), s_bd)

        group = 8
        for d0 in range(0, sub, group):
            prods, v_shift = [], []
            for delta in range(d0, d0 + group):
                if delta == 0:
                    k_s, c_s, v_s = k, cum, v
                else:
                    k_s = pltpu.roll(k, delta, 0)
                    c_s = pltpu.roll(cum, delta, 0)
                    v_s = pltpu.roll(v, delta, 0)
                valid = pos >= delta
                w_pair = jnp.exp(jnp.where(valid, cum - c_s, 0.0))
                prods.append(jnp.where(valid, q * k_s * w_pair, 0.0).astype(BF16))
                v_shift.append(v_s)
            yield
            att = jnp.dot(jnp.concatenate(prods, axis=0), expand, preferred_element_type=F32)
            o = o + sum(att[g * ch:(g + 1) * ch, :] * v_shift[g] for g in range(group))
            yield

        vb = v.astype(BF16)
        atts = []
        for blk in range(1, ch // sub):
            lo = blk * sub
            ref_pt = cum[lo - 1:lo, :]
            qd = q[lo:lo + sub, :] * jnp.exp(cum[lo:lo + sub, :] - ref_pt)
            kp = k * jnp.exp(jnp.where(t_idx < lo, ref_pt - cum, -jnp.inf))
            q4 = jnp.concatenate([qd] * GLA_H, axis=0) * hmask_k
            atts.append(_mm_nt(q4, kp).astype(BF16))
            yield
        o4 = jnp.dot(jnp.concatenate(atts, axis=0), vb, preferred_element_type=F32)
        yield
        cross = [jnp.zeros((sub, GLA_W), F32)]
        for blk in range(ch // sub - 1):
            o4_b = o4[blk * GLA_H * sub:(blk + 1) * GLA_H * sub, :] * hmask_v
            cross.append(sum(o4_b[h * sub:(h + 1) * sub, :] for h in range(GLA_H)))
        o = o + jnp.concatenate(cross, axis=0)

        kt_t = (k * jnp.exp(last - cum)).T
        s_ref[j] = s_bd * _col(jnp.exp(last)) + _mm(kt_t, vb) * expand_f
        yield

        y_ref[j] = _gla_finish(o, gg, ng_ref[...], avg_ref[...]).astype(y_ref.dtype)

    _interleave([chain(j) for j in range(nb)], skew=2)

    @pl.when(c == n_chunks - 1)
    def _():
        s_out_ref[...] = s_ref[...]


def _gla_prompt(slab, small, gk2_pad, b_gk, ng):
    bsz, length, _ = slab.shape
    nb = GLA_NB if bsz % GLA_NB == 0 else 1
    expand = (jnp.arange(GLA_HK)[:, None] // GLA_DK == jnp.arange(GLA_W)[None, :] // GLA_DV).astype(BF16)
    consts = (gk2_pad, b_gk, ng, _seg_ones(GLA_W, GLA_DV, 1.0 / GLA_DV), _tri_incl(GLA_CHUNK), expand)
    return pl.pallas_call(
        _gla_kernel,
        grid=(bsz // nb, length // GLA_CHUNK),
        in_specs=[pl.BlockSpec((nb, GLA_CHUNK, GLA_SLAB), lambda b, c: (b, c, 0)),
                  pl.BlockSpec((nb, GLA_CHUNK, SMALL_W), lambda b, c: (b, c, 0))]
                 + [_const_spec(a) for a in consts],
        out_specs=[pl.BlockSpec((nb, GLA_CHUNK, GLA_W), lambda b, c: (b, c, 0)),
                   pl.BlockSpec((nb, GLA_HK, GLA_W), lambda b, c: (b, 0, 0))],
        out_shape=[jax.ShapeDtypeStruct((bsz, length, GLA_W), BF16),
                   jax.ShapeDtypeStruct((bsz, GLA_HK, GLA_W), F32)],
        scratch_shapes=[pltpu.VMEM((nb, GLA_HK, GLA_W), F32)],
        compiler_params=_cparams("arbitrary", "arbitrary"),
        name="gla_scan",
    )(slab, small, *consts)


def _col(row):
    n = row.shape[1]
    eye = _iota((n, n), 0) == _iota((n, n), 1)
    return jnp.sum(jnp.where(eye, row, 0.0), axis=1, keepdims=True)


def _store_t(dst_ref, row0, x):
    for cblk in range(x.shape[1] // LANES):
        dst_ref[row0 + cblk * LANES:row0 + (cblk + 1) * LANES, :] = x[:, cblk * LANES:(cblk + 1) * LANES].T


def _load_t(src_ref, width):
    return jnp.concatenate([src_ref[cblk * LANES:(cblk + 1) * LANES, :].T for cblk in range(width // LANES)],
                           axis=1)


_TS_X, _TS_B, _TS_C, _TS_DT, _TS_DA = 0, SSD_W, SSD_W + SSD_G * SSD_N, SSD_CONV_DIM, SSD_CONV_DIM + SMALL_W
_TR_R, _TR_W, _TR_K, _TR_V, _TR_KK, _TR_KA = (i * RWKV_W for i in range(6))
_TG_Q, _TG_K, _TG_G, _TG_V = 0, GLA_HK, 2 * GLA_HK, 3 * GLA_HK


def _step_kernel(ssd_ref, small_ref, rw_ref, gla_ref, conv0_ref, shift0_ref, h0_ref, rs0_ref, gs0_ref,
                 cw_ref, cb_ref, dtb_ref, alog_ref, dexp_ref, sng_ref,
                 mu_ref, lw_ref, w0_ref, a0_ref, kk_ref, ka_ref, rk_ref, lng_ref, lnb_ref,
                 gk2_ref, bgk_ref, gng_ref, ones_ref, ravg_ref, gavg_ref,
                 y_ssd_ref, y_rwkv_ref, y_gla_ref, conv_out_ref, h_out_ref, rs_out_ref, gs_out_ref,
                 t_ssd, t_rwkv, t_gla, o_ssd_t, o_rwkv_t, o_gla_t):
    i = pl.program_id(0)
    hk = GLA_HK

    def ssd_tokens():
        xbc = ssd_ref[:, SSD_W:].astype(F32)
        taps = [conv0_ref[:, t * SSD_CONV_DIM:(t + 1) * SSD_CONV_DIM] for t in range(SSD_CONV_W - 1)]
        dt, dta = _ssd_dt(small_ref[...], dtb_ref[...], alog_ref[...])
        return xbc, (_ssd_conv_act(xbc, taps, cw_ref[...], cb_ref[...]), dt, dta)

    def rwkv_tokens():
        return _rwkv_token_math(rw_ref[...].astype(F32), shift0_ref[...], mu_ref[...], lw_ref[...],
                                w0_ref[...], a0_ref[...], kk_ref[...], ka_ref[...], ones_ref[...])

    @pl.when(i == 0)
    def _():
        xbc, (act, dt, dta) = ssd_tokens()
        conv_out_ref[:, 0:2 * SSD_CONV_DIM] = conv0_ref[:, SSD_CONV_DIM:]
        conv_out_ref[:, 2 * SSD_CONV_DIM:] = xbc
        _store_t(t_ssd, _TS_X, act)
        _store_t(t_ssd, _TS_DT, dt)
        _store_t(t_ssd, _TS_DA, jnp.exp(dta))
        r, k2, v, logw, aic, gate, kk = rwkv_tokens()
        for off, x in ((_TR_R, r), (_TR_W, jnp.exp(logw)), (_TR_K, k2), (_TR_V, v), (_TR_KK, kk),
                       (_TR_KA, kk * aic)):
            _store_t(t_rwkv, off, x)
        lg = _gla_gate_log(small_ref[...], gk2_ref[...], bgk_ref[...])
        _store_t(t_gla, _TG_Q, gla_ref[:, 0:hk].astype(F32) * (GLA_DK ** -0.5))
        _store_t(t_gla, _TG_K, gla_ref[:, hk:2 * hk].astype(F32))
        _store_t(t_gla, _TG_G, jnp.exp(lg))
        _store_t(t_gla, _TG_V, gla_ref[:, 2 * hk:2 * hk + GLA_W].astype(F32))
        o_gla_t[...] = jnp.zeros_like(o_gla_t)

    def row(ref, idx):
        return ref[pl.ds(idx, 1), :]

    def rows(ref, idx, count):
        return ref[pl.ds(pl.multiple_of(idx, count), count), :]

    units = h0_ref.shape[0] // SSD_N
    per_head = SSD_P // units
    h = i // per_head
    p0 = (i % per_head) * units
    b_t = rows(t_ssd, _TS_B + (h // (SSD_H // SSD_G)) * SSD_N, SSD_N)
    c_t = rows(t_ssd, _TS_C + (h // (SSD_H // SSD_G)) * SSD_N, SSD_N)
    da_row = row(t_ssd, _TS_DA + DT_OFF + h)
    dt_row = row(t_ssd, _TS_DT + DT_OFF + h)
    for u in range(units):
        p_idx = h * SSD_P + p0 + u
        blk = slice(u * SSD_N, (u + 1) * SSD_N)
        h_new = h0_ref[blk, :] * da_row + (row(t_ssd, _TS_X + p_idx) * dt_row) * b_t
        o_ssd_t[pl.ds(p_idx, 1), :] = jnp.sum(h_new * c_t, axis=0, keepdims=True)
        h_out_ref[blk, :] = h_new

    units = rs0_ref.shape[0] // RWKV_N
    per_head = RWKV_N // units
    h = i // per_head
    v0 = (i % per_head) * units
    seg = lambda off: rows(t_rwkv, off + h * RWKV_N, RWKV_N)
    r_t, w_t, k_t, kk_t, ka_t = seg(_TR_R), seg(_TR_W), seg(_TR_K), seg(_TR_KK), seg(_TR_KA)
    for u in range(units):
        v_idx = h * RWKV_N + v0 + u
        blk = slice(u * RWKV_N, (u + 1) * RWKV_N)
        s = rs0_ref[blk, :]
        sa = jnp.sum(s * (-kk_t), axis=0, keepdims=True)
        s_new = s * w_t + sa * ka_t + row(t_rwkv, _TR_V + v_idx) * k_t
        o_rwkv_t[pl.ds(v_idx, 1), :] = jnp.sum(s_new * r_t, axis=0, keepdims=True)
        rs_out_ref[blk, :] = s_new

    units = gs0_ref.shape[0] // GLA_DV
    per_head = GLA_DK // units
    h = i // per_head
    k0 = (i % per_head) * units
    v_t = rows(t_gla, _TG_V + h * GLA_DV, GLA_DV)
    acc = jnp.zeros((GLA_DV, LANES), F32)
    for u in range(units):
        k_idx = h * GLA_DK + k0 + u
        blk = slice(u * GLA_DV, (u + 1) * GLA_DV)
        s_new = gs0_ref[blk, :] * row(t_gla, _TG_G + k_idx) + row(t_gla, _TG_K + k_idx) * v_t
        acc = acc + row(t_gla, _TG_Q + k_idx) * s_new
        gs_out_ref[blk, :] = s_new
    o_rows = pl.ds(pl.multiple_of(h * GLA_DV, GLA_DV), GLA_DV)
    o_gla_t[o_rows, :] = o_gla_t[o_rows, :] + acc

    @pl.when(i == pl.num_programs(0) - 1)
    def _():
        _, (act, _, _) = ssd_tokens()
        y_ssd_ref[...] = _ssd_finish(_load_t(o_ssd_t, SSD_W), act[:, 0:SSD_W], ssd_ref[:, 0:SSD_W].astype(F32),
                                     dexp_ref[...], sng_ref[...]).astype(y_ssd_ref.dtype)
        r, k2, v, _, _, gate, _ = rwkv_tokens()
        y_rwkv_ref[...] = _rwkv_finish(_load_t(o_rwkv_t, RWKV_W), r, k2, v, gate, rk_ref[...], lng_ref[...],
                                       lnb_ref[...], ones_ref[...], ravg_ref[...]).astype(y_rwkv_ref.dtype)
        y_gla_ref[...] = _gla_finish(_load_t(o_gla_t, GLA_W), gla_ref[:, 2 * hk + GLA_W:].astype(F32),
                                     gng_ref[...], gavg_ref[...]).astype(y_gla_ref.dtype)


def _mixer_step(ssd_slab, small, rw, gla_slab, conv0, shift0, h0, rs0, gs0, ssd_p, rwkv_p, gla_p):
    bsz = ssd_slab.shape[0]
    assert bsz == LANES, "the single-token mixer keeps exactly one vreg row of batch entries on the lanes"
    flat_t = lambda s: s.reshape(bsz, -1).T
    h0, rs0, gs0 = flat_t(h0), flat_t(rs0), flat_t(gs0)
    consts = (tuple(ssd_p) + tuple(rwkv_p) + tuple(gla_p)
              + (_seg_ones(RWKV_W, RWKV_N), _seg_ones(RWKV_W, RWKV_N, 1.0 / RWKV_N),
                 _seg_ones(GLA_W, GLA_DV, 1.0 / GLA_DV)))
    full = lambda a: pl.BlockSpec(a.shape, lambda i: (0,) * a.ndim)
    cols = lambda a: pl.BlockSpec((a.shape[0] // STEP_GRID, bsz), lambda i: (i, 0))
    tokens = (ssd_slab, small, rw, gla_slab, conv0, shift0)
    outs = pl.pallas_call(
        _step_kernel,
        grid=(STEP_GRID,),
        in_specs=[full(a) for a in tokens] + [cols(h0), cols(rs0), cols(gs0)] + [full(a) for a in consts],
        out_specs=[full(jax.ShapeDtypeStruct((bsz, w), BF16)) for w in (SSD_W, RWKV_W, GLA_W)]
                  + [full(conv0), cols(h0), cols(rs0), cols(gs0)],
        out_shape=[jax.ShapeDtypeStruct((bsz, SSD_W), BF16),
                   jax.ShapeDtypeStruct((bsz, RWKV_W), BF16),
                   jax.ShapeDtypeStruct((bsz, GLA_W), BF16),
                   jax.ShapeDtypeStruct(conv0.shape, F32),
                   jax.ShapeDtypeStruct(h0.shape, F32),
                   jax.ShapeDtypeStruct(rs0.shape, F32),
                   jax.ShapeDtypeStruct(gs0.shape, F32)],
        scratch_shapes=[pltpu.VMEM((SSD_CONV_DIM + 2 * SMALL_W, LANES), F32),
                        pltpu.VMEM((6 * RWKV_W, LANES), F32),
                        pltpu.VMEM((3 * GLA_HK + GLA_W, LANES), F32),
                        pltpu.VMEM((SSD_W, LANES), F32),
                        pltpu.VMEM((RWKV_W, LANES), F32),
                        pltpu.VMEM((GLA_W, LANES), F32)],
        compiler_params=_cparams("arbitrary"),
        name="mixer_step",
    )(*tokens, h0, rs0, gs0, *consts)
    y_ssd, y_rwkv, y_gla, conv_new, h_new, rs_new, gs_new = outs
    return (y_ssd, y_rwkv, y_gla, conv_new.reshape(bsz, SSD_CONV_W - 1, SSD_CONV_DIM),
            h_new.T.reshape(bsz, SSD_H, SSD_P, SSD_N), rs_new.T.reshape(bsz, RWKV_H, RWKV_N, RWKV_N),
            gs_new.T.reshape(bsz, GLA_H, GLA_DK, GLA_DV))


def _mod_rows(mod_ref, rows):
    return mod_ref[0] if mod_ref.shape[1] == 1 else mod_ref[0, rows, :]


def _row_parts(tm):
    return 2 if tm % 256 == 0 else 1


def _outproj_kernel(ys_ref, yr_ref, yg_ref, x_ref, gt_ref, sc_ref, sh_ref, w_ref, g_ref, b_ref, rt_ref,
                    o_ref, logit_ref):
    tm = x_ref.shape[1]
    parts = _row_parts(tm)

    def chain(r):
        rows = slice(r * tm // parts, (r + 1) * tm // parts)
        m = (jnp.dot(ys_ref[0, rows, :], w_ref[0:SSD_W, :], preferred_element_type=F32)
             + jnp.dot(yr_ref[0, rows, :], w_ref[SSD_W:SSD_W + RWKV_W, :], preferred_element_type=F32)
             + jnp.dot(yg_ref[0, rows, :], w_ref[SSD_W + RWKV_W:, :], preferred_element_type=F32))
        yield
        u = ALPHA * x_ref[0, rows, :] + (1.0 + _mod_rows(gt_ref, rows)) * m
        x1 = _layernorm(u, g_ref[...], b_ref[...])
        o_ref[0, rows, :] = x1
        h = x1 * (1.0 + _mod_rows(sc_ref, rows)) + _mod_rows(sh_ref, rows)
        logit_ref[0, rows, :] = _mm_hi(h, rt_ref[...])

    _interleave([chain(r) for r in range(parts)], skew=1)


def _outproj(y_ssd, y_rwkv, y_gla, x, gt, sc2, sh2, w_out, ln_g, ln_b, router_pad, tm):
    bsz, length, _ = x.shape
    tok = lambda width: pl.BlockSpec((1, tm, width), lambda b, i: (b, i, 0))
    vec = pl.BlockSpec((1, D_MODEL), lambda b, i: (0, 0))
    return pl.pallas_call(
        _outproj_kernel,
        grid=(bsz, length // tm),
        in_specs=[tok(SSD_W), tok(RWKV_W), tok(GLA_W), tok(D_MODEL),
                  _mod_spec(gt, tm), _mod_spec(sc2, tm), _mod_spec(sh2, tm),
                  pl.BlockSpec((D_MODEL, D_MODEL), lambda b, i: (0, 0)), vec, vec,
                  pl.BlockSpec((D_MODEL, LANES), lambda b, i: (0, 0))],
        out_specs=[tok(D_MODEL), tok(LANES)],
        out_shape=[jax.ShapeDtypeStruct((bsz, length, D_MODEL), F32),
                   jax.ShapeDtypeStruct((bsz, length, LANES), F32)],
        compiler_params=_cparams("arbitrary", "arbitrary"),
        name="outproj_ln",
    )(y_ssd, y_rwkv, y_gla, x, gt, sc2, sh2, w_out, ln_g, ln_b, router_pad)


def _ffn_kernel(ys_ref, yr_ref, yg_ref, x_ref, gt1_ref, sc_ref, sh_ref, gt2_ref, wo_ref, g1_ref, b1_ref,
                wg_ref, wu_ref, wd_ref, g2_ref, b2_ref, o_ref):
    tm = x_ref.shape[1]
    parts = _row_parts(tm)

    def chain(r):
        rows = slice(r * tm // parts, (r + 1) * tm // parts)
        m = (jnp.dot(ys_ref[0, rows, :], wo_ref[0:SSD_W, :], preferred_element_type=F32)
             + jnp.dot(yr_ref[0, rows, :], wo_ref[SSD_W:SSD_W + RWKV_W, :], preferred_element_type=F32)
             + jnp.dot(yg_ref[0, rows, :], wo_ref[SSD_W + RWKV_W:, :], preferred_element_type=F32))
        yield
        x1 = _layernorm(ALPHA * x_ref[0, rows, :] + (1.0 + _mod_rows(gt1_ref, rows)) * m,
                        g1_ref[...], b1_ref[...])
        h = (x1 * (1.0 + _mod_rows(sc_ref, rows)) + _mod_rows(sh_ref, rows)).astype(BF16)
        yield
        gate = jnp.dot(h, wg_ref[...], preferred_element_type=F32)
        up = jnp.dot(h, wu_ref[...], preferred_element_type=F32)
        yield
        f = jnp.dot((_silu(gate) * up).astype(BF16), wd_ref[...], preferred_element_type=F32)
        yield
        u = ALPHA * x1 + (1.0 + _mod_rows(gt2_ref, rows)) * f
        o_ref[0, rows, :] = _layernorm(u, g2_ref[...], b2_ref[...])

    _interleave([chain(r) for r in range(parts)], skew=1)


def _outproj_ffn(y_ssd, y_rwkv, y_gla, x, gt1, sc2, sh2, gt2, w_out, ln1_g, ln1_b, wg, wu, wd, ln2_g, ln2_b, tm):
    bsz, length, _ = x.shape
    tok = lambda width: pl.BlockSpec((1, tm, width), lambda b, i: (b, i, 0))
    vec = pl.BlockSpec((1, D_MODEL), lambda b, i: (0, 0))
    resident = lambda a: pl.BlockSpec(a.shape, lambda b, i: (0, 0), pipeline_mode=pl.Buffered(1))
    return pl.pallas_call(
        _ffn_kernel,
        grid=(bsz, length // tm),
        in_specs=[tok(SSD_W), tok(RWKV_W), tok(GLA_W), tok(D_MODEL),
                  _mod_spec(gt1, tm), _mod_spec(sc2, tm), _mod_spec(sh2, tm), _mod_spec(gt2, tm),
                  resident(w_out), vec, vec, resident(wg), resident(wu), resident(wd), vec, vec],
        out_specs=tok(D_MODEL),
        out_shape=jax.ShapeDtypeStruct((bsz, length, D_MODEL), F32),
        compiler_params=_cparams("arbitrary", "arbitrary"),
        name="outproj_ffn_ln",
    )(y_ssd, y_rwkv, y_gla, x, gt1, sc2, sh2, gt2, w_out, ln1_g, ln1_b, wg, wu, wd, ln2_g, ln2_b)


def _moe_kernel(x_ref, sc_ref, sh_ref, gt_ref, logit_ref, wg_ref, wu_ref, wd_ref, g_ref, b_ref, tri_ref, o_ref,
                h_ref, rank_t_ref, comb_t_ref, cnt_ref, acc_ref, *, half, cap):
    e = pl.program_id(2)
    tm = 2 * half

    @pl.when(e == 0)
    def _():
        h_ref[...] = (x_ref[0] * (1.0 + sc_ref[0]) + sh_ref[0]).astype(BF16)
        acc_ref[...] = jnp.zeros_like(acc_ref)
        lane = _iota((tm, LANES), 1).astype(F32)
        logits = jnp.where(lane < N_EXPERTS, logit_ref[0], -jnp.inf)
        m1 = jnp.max(logits, axis=-1, keepdims=True)
        i1 = jnp.min(jnp.where(logits == m1, lane, float(LANES)), axis=-1, keepdims=True)
        rest = jnp.where(lane == i1, -jnp.inf, logits)
        m2 = jnp.max(rest, axis=-1, keepdims=True)
        i2 = jnp.min(jnp.where(rest == m2, lane, float(LANES)), axis=-1, keepdims=True)
        e2 = jnp.exp(m2 - m1)
        den = 1.0 + e2
        comb = jnp.where(lane == i1, 1.0 / den, 0.0) + jnp.where(lane == i2, e2 / den, 0.0)
        sel = (lane == i1) | (lane == i2)
        sel_f = sel.astype(F32)
        ranks = []
        for hf in range(2):
            s = sel_f[hf * half:(hf + 1) * half, :]
            before = jnp.dot(tri_ref[...], s.astype(BF16), preferred_element_type=F32)
            ranks.append(jnp.where(sel[hf * half:(hf + 1) * half, :], before, -1.0))
            cnt_ref[hf:hf + 1, :] = jnp.sum(s, axis=0, keepdims=True)
        rank = jnp.concatenate(ranks, axis=0)
        for blk in range(tm // LANES):
            rows = slice(blk * LANES, (blk + 1) * LANES)
            rank_t_ref[:, rows] = rank[rows, :].T
            comb_t_ref[:, rows] = comb[rows, :].T

    rank_t_e = rank_t_ref[pl.ds(e, 1), :]
    comb_t_e = comb_t_ref[pl.ds(e, 1), :]
    n_max = jnp.max(jnp.where(_iota((2, LANES), 1) == e, cnt_ref[...], 0.0))
    n_pass = (n_max.astype(jnp.int32) + (cap - 1)) // cap

    def one_pass(p, carry):
        slot = _iota((cap, 1), 0).astype(F32) + (p * cap).astype(F32)
        picks, xs, gates = [], [], []
        for hf in range(2):
            rows = slice(hf * half, (hf + 1) * half)
            hit = rank_t_e[:, rows] == slot
            pick = hit.astype(BF16)
            picks.append(pick)
            xs.append(jnp.dot(pick, h_ref[rows, :], preferred_element_type=F32).astype(BF16))
            gates.append(jnp.sum(jnp.where(hit, comb_t_e[:, rows], 0.0), axis=1, keepdims=True))
        xc = jnp.concatenate(xs, axis=0)
        a = _silu(jnp.dot(xc, wg_ref[0], preferred_element_type=F32)) * jnp.dot(
            xc, wu_ref[0], preferred_element_type=F32)
        out_e = jnp.dot(a.astype(BF16), wd_ref[0], preferred_element_type=F32)
        for hf in range(2):
            rows = slice(hf * half, (hf + 1) * half)
            weighted = (out_e[hf * cap:(hf + 1) * cap, :] * gates[hf]).astype(BF16)
            acc_ref[rows, :] += lax.dot_general(picks[hf], weighted, (((0,), (0,)), ((), ())),
                                                preferred_element_type=F32)
        return carry

    one_pass(jnp.int32(0), 0)
    lax.fori_loop(1, n_pass, one_pass, 0)

    @pl.when(e == pl.num_programs(2) - 1)
    def _():
        u = ALPHA * x_ref[0] + (1.0 + gt_ref[0]) * acc_ref[...]
        o_ref[0] = _layernorm(u, g_ref[...], b_ref[...])


def _moe(x, sc, sh, gt, logits, wg, wu, wd, ln_g, ln_b, tm):
    bsz, length, _ = x.shape
    assert tm % LANES == 0
    half = tm // 2
    cap = MOE_CAP_ROWS if half >= 2 * MOE_CAP_ROWS else half
    tri = (jnp.arange(half)[None, :] < jnp.arange(half)[:, None]).astype(BF16)
    tok = pl.BlockSpec((1, tm, D_MODEL), lambda b, i, e: (b, i, 0))
    vec = pl.BlockSpec((1, D_MODEL), lambda b, i, e: (0, 0))
    wspec = pl.BlockSpec((1, D_MODEL, D_MODEL), lambda b, i, e: (e, 0, 0))
    return pl.pallas_call(
        functools.partial(_moe_kernel, half=half, cap=cap),
        grid=(bsz, length // tm, N_EXPERTS),
        in_specs=[tok, _mod_spec(sc, tm), _mod_spec(sh, tm), _mod_spec(gt, tm),
                  pl.BlockSpec((1, tm, LANES), lambda b, i, e: (b, i, 0)),
                  wspec, wspec, wspec, vec, vec, _const_spec(tri)],
        out_specs=tok,
        out_shape=jax.ShapeDtypeStruct((bsz, length, D_MODEL), F32),
        scratch_shapes=[pltpu.VMEM((tm, D_MODEL), BF16), pltpu.VMEM((LANES, tm), F32),
                        pltpu.VMEM((LANES, tm), F32), pltpu.VMEM((2, LANES), F32),
                        pltpu.VMEM((tm, D_MODEL), F32)],
        compiler_params=_cparams("arbitrary", "arbitrary", "arbitrary"),
        name="moe_ln",
    )(x, sc, sh, gt, logits, wg, wu, wd, ln_g, ln_b, tri)


def _pad_lanes(vec, offset, width=SMALL_W):
    out = jnp.zeros((1, width), F32)
    return out.at[0, offset:offset + vec.shape[0]].set(vec)


def _layer_params(p, l):
    w_in = p["w_in"][l]
    off = [0]
    for s in (SSD_W, SSD_CONV_DIM, SSD_H, RWKV_FEAT, GLA_HK, GLA_HK, GLA_W, GK_LORA, GLA_W):
        off.append(off[-1] + s)
    piece = lambda i: w_in[:, off[i]:off[i + 1]]
    small = jnp.zeros((D_MODEL, SMALL_W), F32)
    small = small.at[:, DT_OFF:DT_OFF + SSD_H].set(piece(2)).at[:, GLO_OFF:GLO_OFF + GK_LORA].set(piece(7))
    w_pad = jnp.concatenate([piece(0), piece(1), small, piece(3), piece(4), piece(5), piece(6), piece(8)],
                            axis=1).astype(BF16)
    ssd_p = (p["ssd_conv_w"][l], p["ssd_conv_b"][l][None, :],
             _pad_lanes(p["ssd_dt_bias"][l], DT_OFF), _pad_lanes(p["ssd_a_log"][l], DT_OFF),
             jnp.repeat(p["ssd_d"][l], SSD_P)[None, :], p["ssd_norm_g"][l][None, :])
    lora_w = jnp.zeros((LANES, 3 * RWKV_W), F32)
    lora_w = (lora_w.at[0:32, 0:RWKV_W].set(p["rwkv_w2"][l])
              .at[32:64, RWKV_W:2 * RWKV_W].set(p["rwkv_a2"][l])
              .at[64:128, 2 * RWKV_W:].set(p["rwkv_g2"][l])).astype(BF16)
    row = lambda name: p[name][l].reshape(1, -1)
    rwkv_p = (row("rwkv_mu"), lora_w, row("rwkv_w0"), row("rwkv_a0"), row("rwkv_k_k"), row("rwkv_k_a"),
              row("rwkv_r_k"), row("rwkv_ln_g"), row("rwkv_ln_b"))
    gk2_pad = jnp.zeros((SMALL_W, GLA_HK), F32).at[GLO_OFF:GLO_OFF + GK_LORA].set(p["gla_w_gk2"][l])
    gla_p = (gk2_pad, row("gla_b_gk"), jnp.tile(p["gla_norm_g"][l], GLA_H)[None, :])
    return w_pad, ssd_p, rwkv_p, gla_p


def _block_diag_inv(s_bd, h):
    b, hr, hc = s_bd.shape
    r, c = hr // h, hc // h
    s = s_bd.reshape(b, h, r, h, c)
    return jnp.stack([s[:, i, :, i, :] for i in range(h)], axis=1)


def _ssd_state_unpack(h_bd):
    b = h_bd.shape[0]
    s = h_bd.reshape(b, SSD_G, SSD_N, SSD_H, SSD_P)
    per_group = SSD_H // SSD_G
    heads = [s[:, h // per_group, :, h, :] for h in range(SSD_H)]
    return jnp.swapaxes(jnp.stack(heads, axis=1), 2, 3)


def _tail(x, mod_l, l, p, y_ssd, y_rwkv, y_gla, tm):
    sh1, sc1, gt1, sh2, sc2, gt2 = mod_l
    row = lambda name: p[name][l].reshape(1, -1)
    w_out = p["w_out"][l].astype(BF16)
    i = l // 2
    if l % 2 == 0:
        x = _outproj_ffn(y_ssd, y_rwkv, y_gla, x, gt1, sc2, sh2, gt2, w_out, row("ln_mix_g"), row("ln_mix_b"),
                         p["ffn_w_gate"][i].astype(BF16), p["ffn_w_up"][i].astype(BF16),
                         p["ffn_w_down"][i].astype(BF16), row("ln_ffn_g"), row("ln_ffn_b"), tm)
    else:
        router_pad = jnp.zeros((D_MODEL, LANES), F32).at[:, :N_EXPERTS].set(p["moe_router"][i])
        x, logits = _outproj(y_ssd, y_rwkv, y_gla, x, gt1, sc2, sh2, w_out, row("ln_mix_g"), row("ln_mix_b"),
                             router_pad, tm)
        x = _moe(x, sc2, sh2, gt2, logits, p["moe_w_gate"][i].astype(BF16), p["moe_w_up"][i].astype(BF16),
                 p["moe_w_down"][i].astype(BF16), row("ln_ffn_g"), row("ln_ffn_b"), min(2 * tm, x.shape[1]))
    return x


def _forward(x_prompt, x_sample, c_prompt, c_sample, states, p):
    bp, seq, _ = x_prompt.shape
    bs = x_sample.shape[0]
    state_ssd, state_conv, state_rwkv, state_shift, state_gla = states
    mod = _ada(jnp.concatenate([c_prompt, c_sample], axis=0), p["w_ada"], p["b_ada"])

    xp = x_prompt
    xs = x_sample.reshape(1, bs, D_MODEL)
    tm_p = min(512, seq)
    outs_p = [[] for _ in range(5)]
    outs_s = [[] for _ in range(5)]
    for l in range(DEPTH):
        w_pad, ssd_p, rwkv_p, gla_p = _layer_params(p, l)
        mods = jnp.split(mod[l], 6, axis=-1)
        mod_p = [m[:bp, None, :] for m in mods]
        mod_s = [m[None, bp:, :] for m in mods]

        ssd_slab, small, rw, gla_slab = _inproj(xp, mod_p[1], mod_p[0], w_pad, min(2 * tm_p, seq))
        y_ssd, conv_new, h_new = _ssd_prompt(ssd_slab, small, *ssd_p)
        h_new = _ssd_state_unpack(h_new)
        y_rwkv, shift_new, rs_bd = _rwkv_prompt(rw, *rwkv_p)
        y_gla, gs_bd = _gla_prompt(gla_slab, small, *gla_p)
        xp = _tail(xp, mod_p, l, p, y_ssd, y_rwkv, y_gla, tm_p)
        for acc, s in zip(outs_p, (h_new, conv_new, _block_diag_inv(rs_bd, RWKV_H),
                                   shift_new.reshape(bp, RWKV_FEAT), _block_diag_inv(gs_bd, GLA_H))):
            acc.append(s)

        ssd_slab, small, rw, gla_slab = _inproj(xs, mod_s[1], mod_s[0], w_pad, bs)
        y_ssd, y_rwkv, y_gla, conv_new, h_new, rs_new, gs_new = _mixer_step(
            ssd_slab[0], small[0], rw[0], gla_slab[0], state_conv[l].reshape(bs, -1), state_shift[l],
            state_ssd[l], state_rwkv[l], state_gla[l], ssd_p, rwkv_p, gla_p)
        xs = _tail(xs, mod_s, l, p, y_ssd[None], y_rwkv[None], y_gla[None], bs)
        for acc, s in zip(outs_s, (h_new, conv_new, rs_new, rw[0].astype(F32), gs_new)):
            acc.append(s)

    stack = lambda accs: tuple(jnp.stack(a, axis=0) for a in accs)
    return (xp, xs.reshape(bs, 1, D_MODEL)) + stack(outs_p) + stack(outs_s)


def kernel(x_prompt, x_sample, c_prompt, c_sample, state_ssd, state_ssd_conv, state_rwkv, state_rwkv_shift, state_gla, w_ada, b_ada, w_in, w_out, ssd_conv_w, ssd_conv_b, ssd_dt_bias, ssd_a_log, ssd_d, ssd_norm_g, rwkv_mu, rwkv_w0, rwkv_w2, rwkv_a0, rwkv_a2, rwkv_g2, rwkv_k_k, rwkv_k_a, rwkv_r_k, rwkv_ln_g, rwkv_ln_b, gla_w_gk2, gla_b_gk, gla_norm_g, ln_mix_g, ln_mix_b, ln_ffn_g, ln_ffn_b, ffn_w_gate, ffn_w_up, ffn_w_down, moe_router, moe_w_gate, moe_w_up, moe_w_down):
    p = dict(w_ada=w_ada, b_ada=b_ada, w_in=w_in, w_out=w_out, ssd_conv_w=ssd_conv_w, ssd_conv_b=ssd_conv_b,
             ssd_dt_bias=ssd_dt_bias, ssd_a_log=ssd_a_log, ssd_d=ssd_d, ssd_norm_g=ssd_norm_g,
             rwkv_mu=rwkv_mu, rwkv_w0=rwkv_w0, rwkv_w2=rwkv_w2, rwkv_a0=rwkv_a0, rwkv_a2=rwkv_a2,
             rwkv_g2=rwkv_g2, rwkv_k_k=rwkv_k_k, rwkv_k_a=rwkv_k_a, rwkv_r_k=rwkv_r_k,
             rwkv_ln_g=rwkv_ln_g, rwkv_ln_b=rwkv_ln_b, gla_w_gk2=gla_w_gk2, gla_b_gk=gla_b_gk,
             gla_norm_g=gla_norm_g, ln_mix_g=ln_mix_g, ln_mix_b=ln_mix_b, ln_ffn_g=ln_ffn_g,
             ln_ffn_b=ln_ffn_b, ffn_w_gate=ffn_w_gate, ffn_w_up=ffn_w_up, ffn_w_down=ffn_w_down,
             moe_router=moe_router, moe_w_gate=moe_w_gate, moe_w_up=moe_w_up, moe_w_down=moe_w_down)
    states = (state_ssd, state_ssd_conv, state_rwkv, state_rwkv_shift, state_gla)
    return _forward(x_prompt, x_sample, c_prompt, c_sample, states, p)
```
